```python
import math
import jax, jax.numpy as jnp
from jax import lax
import numpy as np

D_MODEL = 1024
BATCH = 2
SEQ = 8192
DEPTH = 1

D_RNN = 1024
RNN_BLOCKS = 16
RNN_BW = D_RNN // RNN_BLOCKS
CONV_W = 4
LRU_C = 8.0
N_HEADS = 16
N_KV = 4
HEAD_DIM = 64
GROUP = N_HEADS // N_KV
WINDOW = 128
BLOCK_Q = 128
Q_DIM = N_HEADS * HEAD_DIM
KV_DIM = N_KV * HEAD_DIM
N_EXPERTS = 32
TOP_K = 4
D_FF = 1024
SWIGLU_LIMIT = 7.0
SWIGLU_ALPHA = 1.702
MOE_BLOCK = 128
EPS = 1e-6
SPLITS = (D_RNN, D_RNN, Q_DIM, KV_DIM, KV_DIM, D_MODEL, D_MODEL)
D_IN = sum(SPLITS)

kernel_name = "hybrid_rglru_swa_sink_moe_block"


def rmsnorm(x, g):
    xf = x.astype(jnp.float32)
    y = xf * lax.rsqrt(jnp.mean(xf * xf, axis=-1, keepdims=True) + EPS)
    return (y * g.astype(jnp.float32)).astype(x.dtype)


def causal_depthwise_conv(x, w, b):
    S = x.shape[1]
    xp = jnp.pad(x, ((0, 0), (CONV_W - 1, 0), (0, 0)))
    y = b
    for kk in range(CONV_W):
        y = y + xp[:, kk:kk + S] * w[kk]
    return y


def rg_lru(x, w_a, b_a, w_x, b_x, lam):
    B, S, _ = x.shape
    xb = x.reshape(B, S, RNN_BLOCKS, RNN_BW)
    gate_r = jax.nn.sigmoid(jnp.einsum('bshi,hij->bshj', xb, w_a).reshape(B, S, D_RNN) + b_a)
    gate_i = jax.nn.sigmoid(jnp.einsum('bshi,hij->bshj', xb, w_x).reshape(B, S, D_RNN) + b_x)
    log_a = -LRU_C * gate_r.astype(jnp.float32) * jax.nn.softplus(-lam.astype(jnp.float32))
    a = jnp.exp(log_a)
    mult = jnp.sqrt(-jnp.expm1(2.0 * log_a))
    reset = (jnp.arange(S) == 0)[None, :, None]
    mult = jnp.where(reset, 1.0, mult)
    bterm = (x * gate_i).astype(jnp.float32) * mult

    def combine(left, right):
        a1, b1 = left
        a2, b2 = right
        return a1 * a2, a2 * b1 + b2

    _, h = lax.associative_scan(combine, (a, bterm), axis=1)
    return h.astype(x.dtype)


def alibi_slopes():
    return np.array([2.0 ** (-8.0 * (h + 1) / N_HEADS) for h in range(N_HEADS)], dtype=np.float32)


def swa_sink_attention(q, k, v, sinks):
    B, S = q.shape[0], q.shape[1]
    nb = S // BLOCK_Q
    qb = q.reshape(B, nb, BLOCK_Q, N_KV, GROUP, HEAD_DIM)
    pad = ((0, 0), (BLOCK_Q, 0), (0, 0), (0, 0))
    kb = jnp.pad(k, pad).reshape(B, nb + 1, BLOCK_Q, N_KV, HEAD_DIM)
    vb = jnp.pad(v, pad).reshape(B, nb + 1, BLOCK_Q, N_KV, HEAD_DIM)
    k_band = jnp.concatenate([kb[:, :-1], kb[:, 1:]], axis=2)
    v_band = jnp.concatenate([vb[:, :-1], vb[:, 1:]], axis=2)
    s = jnp.einsum('bnqkgd,bnckd->bnkgqc', qb, k_band,
                   preferred_element_type=jnp.float32) * (HEAD_DIM ** -0.5)
    qi = jnp.arange(BLOCK_Q)[:, None]
    ci = jnp.arange(2 * BLOCK_Q)[None, :]
    dist = qi + BLOCK_Q - ci
    kpos = jnp.arange(nb)[:, None, None] * BLOCK_Q - BLOCK_Q + ci[None]
    valid = (dist >= 0) & (dist < WINDOW) & (kpos >= 0)
    slopes = jnp.asarray(alibi_slopes()).reshape(N_KV, GROUP)[:, :, None, None]
    s = s - slopes * dist.astype(jnp.float32)
    s = jnp.where(valid[None, :, None, None], s, -jnp.inf)
    sink = sinks.astype(jnp.float32).reshape(N_KV, GROUP)[:, :, None, None]
    m = jnp.maximum(jnp.max(s, axis=-1, keepdims=True), sink)
    p = jnp.exp(s - m)
    denom = jnp.sum(p, axis=-1, keepdims=True) + jnp.exp(sink - m)
    o = jnp.einsum('bnkgqc,bnckd->bnqkgd', (p / denom).astype(v.dtype), v_band)
    return o.reshape(B, S, Q_DIM)


def clamped_swiglu(hid):
    x_glu = jnp.minimum(hid[..., ::2], SWIGLU_LIMIT)
    x_lin = jnp.clip(hid[..., 1::2], -SWIGLU_LIMIT, SWIGLU_LIMIT)
    return x_glu * jax.nn.sigmoid(SWIGLU_ALPHA * x_glu) * (x_lin + 1.0)


def moe_ffn(h, router_w, router_b, w1, b1, w2, b2):
    B, S, D = h.shape
    N = B * S
    xt = h.reshape(N, D)
    logits = (xt @ router_w + router_b).astype(jnp.float32)
    top_v, top_e = lax.top_k(logits, TOP_K)
    gates = jax.nn.softmax(top_v, axis=-1)
    A = N * TOP_K
    e_flat = top_e.reshape(A)
    g_flat = gates.reshape(A)
    tok = jnp.arange(A, dtype=jnp.int32) // TOP_K
    order = jnp.argsort(e_flat, stable=True)
    e_s, tok_s, g_s = e_flat[order], tok[order], g_flat[order]
    counts = jnp.bincount(e_flat, length=N_EXPERTS)
    starts = jnp.cumsum(counts) - counts
    padded = ((counts + MOE_BLOCK - 1) // MOE_BLOCK) * MOE_BLOCK
    pend = jnp.cumsum(padded)
    pstart = pend - padded
    dest = pstart[e_s] + jnp.arange(A, dtype=jnp.int32) - starts[e_s]
    P = A + N_EXPERTS * MOE_BLOCK
    nblk = P // MOE_BLOCK
    tok_buf = jnp.full((P,), N, dtype=jnp.int32).at[dest].set(tok_s)
    g_buf = jnp.zeros((P,), jnp.float32).at[dest].set(g_s)
    blk_e = jnp.clip(jnp.searchsorted(pend, jnp.arange(nblk, dtype=jnp.int32) * MOE_BLOCK,
                                      side='right'), 0, N_EXPERTS - 1)
    x_pad = jnp.concatenate([xt, jnp.zeros((1, D), xt.dtype)], axis=0)
    xb = x_pad[tok_buf].reshape(nblk, MOE_BLOCK, D)

    def expert_block(args):
        xblk, e = args
        act = clamped_swiglu(xblk @ w1[e] + b1[e])
        return act @ w2[e] + b2[e]

    yb = lax.map(expert_block, (xb, blk_e)).reshape(P, D)
    yb = yb * g_buf[:, None].astype(yb.dtype)
    out = jax.ops.segment_sum(yb, tok_buf, num_segments=N + 1)[:N]
    return out.reshape(B, S, D)


def setup_inputs(seed: int = 0) -> dict:
    key = jax.random.key(seed)
    ks = jax.random.split(key, 24)
    f32 = jnp.float32
    L = DEPTH

    def nrm(k, shape, scale):
        return jax.random.normal(k, shape, f32) * scale

    def gain(k, shape):
        return 1.0 + 0.05 * jax.random.normal(k, shape, f32)

    u = jax.random.uniform(ks[12], (L, D_RNN), f32, 0.9, 0.999)
    sig = u ** (1.0 / LRU_C)
    rg_lambda = jnp.log(sig) - jnp.log1p(-sig)
    return {
        "x": nrm(ks[0], (BATCH, SEQ, D_MODEL), 1.0),
        "c": nrm(ks[1], (BATCH, D_MODEL), 1.0),
        "w_ada": nrm(ks[2], (L, D_MODEL, 6 * D_MODEL), 0.5 * D_MODEL ** -0.5),
        "b_ada": nrm(ks[3], (L, 6 * D_MODEL), 0.02),
        "norm_pre_mix": gain(ks[4], (L, D_MODEL)),
        "norm_post_mix": gain(ks[5], (L, D_MODEL)),
        "norm_pre_ffn": gain(ks[6], (L, D_MODEL)),
        "norm_post_ffn": gain(ks[7], (L, D_MODEL)),
        "w_in": nrm(ks[8], (L, D_MODEL, D_IN), D_MODEL ** -0.5),
        "b_in": nrm(ks[9], (L, D_IN), 0.02),
        "conv_w": nrm(ks[10], (L, CONV_W, D_RNN), CONV_W ** -0.5),
        "conv_b": nrm(ks[11], (L, D_RNN), 0.02),
        "rg_w_a": nrm(ks[13], (L, RNN_BLOCKS, RNN_BW, RNN_BW), RNN_BW ** -0.5),
        "rg_b_a": nrm(ks[14], (L, D_RNN), 0.02),
        "rg_w_x": nrm(ks[15], (L, RNN_BLOCKS, RNN_BW, RNN_BW), RNN_BW ** -0.5),
        "rg_b_x": nrm(ks[16], (L, D_RNN), 0.02),
        "rg_lambda": rg_lambda,
        "attn_sinks": nrm(ks[17], (L, N_HEADS), 0.5),
        "w_o_rnn": nrm(ks[18], (L, D_RNN, D_MODEL), D_RNN ** -0.5),
        "w_o_attn": nrm(ks[19], (L, Q_DIM, D_MODEL), Q_DIM ** -0.5),
        "w_out": nrm(ks[20], (L, D_MODEL, D_MODEL), D_MODEL ** -0.5),
        "router_w": nrm(ks[21], (L, D_MODEL, N_EXPERTS), D_MODEL ** -0.5),
        "router_b": nrm(ks[22], (L, N_EXPERTS), 0.01),
        "moe_w1": nrm(jax.random.fold_in(ks[23], 0), (L, N_EXPERTS, D_MODEL, 2 * D_FF), D_MODEL ** -0.5),
        "moe_b1": nrm(jax.random.fold_in(ks[23], 1), (L, N_EXPERTS, 2 * D_FF), 0.02),
        "moe_w2": nrm(jax.random.fold_in(ks[23], 2), (L, N_EXPERTS, D_FF, D_MODEL), D_FF ** -0.5),
        "moe_b2": nrm(jax.random.fold_in(ks[23], 3), (L, N_EXPERTS, D_MODEL), 0.02),
    }


def reference(x, c, w_ada, b_ada, norm_pre_mix, norm_post_mix, norm_pre_ffn, norm_post_ffn,
              w_in, b_in, conv_w, conv_b, rg_w_a, rg_b_a, rg_w_x, rg_b_x, rg_lambda,
              attn_sinks, w_o_rnn, w_o_attn, w_out, router_w, router_b,
              moe_w1, moe_b1, moe_w2, moe_b2):
    B, S, _ = x.shape
    split_points = np.cumsum(np.array(SPLITS))[:-1].tolist()
    for layer in range(DEPTH):
        ada = (jax.nn.silu(c) @ w_ada[layer] + b_ada[layer])[:, None, :]
        sh1, sc1, g1, sh2, sc2, g2 = jnp.split(ada, 6, axis=-1)

        h = rmsnorm(x, norm_pre_mix[layer]) * (1.0 + sc1) + sh1
        proj = h @ w_in[layer] + b_in[layer]
        xr, gr, q, k, v, gate_r, gate_a = jnp.split(proj, split_points, axis=-1)
        xr = causal_depthwise_conv(xr, conv_w[layer], conv_b[layer])
        y_rnn = rg_lru(xr, rg_w_a[layer], rg_b_a[layer], rg_w_x[layer], rg_b_x[layer],
                       rg_lambda[layer]) * jax.nn.gelu(gr)
        y_att = swa_sink_attention(q.reshape(B, S, N_HEADS, HEAD_DIM),
                                   k.reshape(B, S, N_KV, HEAD_DIM),
                                   v.reshape(B, S, N_KV, HEAD_DIM), attn_sinks[layer])
        merged = (jax.nn.sigmoid(gate_r) * (y_rnn @ w_o_rnn[layer])
                  + jax.nn.sigmoid(gate_a) * (y_att @ w_o_attn[layer]))
        mix = merged @ w_out[layer]
        x = x + g1 * rmsnorm(mix, norm_post_mix[layer])

        h = rmsnorm(x, norm_pre_ffn[layer]) * (1.0 + sc2) + sh2
        ff = moe_ffn(h, router_w[layer], router_b[layer], moe_w1[layer], moe_b1[layer],
                     moe_w2[layer], moe_b2[layer])
        x = x + g2 * rmsnorm(ff, norm_post_ffn[layer])
    return x
```

```python
import functools
import math

import jax
import jax.numpy as jnp
import numpy as np
from jax import lax
from jax.experimental import pallas as pl
from jax.experimental.pallas import tpu as pltpu

D_MODEL = 1024
D_RNN = 1024
RNN_BLOCKS = 16
RNN_BW = D_RNN // RNN_BLOCKS
CONV_W = 4
LRU_C = 8.0
N_HEADS = 16
N_KV = 4
HEAD_DIM = 64
GROUP = N_HEADS // N_KV
WINDOW = 128
Q_DIM = N_HEADS * HEAD_DIM
KV_DIM = N_KV * HEAD_DIM
N_EXPERTS = 32
TOP_K = 4
D_FF = 1024
SWIGLU_LIMIT = 7.0
SWIGLU_ALPHA = 1.702
EPS = 1e-6
D_IN = 2 * D_RNN + Q_DIM + 2 * KV_DIM + 2 * D_MODEL
D_REST = D_IN - 2 * D_RNN

V7X_LANES = 128
V7X_SUBLANES = 8
V7X_MXU_DIM = 256
V7X_VMEM_BYTES = 64 * 1024 * 1024

F32 = jnp.float32
BF16 = jnp.bfloat16


def _tiles():
    return dict(
        ada_tn=1024,
        tok=512,
        attn_q=WINDOW,
        route_t=512,
        disp_t=256,
        moe_bm=128,
        comb_t=256,
        rnn_group=V7X_MXU_DIM,
    )


def _vmem_limit(nbytes):
    return int(min(max(nbytes, 16 * 1024 * 1024), V7X_VMEM_BYTES - 8 * 1024 * 1024))


def _rms(x, g):
    return x * lax.rsqrt(jnp.mean(x * x, axis=-1, keepdims=True) + EPS) * g


def _ada_kernel(ct_ref, w_ref, b_ref, o_ref):
    ct = ct_ref[...]
    sc = ct * jax.nn.sigmoid(ct)
    w = w_ref[...]
    rows = [jnp.sum(w * sc[:, b:b + 1], axis=0, keepdims=True) for b in range(ct.shape[1])]
    o_ref[...] = jnp.concatenate(rows, axis=0) + b_ref[...]


def _ada(c, w_ada, b_ada):
    B, D = c.shape
    n_out = w_ada.shape[1]
    tn = _tiles()["ada_tn"]
    return pl.pallas_call(
        _ada_kernel,
        grid=(n_out // tn,),
        in_specs=[
            pl.BlockSpec((D, B), lambda j: (0, 0)),
            pl.BlockSpec((D, tn), lambda j: (0, j)),
            pl.BlockSpec((1, tn), lambda j: (0, j)),
        ],
        out_specs=pl.BlockSpec((B, tn), lambda j: (0, j)),
        out_shape=jax.ShapeDtypeStruct((B, n_out), F32),
        compiler_params=pltpu.CompilerParams(
            dimension_semantics=("arbitrary",),
            vmem_limit_bytes=_vmem_limit(4 * D * tn * 4)),
        name="ada",
    )(c.T, w_ada, b_ada.reshape(1, n_out))


def _inproj_kernel(x_ref, g_ref, sc_ref, sh_ref, w_ref, b_ref, rnn_ref, rest_ref, *, chunk):
    x = x_ref[...]
    h = _rms(x, g_ref[...]) * (1.0 + sc_ref[0]) + sh_ref[0]
    hb = h.astype(BF16)
    n_rnn = rnn_ref.shape[1]
    for c0 in range(0, w_ref.shape[1], chunk):
        acc = jnp.dot(hb, w_ref[:, c0:c0 + chunk], preferred_element_type=F32)
        acc = acc + b_ref[:, c0:c0 + chunk]
        if c0 < n_rnn:
            rnn_ref[:, c0:c0 + chunk] = acc
        else:
            rest_ref[:, c0 - n_rnn:c0 - n_rnn + chunk] = acc.astype(BF16)


def _inproj(x2, g, sc, sh, w_bf, b_in, S):
    N, D = x2.shape
    tm = _tiles()["tok"]
    per_b = S // tm
    chunk = 2 * V7X_MXU_DIM
    vmem = D * D_IN * 2 + 2 * tm * (D * 4 + 2 * D_RNN * 4 + D_REST * 2) + 4 * tm * chunk * 4
    return pl.pallas_call(
        functools.partial(_inproj_kernel, chunk=chunk),
        grid=(N // tm,),
        in_specs=[
            pl.BlockSpec((tm, D), lambda i: (i, 0)),
            pl.BlockSpec((1, D), lambda i: (0, 0)),
            pl.BlockSpec((1, 1, D), lambda i: (i // per_b, 0, 0)),
            pl.BlockSpec((1, 1, D), lambda i: (i // per_b, 0, 0)),
            pl.BlockSpec((D, D_IN), lambda i: (0, 0), pipeline_mode=pl.Buffered(1)),
            pl.BlockSpec((1, D_IN), lambda i: (0, 0)),
        ],
        out_specs=[
            pl.BlockSpec((tm, 2 * D_RNN), lambda i: (i, 0)),
            pl.BlockSpec((tm, D_REST), lambda i: (i, 0)),
        ],
        out_shape=[
            jax.ShapeDtypeStruct((N, 2 * D_RNN), F32),
            jax.ShapeDtypeStruct((N, D_REST), BF16),
        ],
        compiler_params=pltpu.CompilerParams(
            dimension_semantics=("arbitrary",),
            vmem_limit_bytes=_vmem_limit(vmem + 8 * 1024 * 1024)),
        name="inproj",
    )(x2, g, sc, sh, w_bf, b_in)


def _gelu_tanh(x):
    return 0.5 * x * (1.0 + jnp.tanh(math.sqrt(2.0 / math.pi) * (x + 0.044715 * (x * x * x))))


def _softplus(z):
    return jnp.maximum(z, 0.0) + jnp.log1p(jnp.exp(-jnp.abs(z)))


def _rnn_kernel(xr_ref, gr_ref, cw_ref, cb_ref, wa_ref, ba_ref, wx_ref, bx_ref, lam_ref,
                y_ref, xbuf, a_s, b_s, h_s, carry, *, per_b, gw):
    t = pl.program_id(0) % per_b
    tt = xr_ref.shape[0]
    halo = V7X_SUBLANES

    @pl.when(t == 0)
    def _():
        xbuf[0:halo, :] = jnp.zeros((halo, xbuf.shape[1]), F32)
        carry[...] = jnp.zeros_like(carry)

    xbuf[halo:halo + tt, :] = xr_ref[...]
    xc = cb_ref[...] + xbuf[pl.ds(halo - (CONV_W - 1), tt), :] * cw_ref[0:1, :]
    for kk in range(1, CONV_W):
        xc = xc + xbuf[pl.ds(halo - (CONV_W - 1) + kk, tt), :] * cw_ref[kk:kk + 1, :]
    xbuf[0:halo, :] = xbuf[tt:tt + halo, :]

    xcb = xc.astype(BF16)
    sp = _softplus(-lam_ref[...])
    first = t == 0
    row = lax.broadcasted_iota(jnp.int32, (tt, gw), 0)
    for g in range(xc.shape[1] // gw):
        cs = slice(g * gw, (g + 1) * gw)
        xg = xcb[:, cs]
        gate_r = jax.nn.sigmoid(jnp.dot(xg, wa_ref[g], preferred_element_type=F32) + ba_ref[:, cs])
        gate_i = jax.nn.sigmoid(jnp.dot(xg, wx_ref[g], preferred_element_type=F32) + bx_ref[:, cs])
        log_a = -LRU_C * gate_r * sp[:, cs]
        th = jnp.tanh(log_a)
        mult = jnp.sqrt(-2.0 * th / (1.0 - th))
        mult = jnp.where(jnp.logical_and(first, row == 0), 1.0, mult)
        a_s[:, cs] = jnp.exp(log_a)
        b_s[:, cs] = (xc[:, cs] * gate_i) * mult

    srow = lax.broadcasted_iota(jnp.int32, (V7X_SUBLANES, xc.shape[1]), 0)

    def body(i, h_prev):
        r = pl.multiple_of(i * V7X_SUBLANES, V7X_SUBLANES)
        a = a_s[pl.ds(r, V7X_SUBLANES), :]
        b = b_s[pl.ds(r, V7X_SUBLANES), :]
        for s in (1, 2, 4):
            keep = srow >= s
            b = jnp.where(keep, a * pltpu.roll(b, s, 0) + b, b)
            a = jnp.where(keep, a * pltpu.roll(a, s, 0), a)
        h = a * h_prev + b
        h_s[pl.ds(r, V7X_SUBLANES), :] = h
        return h[V7X_SUBLANES - 1:V7X_SUBLANES, :]

    carry[...] = lax.fori_loop(0, tt // V7X_SUBLANES, body, carry[...])
    y_ref[...] = (h_s[...] * _gelu_tanh(gr_ref[...])).astype(BF16)


def _block_diag_tiles(w, gw):
    nb, bw, _ = w.shape
    per = gw // bw
    w4 = w.reshape(nb // per, per, bw, bw)
    eye = jnp.eye(per, dtype=w.dtype)
    return jnp.einsum("gpij,pq->gpiqj", w4, eye).reshape(nb // per, gw, gw)


def _rnn(proj_rnn, conv_w, conv_b, wa, ba, wx, bx, lam, S):
    N = proj_rnn.shape[0]
    C = D_RNN
    tt = _tiles()["tok"]
    gw = _tiles()["rnn_group"]
    per_b = S // tt
    vec = lambda: pl.BlockSpec((1, C), lambda i: (0, 0))
    return pl.pallas_call(
        functools.partial(_rnn_kernel, per_b=per_b, gw=gw),
        grid=(N // tt,),
        in_specs=[
            pl.BlockSpec((tt, C), lambda i: (i, 0)),
            pl.BlockSpec((tt, C), lambda i: (i, 1)),
            pl.BlockSpec((CONV_W, C), lambda i: (0, 0)),
            vec(),
            pl.BlockSpec((C // gw, gw, gw), lambda i: (0, 0, 0)),
            vec(),
            pl.BlockSpec((C // gw, gw, gw), lambda i: (0, 0, 0)),
            vec(),
            vec(),
        ],
        out_specs=pl.BlockSpec((tt, C), lambda i: (i, 0)),
        out_shape=jax.ShapeDtypeStruct((N, C), BF16),
        scratch_shapes=[
            pltpu.VMEM((tt + V7X_SUBLANES, C), F32),
            pltpu.VMEM((tt, C), F32),
            pltpu.VMEM((tt, C), F32),
            pltpu.VMEM((tt, C), F32),
            pltpu.VMEM((1, C), F32),
        ],
        compiler_params=pltpu.CompilerParams(
            dimension_semantics=("arbitrary",),
            vmem_limit_bytes=_vmem_limit(16 * tt * C * 4)),
        name="rnn",
    )(proj_rnn, proj_rnn, conv_w, conv_b, wa, ba, wx, bx, lam)


def _alibi_slopes():
    return [2.0 ** (-8.0 * (h + 1) / N_HEADS) for h in range(N_HEADS)]


def _attn_kernel(sink_ref, q_ref, kp_ref, kc_ref, vp_ref, vc_ref, o_ref, *, per_b):
    blk = pl.program_id(0) % per_b
    bq = q_ref.shape[0]
    qi = lax.broadcasted_iota(jnp.int32, (bq, 2 * bq), 0)
    ci = lax.broadcasted_iota(jnp.int32, (bq, 2 * bq), 1)
    dist = qi + bq - ci
    valid = (dist >= 0) & (dist < WINDOW) & ((ci >= bq) | (blk > 0))
    distf = dist.astype(F32)
    slopes = _alibi_slopes()
    outs = []
    for kv in range(N_KV):
        ks = slice(kv * HEAD_DIM, (kv + 1) * HEAD_DIM)
        kb = jnp.concatenate([kp_ref[:, ks], kc_ref[:, ks]], axis=0)
        vb = jnp.concatenate([vp_ref[:, ks], vc_ref[:, ks]], axis=0)
        for g in range(GROUP):
            h = kv * GROUP + g
            qh = q_ref[:, h * HEAD_DIM:(h + 1) * HEAD_DIM]
            s = lax.dot_general(qh, kb, (((1,), (1,)), ((), ())), preferred_element_type=F32)
            s = s * (HEAD_DIM ** -0.5) - slopes[h] * distf
            s = jnp.where(valid, s, -jnp.inf)
            sink = sink_ref[h]
            m = jnp.maximum(jnp.max(s, axis=-1, keepdims=True), sink)
            p = jnp.exp(s - m)
            denom = jnp.sum(p, axis=-1, keepdims=True) + jnp.exp(sink - m)
            o = jnp.dot(p.astype(BF16), vb, preferred_element_type=F32) / denom
            outs.append(o)
    o_ref[...] = jnp.concatenate(outs, axis=-1).astype(BF16)


def _attn(proj_rest, sinks, S):
    N = proj_rest.shape[0]
    bq = _tiles()["attn_q"]
    per_b = S // bq
    k_col = (Q_DIM + 2 * D_MODEL) // KV_DIM
    v_col = k_col + 1

    def prev(i):
        return jnp.where(i % per_b == 0, i, i - 1)

    return pl.pallas_call(
        functools.partial(_attn_kernel, per_b=per_b),
        grid=(N // bq,),
        in_specs=[
            pl.BlockSpec(memory_space=pltpu.SMEM),
            pl.BlockSpec((bq, Q_DIM), lambda i: (i, 0)),
            pl.BlockSpec((bq, KV_DIM), lambda i: (prev(i), k_col)),
            pl.BlockSpec((bq, KV_DIM), lambda i: (i, k_col)),
            pl.BlockSpec((bq, KV_DIM), lambda i: (prev(i), v_col)),
            pl.BlockSpec((bq, KV_DIM), lambda i: (i, v_col)),
        ],
        out_specs=pl.BlockSpec((bq, Q_DIM), lambda i: (i, 0)),
        out_shape=jax.ShapeDtypeStruct((N, Q_DIM), BF16),
        compiler_params=pltpu.CompilerParams(dimension_semantics=("arbitrary",)),
        name="attn",
    )(sinks, proj_rest, proj_rest, proj_rest, proj_rest, proj_rest)


def _merge_kernel(x_ref, yr_ref, ya_ref, gr_ref, ga_ref, wr_ref, wa_ref, wo_ref,
                  gpost_ref, g1_ref, gpre_ref, sc2_ref, sh2_ref, rwt_ref, rb_ref,
                  x1_ref, h2_ref, lg_ref):
    r = jnp.dot(yr_ref[...], wr_ref[...], preferred_element_type=F32)
    a = jnp.dot(ya_ref[...], wa_ref[...], preferred_element_type=F32)
    merged = (jax.nn.sigmoid(gr_ref[...].astype(F32)) * r
              + jax.nn.sigmoid(ga_ref[...].astype(F32)) * a)
    mix = jnp.dot(merged.astype(BF16), wo_ref[...], preferred_element_type=F32)
    x1 = x_ref[...] + g1_ref[0] * _rms(mix, gpost_ref[...])
    x1_ref[...] = x1
    h2 = _rms(x1, gpre_ref[...]) * (1.0 + sc2_ref[0]) + sh2_ref[0]
    h2_ref[...] = h2
    lg = lax.dot_general(rwt_ref[...], h2.astype(BF16), (((1,), (1,)), ((), ())),
                         preferred_element_type=F32)
    lg_ref[...] = lg + rb_ref[...]


def _merge(x2, y_rnn, y_att, proj_rest, wr, wa, wo, gpost, g1, gpre, sc2, sh2, rwt, rb, S):
    N, D = x2.shape
    tm = _tiles()["tok"]
    per_b = S // tm
    gate_r_col = Q_DIM // D
    mat = lambda: pl.BlockSpec((D, D), lambda i: (0, 0))
    vec = lambda: pl.BlockSpec((1, D), lambda i: (0, 0))
    bvec = lambda: pl.BlockSpec((1, 1, D), lambda i: (i // per_b, 0, 0))
    tile = lambda col=0: pl.BlockSpec((tm, D), lambda i: (i, col))
    return pl.pallas_call(
        _merge_kernel,
        grid=(N // tm,),
        in_specs=[
            tile(), tile(), tile(), tile(gate_r_col), tile(gate_r_col + 1),
            mat(), mat(), mat(),
            vec(), bvec(), vec(), bvec(), bvec(),
            pl.BlockSpec((N_EXPERTS, D), lambda i: (0, 0)),
            pl.BlockSpec((N_EXPERTS, 1), lambda i: (0, 0)),
        ],
        out_specs=[
            tile(), tile(),
            pl.BlockSpec((N_EXPERTS, tm), lambda i: (0, i)),
        ],
        out_shape=[
            jax.ShapeDtypeStruct((N, D), F32),
            jax.ShapeDtypeStruct((N, D), F32),
            jax.ShapeDtypeStruct((N_EXPERTS, N), F32),
        ],
        compiler_params=pltpu.CompilerParams(
            dimension_semantics=("arbitrary",),
            vmem_limit_bytes=_vmem_limit(6 * D * D * 2 + 24 * tm * D * 4)),
        name="merge",
    )(x2, y_rnn, y_att, proj_rest, proj_rest, wr, wa, wo, gpost, g1, gpre, sc2, sh2, rwt, rb)


def _route_kernel(lg_ref, e_ref, g_ref, rk_ref, cnt_ref, carry):
    @pl.when(pl.program_id(0) == 0)
    def _():
        carry[...] = jnp.zeros_like(carry)

    l = lg_ref[...]
    E, T = l.shape
    row = lax.broadcasted_iota(jnp.int32, (E, T), 0).astype(F32)
    vals, idxs = [], []
    for _ in range(TOP_K):
        m = jnp.max(l, axis=0, keepdims=True)
        idx = jnp.min(jnp.where(l == m, row, float(E)), axis=0, keepdims=True)
        vals.append(m)
        idxs.append(idx)
        l = jnp.where(row == idx, -jnp.inf, l)
    ex = [jnp.exp(v - vals[0]) for v in vals]
    tot = ex[0]
    for e in ex[1:]:
        tot = tot + e
    g_ref[...] = jnp.concatenate([e / tot for e in ex], axis=0)
    e_ref[...] = jnp.concatenate(idxs, axis=0).astype(jnp.int32)

    hot = [row == idx for idx in idxs]
    onehot = jnp.zeros((E, T), F32)
    for hk in hot:
        onehot = onehot + hk.astype(F32)
    tri = (lax.broadcasted_iota(jnp.int32, (T, T), 0)
           < lax.broadcasted_iota(jnp.int32, (T, T), 1)).astype(BF16)
    before = jnp.dot(onehot.astype(BF16), tri, preferred_element_type=F32) + carry[...]
    ranks = [jnp.sum(jnp.where(hk, before, 0.0), axis=0, keepdims=True) for hk in hot]
    rk_ref[...] = jnp.concatenate(ranks, axis=0).astype(jnp.int32)
    total = carry[...] + jnp.sum(onehot, axis=1, keepdims=True)
    carry[...] = total
    cnt_ref[...] = total.astype(jnp.int32)


def _route(logits_t):
    E, N = logits_t.shape
    T = _tiles()["route_t"]
    out = lambda: pl.BlockSpec((TOP_K, T), lambda i: (0, i))
    return pl.pallas_call(
        _route_kernel,
        grid=(N // T,),
        in_specs=[pl.BlockSpec((E, T), lambda i: (0, i))],
        out_specs=[out(), out(), out(), pl.BlockSpec((E, 1), lambda i: (0, 0))],
        out_shape=[
            jax.ShapeDtypeStruct((TOP_K, N), jnp.int32),
            jax.ShapeDtypeStruct((TOP_K, N), F32),
            jax.ShapeDtypeStruct((TOP_K, N), jnp.int32),
            jax.ShapeDtypeStruct((E, 1), jnp.int32),
        ],
        scratch_shapes=[pltpu.VMEM((E, 1), F32)],
        compiler_params=pltpu.CompilerParams(dimension_semantics=("arbitrary",)),
        name="route",
    )(logits_t)


def _dispatch_kernel(tail_ref, nu_ref, pos_ref, h_ref, xs_ref, zbuf, sem):
    i = pl.program_id(0)
    T = h_ref.shape[0]
    bm = zbuf.shape[0]
    nblk = xs_ref.shape[0] // bm

    def zero_copy(row0):
        return pltpu.make_async_copy(zbuf, xs_ref.at[pl.ds(pl.multiple_of(row0, bm), bm)], sem)

    @pl.when(i == 0)
    def _():
        zbuf[...] = jnp.zeros_like(zbuf)

        def start(e, c):
            zero_copy(tail_ref[e]).start()
            return c

        def wait(e, c):
            zero_copy(tail_ref[e]).wait()
            return c

        def start_unused(j, c):
            zero_copy(j * bm).start()
            return c

        def wait_unused(j, c):
            zero_copy(j * bm).wait()
            return c

        lax.fori_loop(0, N_EXPERTS, start, 0)
        lax.fori_loop(nu_ref[0], nblk, start_unused, 0)
        lax.fori_loop(0, N_EXPERTS, wait, 0)
        lax.fori_loop(nu_ref[0], nblk, wait_unused, 0)

    def row_copy(r, k):
        dst = pos_ref[r * TOP_K + k]
        return pltpu.make_async_copy(h_ref.at[pl.ds(r, 1)], xs_ref.at[pl.ds(dst, 1)], sem)

    def start(r, c):
        for k in range(TOP_K):
            row_copy(r, k).start()
        return c

    def wait(r, c):
        for k in range(TOP_K):
            row_copy(r, k).wait()
        return c

    lax.fori_loop(0, T, start, 0)
    lax.fori_loop(0, T, wait, 0)


def _dispatch(h2, pos_flat, tail_start, n_used, n_rows, bm):
    N, D = h2.shape
    T = _tiles()["disp_t"]
    return pl.pallas_call(
        _dispatch_kernel,
        grid_spec=pltpu.PrefetchScalarGridSpec(
            num_scalar_prefetch=2,
            grid=(N // T,),
            in_specs=[
                pl.BlockSpec((T * TOP_K,), lambda i, tail, nu: (i,), memory_space=pltpu.SMEM),
                pl.BlockSpec((T, D), lambda i, tail, nu: (i, 0)),
            ],
            out_specs=pl.BlockSpec(memory_space=pl.ANY),
            scratch_shapes=[pltpu.VMEM((bm, D), F32), pltpu.SemaphoreType.DMA(())],
        ),
        out_shape=jax.ShapeDtypeStruct((n_rows, D), F32),
        compiler_params=pltpu.CompilerParams(dimension_semantics=("arbitrary",)),
        name="dispatch",
    )(tail_start, n_used, pos_flat, h2)


def _ffn_kernel(be_ref, bv_ref, bf_ref, nu_ref, x_ref, w1_ref, b1g_ref, b1l_ref, w2_ref, b2_ref,
                y_ref, w1g_s, w1l_s, w2_s):
    i = pl.program_id(0)
    bm, D = x_ref.shape
    pw = 2 * V7X_LANES

    @pl.when(jnp.logical_and(i < nu_ref[0], bf_ref[i] == 1))
    def _():
        src = lax.broadcasted_iota(jnp.int32, (pw, pw), 0)
        dst = lax.broadcasted_iota(jnp.int32, (pw, pw), 1)
        want = jnp.where(dst < V7X_LANES, 2 * dst, 2 * (dst - V7X_LANES) + 1)
        perm = (src == want).astype(BF16)
        for c in range(w1_ref.shape[2] // pw):
            wp = jnp.dot(w1_ref[0, :, c * pw:(c + 1) * pw].astype(BF16), perm,
                         preferred_element_type=F32).astype(BF16)
            w1g_s[:, c * V7X_LANES:(c + 1) * V7X_LANES] = wp[:, :V7X_LANES]
            w1l_s[:, c * V7X_LANES:(c + 1) * V7X_LANES] = wp[:, V7X_LANES:]
        w2_s[...] = w2_ref[0].astype(BF16)

    @pl.when(i < nu_ref[0])
    def _():
        rows = lax.broadcasted_iota(jnp.int32, (bm, D), 0)
        x = jnp.where(rows < bv_ref[i], x_ref[...], 0.0).astype(BF16)
        glu = jnp.dot(x, w1g_s[...], preferred_element_type=F32) + b1g_ref[0]
        lin = jnp.dot(x, w1l_s[...], preferred_element_type=F32) + b1l_ref[0]
        glu = jnp.minimum(glu, SWIGLU_LIMIT)
        lin = jnp.clip(lin, -SWIGLU_LIMIT, SWIGLU_LIMIT)
        act = glu * jax.nn.sigmoid(SWIGLU_ALPHA * glu) * (lin + 1.0)
        y_ref[...] = jnp.dot(act.astype(BF16), w2_s[...], preferred_element_type=F32) + b2_ref[0]

    @pl.when(i >= nu_ref[0])
    def _():
        y_ref[...] = jnp.zeros_like(y_ref)


def _ffn(xs, blk_e, blk_valid, blk_first, n_used, w1, b1g, b1l, w2, b2, bm):
    P, D = xs.shape
    E, _, F2 = w1.shape
    F = F2 // 2
    nblk = P // bm

    def row_blk(i, be, bv, bf, nu):
        return (jnp.minimum(i, nu[0] - 1), 0)

    def per_e(i, be, bv, bf, nu):
        return (be[i], 0, 0)

    vmem = 2 * (D * F2 + F * D) * 4 + (D * F2 + F * D) * 2 + 8 * bm * F2 * 4
    return pl.pallas_call(
        _ffn_kernel,
        grid_spec=pltpu.PrefetchScalarGridSpec(
            num_scalar_prefetch=4,
            grid=(nblk,),
            in_specs=[
                pl.BlockSpec((bm, D), row_blk),
                pl.BlockSpec((1, D, F2), per_e),
                pl.BlockSpec((1, 1, F), per_e),
                pl.BlockSpec((1, 1, F), per_e),
                pl.BlockSpec((1, F, D), per_e),
                pl.BlockSpec((1, 1, D), per_e),
            ],
            out_specs=pl.BlockSpec((bm, D), lambda i, be, bv, bf, nu: (i, 0)),
            scratch_shapes=[
                pltpu.VMEM((D, F), BF16),
                pltpu.VMEM((D, F), BF16),
                pltpu.VMEM((F, D), BF16),
            ],
        ),
        out_shape=jax.ShapeDtypeStruct((P, D), F32),
        compiler_params=pltpu.CompilerParams(
            dimension_semantics=("arbitrary",),
            vmem_limit_bytes=_vmem_limit(vmem)),
        name="ffn",
    )(blk_e, blk_valid, blk_first, n_used, xs, w1, b1g, b1l, w2, b2)


def _combine_kernel(pos_ref, ys_ref, x1_ref, gate_ref, gpost_ref, g2_ref, o_ref, buf, sem):
    T = x1_ref.shape[0]

    def row_copy(r, k):
        src = pos_ref[r * TOP_K + k]
        return pltpu.make_async_copy(ys_ref.at[pl.ds(src, 1)], buf.at[k, pl.ds(r, 1)], sem)

    def start(r, c):
        for k in range(TOP_K):
            row_copy(r, k).start()
        return c

    def wait(r, c):
        for k in range(TOP_K):
            row_copy(r, k).wait()
        return c

    lax.fori_loop(0, T, start, 0)
    lax.fori_loop(0, T, wait, 0)
    gates = gate_ref[...]
    ff = gates[:, 0:1] * buf[0]
    for k in range(1, TOP_K):
        ff = ff + gates[:, k:k + 1] * buf[k]
    o_ref[...] = x1_ref[...] + g2_ref[0] * _rms(ff, gpost_ref[...])


def _combine(ys, pos_flat, x1, gates_tk, gpost, g2, S):
    N, D = x1.shape
    T = _tiles()["comb_t"]
    per_b = S // T
    return pl.pallas_call(
        _combine_kernel,
        grid=(N // T,),
        in_specs=[
            pl.BlockSpec((T * TOP_K,), lambda i: (i,), memory_space=pltpu.SMEM),
            pl.BlockSpec(memory_space=pl.ANY),
            pl.BlockSpec((T, D), lambda i: (i, 0)),
            pl.BlockSpec((T, TOP_K), lambda i: (i, 0)),
            pl.BlockSpec((1, D), lambda i: (0, 0)),
            pl.BlockSpec((1, 1, D), lambda i: (i // per_b, 0, 0)),
        ],
        out_specs=pl.BlockSpec((T, D), lambda i: (i, 0)),
        out_shape=jax.ShapeDtypeStruct((N, D), F32),
        scratch_shapes=[pltpu.VMEM((TOP_K, T, D), F32), pltpu.SemaphoreType.DMA(())],
        compiler_params=pltpu.CompilerParams(
            dimension_semantics=("arbitrary",),
            vmem_limit_bytes=_vmem_limit(16 * T * D * 4)),
        name="combine",
    )(pos_flat, ys, x1, gates_tk, gpost, g2)


def _rest_order(w):
    xr, gr, q, k, v, mg_r, mg_a = jnp.split(
        w, np.cumsum([D_RNN, D_RNN, Q_DIM, KV_DIM, KV_DIM, D_MODEL]).tolist(), axis=-1)
    return jnp.concatenate([xr, gr, q, mg_r, mg_a, k, v], axis=-1)


def _layer(x2, c, B, S, p):
    D = D_MODEL
    N = B * S
    ada = _ada(c, p["w_ada"], p["b_ada"])
    sh1, sc1, g1, sh2, sc2, g2 = [a.reshape(B, 1, D) for a in jnp.split(ada, 6, axis=-1)]
    row = lambda v: v.reshape(1, -1)

    proj_rnn, proj_rest = _inproj(x2, row(p["norm_pre_mix"]), sc1, sh1,
                                  _rest_order(p["w_in"]).astype(BF16),
                                  _rest_order(row(p["b_in"])), S)

    gw = _tiles()["rnn_group"]
    y_rnn = _rnn(proj_rnn, p["conv_w"], row(p["conv_b"]),
                 _block_diag_tiles(p["rg_w_a"], gw).astype(BF16), row(p["rg_b_a"]),
                 _block_diag_tiles(p["rg_w_x"], gw).astype(BF16), row(p["rg_b_x"]),
                 row(p["rg_lambda"]), S)
    y_att = _attn(proj_rest, p["attn_sinks"], S)

    x1, h2, logits_t = _merge(
        x2, y_rnn, y_att, proj_rest,
        p["w_o_rnn"].astype(BF16), p["w_o_attn"].astype(BF16), p["w_out"].astype(BF16),
        row(p["norm_post_mix"]), g1, row(p["norm_pre_ffn"]), sc2, sh2,
        p["router_w"].T.astype(BF16), p["router_b"].reshape(N_EXPERTS, 1), S)

    top_e, gates, rank, counts = _route(logits_t)

    bm = _tiles()["moe_bm"]
    n_rows = N * TOP_K + N_EXPERTS * bm
    nblk = n_rows // bm
    counts = counts.reshape(N_EXPERTS)
    padded = ((counts + bm - 1) // bm) * bm
    pend = jnp.cumsum(padded)
    pstart = pend - padded
    pos_flat = (pstart[top_e] + rank).T.reshape(N * TOP_K).astype(jnp.int32)
    tail_start = jnp.maximum(pend - bm, 0).astype(jnp.int32)
    n_used = (pend[-1] // bm).astype(jnp.int32).reshape(1)
    blk_row0 = jnp.arange(nblk, dtype=jnp.int32) * bm
    blk_e = jnp.clip(jnp.searchsorted(pend, blk_row0, side="right"), 0, N_EXPERTS - 1).astype(jnp.int32)
    last = jnp.maximum(n_used[0] - 1, 0)
    blk_e = jnp.where(jnp.arange(nblk) < n_used[0], blk_e, blk_e[last])
    blk_valid = jnp.clip(counts[blk_e] - (blk_row0 - pstart[blk_e]), 0, bm).astype(jnp.int32)
    blk_first = (blk_row0 == pstart[blk_e]).astype(jnp.int32)

    xs = _dispatch(h2, pos_flat, tail_start, n_used, n_rows, bm)
    b1 = p["moe_b1"].reshape(N_EXPERTS, D_FF, 2)
    ys = _ffn(xs, blk_e, blk_valid, blk_first, n_used, p["moe_w1"],
              b1[:, :, 0].reshape(N_EXPERTS, 1, D_FF), b1[:, :, 1].reshape(N_EXPERTS, 1, D_FF),
              p["moe_w2"], p["moe_b2"].reshape(N_EXPERTS, 1, D), bm)
    return _combine(ys, pos_flat, x1, gates.T, row(p["norm_post_ffn"]), g2, S)


def kernel(x, c, w_ada, b_ada, norm_pre_mix, norm_post_mix, norm_pre_ffn, norm_post_ffn, w_in, b_in, conv_w, conv_b, rg_w_a, rg_b_a, rg_w_x, rg_b_x, rg_lambda, attn_sinks, w_o_rnn, w_o_attn, w_out, router_w, router_b, moe_w1, moe_b1, moe_w2, moe_b2):
    B, S, D = x.shape
    params = dict(
        w_ada=w_ada, b_ada=b_ada, norm_pre_mix=norm_pre_mix, norm_post_mix=norm_post_mix,
        norm_pre_ffn=norm_pre_ffn, norm_post_ffn=norm_post_ffn, w_in=w_in, b_in=b_in,
        conv_w=conv_w, conv_b=conv_b, rg_w_a=rg_w_a, rg_b_a=rg_b_a, rg_w_x=rg_w_x, rg_b_x=rg_b_x,
        rg_lambda=rg_lambda, attn_sinks=attn_sinks, w_o_rnn=w_o_rnn, w_o_attn=w_o_attn,
        w_out=w_out, router_w=router_w, router_b=router_b, moe_w1=moe_w1, moe_b1=moe_b1,
        moe_w2=moe_w2, moe_b2=moe_b2)
    x2 = x.reshape(B * S, D)
    for layer in range(w_ada.shape[0]):
        x2 = _layer(x2, c, B, S, {k: v[layer] for k, v in params.items()})
    return x2.reshape(B, S, D)
```

```python
import functools
import math

import jax
import jax.numpy as jnp
import numpy as np
from jax import lax
from jax.experimental import pallas as pl
from jax.experimental.pallas import tpu as pltpu

D_MODEL = 1024
D_RNN = 1024
RNN_BLOCKS = 16
RNN_BW = D_RNN // RNN_BLOCKS
CONV_W = 4
LRU_C = 8.0
N_HEADS = 16
N_KV = 4
HEAD_DIM = 64
GROUP = N_HEADS // N_KV
WINDOW = 128
Q_DIM = N_HEADS * HEAD_DIM
KV_DIM = N_KV * HEAD_DIM
N_EXPERTS = 32
TOP_K = 4
D_FF = 1024
SWIGLU_LIMIT = 7.0
SWIGLU_ALPHA = 1.702
EPS = 1e-6
D_IN = 2 * D_RNN + Q_DIM + 2 * KV_DIM + 2 * D_MODEL
D_REST = D_IN - 2 * D_RNN

V7X_LANES = 128
V7X_SUBLANES = 8
V7X_MXU_DIM = 256
V7X_VMEM_BYTES = 64 * 1024 * 1024

F32 = jnp.float32
BF16 = jnp.bfloat16


def _tiles():
    return dict(
        ada_tn=1024,
        tok=512,
        attn_q=WINDOW,
        route_t=512,
        disp_t=256,
        moe_bm=256,
        comb_t=256,
        rnn_group=V7X_MXU_DIM,
    )


def _vmem_limit(nbytes):
    return int(min(max(nbytes, 16 * 1024 * 1024), V7X_VMEM_BYTES - 8 * 1024 * 1024))


def _rms(x, g):
    return x * lax.rsqrt(jnp.mean(x * x, axis=-1, keepdims=True) + EPS) * g


def _ada_kernel(ct_ref, w_ref, b_ref, o_ref):
    ct = ct_ref[...]
    sc = ct * jax.nn.sigmoid(ct)
    w = w_ref[...]
    rows = [jnp.sum(w * sc[:, b:b + 1], axis=0, keepdims=True) for b in range(ct.shape[1])]
    o_ref[...] = jnp.concatenate(rows, axis=0) + b_ref[...]


def _ada(c, w_ada, b_ada):
    B, D = c.shape
    n_out = w_ada.shape[1]
    tn = _tiles()["ada_tn"]
    return pl.pallas_call(
        _ada_kernel,
        grid=(n_out // tn,),
        in_specs=[
            pl.BlockSpec((D, B), lambda j: (0, 0)),
            pl.BlockSpec((D, tn), lambda j: (0, j)),
            pl.BlockSpec((1, tn), lambda j: (0, j)),
        ],
        out_specs=pl.BlockSpec((B, tn), lambda j: (0, j)),
        out_shape=jax.ShapeDtypeStruct((B, n_out), F32),
        compiler_params=pltpu.CompilerParams(
            dimension_semantics=("arbitrary",),
            vmem_limit_bytes=_vmem_limit(4 * D * tn * 4)),
        name="ada",
    )(c.T, w_ada, b_ada.reshape(1, n_out))


def _inproj_kernel(x_ref, g_ref, sc_ref, sh_ref, w_ref, b_ref, rnn_ref, rest_ref, *, chunk):
    x = x_ref[...]
    h = _rms(x, g_ref[...]) * (1.0 + sc_ref[0]) + sh_ref[0]
    hb = h.astype(BF16)
    n_rnn = rnn_ref.shape[1]
    for c0 in range(0, w_ref.shape[1], chunk):
        acc = jnp.dot(hb, w_ref[:, c0:c0 + chunk], preferred_element_type=F32)
        acc = acc + b_ref[:, c0:c0 + chunk]
        if c0 < n_rnn:
            rnn_ref[:, c0:c0 + chunk] = acc
        else:
            rest_ref[:, c0 - n_rnn:c0 - n_rnn + chunk] = acc.astype(BF16)


def _inproj(x2, g, sc, sh, w_bf, b_in, S):
    N, D = x2.shape
    tm = _tiles()["tok"]
    per_b = S // tm
    chunk = 2 * V7X_MXU_DIM
    vmem = D * D_IN * 2 + 2 * tm * (D * 4 + 2 * D_RNN * 4 + D_REST * 2) + 4 * tm * chunk * 4
    return pl.pallas_call(
        functools.partial(_inproj_kernel, chunk=chunk),
        grid=(N // tm,),
        in_specs=[
            pl.BlockSpec((tm, D), lambda i: (i, 0)),
            pl.BlockSpec((1, D), lambda i: (0, 0)),
            pl.BlockSpec((1, 1, D), lambda i: (i // per_b, 0, 0)),
            pl.BlockSpec((1, 1, D), lambda i: (i // per_b, 0, 0)),
            pl.BlockSpec((D, D_IN), lambda i: (0, 0), pipeline_mode=pl.Buffered(1)),
            pl.BlockSpec((1, D_IN), lambda i: (0, 0)),
        ],
        out_specs=[
            pl.BlockSpec((tm, 2 * D_RNN), lambda i: (i, 0)),
            pl.BlockSpec((tm, D_REST), lambda i: (i, 0)),
        ],
        out_shape=[
            jax.ShapeDtypeStruct((N, 2 * D_RNN), F32),
            jax.ShapeDtypeStruct((N, D_REST), BF16),
        ],
        compiler_params=pltpu.CompilerParams(
            dimension_semantics=("arbitrary",),
            vmem_limit_bytes=_vmem_limit(vmem + 8 * 1024 * 1024)),
        name="inproj",
    )(x2, g, sc, sh, w_bf, b_in)


def _gelu_tanh(x):
    return 0.5 * x * (1.0 + jnp.tanh(math.sqrt(2.0 / math.pi) * (x + 0.044715 * (x * x * x))))


def _softplus(z):
    return jnp.maximum(z, 0.0) + jnp.log1p(jnp.exp(-jnp.abs(z)))


def _rnn_kernel(xr_ref, gr_ref, cw_ref, cb_ref, wa_ref, ba_ref, wx_ref, bx_ref, lam_ref,
                y_ref, xbuf, a_s, b_s, h_s, carry, *, per_b, gw):
    t = pl.program_id(0) % per_b
    tt = xr_ref.shape[0]
    halo = V7X_SUBLANES

    @pl.when(t == 0)
    def _():
        xbuf[0:halo, :] = jnp.zeros((halo, xbuf.shape[1]), F32)
        carry[...] = jnp.zeros_like(carry)

    xbuf[halo:halo + tt, :] = xr_ref[...]
    xc = cb_ref[...] + xbuf[pl.ds(halo - (CONV_W - 1), tt), :] * cw_ref[0:1, :]
    for kk in range(1, CONV_W):
        xc = xc + xbuf[pl.ds(halo - (CONV_W - 1) + kk, tt), :] * cw_ref[kk:kk + 1, :]
    xbuf[0:halo, :] = xbuf[tt:tt + halo, :]

    xcb = xc.astype(BF16)
    sp = _softplus(-lam_ref[...])
    first = t == 0
    row = lax.broadcasted_iota(jnp.int32, (tt, gw), 0)
    for g in range(xc.shape[1] // gw):
        cs = slice(g * gw, (g + 1) * gw)
        xg = xcb[:, cs]
        gate_r = jax.nn.sigmoid(jnp.dot(xg, wa_ref[g], preferred_element_type=F32) + ba_ref[:, cs])
        gate_i = jax.nn.sigmoid(jnp.dot(xg, wx_ref[g], preferred_element_type=F32) + bx_ref[:, cs])
        log_a = -LRU_C * gate_r * sp[:, cs]
        th = jnp.tanh(log_a)
        mult = jnp.sqrt(-2.0 * th / (1.0 - th))
        mult = jnp.where(jnp.logical_and(first, row == 0), 1.0, mult)
        a_s[:, cs] = jnp.exp(log_a)
        b_s[:, cs] = (xc[:, cs] * gate_i) * mult

    srow = lax.broadcasted_iota(jnp.int32, (V7X_SUBLANES, xc.shape[1]), 0)

    def body(i, h_prev):
        r = pl.multiple_of(i * V7X_SUBLANES, V7X_SUBLANES)
        a = a_s[pl.ds(r, V7X_SUBLANES), :]
        b = b_s[pl.ds(r, V7X_SUBLANES), :]
        for s in (1, 2, 4):
            keep = srow >= s
            b = jnp.where(keep, a * pltpu.roll(b, s, 0) + b, b)
            a = jnp.where(keep, a * pltpu.roll(a, s, 0), a)
        h = a * h_prev + b
        h_s[pl.ds(r, V7X_SUBLANES), :] = h
        return h[V7X_SUBLANES - 1:V7X_SUBLANES, :]

    carry[...] = lax.fori_loop(0, tt // V7X_SUBLANES, body, carry[...])
    y_ref[...] = (h_s[...] * _gelu_tanh(gr_ref[...])).astype(BF16)


def _block_diag_tiles(w, gw):
    nb, bw, _ = w.shape
    per = gw // bw
    w4 = w.reshape(nb // per, per, bw, bw)
    eye = jnp.eye(per, dtype=w.dtype)
    return jnp.einsum("gpij,pq->gpiqj", w4, eye).reshape(nb // per, gw, gw)


def _rnn(proj_rnn, conv_w, conv_b, wa, ba, wx, bx, lam, S):
    N = proj_rnn.shape[0]
    C = D_RNN
    tt = _tiles()["tok"]
    gw = _tiles()["rnn_group"]
    per_b = S // tt
    vec = lambda: pl.BlockSpec((1, C), lambda i: (0, 0))
    return pl.pallas_call(
        functools.partial(_rnn_kernel, per_b=per_b, gw=gw),
        grid=(N // tt,),
        in_specs=[
            pl.BlockSpec((tt, C), lambda i: (i, 0)),
            pl.BlockSpec((tt, C), lambda i: (i, 1)),
            pl.BlockSpec((CONV_W, C), lambda i: (0, 0)),
            vec(),
            pl.BlockSpec((C // gw, gw, gw), lambda i: (0, 0, 0)),
            vec(),
            pl.BlockSpec((C // gw, gw, gw), lambda i: (0, 0, 0)),
            vec(),
            vec(),
        ],
        out_specs=pl.BlockSpec((tt, C), lambda i: (i, 0)),
        out_shape=jax.ShapeDtypeStruct((N, C), BF16),
        scratch_shapes=[
            pltpu.VMEM((tt + V7X_SUBLANES, C), F32),
            pltpu.VMEM((tt, C), F32),
            pltpu.VMEM((tt, C), F32),
            pltpu.VMEM((tt, C), F32),
            pltpu.VMEM((1, C), F32),
        ],
        compiler_params=pltpu.CompilerParams(
            dimension_semantics=("arbitrary",),
            vmem_limit_bytes=_vmem_limit(16 * tt * C * 4)),
        name="rnn",
    )(proj_rnn, proj_rnn, conv_w, conv_b, wa, ba, wx, bx, lam)


def _alibi_slopes():
    return [2.0 ** (-8.0 * (h + 1) / N_HEADS) for h in range(N_HEADS)]


def _attn_kernel(sink_ref, q_ref, kp_ref, kc_ref, vp_ref, vc_ref, o_ref, *, per_b):
    blk = pl.program_id(0) % per_b
    bq = q_ref.shape[0]
    qi = lax.broadcasted_iota(jnp.int32, (bq, 2 * bq), 0)
    ci = lax.broadcasted_iota(jnp.int32, (bq, 2 * bq), 1)
    dist = qi + bq - ci
    valid = (dist >= 0) & (dist < WINDOW) & ((ci >= bq) | (blk > 0))
    distf = dist.astype(F32)
    slopes = _alibi_slopes()
    outs = []
    for kv in range(N_KV):
        ks = slice(kv * HEAD_DIM, (kv + 1) * HEAD_DIM)
        kb = jnp.concatenate([kp_ref[:, ks], kc_ref[:, ks]], axis=0)
        vb = jnp.concatenate([vp_ref[:, ks], vc_ref[:, ks]], axis=0)
        for g in range(GROUP):
            h = kv * GROUP + g
            qh = q_ref[:, h * HEAD_DIM:(h + 1) * HEAD_DIM]
            s = lax.dot_general(qh, kb, (((1,), (1,)), ((), ())), preferred_element_type=F32)
            s = s * (HEAD_DIM ** -0.5) - slopes[h] * distf
            s = jnp.where(valid, s, -jnp.inf)
            sink = sink_ref[h]
            m = jnp.maximum(jnp.max(s, axis=-1, keepdims=True), sink)
            p = jnp.exp(s - m)
            denom = jnp.sum(p, axis=-1, keepdims=True) + jnp.exp(sink - m)
            o = jnp.dot(p.astype(BF16), vb, preferred_element_type=F32) / denom
            outs.append(o)
    o_ref[...] = jnp.concatenate(outs, axis=-1).astype(BF16)


def _attn(proj_rest, sinks, S):
    N = proj_rest.shape[0]
    bq = _tiles()["attn_q"]
    per_b = S // bq
    k_col = (Q_DIM + 2 * D_MODEL) // KV_DIM
    v_col = k_col + 1

    def prev(i):
        return jnp.where(i % per_b == 0, i, i - 1)

    return pl.pallas_call(
        functools.partial(_attn_kernel, per_b=per_b),
        grid=(N // bq,),
        in_specs=[
            pl.BlockSpec(memory_space=pltpu.SMEM),
            pl.BlockSpec((bq, Q_DIM), lambda i: (i, 0)),
            pl.BlockSpec((bq, KV_DIM), lambda i: (prev(i), k_col)),
            pl.BlockSpec((bq, KV_DIM), lambda i: (i, k_col)),
            pl.BlockSpec((bq, KV_DIM), lambda i: (prev(i), v_col)),
            pl.BlockSpec((bq, KV_DIM), lambda i: (i, v_col)),
        ],
        out_specs=pl.BlockSpec((bq, Q_DIM), lambda i: (i, 0)),
        out_shape=jax.ShapeDtypeStruct((N, Q_DIM), BF16),
        compiler_params=pltpu.CompilerParams(dimension_semantics=("arbitrary",)),
        name="attn",
    )(sinks, proj_rest, proj_rest, proj_rest, proj_rest, proj_rest)


def _merge_kernel(x_ref, yr_ref, ya_ref, gr_ref, ga_ref, wr_ref, wa_ref, wo_ref,
                  gpost_ref, g1_ref, gpre_ref, sc2_ref, sh2_ref, rwt_ref, rb_ref,
                  x1_ref, h2_ref, lg_ref):
    r = jnp.dot(yr_ref[...], wr_ref[...], preferred_element_type=F32)
    a = jnp.dot(ya_ref[...], wa_ref[...], preferred_element_type=F32)
    merged = (jax.nn.sigmoid(gr_ref[...].astype(F32)) * r
              + jax.nn.sigmoid(ga_ref[...].astype(F32)) * a)
    mix = jnp.dot(merged.astype(BF16), wo_ref[...], preferred_element_type=F32)
    x1 = x_ref[...] + g1_ref[0] * _rms(mix, gpost_ref[...])
    x1_ref[...] = x1
    h2 = _rms(x1, gpre_ref[...]) * (1.0 + sc2_ref[0]) + sh2_ref[0]
    h2_ref[...] = h2
    lg = lax.dot_general(rwt_ref[...], h2.astype(BF16), (((1,), (1,)), ((), ())),
                         preferred_element_type=F32)
    lg_ref[...] = lg + rb_ref[...]


def _merge(x2, y_rnn, y_att, proj_rest, wr, wa, wo, gpost, g1, gpre, sc2, sh2, rwt, rb, S):
    N, D = x2.shape
    tm = _tiles()["tok"]
    per_b = S // tm
    gate_r_col = Q_DIM // D
    mat = lambda: pl.BlockSpec((D, D), lambda i: (0, 0))
    vec = lambda: pl.BlockSpec((1, D), lambda i: (0, 0))
    bvec = lambda: pl.BlockSpec((1, 1, D), lambda i: (i // per_b, 0, 0))
    tile = lambda col=0: pl.BlockSpec((tm, D), lambda i: (i, col))
    return pl.pallas_call(
        _merge_kernel,
        grid=(N // tm,),
        in_specs=[
            tile(), tile(), tile(), tile(gate_r_col), tile(gate_r_col + 1),
            mat(), mat(), mat(),
            vec(), bvec(), vec(), bvec(), bvec(),
            pl.BlockSpec((N_EXPERTS, D), lambda i: (0, 0)),
            pl.BlockSpec((N_EXPERTS, 1), lambda i: (0, 0)),
        ],
        out_specs=[
            tile(), tile(),
            pl.BlockSpec((N_EXPERTS, tm), lambda i: (0, i)),
        ],
        out_shape=[
            jax.ShapeDtypeStruct((N, D), F32),
            jax.ShapeDtypeStruct((N, D), F32),
            jax.ShapeDtypeStruct((N_EXPERTS, N), F32),
        ],
        compiler_params=pltpu.CompilerParams(
            dimension_semantics=("arbitrary",),
            vmem_limit_bytes=_vmem_limit(6 * D * D * 2 + 24 * tm * D * 4)),
        name="merge",
    )(x2, y_rnn, y_att, proj_rest, proj_rest, wr, wa, wo, gpost, g1, gpre, sc2, sh2, rwt, rb)


def _route_kernel(lg_ref, e_ref, g_ref, rk_ref, cnt_ref, carry):
    @pl.when(pl.program_id(0) == 0)
    def _():
        carry[...] = jnp.zeros_like(carry)

    l = lg_ref[...]
    E, T = l.shape
    row = lax.broadcasted_iota(jnp.int32, (E, T), 0).astype(F32)
    vals, idxs = [], []
    for _ in range(TOP_K):
        m = jnp.max(l, axis=0, keepdims=True)
        idx = jnp.min(jnp.where(l == m, row, float(E)), axis=0, keepdims=True)
        vals.append(m)
        idxs.append(idx)
        l = jnp.where(row == idx, -jnp.inf, l)
    ex = [jnp.exp(v - vals[0]) for v in vals]
    tot = ex[0]
    for e in ex[1:]:
        tot = tot + e
    g_ref[...] = jnp.concatenate([e / tot for e in ex], axis=0)
    e_ref[...] = jnp.concatenate(idxs, axis=0).astype(jnp.int32)

    hot = [row == idx for idx in idxs]
    onehot = jnp.zeros((E, T), F32)
    for hk in hot:
        onehot = onehot + hk.astype(F32)
    tri = (lax.broadcasted_iota(jnp.int32, (T, T), 0)
           < lax.broadcasted_iota(jnp.int32, (T, T), 1)).astype(BF16)
    before = jnp.dot(onehot.astype(BF16), tri, preferred_element_type=F32) + carry[...]
    ranks = [jnp.sum(jnp.where(hk, before, 0.0), axis=0, keepdims=True) for hk in hot]
    rk_ref[...] = jnp.concatenate(ranks, axis=0).astype(jnp.int32)
    total = carry[...] + jnp.sum(onehot, axis=1, keepdims=True)
    carry[...] = total
    cnt_ref[...] = total.astype(jnp.int32)


def _route(logits_t):
    E, N = logits_t.shape
    T = _tiles()["route_t"]
    out = lambda: pl.BlockSpec((TOP_K, T), lambda i: (0, i))
    return pl.pallas_call(
        _route_kernel,
        grid=(N // T,),
        in_specs=[pl.BlockSpec((E, T), lambda i: (0, i))],
        out_specs=[out(), out(), out(), pl.BlockSpec((E, 1), lambda i: (0, 0))],
        out_shape=[
            jax.ShapeDtypeStruct((TOP_K, N), jnp.int32),
            jax.ShapeDtypeStruct((TOP_K, N), F32),
            jax.ShapeDtypeStruct((TOP_K, N), jnp.int32),
            jax.ShapeDtypeStruct((E, 1), jnp.int32),
        ],
        scratch_shapes=[pltpu.VMEM((E, 1), F32)],
        compiler_params=pltpu.CompilerParams(dimension_semantics=("arbitrary",)),
        name="route",
    )(logits_t)


def _dispatch_kernel(tail_ref, nu_ref, pos_ref, h_ref, xs_ref, zbuf, sem):
    i = pl.program_id(0)
    T = h_ref.shape[0]
    bm = zbuf.shape[0]
    nblk = xs_ref.shape[0] // bm

    def zero_copy(row0):
        return pltpu.make_async_copy(zbuf, xs_ref.at[pl.ds(pl.multiple_of(row0, bm), bm)], sem)

    @pl.when(i == 0)
    def _():
        zbuf[...] = jnp.zeros_like(zbuf)

        def start(e, c):
            zero_copy(tail_ref[e]).start()
            return c

        def wait(e, c):
            zero_copy(tail_ref[e]).wait()
            return c

        def start_unused(j, c):
            zero_copy(j * bm).start()
            return c

        def wait_unused(j, c):
            zero_copy(j * bm).wait()
            return c

        lax.fori_loop(0, N_EXPERTS, start, 0)
        lax.fori_loop(nu_ref[0], nblk, start_unused, 0)
        lax.fori_loop(0, N_EXPERTS, wait, 0)
        lax.fori_loop(nu_ref[0], nblk, wait_unused, 0)

    def row_copy(r, k):
        dst = pos_ref[r * TOP_K + k]
        return pltpu.make_async_copy(h_ref.at[pl.ds(r, 1)], xs_ref.at[pl.ds(dst, 1)], sem)

    def start(r, c):
        for k in range(TOP_K):
            row_copy(r, k).start()
        return c

    def wait(r, c):
        for k in range(TOP_K):
            row_copy(r, k).wait()
        return c

    lax.fori_loop(0, T, start, 0)
    lax.fori_loop(0, T, wait, 0)


def _dispatch(h2, pos_flat, tail_start, n_used, n_rows, bm):
    N, D = h2.shape
    T = _tiles()["disp_t"]
    return pl.pallas_call(
        _dispatch_kernel,
        grid_spec=pltpu.PrefetchScalarGridSpec(
            num_scalar_prefetch=2,
            grid=(N // T,),
            in_specs=[
                pl.BlockSpec((T * TOP_K,), lambda i, tail, nu: (i,), memory_space=pltpu.SMEM),
                pl.BlockSpec((T, D), lambda i, tail, nu: (i, 0)),
            ],
            out_specs=pl.BlockSpec(memory_space=pl.ANY),
            scratch_shapes=[pltpu.VMEM((bm, D), F32), pltpu.SemaphoreType.DMA(())],
        ),
        out_shape=jax.ShapeDtypeStruct((n_rows, D), F32),
        compiler_params=pltpu.CompilerParams(dimension_semantics=("arbitrary",)),
        name="dispatch",
    )(tail_start, n_used, pos_flat, h2)


def _ffn_kernel(be_ref, bv_ref, bf_ref, nu_ref, x_ref, w1_ref, b1g_ref, b1l_ref, w2_ref, b2_ref,
                y_ref, w1g_s, w1l_s, w2_s):
    i = pl.program_id(0)
    bm, D = x_ref.shape
    pw = 2 * V7X_LANES

    @pl.when(jnp.logical_and(i < nu_ref[0], bf_ref[i] == 1))
    def _():
        src = lax.broadcasted_iota(jnp.int32, (pw, pw), 0)
        dst = lax.broadcasted_iota(jnp.int32, (pw, pw), 1)
        want = jnp.where(dst < V7X_LANES, 2 * dst, 2 * (dst - V7X_LANES) + 1)
        perm = (src == want).astype(BF16)
        for c in range(w1_ref.shape[2] // pw):
            wp = jnp.dot(w1_ref[0, :, c * pw:(c + 1) * pw].astype(BF16), perm,
                         preferred_element_type=F32).astype(BF16)
            w1g_s[:, c * V7X_LANES:(c + 1) * V7X_LANES] = wp[:, :V7X_LANES]
            w1l_s[:, c * V7X_LANES:(c + 1) * V7X_LANES] = wp[:, V7X_LANES:]
        w2_s[...] = w2_ref[0].astype(BF16)

    @pl.when(i < nu_ref[0])
    def _():
        rows = lax.broadcasted_iota(jnp.int32, (bm, D), 0)
        x = jnp.where(rows < bv_ref[i], x_ref[...], 0.0).astype(BF16)
        glu = jnp.dot(x, w1g_s[...], preferred_element_type=F32) + b1g_ref[0]
        lin = jnp.dot(x, w1l_s[...], preferred_element_type=F32) + b1l_ref[0]
        glu = jnp.minimum(glu, SWIGLU_LIMIT)
        lin = jnp.clip(lin, -SWIGLU_LIMIT, SWIGLU_LIMIT)
        act = glu * jax.nn.sigmoid(SWIGLU_ALPHA * glu) * (lin + 1.0)
        y_ref[...] = jnp.dot(act.astype(BF16), w2_s[...], preferred_element_type=F32) + b2_ref[0]

    @pl.when(i >= nu_ref[0])
    def _():
        y_ref[...] = jnp.zeros_like(y_ref)


def _ffn(xs, blk_e, blk_valid, blk_first, n_used, w1, b1g, b1l, w2, b2, bm):
    P, D = xs.shape
    E, _, F2 = w1.shape
    F = F2 // 2
    nblk = P // bm

    def row_blk(i, be, bv, bf, nu):
        return (jnp.minimum(i, nu[0] - 1), 0)

    def per_e(i, be, bv, bf, nu):
        return (be[i], 0, 0)

    vmem = 2 * (D * F2 + F * D) * 4 + (D * F2 + F * D) * 2 + 8 * bm * F2 * 4
    return pl.pallas_call(
        _ffn_kernel,
        grid_spec=pltpu.PrefetchScalarGridSpec(
            num_scalar_prefetch=4,
            grid=(nblk,),
            in_specs=[
                pl.BlockSpec((bm, D), row_blk),
                pl.BlockSpec((1, D, F2), per_e),
                pl.BlockSpec((1, 1, F), per_e),
                pl.BlockSpec((1, 1, F), per_e),
                pl.BlockSpec((1, F, D), per_e),
                pl.BlockSpec((1, 1, D), per_e),
            ],
            out_specs=pl.BlockSpec((bm, D), lambda i, be, bv, bf, nu: (i, 0)),
            scratch_shapes=[
                pltpu.VMEM((D, F), BF16),
                pltpu.VMEM((D, F), BF16),
                pltpu.VMEM((F, D), BF16),
            ],
        ),
        out_shape=jax.ShapeDtypeStruct((P, D), F32),
        compiler_params=pltpu.CompilerParams(
            dimension_semantics=("arbitrary",),
            vmem_limit_bytes=_vmem_limit(vmem)),
        name="ffn",
    )(blk_e, blk_valid, blk_first, n_used, xs, w1, b1g, b1l, w2, b2)


def _combine_kernel(pos_ref, ys_ref, x1_ref, gate_ref, gpost_ref, g2_ref, o_ref, buf, sem):
    T = x1_ref.shape[0]

    def row_copy(r, k):
        src = pos_ref[r * TOP_K + k]
        return pltpu.make_async_copy(ys_ref.at[pl.ds(src, 1)], buf.at[k, pl.ds(r, 1)], sem)

    def start(r, c):
        for k in range(TOP_K):
            row_copy(r, k).start()
        return c

    def wait(r, c):
        for k in range(TOP_K):
            row_copy(r, k).wait()
        return c

    lax.fori_loop(0, T, start, 0)
    lax.fori_loop(0, T, wait, 0)
    gates = gate_ref[...]
    ff = gates[:, 0:1] * buf[0]
    for k in range(1, TOP_K):
        ff = ff + gates[:, k:k + 1] * buf[k]
    o_ref[...] = x1_ref[...] + g2_ref[0] * _rms(ff, gpost_ref[...])


def _combine(ys, pos_flat, x1, gates_tk, gpost, g2, S):
    N, D = x1.shape
    T = _tiles()["comb_t"]
    per_b = S // T
    return pl.pallas_call(
        _combine_kernel,
        grid=(N // T,),
        in_specs=[
            pl.BlockSpec((T * TOP_K,), lambda i: (i,), memory_space=pltpu.SMEM),
            pl.BlockSpec(memory_space=pl.ANY),
            pl.BlockSpec((T, D), lambda i: (i, 0)),
            pl.BlockSpec((T, TOP_K), lambda i: (i, 0)),
            pl.BlockSpec((1, D), lambda i: (0, 0)),
            pl.BlockSpec((1, 1, D), lambda i: (i // per_b, 0, 0)),
        ],
        out_specs=pl.BlockSpec((T, D), lambda i: (i, 0)),
        out_shape=jax.ShapeDtypeStruct((N, D), F32),
        scratch_shapes=[pltpu.VMEM((TOP_K, T, D), F32), pltpu.SemaphoreType.DMA(())],
        compiler_params=pltpu.CompilerParams(
            dimension_semantics=("arbitrary",),
            vmem_limit_bytes=_vmem_limit(16 * T * D * 4)),
        name="combine",
    )(pos_flat, ys, x1, gates_tk, gpost, g2)


def _rest_order(w):
    xr, gr, q, k, v, mg_r, mg_a = jnp.split(
        w, np.cumsum([D_RNN, D_RNN, Q_DIM, KV_DIM, KV_DIM, D_MODEL]).tolist(), axis=-1)
    return jnp.concatenate([xr, gr, q, mg_r, mg_a, k, v], axis=-1)


def _layer(x2, c, B, S, p):
    D = D_MODEL
    N = B * S
    ada = _ada(c, p["w_ada"], p["b_ada"])
    sh1, sc1, g1, sh2, sc2, g2 = [a.reshape(B, 1, D) for a in jnp.split(ada, 6, axis=-1)]
    row = lambda v: v.reshape(1, -1)

    proj_rnn, proj_rest = _inproj(x2, row(p["norm_pre_mix"]), sc1, sh1,
                                  _rest_order(p["w_in"]).astype(BF16),
                                  _rest_order(row(p["b_in"])), S)

    gw = _tiles()["rnn_group"]
    y_rnn = _rnn(proj_rnn, p["conv_w"], row(p["conv_b"]),
                 _block_diag_tiles(p["rg_w_a"], gw).astype(BF16), row(p["rg_b_a"]),
                 _block_diag_tiles(p["rg_w_x"], gw).astype(BF16), row(p["rg_b_x"]),
                 row(p["rg_lambda"]), S)
    y_att = _attn(proj_rest, p["attn_sinks"], S)

    x1, h2, logits_t = _merge(
        x2, y_rnn, y_att, proj_rest,
        p["w_o_rnn"].astype(BF16), p["w_o_attn"].astype(BF16), p["w_out"].astype(BF16),
        row(p["norm_post_mix"]), g1, row(p["norm_pre_ffn"]), sc2, sh2,
        p["router_w"].T.astype(BF16), p["router_b"].reshape(N_EXPERTS, 1), S)

    top_e, gates, rank, counts = _route(logits_t)

    bm = _tiles()["moe_bm"]
    n_rows = N * TOP_K + N_EXPERTS * bm
    nblk = n_rows // bm
    counts = counts.reshape(N_EXPERTS)
    padded = ((counts + bm - 1) // bm) * bm
    pend = jnp.cumsum(padded)
    pstart = pend - padded
    eids = jnp.arange(N_EXPERTS, dtype=jnp.int32)
    pos = rank + jnp.sum(jnp.where(top_e[..., None] == eids, pstart, 0), axis=-1)
    pos_flat = pos.T.reshape(N * TOP_K).astype(jnp.int32)
    tail_start = jnp.maximum(pend - bm, 0).astype(jnp.int32)
    n_used = (pend[-1] // bm).astype(jnp.int32).reshape(1)
    blk_row0 = jnp.minimum(jnp.arange(nblk, dtype=jnp.int32), n_used[0] - 1) * bm
    blk_e = jnp.sum(blk_row0[:, None] >= pend[None, :], axis=1).astype(jnp.int32)
    mine = blk_e[:, None] == eids[None, :]
    blk_cnt = jnp.sum(jnp.where(mine, counts, 0), axis=1)
    blk_pstart = jnp.sum(jnp.where(mine, pstart, 0), axis=1)
    blk_valid = jnp.clip(blk_cnt - (blk_row0 - blk_pstart), 0, bm).astype(jnp.int32)
    blk_first = (blk_row0 == blk_pstart).astype(jnp.int32)

    xs = _dispatch(h2, pos_flat, tail_start, n_used, n_rows, bm)
    b1 = p["moe_b1"].reshape(N_EXPERTS, D_FF, 2)
    ys = _ffn(xs, blk_e, blk_valid, blk_first, n_used, p["moe_w1"],
              b1[:, :, 0].reshape(N_EXPERTS, 1, D_FF), b1[:, :, 1].reshape(N_EXPERTS, 1, D_FF),
              p["moe_w2"], p["moe_b2"].reshape(N_EXPERTS, 1, D), bm)
    return _combine(ys, pos_flat, x1, gates.T, row(p["norm_post_ffn"]), g2, S)


def kernel(x, c, w_ada, b_ada, norm_pre_mix, norm_post_mix, norm_pre_ffn, norm_post_ffn, w_in, b_in, conv_w, conv_b, rg_w_a, rg_b_a, rg_w_x, rg_b_x, rg_lambda, attn_sinks, w_o_rnn, w_o_attn, w_out, router_w, router_b, moe_w1, moe_b1, moe_w2, moe_b2):
    B, S, D = x.shape
    params = dict(
        w_ada=w_ada, b_ada=b_ada, norm_pre_mix=norm_pre_mix, norm_post_mix=norm_post_mix,
        norm_pre_ffn=norm_pre_ffn, norm_post_ffn=norm_post_ffn, w_in=w_in, b_in=b_in,
        conv_w=conv_w, conv_b=conv_b, rg_w_a=rg_w_a, rg_b_a=rg_b_a, rg_w_x=rg_w_x, rg_b_x=rg_b_x,
        rg_lambda=rg_lambda, attn_sinks=attn_sinks, w_o_rnn=w_o_rnn, w_o_attn=w_o_attn,
        w_out=w_out, router_w=router_w, router_b=router_b, moe_w1=moe_w1, moe_b1=moe_b1,
        moe_w2=moe_w2, moe_b2=moe_b2)
    x2 = x.reshape(B * S, D)
    for layer in range(w_ada.shape[0]):
        x2 = _layer(x2, c, B, S, {k: v[layer] for k, v in params.items()})
    return x2.reshape(B, S, D)
```

```python
import functools
import math

import jax
import jax.numpy as jnp
import numpy as np
from jax import lax
from jax.experimental import pallas as pl
from jax.experimental.pallas import tpu as pltpu

D_MODEL = 1024
D_RNN = 1024
RNN_BLOCKS = 16
RNN_BW = D_RNN // RNN_BLOCKS
CONV_W = 4
LRU_C = 8.0
N_HEADS = 16
N_KV = 4
HEAD_DIM = 64
GROUP = N_HEADS // N_KV
WINDOW = 128
Q_DIM = N_HEADS * HEAD_DIM
KV_DIM = N_KV * HEAD_DIM
N_EXPERTS = 32
TOP_K = 4
D_FF = 1024
SWIGLU_LIMIT = 7.0
SWIGLU_ALPHA = 1.702
EPS = 1e-6
D_IN = 2 * D_RNN + Q_DIM + 2 * KV_DIM + 2 * D_MODEL
D_REST = D_IN - 2 * D_RNN

V7X_LANES = 128
V7X_SUBLANES = 8
V7X_MXU_DIM = 256
V7X_VMEM_BYTES = 64 * 1024 * 1024

F32 = jnp.float32
BF16 = jnp.bfloat16


def _tiles():
    return dict(
        ada_tn=1024,
        tok=512,
        attn_q=WINDOW,
        moe_t=256,
        moe_bm=256,
        rnn_group=V7X_MXU_DIM,
    )


def _vmem_limit(nbytes):
    return int(min(max(nbytes, 16 * 1024 * 1024), V7X_VMEM_BYTES - 8 * 1024 * 1024))


def _rms(x, g):
    return x * lax.rsqrt(jnp.mean(x * x, axis=-1, keepdims=True) + EPS) * g


def _ada_kernel(ct_ref, w_ref, b_ref, o_ref):
    ct = ct_ref[...]
    sc = ct * jax.nn.sigmoid(ct)
    w = w_ref[...]
    rows = [jnp.sum(w * sc[:, b:b + 1], axis=0, keepdims=True) for b in range(ct.shape[1])]
    o_ref[...] = jnp.concatenate(rows, axis=0) + b_ref[...]


def _ada(c, w_ada, b_ada):
    B, D = c.shape
    n_out = w_ada.shape[1]
    tn = _tiles()["ada_tn"]
    return pl.pallas_call(
        _ada_kernel,
        grid=(n_out // tn,),
        in_specs=[
            pl.BlockSpec((D, B), lambda j: (0, 0)),
            pl.BlockSpec((D, tn), lambda j: (0, j)),
            pl.BlockSpec((1, tn), lambda j: (0, j)),
        ],
        out_specs=pl.BlockSpec((B, tn), lambda j: (0, j)),
        out_shape=jax.ShapeDtypeStruct((B, n_out), F32),
        compiler_params=pltpu.CompilerParams(
            dimension_semantics=("arbitrary",),
            vmem_limit_bytes=_vmem_limit(4 * D * tn * 4)),
        name="ada",
    )(c.T, w_ada, b_ada.reshape(1, n_out))


def _inproj_kernel(x_ref, g_ref, sc_ref, sh_ref, w_ref, b_ref, rnn_ref, rest_ref, *, chunk):
    x = x_ref[...]
    h = _rms(x, g_ref[...]) * (1.0 + sc_ref[0]) + sh_ref[0]
    hb = h.astype(BF16)
    n_rnn = rnn_ref.shape[1]
    for c0 in range(0, w_ref.shape[1], chunk):
        acc = jnp.dot(hb, w_ref[:, c0:c0 + chunk], preferred_element_type=F32)
        acc = acc + b_ref[:, c0:c0 + chunk]
        if c0 < n_rnn:
            rnn_ref[:, c0:c0 + chunk] = acc
        else:
            rest_ref[:, c0 - n_rnn:c0 - n_rnn + chunk] = acc.astype(BF16)


def _inproj(x2, g, sc, sh, w_bf, b_in, S):
    N, D = x2.shape
    tm = _tiles()["tok"]
    per_b = S // tm
    chunk = 2 * V7X_MXU_DIM
    vmem = D * D_IN * 2 + 2 * tm * (D * 4 + 2 * D_RNN * 4 + D_REST * 2) + 4 * tm * chunk * 4
    return pl.pallas_call(
        functools.partial(_inproj_kernel, chunk=chunk),
        grid=(N // tm,),
        in_specs=[
            pl.BlockSpec((tm, D), lambda i: (i, 0)),
            pl.BlockSpec((1, D), lambda i: (0, 0)),
            pl.BlockSpec((1, 1, D), lambda i: (i // per_b, 0, 0)),
            pl.BlockSpec((1, 1, D), lambda i: (i // per_b, 0, 0)),
            pl.BlockSpec((D, D_IN), lambda i: (0, 0), pipeline_mode=pl.Buffered(1)),
            pl.BlockSpec((1, D_IN), lambda i: (0, 0)),
        ],
        out_specs=[
            pl.BlockSpec((tm, 2 * D_RNN), lambda i: (i, 0)),
            pl.BlockSpec((tm, D_REST), lambda i: (i, 0)),
        ],
        out_shape=[
            jax.ShapeDtypeStruct((N, 2 * D_RNN), F32),
            jax.ShapeDtypeStruct((N, D_REST), BF16),
        ],
        compiler_params=pltpu.CompilerParams(
            dimension_semantics=("arbitrary",),
            vmem_limit_bytes=_vmem_limit(vmem + 8 * 1024 * 1024)),
        name="inproj",
    )(x2, g, sc, sh, w_bf, b_in)


def _gelu_tanh(x):
    return 0.5 * x * (1.0 + jnp.tanh(math.sqrt(2.0 / math.pi) * (x + 0.044715 * (x * x * x))))


def _softplus(z):
    return jnp.maximum(z, 0.0) + jnp.log1p(jnp.exp(-jnp.abs(z)))


def _rnn_kernel(xr_ref, gr_ref, cw_ref, cb_ref, wa_ref, ba_ref, wx_ref, bx_ref, lam_ref,
                y_ref, xbuf, a_s, b_s, h_s, carry, *, per_b, gw):
    t = pl.program_id(0) % per_b
    tt = xr_ref.shape[0]
    halo = V7X_SUBLANES

    @pl.when(t == 0)
    def _():
        xbuf[0:halo, :] = jnp.zeros((halo, xbuf.shape[1]), F32)
        carry[...] = jnp.zeros_like(carry)

    xbuf[halo:halo + tt, :] = xr_ref[...]
    xc = cb_ref[...] + xbuf[pl.ds(halo - (CONV_W - 1), tt), :] * cw_ref[0:1, :]
    for kk in range(1, CONV_W):
        xc = xc + xbuf[pl.ds(halo - (CONV_W - 1) + kk, tt), :] * cw_ref[kk:kk + 1, :]
    xbuf[0:halo, :] = xbuf[tt:tt + halo, :]

    xcb = xc.astype(BF16)
    sp = _softplus(-lam_ref[...])
    first = t == 0
    row = lax.broadcasted_iota(jnp.int32, (tt, gw), 0)
    for g in range(xc.shape[1] // gw):
        cs = slice(g * gw, (g + 1) * gw)
        xg = xcb[:, cs]
        gate_r = jax.nn.sigmoid(jnp.dot(xg, wa_ref[g], preferred_element_type=F32) + ba_ref[:, cs])
        gate_i = jax.nn.sigmoid(jnp.dot(xg, wx_ref[g], preferred_element_type=F32) + bx_ref[:, cs])
        log_a = -LRU_C * gate_r * sp[:, cs]
        th = jnp.tanh(log_a)
        mult = jnp.sqrt(-2.0 * th / (1.0 - th))
        mult = jnp.where(jnp.logical_and(first, row == 0), 1.0, mult)
        a_s[:, cs] = jnp.exp(log_a)
        b_s[:, cs] = (xc[:, cs] * gate_i) * mult

    srow = lax.broadcasted_iota(jnp.int32, (V7X_SUBLANES, xc.shape[1]), 0)

    def body(i, h_prev):
        r = pl.multiple_of(i * V7X_SUBLANES, V7X_SUBLANES)
        a = a_s[pl.ds(r, V7X_SUBLANES), :]
        b = b_s[pl.ds(r, V7X_SUBLANES), :]
        for s in (1, 2, 4):
            keep = srow >= s
            b = jnp.where(keep, a * pltpu.roll(b, s, 0) + b, b)
            a = jnp.where(keep, a * pltpu.roll(a, s, 0), a)
        h = a * h_prev + b
        h_s[pl.ds(r, V7X_SUBLANES), :] = h
        return h[V7X_SUBLANES - 1:V7X_SUBLANES, :]

    carry[...] = lax.fori_loop(0, tt // V7X_SUBLANES, body, carry[...])
    y_ref[...] = (h_s[...] * _gelu_tanh(gr_ref[...])).astype(BF16)


def _block_diag_tiles(w, gw):
    nb, bw, _ = w.shape
    per = gw // bw
    w4 = w.reshape(nb // per, per, bw, bw)
    eye = jnp.eye(per, dtype=w.dtype)
    return jnp.einsum("gpij,pq->gpiqj", w4, eye).reshape(nb // per, gw, gw)


def _rnn(proj_rnn, conv_w, conv_b, wa, ba, wx, bx, lam, S):
    N = proj_rnn.shape[0]
    C = D_RNN
    tt = _tiles()["tok"]
    gw = _tiles()["rnn_group"]
    per_b = S // tt
    vec = lambda: pl.BlockSpec((1, C), lambda i: (0, 0))
    return pl.pallas_call(
        functools.partial(_rnn_kernel, per_b=per_b, gw=gw),
        grid=(N // tt,),
        in_specs=[
            pl.BlockSpec((tt, C), lambda i: (i, 0)),
            pl.BlockSpec((tt, C), lambda i: (i, 1)),
            pl.BlockSpec((CONV_W, C), lambda i: (0, 0)),
            vec(),
            pl.BlockSpec((C // gw, gw, gw), lambda i: (0, 0, 0)),
            vec(),
            pl.BlockSpec((C // gw, gw, gw), lambda i: (0, 0, 0)),
            vec(),
            vec(),
        ],
        out_specs=pl.BlockSpec((tt, C), lambda i: (i, 0)),
        out_shape=jax.ShapeDtypeStruct((N, C), BF16),
        scratch_shapes=[
            pltpu.VMEM((tt + V7X_SUBLANES, C), F32),
            pltpu.VMEM((tt, C), F32),
            pltpu.VMEM((tt, C), F32),
            pltpu.VMEM((tt, C), F32),
            pltpu.VMEM((1, C), F32),
        ],
        compiler_params=pltpu.CompilerParams(
            dimension_semantics=("arbitrary",),
            vmem_limit_bytes=_vmem_limit(16 * tt * C * 4)),
        name="rnn",
    )(proj_rnn, proj_rnn, conv_w, conv_b, wa, ba, wx, bx, lam)


def _alibi_slopes():
    return [2.0 ** (-8.0 * (h + 1) / N_HEADS) for h in range(N_HEADS)]


def _attn_kernel(sink_ref, q_ref, kp_ref, kc_ref, vp_ref, vc_ref, o_ref, *, per_b):
    blk = pl.program_id(0) % per_b
    bq = q_ref.shape[0]
    qi = lax.broadcasted_iota(jnp.int32, (bq, 2 * bq), 0)
    ci = lax.broadcasted_iota(jnp.int32, (bq, 2 * bq), 1)
    dist = qi + bq - ci
    valid = (dist >= 0) & (dist < WINDOW) & ((ci >= bq) | (blk > 0))
    distf = dist.astype(F32)
    slopes = _alibi_slopes()
    outs = []
    for kv in range(N_KV):
        ks = slice(kv * HEAD_DIM, (kv + 1) * HEAD_DIM)
        kb = jnp.concatenate([kp_ref[:, ks], kc_ref[:, ks]], axis=0)
        vb = jnp.concatenate([vp_ref[:, ks], vc_ref[:, ks]], axis=0)
        for g in range(GROUP):
            h = kv * GROUP + g
            qh = q_ref[:, h * HEAD_DIM:(h + 1) * HEAD_DIM]
            s = lax.dot_general(qh, kb, (((1,), (1,)), ((), ())), preferred_element_type=F32)
            s = s * (HEAD_DIM ** -0.5) - slopes[h] * distf
            s = jnp.where(valid, s, -jnp.inf)
            sink = sink_ref[h]
            m = jnp.maximum(jnp.max(s, axis=-1, keepdims=True), sink)
            p = jnp.exp(s - m)
            denom = jnp.sum(p, axis=-1, keepdims=True) + jnp.exp(sink - m)
            o = jnp.dot(p.astype(BF16), vb, preferred_element_type=F32) / denom
            outs.append(o)
    o_ref[...] = jnp.concatenate(outs, axis=-1).astype(BF16)


def _attn(proj_rest, sinks, S):
    N = proj_rest.shape[0]
    bq = _tiles()["attn_q"]
    per_b = S // bq
    k_col = (Q_DIM + 2 * D_MODEL) // KV_DIM
    v_col = k_col + 1

    def prev(i):
        return jnp.where(i % per_b == 0, i, i - 1)

    return pl.pallas_call(
        functools.partial(_attn_kernel, per_b=per_b),
        grid=(N // bq,),
        in_specs=[
            pl.BlockSpec(memory_space=pltpu.SMEM),
            pl.BlockSpec((bq, Q_DIM), lambda i: (i, 0)),
            pl.BlockSpec((bq, KV_DIM), lambda i: (prev(i), k_col)),
            pl.BlockSpec((bq, KV_DIM), lambda i: (i, k_col)),
            pl.BlockSpec((bq, KV_DIM), lambda i: (prev(i), v_col)),
            pl.BlockSpec((bq, KV_DIM), lambda i: (i, v_col)),
        ],
        out_specs=pl.BlockSpec((bq, Q_DIM), lambda i: (i, 0)),
        out_shape=jax.ShapeDtypeStruct((N, Q_DIM), BF16),
        compiler_params=pltpu.CompilerParams(dimension_semantics=("arbitrary",)),
        name="attn",
    )(sinks, proj_rest, proj_rest, proj_rest, proj_rest, proj_rest)


def _merge_kernel(x_ref, yr_ref, ya_ref, gr_ref, ga_ref, wr_ref, wa_ref, wo_ref,
                  gpost_ref, g1_ref, gpre_ref, sc2_ref, sh2_ref, rwt_ref, rb_ref,
                  x1_ref, h2_ref, lg_ref):
    r = jnp.dot(yr_ref[...], wr_ref[...], preferred_element_type=F32)
    a = jnp.dot(ya_ref[...], wa_ref[...], preferred_element_type=F32)
    merged = (jax.nn.sigmoid(gr_ref[...].astype(F32)) * r
              + jax.nn.sigmoid(ga_ref[...].astype(F32)) * a)
    mix = jnp.dot(merged.astype(BF16), wo_ref[...], preferred_element_type=F32)
    x1 = x_ref[...] + g1_ref[0] * _rms(mix, gpost_ref[...])
    x1_ref[...] = x1
    h2 = _rms(x1, gpre_ref[...]) * (1.0 + sc2_ref[0]) + sh2_ref[0]
    h2_ref[...] = h2
    lg = lax.dot_general(rwt_ref[...], h2.astype(BF16), (((1,), (1,)), ((), ())),
                         preferred_element_type=F32)
    lg_ref[...] = lg + rb_ref[...]


def _merge(x2, y_rnn, y_att, proj_rest, wr, wa, wo, gpost, g1, gpre, sc2, sh2, rwt, rb, S):
    N, D = x2.shape
    tm = _tiles()["tok"]
    per_b = S // tm
    gate_r_col = Q_DIM // D
    mat = lambda: pl.BlockSpec((D, D), lambda i: (0, 0))
    vec = lambda: pl.BlockSpec((1, D), lambda i: (0, 0))
    bvec = lambda: pl.BlockSpec((1, 1, D), lambda i: (i // per_b, 0, 0))
    tile = lambda col=0: pl.BlockSpec((tm, D), lambda i: (i, col))
    return pl.pallas_call(
        _merge_kernel,
        grid=(N // tm,),
        in_specs=[
            tile(), tile(), tile(), tile(gate_r_col), tile(gate_r_col + 1),
            mat(), mat(), mat(),
            vec(), bvec(), vec(), bvec(), bvec(),
            pl.BlockSpec((N_EXPERTS, D), lambda i: (0, 0)),
            pl.BlockSpec((N_EXPERTS, 1), lambda i: (0, 0)),
        ],
        out_specs=[
            tile(), tile(),
            pl.BlockSpec((N_EXPERTS, tm), lambda i: (0, i)),
        ],
        out_shape=[
            jax.ShapeDtypeStruct((N, D), F32),
            jax.ShapeDtypeStruct((N, D), F32),
            jax.ShapeDtypeStruct((N_EXPERTS, N), F32),
        ],
        compiler_params=pltpu.CompilerParams(
            dimension_semantics=("arbitrary",),
            vmem_limit_bytes=_vmem_limit(6 * D * D * 2 + 24 * tm * D * 4)),
        name="merge",
    )(x2, y_rnn, y_att, proj_rest, proj_rest, wr, wa, wo, gpost, g1, gpre, sc2, sh2, rwt, rb)


def _route_kernel(lg_ref, g_ref, slot_ref, cnt_ref, tcnt_ref, tcar_ref, tlst_ref, carry):
    i = pl.program_id(0)

    @pl.when(i == 0)
    def _():
        carry[...] = jnp.zeros_like(carry)
        tcnt_ref[...] = jnp.zeros_like(tcnt_ref)
        tcar_ref[...] = jnp.zeros_like(tcar_ref)
        tlst_ref[...] = jnp.zeros_like(tlst_ref)

    l = lg_ref[...]
    E, T = l.shape
    row = lax.broadcasted_iota(jnp.int32, (E, T), 0).astype(F32)
    vals, idxs = [], []
    for _ in range(TOP_K):
        m = jnp.max(l, axis=0, keepdims=True)
        idx = jnp.min(jnp.where(l == m, row, float(E)), axis=0, keepdims=True)
        vals.append(m)
        idxs.append(idx)
        l = jnp.where(row == idx, -jnp.inf, l)
    ex = [jnp.exp(v - vals[0]) for v in vals]
    tot = ex[0]
    for e in ex[1:]:
        tot = tot + e
    g_ref[...] = jnp.concatenate([e / tot for e in ex], axis=0)

    hot = [row == idx for idx in idxs]
    onehot = jnp.zeros((E, T), F32)
    for hk in hot:
        onehot = onehot + hk.astype(F32)
    tri_t = (lax.broadcasted_iota(jnp.int32, (T, T), 0)
             < lax.broadcasted_iota(jnp.int32, (T, T), 1)).astype(BF16)
    before = jnp.dot(onehot.astype(BF16), tri_t, preferred_element_type=F32)
    cnt = jnp.sum(onehot, axis=1, keepdims=True)
    tri_e = (lax.broadcasted_iota(jnp.int32, (E, E), 1)
             < lax.broadcasted_iota(jnp.int32, (E, E), 0)).astype(BF16)
    lstart = jnp.dot(tri_e, jnp.broadcast_to(cnt, (E, V7X_LANES)).astype(BF16),
                     preferred_element_type=F32)[:, 0:1]
    local = before + lstart
    slots = [jnp.sum(jnp.where(hk, local, 0.0), axis=0, keepdims=True) for hk in hot]
    slot_ref[...] = jnp.concatenate(slots, axis=0).astype(jnp.int32)

    mine = lax.broadcasted_iota(jnp.int32, tcnt_ref.shape, 1) == i
    tcnt_ref[...] = jnp.where(mine, cnt.astype(jnp.int32), tcnt_ref[...])
    tcar_ref[...] = jnp.where(mine, carry[...].astype(jnp.int32), tcar_ref[...])
    tlst_ref[...] = jnp.where(mine, lstart.astype(jnp.int32), tlst_ref[...])
    total = carry[...] + cnt
    carry[...] = total
    cnt_ref[...] = total.astype(jnp.int32)


def _route(logits_t):
    E, N = logits_t.shape
    T = _tiles()["moe_t"]
    assert T <= 2 ** 8 and N // T <= V7X_LANES
    out = lambda: pl.BlockSpec((TOP_K, T), lambda i: (0, i))
    per_tile = lambda: pl.BlockSpec((E, V7X_LANES), lambda i: (0, 0))
    return pl.pallas_call(
        _route_kernel,
        grid=(N // T,),
        in_specs=[pl.BlockSpec((E, T), lambda i: (0, i))],
        out_specs=[out(), out(), pl.BlockSpec((E, 1), lambda i: (0, 0)),
                   per_tile(), per_tile(), per_tile()],
        out_shape=[
            jax.ShapeDtypeStruct((TOP_K, N), F32),
            jax.ShapeDtypeStruct((TOP_K, N), jnp.int32),
            jax.ShapeDtypeStruct((E, 1), jnp.int32),
            jax.ShapeDtypeStruct((E, V7X_LANES), jnp.int32),
            jax.ShapeDtypeStruct((E, V7X_LANES), jnp.int32),
            jax.ShapeDtypeStruct((E, V7X_LANES), jnp.int32),
        ],
        scratch_shapes=[pltpu.VMEM((E, 1), F32)],
        compiler_params=pltpu.CompilerParams(dimension_semantics=("arbitrary",)),
        name="route",
    )(logits_t)


def _run_copies(i, cnt_ref, lst_ref, dst_ref, max_rows, make):
    def per_expert(e, c):
        j = i * N_EXPERTS + e
        cnt, l0, d0 = cnt_ref[j], lst_ref[j], dst_ref[j]
        b = max_rows
        while b >= 1:
            off = (cnt // (2 * b)) * (2 * b)

            @pl.when((cnt & b) != 0)
            def _(b=b, off=off):
                make(l0 + off, d0 + off, b).start()

            b //= 2
        return c

    lax.fori_loop(0, N_EXPERTS, per_expert, 0)


def _row_slab(ref, row, nrows):
    return ref.at[pl.ds(pl.multiple_of(row * V7X_SUBLANES, V7X_SUBLANES), nrows * V7X_SUBLANES)]


def _to_slabs(ref, val):
    rows = val.shape[0]
    for s in range(V7X_SUBLANES):
        ref[pl.ds(s, rows, stride=V7X_SUBLANES), :] = val[:, s * V7X_LANES:(s + 1) * V7X_LANES]


def _from_slabs(ref):
    rows = ref.shape[0] // V7X_SUBLANES
    return jnp.concatenate(
        [ref[pl.ds(s, rows, stride=V7X_SUBLANES), :] for s in range(V7X_SUBLANES)], axis=1)


def _dispatch_kernel(cnt_ref, lst_ref, dst_ref, tail_ref, nu_ref, slot_ref, h_ref, xs_ref,
                     buf, zbuf, sem):
    i = pl.program_id(0)
    T = h_ref.shape[0]
    bm = zbuf.shape[0] // V7X_SUBLANES
    nblk = xs_ref.shape[0] // zbuf.shape[0]

    def zero_copy(row0):
        return pltpu.make_async_copy(zbuf, _row_slab(xs_ref, row0, bm), sem)

    @pl.when(i == 0)
    def _():
        zbuf[...] = jnp.zeros_like(zbuf)

        def start(e, c):
            zero_copy(tail_ref[e]).start()
            return c

        def wait(e, c):
            zero_copy(tail_ref[e]).wait()
            return c

        def start_unused(j, c):
            zero_copy(j * bm).start()
            return c

        def wait_unused(j, c):
            zero_copy(j * bm).wait()
            return c

        lax.fori_loop(0, N_EXPERTS, start, 0)
        lax.fori_loop(nu_ref[0], nblk, start_unused, 0)
        lax.fori_loop(0, N_EXPERTS, wait, 0)
        lax.fori_loop(nu_ref[0], nblk, wait_unused, 0)

    slot = slot_ref[...]
    r_id = lax.broadcasted_iota(jnp.int32, (TOP_K * T, T), 0)
    perm = r_id == slot[0:1, :]
    for k in range(1, TOP_K):
        perm = jnp.logical_or(perm, r_id == slot[k:k + 1, :])
    grouped = jnp.dot(perm.astype(BF16), h_ref[...].astype(BF16), preferred_element_type=F32)
    _to_slabs(buf, grouped)

    _run_copies(i, cnt_ref, lst_ref, dst_ref, T,
                lambda l, d, n: pltpu.make_async_copy(_row_slab(buf, l, n), _row_slab(xs_ref, d, n), sem))
    pltpu.make_async_copy(buf, _row_slab(xs_ref, 0, TOP_K * T), sem).wait()


def _dispatch(h2, slot, tile_cnt, tile_lst, tile_dst, tail_start, n_used, n_rows, bm):
    N, D = h2.shape
    T = _tiles()["moe_t"]
    S8 = V7X_SUBLANES
    imap = lambda f: (lambda i, *_: f(i))
    return pl.pallas_call(
        _dispatch_kernel,
        grid_spec=pltpu.PrefetchScalarGridSpec(
            num_scalar_prefetch=5,
            grid=(N // T,),
            in_specs=[
                pl.BlockSpec((TOP_K, T), imap(lambda i: (0, i))),
                pl.BlockSpec((T, D), imap(lambda i: (i, 0))),
            ],
            out_specs=pl.BlockSpec(memory_space=pl.ANY),
            scratch_shapes=[
                pltpu.VMEM((TOP_K * T * S8, V7X_LANES), F32),
                pltpu.VMEM((bm * S8, V7X_LANES), F32),
                pltpu.SemaphoreType.DMA(()),
            ],
        ),
        out_shape=jax.ShapeDtypeStruct((n_rows * S8, V7X_LANES), F32),
        compiler_params=pltpu.CompilerParams(
            dimension_semantics=("arbitrary",),
            vmem_limit_bytes=_vmem_limit(6 * TOP_K * T * D * 4)),
        name="dispatch",
    )(tile_cnt, tile_lst, tile_dst, tail_start, n_used, slot, h2)


def _ffn_kernel(be_ref, bv_ref, bf_ref, nu_ref, x_ref, w1_ref, b1g_ref, b1l_ref, w2_ref, b2_ref,
                y_ref, w1g_s, w1l_s, w2_s):
    i = pl.program_id(0)
    bm, D = x_ref.shape[0] // V7X_SUBLANES, w2_s.shape[1]
    pw = 2 * V7X_LANES

    @pl.when(jnp.logical_and(i < nu_ref[0], bf_ref[i] == 1))
    def _():
        src = lax.broadcasted_iota(jnp.int32, (pw, pw), 0)
        dst = lax.broadcasted_iota(jnp.int32, (pw, pw), 1)
        want = jnp.where(dst < V7X_LANES, 2 * dst, 2 * (dst - V7X_LANES) + 1)
        perm = (src == want).astype(BF16)
        for c in range(w1_ref.shape[2] // pw):
            wp = jnp.dot(w1_ref[0, :, c * pw:(c + 1) * pw].astype(BF16), perm,
                         preferred_element_type=F32).astype(BF16)
            w1g_s[:, c * V7X_LANES:(c + 1) * V7X_LANES] = wp[:, :V7X_LANES]
            w1l_s[:, c * V7X_LANES:(c + 1) * V7X_LANES] = wp[:, V7X_LANES:]
        w2_s[...] = w2_ref[0].astype(BF16)

    @pl.when(i < nu_ref[0])
    def _():
        rows = lax.broadcasted_iota(jnp.int32, (bm, D), 0)
        x = jnp.where(rows < bv_ref[i], _from_slabs(x_ref), 0.0).astype(BF16)
        glu = jnp.dot(x, w1g_s[...], preferred_element_type=F32) + b1g_ref[0]
        lin = jnp.dot(x, w1l_s[...], preferred_element_type=F32) + b1l_ref[0]
        glu = jnp.minimum(glu, SWIGLU_LIMIT)
        lin = jnp.clip(lin, -SWIGLU_LIMIT, SWIGLU_LIMIT)
        act = glu * jax.nn.sigmoid(SWIGLU_ALPHA * glu) * (lin + 1.0)
        _to_slabs(y_ref, jnp.dot(act.astype(BF16), w2_s[...], preferred_element_type=F32) + b2_ref[0])

    @pl.when(i >= nu_ref[0])
    def _():
        y_ref[...] = jnp.zeros_like(y_ref)


def _ffn(xs, blk_e, blk_valid, blk_first, n_used, w1, b1g, b1l, w2, b2, bm):
    E, D, F2 = w1.shape
    F = F2 // 2
    slab = bm * V7X_SUBLANES
    nblk = xs.shape[0] // slab

    def row_blk(i, be, bv, bf, nu):
        return (jnp.minimum(i, nu[0] - 1), 0)

    def per_e(i, be, bv, bf, nu):
        return (be[i], 0, 0)

    vmem = 2 * (D * F2 + F * D) * 4 + (D * F2 + F * D) * 2 + 8 * bm * F2 * 4
    return pl.pallas_call(
        _ffn_kernel,
        grid_spec=pltpu.PrefetchScalarGridSpec(
            num_scalar_prefetch=4,
            grid=(nblk,),
            in_specs=[
                pl.BlockSpec((slab, V7X_LANES), row_blk),
                pl.BlockSpec((1, D, F2), per_e),
                pl.BlockSpec((1, 1, F), per_e),
                pl.BlockSpec((1, 1, F), per_e),
                pl.BlockSpec((1, F, D), per_e),
                pl.BlockSpec((1, 1, D), per_e),
            ],
            out_specs=pl.BlockSpec((slab, V7X_LANES), lambda i, be, bv, bf, nu: (i, 0)),
            scratch_shapes=[
                pltpu.VMEM((D, F), BF16),
                pltpu.VMEM((D, F), BF16),
                pltpu.VMEM((F, D), BF16),
            ],
        ),
        out_shape=jax.ShapeDtypeStruct(xs.shape, F32),
        compiler_params=pltpu.CompilerParams(
            dimension_semantics=("arbitrary",),
            vmem_limit_bytes=_vmem_limit(vmem)),
        name="ffn",
    )(blk_e, blk_valid, blk_first, n_used, xs, w1, b1g, b1l, w2, b2)


def _split_bf16(v):
    hi = v.astype(BF16)
    return hi, (v - hi.astype(F32)).astype(BF16)


def _combine_kernel(cnt_ref, lst_ref, dst_ref, ys_ref, x1_ref, slot_ref, gate_ref, gpost_ref, g2_ref,
                    o_ref, buf, sem):
    i = pl.program_id(0)
    T = x1_ref.shape[0]
    _run_copies(i, cnt_ref, lst_ref, dst_ref, T,
                lambda l, d, n: pltpu.make_async_copy(_row_slab(ys_ref, d, n), _row_slab(buf, l, n), sem))
    pltpu.make_async_copy(_row_slab(ys_ref, 0, TOP_K * T), buf, sem).wait()

    slot = slot_ref[...]
    gates = gate_ref[...]
    r_id = lax.broadcasted_iota(jnp.int32, (T, TOP_K * T), 1)
    w = jnp.zeros((T, TOP_K * T), F32)
    for k in range(TOP_K):
        w = jnp.where(r_id == slot[:, k:k + 1], gates[:, k:k + 1], w)
    w_hi, w_lo = _split_bf16(w)
    y_hi, y_lo = _split_bf16(_from_slabs(buf))
    ff = (jnp.dot(w_hi, y_hi, preferred_element_type=F32)
          + (jnp.dot(w_hi, y_lo, preferred_element_type=F32)
             + jnp.dot(w_lo, y_hi, preferred_element_type=F32)))
    o_ref[...] = x1_ref[...] + g2_ref[0] * _rms(ff, gpost_ref[...])


def _combine(ys, tile_cnt, tile_lst, tile_dst, x1, slot_tk, gates_tk, gpost, g2, S):
    N, D = x1.shape
    T = _tiles()["moe_t"]
    per_b = S // T
    imap = lambda f: (lambda i, *_: f(i))
    return pl.pallas_call(
        _combine_kernel,
        grid_spec=pltpu.PrefetchScalarGridSpec(
            num_scalar_prefetch=3,
            grid=(N // T,),
            in_specs=[
                pl.BlockSpec(memory_space=pl.ANY),
                pl.BlockSpec((T, D), imap(lambda i: (i, 0))),
                pl.BlockSpec((T, TOP_K), imap(lambda i: (i, 0))),
                pl.BlockSpec((T, TOP_K), imap(lambda i: (i, 0))),
                pl.BlockSpec((1, D), imap(lambda i: (0, 0))),
                pl.BlockSpec((1, 1, D), imap(lambda i: (i // per_b, 0, 0))),
            ],
            out_specs=pl.BlockSpec((T, D), imap(lambda i: (i, 0))),
            scratch_shapes=[
                pltpu.VMEM((TOP_K * T * V7X_SUBLANES, V7X_LANES), F32),
                pltpu.SemaphoreType.DMA(()),
            ],
        ),
        out_shape=jax.ShapeDtypeStruct((N, D), F32),
        compiler_params=pltpu.CompilerParams(
            dimension_semantics=("arbitrary",),
            vmem_limit_bytes=_vmem_limit(8 * TOP_K * T * D * 4)),
        name="combine",
    )(tile_cnt, tile_lst, tile_dst, ys, x1, slot_tk, gates_tk, gpost, g2)


def _rest_order(w):
    xr, gr, q, k, v, mg_r, mg_a = jnp.split(
        w, np.cumsum([D_RNN, D_RNN, Q_DIM, KV_DIM, KV_DIM, D_MODEL]).tolist(), axis=-1)
    return jnp.concatenate([xr, gr, q, mg_r, mg_a, k, v], axis=-1)


def _layer(x2, c, B, S, p):
    D = D_MODEL
    N = B * S
    ada = _ada(c, p["w_ada"], p["b_ada"])
    sh1, sc1, g1, sh2, sc2, g2 = [a.reshape(B, 1, D) for a in jnp.split(ada, 6, axis=-1)]
    row = lambda v: v.reshape(1, -1)

    proj_rnn, proj_rest = _inproj(x2, row(p["norm_pre_mix"]), sc1, sh1,
                                  _rest_order(p["w_in"]).astype(BF16),
                                  _rest_order(row(p["b_in"])), S)

    gw = _tiles()["rnn_group"]
    y_rnn = _rnn(proj_rnn, p["conv_w"], row(p["conv_b"]),
                 _block_diag_tiles(p["rg_w_a"], gw).astype(BF16), row(p["rg_b_a"]),
                 _block_diag_tiles(p["rg_w_x"], gw).astype(BF16), row(p["rg_b_x"]),
                 row(p["rg_lambda"]), S)
    y_att = _attn(proj_rest, p["attn_sinks"], S)

    x1, h2, logits_t = _merge(
        x2, y_rnn, y_att, proj_rest,
        p["w_o_rnn"].astype(BF16), p["w_o_attn"].astype(BF16), p["w_out"].astype(BF16),
        row(p["norm_post_mix"]), g1, row(p["norm_pre_ffn"]), sc2, sh2,
        p["router_w"].T.astype(BF16), p["router_b"].reshape(N_EXPERTS, 1), S)

    gates, slot, counts, tile_cnt, tile_car, tile_lst = _route(logits_t)

    bm = _tiles()["moe_bm"]
    n_tiles = N // _tiles()["moe_t"]
    n_rows = N * TOP_K + N_EXPERTS * bm
    nblk = n_rows // bm
    counts = counts.reshape(N_EXPERTS)
    padded = ((counts + bm - 1) // bm) * bm
    pend = jnp.cumsum(padded)
    pstart = pend - padded
    eids = jnp.arange(N_EXPERTS, dtype=jnp.int32)
    per_run = lambda a: a[:, :n_tiles].T.reshape(n_tiles * N_EXPERTS).astype(jnp.int32)
    tile_dst = per_run(tile_car + pstart[:, None])
    tile_cnt, tile_lst = per_run(tile_cnt), per_run(tile_lst)
    tail_start = jnp.maximum(pend - bm, 0).astype(jnp.int32)
    n_used = (pend[-1] // bm).astype(jnp.int32).reshape(1)
    blk_row0 = jnp.minimum(jnp.arange(nblk, dtype=jnp.int32), n_used[0] - 1) * bm
    blk_e = jnp.sum(blk_row0[:, None] >= pend[None, :], axis=1).astype(jnp.int32)
    mine = blk_e[:, None] == eids[None, :]
    blk_cnt = jnp.sum(jnp.where(mine, counts, 0), axis=1)
    blk_pstart = jnp.sum(jnp.where(mine, pstart, 0), axis=1)
    blk_valid = jnp.clip(blk_cnt - (blk_row0 - blk_pstart), 0, bm).astype(jnp.int32)
    blk_first = (blk_row0 == blk_pstart).astype(jnp.int32)

    xs = _dispatch(h2, slot, tile_cnt, tile_lst, tile_dst, tail_start, n_used, n_rows, bm)
    b1 = p["moe_b1"].reshape(N_EXPERTS, D_FF, 2)
    ys = _ffn(xs, blk_e, blk_valid, blk_first, n_used, p["moe_w1"],
              b1[:, :, 0].reshape(N_EXPERTS, 1, D_FF), b1[:, :, 1].reshape(N_EXPERTS, 1, D_FF),
              p["moe_w2"], p["moe_b2"].reshape(N_EXPERTS, 1, D), bm)
    return _combine(ys, tile_cnt, tile_lst, tile_dst, x1, slot.T, gates.T,
                    row(p["norm_post_ffn"]), g2, S)


def kernel(x, c, w_ada, b_ada, norm_pre_mix, norm_post_mix, norm_pre_ffn, norm_post_ffn, w_in, b_in, conv_w, conv_b, rg_w_a, rg_b_a, rg_w_x, rg_b_x, rg_lambda, attn_sinks, w_o_rnn, w_o_attn, w_out, router_w, router_b, moe_w1, moe_b1, moe_w2, moe_b2):
    B, S, D = x.shape
    params = dict(
        w_ada=w_ada, b_ada=b_ada, norm_pre_mix=norm_pre_mix, norm_post_mix=norm_post_mix,
        norm_pre_ffn=norm_pre_ffn, norm_post_ffn=norm_post_ffn, w_in=w_in, b_in=b_in,
        conv_w=conv_w, conv_b=conv_b, rg_w_a=rg_w_a, rg_b_a=rg_b_a, rg_w_x=rg_w_x, rg_b_x=rg_b_x,
        rg_lambda=rg_lambda, attn_sinks=attn_sinks, w_o_rnn=w_o_rnn, w_o_attn=w_o_attn,
        w_out=w_out, router_w=router_w, router_b=router_b, moe_w1=moe_w1, moe_b1=moe_b1,
        moe_w2=moe_w2, moe_b2=moe_b2)
    x2 = x.reshape(B * S, D)
    for layer in range(w_ada.shape[0]):
        x2 = _layer(x2, c, B, S, {k: v[layer] for k, v in params.items()})
    return x2.reshape(B, S, D)
```

```python
import functools
import math

import jax
import jax.numpy as jnp
import numpy as np
from jax import lax
from jax.experimental import pallas as pl
from jax.experimental.pallas import tpu as pltpu

D_MODEL = 1024
D_RNN = 1024
RNN_BLOCKS = 16
RNN_BW = D_RNN // RNN_BLOCKS
CONV_W = 4
LRU_C = 8.0
N_HEADS = 16
N_KV = 4
HEAD_DIM = 64
GROUP = N_HEADS // N_KV
WINDOW = 128
Q_DIM = N_HEADS * HEAD_DIM
KV_DIM = N_KV * HEAD_DIM
N_EXPERTS = 32
TOP_K = 4
D_FF = 1024
SWIGLU_LIMIT = 7.0
SWIGLU_ALPHA = 1.702
EPS = 1e-6
D_IN = 2 * D_RNN + Q_DIM + 2 * KV_DIM + 2 * D_MODEL
D_REST = D_IN - 2 * D_RNN

V7X_LANES = 128
V7X_SUBLANES = 8
V7X_MXU_DIM = 256
V7X_VMEM_BYTES = 64 * 1024 * 1024

F32 = jnp.float32
BF16 = jnp.bfloat16


def _tiles():
    return dict(
        ada_tn=1024,
        tok=512,
        attn_q=WINDOW,
        moe_t=256,
        moe_bm=256,
        rnn_group=V7X_MXU_DIM,
    )


def _vmem_limit(nbytes):
    return int(min(max(nbytes, 16 * 1024 * 1024), V7X_VMEM_BYTES - 8 * 1024 * 1024))


def _rms(x, g):
    return x * lax.rsqrt(jnp.mean(x * x, axis=-1, keepdims=True) + EPS) * g


def _ada_kernel(ct_ref, w_ref, b_ref, o_ref):
    ct = ct_ref[...]
    sc = ct * jax.nn.sigmoid(ct)
    w = w_ref[...]
    rows = [jnp.sum(w * sc[:, b:b + 1], axis=0, keepdims=True) for b in range(ct.shape[1])]
    o_ref[...] = jnp.concatenate(rows, axis=0) + b_ref[...]


def _ada(c, w_ada, b_ada):
    B, D = c.shape
    n_out = w_ada.shape[1]
    tn = _tiles()["ada_tn"]
    return pl.pallas_call(
        _ada_kernel,
        grid=(n_out // tn,),
        in_specs=[
            pl.BlockSpec((D, B), lambda j: (0, 0)),
            pl.BlockSpec((D, tn), lambda j: (0, j)),
            pl.BlockSpec((1, tn), lambda j: (0, j)),
        ],
        out_specs=pl.BlockSpec((B, tn), lambda j: (0, j)),
        out_shape=jax.ShapeDtypeStruct((B, n_out), F32),
        compiler_params=pltpu.CompilerParams(
            dimension_semantics=("arbitrary",),
            vmem_limit_bytes=_vmem_limit(4 * D * tn * 4)),
        name="ada",
    )(c.T, w_ada, b_ada.reshape(1, n_out))


def _inproj_kernel(x_ref, g_ref, sc_ref, sh_ref, w_ref, b_ref, rnn_ref, rest_ref, *, chunk):
    x = x_ref[...]
    h = _rms(x, g_ref[...]) * (1.0 + sc_ref[0]) + sh_ref[0]
    hb = h.astype(BF16)
    n_rnn = rnn_ref.shape[1]
    for c0 in range(0, w_ref.shape[1], chunk):
        acc = jnp.dot(hb, w_ref[:, c0:c0 + chunk], preferred_element_type=F32)
        acc = acc + b_ref[:, c0:c0 + chunk]
        if c0 < n_rnn:
            rnn_ref[:, c0:c0 + chunk] = acc
        else:
            rest_ref[:, c0 - n_rnn:c0 - n_rnn + chunk] = acc.astype(BF16)


def _inproj(x2, g, sc, sh, w_bf, b_in, S):
    N, D = x2.shape
    tm = _tiles()["tok"]
    per_b = S // tm
    chunk = 2 * V7X_MXU_DIM
    vmem = D * D_IN * 2 + 2 * tm * (D * 4 + 2 * D_RNN * 4 + D_REST * 2) + 4 * tm * chunk * 4
    return pl.pallas_call(
        functools.partial(_inproj_kernel, chunk=chunk),
        grid=(N // tm,),
        in_specs=[
            pl.BlockSpec((tm, D), lambda i: (i, 0)),
            pl.BlockSpec((1, D), lambda i: (0, 0)),
            pl.BlockSpec((1, 1, D), lambda i: (i // per_b, 0, 0)),
            pl.BlockSpec((1, 1, D), lambda i: (i // per_b, 0, 0)),
            pl.BlockSpec((D, D_IN), lambda i: (0, 0), pipeline_mode=pl.Buffered(1)),
            pl.BlockSpec((1, D_IN), lambda i: (0, 0)),
        ],
        out_specs=[
            pl.BlockSpec((tm, 2 * D_RNN), lambda i: (i, 0)),
            pl.BlockSpec((tm, D_REST), lambda i: (i, 0)),
        ],
        out_shape=[
            jax.ShapeDtypeStruct((N, 2 * D_RNN), F32),
            jax.ShapeDtypeStruct((N, D_REST), BF16),
        ],
        compiler_params=pltpu.CompilerParams(
            dimension_semantics=("arbitrary",),
            vmem_limit_bytes=_vmem_limit(vmem + 8 * 1024 * 1024)),
        name="inproj",
    )(x2, g, sc, sh, w_bf, b_in)


def _gelu_tanh(x):
    return 0.5 * x * (1.0 + jnp.tanh(math.sqrt(2.0 / math.pi) * (x + 0.044715 * (x * x * x))))


def _softplus(z):
    return jnp.maximum(z, 0.0) + jnp.log1p(jnp.exp(-jnp.abs(z)))


def _sigmoid(x):
    return 0.5 * jnp.tanh(0.5 * x) + 0.5


def _rnn_kernel(xr_ref, gr_ref, cw_ref, cb_ref, wa_ref, ba_ref, wx_ref, bx_ref, lam_ref,
                y_ref, xbuf, a_s, b_s, h_s, ga_s, gb_s, gc_s, carry, *, per_b, gw):
    t = pl.program_id(0) % per_b
    tt = xr_ref.shape[0]
    halo = V7X_SUBLANES

    ng = tt // V7X_SUBLANES
    n_lt = a_s.shape[0]
    per_g = gw // V7X_LANES
    slab = lambda j, r: (j, pl.ds(r, ng, stride=V7X_SUBLANES), slice(None))

    @pl.when(t == 0)
    def _():
        xbuf[:, 0:halo, :] = jnp.zeros((n_lt, halo, V7X_LANES), F32)
        carry[...] = jnp.zeros_like(carry)

    for j in range(n_lt):
        ls = slice(j * V7X_LANES, (j + 1) * V7X_LANES)
        xbuf[j, halo:halo + tt, :] = xr_ref[:, ls]
        taps = {o: xbuf[slab(j, halo + o)] for o in range(-(CONV_W - 1), V7X_SUBLANES)}
        for r in range(V7X_SUBLANES):
            acc = cb_ref[:, ls] + taps[r - (CONV_W - 1)] * cw_ref[0:1, ls]
            for kk in range(1, CONV_W):
                acc = acc + taps[r - (CONV_W - 1) + kk] * cw_ref[kk:kk + 1, ls]
            h_s[slab(j, r)] = acc
        xbuf[j, 0:halo, :] = xbuf[j, tt:tt + halo, :]

    sp = _softplus(-lam_ref[...])
    reset = jnp.logical_and(t == 0, lax.broadcasted_iota(jnp.int32, (tt, gw), 0) == 0)
    for g in range(n_lt // per_g):
        cs = slice(g * gw, (g + 1) * gw)
        xc = jnp.concatenate([h_s[g * per_g + j] for j in range(per_g)], axis=1)
        xg = xc.astype(BF16)
        gate_r = _sigmoid(jnp.dot(xg, wa_ref[g], preferred_element_type=F32) + ba_ref[:, cs])
        gate_i = _sigmoid(jnp.dot(xg, wx_ref[g], preferred_element_type=F32) + bx_ref[:, cs])
        a = jnp.exp(-LRU_C * gate_r * sp[:, cs])
        mult = jnp.where(reset, 1.0, jnp.sqrt((1.0 - a) * (1.0 + a)))
        bt = (xc * gate_i) * mult
        for j in range(per_g):
            a_s[g * per_g + j] = a[:, j * V7X_LANES:(j + 1) * V7X_LANES]
            b_s[g * per_g + j] = bt[:, j * V7X_LANES:(j + 1) * V7X_LANES]

    for j in range(n_lt):
        ls = slice(j * V7X_LANES, (j + 1) * V7X_LANES)
        acc_a = a_s[slab(j, 0)]
        acc_h = b_s[slab(j, 0)]
        h_s[slab(j, 0)] = acc_h
        for r in range(1, V7X_SUBLANES):
            a_r = a_s[slab(j, r)]
            acc_h = a_r * acc_h + b_s[slab(j, r)]
            acc_a = a_r * acc_a
            h_s[slab(j, r)] = acc_h
            a_s[slab(j, r)] = acc_a
        ga_s[:, ls] = acc_a
        gb_s[:, ls] = acc_h

    def across(gi, h_prev):
        gc_s[pl.ds(gi, 1), :] = h_prev
        return ga_s[pl.ds(gi, 1), :] * h_prev + gb_s[pl.ds(gi, 1), :]

    carry[...] = lax.fori_loop(0, ng, across, carry[...])
    for j in range(n_lt):
        ls = slice(j * V7X_LANES, (j + 1) * V7X_LANES)
        h_in = gc_s[:, ls]
        for r in range(V7X_SUBLANES):
            h_s[slab(j, r)] = a_s[slab(j, r)] * h_in + h_s[slab(j, r)]
        y_ref[:, ls] = (h_s[j] * _gelu_tanh(gr_ref[:, ls])).astype(BF16)


def _block_diag_tiles(w, gw):
    nb, bw, _ = w.shape
    per = gw // bw
    w4 = w.reshape(nb // per, per, bw, bw)
    eye = jnp.eye(per, dtype=w.dtype)
    return jnp.einsum("gpij,pq->gpiqj", w4, eye).reshape(nb // per, gw, gw)


def _rnn(proj_rnn, conv_w, conv_b, wa, ba, wx, bx, lam, S):
    N = proj_rnn.shape[0]
    C = D_RNN
    tt = _tiles()["tok"]
    gw = _tiles()["rnn_group"]
    per_b = S // tt
    vec = lambda: pl.BlockSpec((1, C), lambda i: (0, 0))
    return pl.pallas_call(
        functools.partial(_rnn_kernel, per_b=per_b, gw=gw),
        grid=(N // tt,),
        in_specs=[
            pl.BlockSpec((tt, C), lambda i: (i, 0)),
            pl.BlockSpec((tt, C), lambda i: (i, 1)),
            pl.BlockSpec((CONV_W, C), lambda i: (0, 0)),
            vec(),
            pl.BlockSpec((C // gw, gw, gw), lambda i: (0, 0, 0)),
            vec(),
            pl.BlockSpec((C // gw, gw, gw), lambda i: (0, 0, 0)),
            vec(),
            vec(),
        ],
        out_specs=pl.BlockSpec((tt, C), lambda i: (i, 0)),
        out_shape=jax.ShapeDtypeStruct((N, C), BF16),
        scratch_shapes=[
            pltpu.VMEM((C // V7X_LANES, tt + V7X_SUBLANES, V7X_LANES), F32),
            pltpu.VMEM((C // V7X_LANES, tt, V7X_LANES), F32),
            pltpu.VMEM((C // V7X_LANES, tt, V7X_LANES), F32),
            pltpu.VMEM((C // V7X_LANES, tt, V7X_LANES), F32),
            pltpu.VMEM((tt // V7X_SUBLANES, C), F32),
            pltpu.VMEM((tt // V7X_SUBLANES, C), F32),
            pltpu.VMEM((tt // V7X_SUBLANES, C), F32),
            pltpu.VMEM((1, C), F32),
        ],
        compiler_params=pltpu.CompilerParams(
            dimension_semantics=("arbitrary",),
            vmem_limit_bytes=_vmem_limit(16 * tt * C * 4)),
        name="rnn",
    )(proj_rnn, proj_rnn, conv_w, conv_b, wa, ba, wx, bx, lam)


def _alibi_slopes():
    return [2.0 ** (-8.0 * (h + 1) / N_HEADS) for h in range(N_HEADS)]


def _attn_kernel(sink_ref, q_ref, kp_ref, kc_ref, vp_ref, vc_ref, o_ref, bias_s, *, per_b):
    blk = pl.program_id(0) % per_b
    bq = q_ref.shape[0]
    slopes = _alibi_slopes()
    scale = HEAD_DIM ** -0.5
    assert math.log2(scale).is_integer()

    @pl.when(pl.program_id(0) == 0)
    def _():
        qi = lax.broadcasted_iota(jnp.int32, (bq, 2 * bq), 0)
        ci = lax.broadcasted_iota(jnp.int32, (bq, 2 * bq), 1)
        dist = qi + bq - ci
        valid = (dist >= 0) & (dist < WINDOW)
        distf = dist.astype(F32)
        for h in range(N_HEADS):
            b = jnp.where(valid, -slopes[h] * distf, -jnp.inf)
            bias_s[0, h] = jnp.where(ci >= bq, b, -jnp.inf)
            bias_s[1, h] = b

    table = jnp.minimum(blk, 1)
    outs = []
    for kv in range(N_KV):
        ks = slice(kv * HEAD_DIM, (kv + 1) * HEAD_DIM)
        kb = jnp.concatenate([kp_ref[:, ks], kc_ref[:, ks]], axis=0)
        vb = jnp.concatenate([vp_ref[:, ks], vc_ref[:, ks]], axis=0)
        for g in range(GROUP):
            h = kv * GROUP + g
            qh = q_ref[:, h * HEAD_DIM:(h + 1) * HEAD_DIM] * jnp.asarray(scale, BF16)
            s = lax.dot_general(qh, kb, (((1,), (1,)), ((), ())), preferred_element_type=F32)
            s = s + bias_s[table, h]
            sink = sink_ref[h]
            m = jnp.maximum(jnp.max(s, axis=-1, keepdims=True), sink)
            p = jnp.exp(s - m)
            denom = jnp.sum(p, axis=-1, keepdims=True) + jnp.exp(sink - m)
            o = jnp.dot(p.astype(BF16), vb, preferred_element_type=F32) / denom
            outs.append(o)
    o_ref[...] = jnp.concatenate(outs, axis=-1).astype(BF16)


def _attn(proj_rest, sinks, S):
    N = proj_rest.shape[0]
    bq = _tiles()["attn_q"]
    per_b = S // bq
    k_col = (Q_DIM + 2 * D_MODEL) // KV_DIM
    v_col = k_col + 1

    def prev(i):
        return jnp.where(i % per_b == 0, i, i - 1)

    return pl.pallas_call(
        functools.partial(_attn_kernel, per_b=per_b),
        grid=(N // bq,),
        in_specs=[
            pl.BlockSpec(memory_space=pltpu.SMEM),
            pl.BlockSpec((bq, Q_DIM), lambda i: (i, 0)),
            pl.BlockSpec((bq, KV_DIM), lambda i: (prev(i), k_col)),
            pl.BlockSpec((bq, KV_DIM), lambda i: (i, k_col)),
            pl.BlockSpec((bq, KV_DIM), lambda i: (prev(i), v_col)),
            pl.BlockSpec((bq, KV_DIM), lambda i: (i, v_col)),
        ],
        out_specs=pl.BlockSpec((bq, Q_DIM), lambda i: (i, 0)),
        out_shape=jax.ShapeDtypeStruct((N, Q_DIM), BF16),
        scratch_shapes=[pltpu.VMEM((2, N_HEADS, bq, 2 * bq), F32)],
        compiler_params=pltpu.CompilerParams(
            dimension_semantics=("arbitrary",),
            vmem_limit_bytes=_vmem_limit(3 * 2 * N_HEADS * bq * 2 * bq * 4)),
        name="attn",
    )(sinks, proj_rest, proj_rest, proj_rest, proj_rest, proj_rest)


def _merge_kernel(x_ref, yr_ref, ya_ref, gr_ref, ga_ref, wr_ref, wa_ref, wo_ref,
                  gpost_ref, g1_ref, gpre_ref, sc2_ref, sh2_ref, rwt_ref, rb_ref,
                  x1_ref, h2_ref, lg_ref):
    r = jnp.dot(yr_ref[...], wr_ref[...], preferred_element_type=F32)
    a = jnp.dot(ya_ref[...], wa_ref[...], preferred_element_type=F32)
    merged = (_sigmoid(gr_ref[...].astype(F32)) * r
              + _sigmoid(ga_ref[...].astype(F32)) * a)
    mix = jnp.dot(merged.astype(BF16), wo_ref[...], preferred_element_type=F32)
    x1 = x_ref[...] + g1_ref[0] * _rms(mix, gpost_ref[...])
    x1_ref[...] = x1
    h2 = _rms(x1, gpre_ref[...]) * (1.0 + sc2_ref[0]) + sh2_ref[0]
    h2_ref[...] = h2
    lg = lax.dot_general(rwt_ref[...], h2.astype(BF16), (((1,), (1,)), ((), ())),
                         preferred_element_type=F32)
    lg_ref[...] = lg + rb_ref[...]


def _merge(x2, y_rnn, y_att, proj_rest, wr, wa, wo, gpost, g1, gpre, sc2, sh2, rwt, rb, S):
    N, D = x2.shape
    tm = _tiles()["tok"]
    per_b = S // tm
    gate_r_col = Q_DIM // D
    mat = lambda: pl.BlockSpec((D, D), lambda i: (0, 0))
    vec = lambda: pl.BlockSpec((1, D), lambda i: (0, 0))
    bvec = lambda: pl.BlockSpec((1, 1, D), lambda i: (i // per_b, 0, 0))
    tile = lambda col=0: pl.BlockSpec((tm, D), lambda i: (i, col))
    return pl.pallas_call(
        _merge_kernel,
        grid=(N // tm,),
        in_specs=[
            tile(), tile(), tile(), tile(gate_r_col), tile(gate_r_col + 1),
            mat(), mat(), mat(),
            vec(), bvec(), vec(), bvec(), bvec(),
            pl.BlockSpec((N_EXPERTS, D), lambda i: (0, 0)),
            pl.BlockSpec((N_EXPERTS, 1), lambda i: (0, 0)),
        ],
        out_specs=[
            tile(), tile(),
            pl.BlockSpec((N_EXPERTS, tm), lambda i: (0, i)),
        ],
        out_shape=[
            jax.ShapeDtypeStruct((N, D), F32),
            jax.ShapeDtypeStruct((N, D), F32),
            jax.ShapeDtypeStruct((N_EXPERTS, N), F32),
        ],
        compiler_params=pltpu.CompilerParams(
            dimension_semantics=("arbitrary",),
            vmem_limit_bytes=_vmem_limit(6 * D * D * 2 + 24 * tm * D * 4)),
        name="merge",
    )(x2, y_rnn, y_att, proj_rest, proj_rest, wr, wa, wo, gpost, g1, gpre, sc2, sh2, rwt, rb)


def _route_kernel(lg_ref, g_ref, slot_ref, cnt_ref, tcnt_ref, tcar_ref, tlst_ref, carry):
    i = pl.program_id(0)

    @pl.when(i == 0)
    def _():
        carry[...] = jnp.zeros_like(carry)
        tcnt_ref[...] = jnp.zeros_like(tcnt_ref)
        tcar_ref[...] = jnp.zeros_like(tcar_ref)
        tlst_ref[...] = jnp.zeros_like(tlst_ref)

    l = lg_ref[...]
    E, T = l.shape
    row = lax.broadcasted_iota(jnp.int32, (E, T), 0).astype(F32)
    vals, idxs = [], []
    for _ in range(TOP_K):
        m = jnp.max(l, axis=0, keepdims=True)
        idx = jnp.min(jnp.where(l == m, row, float(E)), axis=0, keepdims=True)
        vals.append(m)
        idxs.append(idx)
        l = jnp.where(row == idx, -jnp.inf, l)
    ex = [jnp.exp(v - vals[0]) for v in vals]
    tot = ex[0]
    for e in ex[1:]:
        tot = tot + e
    g_ref[...] = jnp.concatenate([e / tot for e in ex], axis=0)

    hot = [row == idx for idx in idxs]
    onehot = jnp.zeros((E, T), F32)
    for hk in hot:
        onehot = onehot + hk.astype(F32)
    tri_t = (lax.broadcasted_iota(jnp.int32, (T, T), 0)
             < lax.broadcasted_iota(jnp.int32, (T, T), 1)).astype(BF16)
    before = jnp.dot(onehot.astype(BF16), tri_t, preferred_element_type=F32)
    cnt = jnp.sum(onehot, axis=1, keepdims=True)
    tri_e = (lax.broadcasted_iota(jnp.int32, (E, E), 1)
             < lax.broadcasted_iota(jnp.int32, (E, E), 0)).astype(BF16)
    lstart = jnp.dot(tri_e, jnp.broadcast_to(cnt, (E, V7X_LANES)).astype(BF16),
                     preferred_element_type=F32)[:, 0:1]
    local = before + lstart
    slots = [jnp.sum(jnp.where(hk, local, 0.0), axis=0, keepdims=True) for hk in hot]
    slot_ref[...] = jnp.concatenate(slots, axis=0).astype(jnp.int32)

    mine = lax.broadcasted_iota(jnp.int32, tcnt_ref.shape, 1) == i
    tcnt_ref[...] = jnp.where(mine, cnt.astype(jnp.int32), tcnt_ref[...])
    tcar_ref[...] = jnp.where(mine, carry[...].astype(jnp.int32), tcar_ref[...])
    tlst_ref[...] = jnp.where(mine, lstart.astype(jnp.int32), tlst_ref[...])
    total = carry[...] + cnt
    carry[...] = total
    cnt_ref[...] = total.astype(jnp.int32)


def _route(logits_t):
    E, N = logits_t.shape
    T = _tiles()["moe_t"]
    assert T <= 2 ** 8 and N // T <= V7X_LANES
    out = lambda: pl.BlockSpec((TOP_K, T), lambda i: (0, i))
    per_tile = lambda: pl.BlockSpec((E, V7X_LANES), lambda i: (0, 0))
    return pl.pallas_call(
        _route_kernel,
        grid=(N // T,),
        in_specs=[pl.BlockSpec((E, T), lambda i: (0, i))],
        out_specs=[out(), out(), pl.BlockSpec((E, 1), lambda i: (0, 0)),
                   per_tile(), per_tile(), per_tile()],
        out_shape=[
            jax.ShapeDtypeStruct((TOP_K, N), F32),
            jax.ShapeDtypeStruct((TOP_K, N), jnp.int32),
            jax.ShapeDtypeStruct((E, 1), jnp.int32),
            jax.ShapeDtypeStruct((E, V7X_LANES), jnp.int32),
            jax.ShapeDtypeStruct((E, V7X_LANES), jnp.int32),
            jax.ShapeDtypeStruct((E, V7X_LANES), jnp.int32),
        ],
        scratch_shapes=[pltpu.VMEM((E, 1), F32)],
        compiler_params=pltpu.CompilerParams(dimension_semantics=("arbitrary",)),
        name="route",
    )(logits_t)


def _run_copies(i, cnt_ref, lst_ref, dst_ref, max_rows, make):
    def per_expert(e, c):
        j = i * N_EXPERTS + e
        cnt, l0, d0 = cnt_ref[j], lst_ref[j], dst_ref[j]
        b = max_rows
        while b >= 1:
            off = (cnt // (2 * b)) * (2 * b)

            @pl.when((cnt & b) != 0)
            def _(b=b, off=off):
                make(l0 + off, d0 + off, b).start()

            b //= 2
        return c

    lax.fori_loop(0, N_EXPERTS, per_expert, 0)


def _row_slab(ref, row, nrows):
    return ref.at[pl.ds(pl.multiple_of(row * V7X_SUBLANES, V7X_SUBLANES), nrows * V7X_SUBLANES)]


def _to_slabs(ref, val):
    rows = val.shape[0]
    for s in range(V7X_SUBLANES):
        ref[pl.ds(s, rows, stride=V7X_SUBLANES), :] = val[:, s * V7X_LANES:(s + 1) * V7X_LANES]


def _from_slabs(ref):
    rows = ref.shape[0] // V7X_SUBLANES
    return jnp.concatenate(
        [ref[pl.ds(s, rows, stride=V7X_SUBLANES), :] for s in range(V7X_SUBLANES)], axis=1)


def _dispatch_kernel(cnt_ref, lst_ref, dst_ref, tail_ref, nu_ref, slot_ref, h_ref, xs_ref,
                     buf, zbuf, sem):
    i = pl.program_id(0)
    T = h_ref.shape[0]
    bm = zbuf.shape[0] // V7X_SUBLANES
    nblk = xs_ref.shape[0] // zbuf.shape[0]

    def zero_copy(row0):
        return pltpu.make_async_copy(zbuf, _row_slab(xs_ref, row0, bm), sem)

    @pl.when(i == 0)
    def _():
        zbuf[...] = jnp.zeros_like(zbuf)

        def start(e, c):
            zero_copy(tail_ref[e]).start()
            return c

        def wait(e, c):
            zero_copy(tail_ref[e]).wait()
            return c

        def start_unused(j, c):
            zero_copy(j * bm).start()
            return c

        def wait_unused(j, c):
            zero_copy(j * bm).wait()
            return c

        lax.fori_loop(0, N_EXPERTS, start, 0)
        lax.fori_loop(nu_ref[0], nblk, start_unused, 0)
        lax.fori_loop(0, N_EXPERTS, wait, 0)
        lax.fori_loop(nu_ref[0], nblk, wait_unused, 0)

    slot = slot_ref[...]
    r_id = lax.broadcasted_iota(jnp.int32, (TOP_K * T, T), 0)
    perm = r_id == slot[0:1, :]
    for k in range(1, TOP_K):
        perm = jnp.logical_or(perm, r_id == slot[k:k + 1, :])
    grouped = jnp.dot(perm.astype(BF16), h_ref[...].astype(BF16), preferred_element_type=F32)
    _to_slabs(buf, grouped)

    _run_copies(i, cnt_ref, lst_ref, dst_ref, T,
                lambda l, d, n: pltpu.make_async_copy(_row_slab(buf, l, n), _row_slab(xs_ref, d, n), sem))
    pltpu.make_async_copy(buf, _row_slab(xs_ref, 0, TOP_K * T), sem).wait()


def _dispatch(h2, slot, tile_cnt, tile_lst, tile_dst, tail_start, n_used, n_rows, bm):
    N, D = h2.shape
    T = _tiles()["moe_t"]
    S8 = V7X_SUBLANES
    imap = lambda f: (lambda i, *_: f(i))
    return pl.pallas_call(
        _dispatch_kernel,
        grid_spec=pltpu.PrefetchScalarGridSpec(
            num_scalar_prefetch=5,
            grid=(N // T,),
            in_specs=[
                pl.BlockSpec((TOP_K, T), imap(lambda i: (0, i))),
                pl.BlockSpec((T, D), imap(lambda i: (i, 0))),
            ],
            out_specs=pl.BlockSpec(memory_space=pl.ANY),
            scratch_shapes=[
                pltpu.VMEM((TOP_K * T * S8, V7X_LANES), F32),
                pltpu.VMEM((bm * S8, V7X_LANES), F32),
                pltpu.SemaphoreType.DMA(()),
            ],
        ),
        out_shape=jax.ShapeDtypeStruct((n_rows * S8, V7X_LANES), F32),
        compiler_params=pltpu.CompilerParams(
            dimension_semantics=("arbitrary",),
            vmem_limit_bytes=_vmem_limit(6 * TOP_K * T * D * 4)),
        name="dispatch",
    )(tile_cnt, tile_lst, tile_dst, tail_start, n_used, slot, h2)


def _ffn_kernel(be_ref, bv_ref, bf_ref, nu_ref, nxt_ref, x_ref, w1_hbm, b1g_ref, b1l_ref, w2_hbm,
                b2_ref, y_ref, w1f_s, w2f_s, w1g_s, w1l_s, w2_s, sem):
    i = pl.program_id(0)
    bm, D = x_ref.shape[0] // V7X_SUBLANES, w2_s.shape[1]
    pw = 2 * V7X_LANES

    def fetch(e):
        return (pltpu.make_async_copy(w1_hbm.at[e], w1f_s, sem.at[0]),
                pltpu.make_async_copy(w2_hbm.at[e], w2f_s, sem.at[1]))

    @pl.when(i == 0)
    def _():
        for cp in fetch(be_ref[0]):
            cp.start()

    @pl.when(jnp.logical_and(i < nu_ref[0], bf_ref[i] == 1))
    def _():
        e = be_ref[i]
        for cp in fetch(e):
            cp.wait()
        src = lax.broadcasted_iota(jnp.int32, (pw, pw), 0)
        dst = lax.broadcasted_iota(jnp.int32, (pw, pw), 1)
        want = jnp.where(dst < V7X_LANES, 2 * dst, 2 * (dst - V7X_LANES) + 1)
        perm = (src == want).astype(BF16)
        for c in range(w1f_s.shape[1] // pw):
            wp = jnp.dot(w1f_s[:, c * pw:(c + 1) * pw].astype(BF16), perm,
                         preferred_element_type=F32).astype(BF16)
            w1g_s[:, c * V7X_LANES:(c + 1) * V7X_LANES] = wp[:, :V7X_LANES]
            w1l_s[:, c * V7X_LANES:(c + 1) * V7X_LANES] = wp[:, V7X_LANES:]
        w2_s[...] = w2f_s[...].astype(BF16)

        @pl.when(nxt_ref[e] >= 0)
        def _():
            for cp in fetch(nxt_ref[e]):
                cp.start()

    @pl.when(i < nu_ref[0])
    def _():
        rows = lax.broadcasted_iota(jnp.int32, (bm, D), 0)
        x = jnp.where(rows < bv_ref[i], _from_slabs(x_ref), 0.0).astype(BF16)
        glu = jnp.dot(x, w1g_s[...], preferred_element_type=F32) + b1g_ref[0]
        lin = jnp.dot(x, w1l_s[...], preferred_element_type=F32) + b1l_ref[0]
        glu = jnp.minimum(glu, SWIGLU_LIMIT)
        lin = jnp.clip(lin, -SWIGLU_LIMIT, SWIGLU_LIMIT)
        act = glu * _sigmoid(SWIGLU_ALPHA * glu) * (lin + 1.0)
        _to_slabs(y_ref, jnp.dot(act.astype(BF16), w2_s[...], preferred_element_type=F32) + b2_ref[0])

    @pl.when(i >= nu_ref[0])
    def _():
        y_ref[...] = jnp.zeros_like(y_ref)


def _ffn(xs, blk_e, blk_valid, blk_first, n_used, next_e, w1, b1g, b1l, w2, b2, bm):
    E, D, F2 = w1.shape
    F = F2 // 2
    slab = bm * V7X_SUBLANES
    nblk = xs.shape[0] // slab

    def row_blk(i, be, bv, bf, nu, nxt):
        return (jnp.minimum(i, nu[0] - 1), 0)

    def per_e(i, be, bv, bf, nu, nxt):
        return (be[i], 0, 0)

    vmem = (D * F2 + F * D) * (4 + 2) + 8 * bm * F2 * 4
    return pl.pallas_call(
        _ffn_kernel,
        grid_spec=pltpu.PrefetchScalarGridSpec(
            num_scalar_prefetch=5,
            grid=(nblk,),
            in_specs=[
                pl.BlockSpec((slab, V7X_LANES), row_blk),
                pl.BlockSpec(memory_space=pl.ANY),
                pl.BlockSpec((1, 1, F), per_e),
                pl.BlockSpec((1, 1, F), per_e),
                pl.BlockSpec(memory_space=pl.ANY),
                pl.BlockSpec((1, 1, D), per_e),
            ],
            out_specs=pl.BlockSpec((slab, V7X_LANES), lambda i, be, bv, bf, nu, nxt: (i, 0)),
            scratch_shapes=[
                pltpu.VMEM((D, F2), F32),
                pltpu.VMEM((F, D), F32),
                pltpu.VMEM((D, F), BF16),
                pltpu.VMEM((D, F), BF16),
                pltpu.VMEM((F, D), BF16),
                pltpu.SemaphoreType.DMA((2,)),
            ],
        ),
        out_shape=jax.ShapeDtypeStruct(xs.shape, F32),
        compiler_params=pltpu.CompilerParams(
            dimension_semantics=("arbitrary",),
            vmem_limit_bytes=_vmem_limit(vmem)),
        name="ffn",
    )(blk_e, blk_valid, blk_first, n_used, next_e, xs, w1, b1g, b1l, w2, b2)


def _split_bf16(v):
    hi = v.astype(BF16)
    return hi, (v - hi.astype(F32)).astype(BF16)


def _combine_kernel(cnt_ref, lst_ref, dst_ref, ys_ref, x1_ref, slot_ref, gate_ref, gpost_ref, g2_ref,
                    o_ref, buf, sem):
    i = pl.program_id(0)
    T = x1_ref.shape[0]
    _run_copies(i, cnt_ref, lst_ref, dst_ref, T,
                lambda l, d, n: pltpu.make_async_copy(_row_slab(ys_ref, d, n), _row_slab(buf, l, n), sem))
    pltpu.make_async_copy(_row_slab(ys_ref, 0, TOP_K * T), buf, sem).wait()

    slot = slot_ref[...]
    gates = gate_ref[...]
    r_id = lax.broadcasted_iota(jnp.int32, (T, TOP_K * T), 1)
    w = jnp.zeros((T, TOP_K * T), F32)
    for k in range(TOP_K):
        w = jnp.where(r_id == slot[:, k:k + 1], gates[:, k:k + 1], w)
    w_hi, w_lo = _split_bf16(w)
    y_hi, y_lo = _split_bf16(_from_slabs(buf))
    ff = (jnp.dot(w_hi, y_hi, preferred_element_type=F32)
          + (jnp.dot(w_hi, y_lo, preferred_element_type=F32)
             + jnp.dot(w_lo, y_hi, preferred_element_type=F32)))
    o_ref[...] = x1_ref[...] + g2_ref[0] * _rms(ff, gpost_ref[...])


def _combine(ys, tile_cnt, tile_lst, tile_dst, x1, slot_tk, gates_tk, gpost, g2, S):
    N, D = x1.shape
    T = _tiles()["moe_t"]
    per_b = S // T
    imap = lambda f: (lambda i, *_: f(i))
    return pl.pallas_call(
        _combine_kernel,
        grid_spec=pltpu.PrefetchScalarGridSpec(
            num_scalar_prefetch=3,
            grid=(N // T,),
            in_specs=[
                pl.BlockSpec(memory_space=pl.ANY),
                pl.BlockSpec((T, D), imap(lambda i: (i, 0))),
                pl.BlockSpec((T, TOP_K), imap(lambda i: (i, 0))),
                pl.BlockSpec((T, TOP_K), imap(lambda i: (i, 0))),
                pl.BlockSpec((1, D), imap(lambda i: (0, 0))),
                pl.BlockSpec((1, 1, D), imap(lambda i: (i // per_b, 0, 0))),
            ],
            out_specs=pl.BlockSpec((T, D), imap(lambda i: (i, 0))),
            scratch_shapes=[
                pltpu.VMEM((TOP_K * T * V7X_SUBLANES, V7X_LANES), F32),
                pltpu.SemaphoreType.DMA(()),
            ],
        ),
        out_shape=jax.ShapeDtypeStruct((N, D), F32),
        compiler_params=pltpu.CompilerParams(
            dimension_semantics=("arbitrary",),
            vmem_limit_bytes=_vmem_limit(8 * TOP_K * T * D * 4)),
        name="combine",
    )(tile_cnt, tile_lst, tile_dst, ys, x1, slot_tk, gates_tk, gpost, g2)


def _rest_order(w):
    xr, gr, q, k, v, mg_r, mg_a = jnp.split(
        w, np.cumsum([D_RNN, D_RNN, Q_DIM, KV_DIM, KV_DIM, D_MODEL]).tolist(), axis=-1)
    return jnp.concatenate([xr, gr, q, mg_r, mg_a, k, v], axis=-1)


def _layer(x2, c, B, S, p):
    D = D_MODEL
    N = B * S
    ada = _ada(c, p["w_ada"], p["b_ada"])
    sh1, sc1, g1, sh2, sc2, g2 = [a.reshape(B, 1, D) for a in jnp.split(ada, 6, axis=-1)]
    row = lambda v: v.reshape(1, -1)

    proj_rnn, proj_rest = _inproj(x2, row(p["norm_pre_mix"]), sc1, sh1,
                                  _rest_order(p["w_in"]).astype(BF16),
                                  _rest_order(row(p["b_in"])), S)

    gw = _tiles()["rnn_group"]
    y_rnn = _rnn(proj_rnn, p["conv_w"], row(p["conv_b"]),
                 _block_diag_tiles(p["rg_w_a"], gw).astype(BF16), row(p["rg_b_a"]),
                 _block_diag_tiles(p["rg_w_x"], gw).astype(BF16), row(p["rg_b_x"]),
                 row(p["rg_lambda"]), S)
    y_att = _attn(proj_rest, p["attn_sinks"], S)

    x1, h2, logits_t = _merge(
        x2, y_rnn, y_att, proj_rest,
        p["w_o_rnn"].astype(BF16), p["w_o_attn"].astype(BF16), p["w_out"].astype(BF16),
        row(p["norm_post_mix"]), g1, row(p["norm_pre_ffn"]), sc2, sh2,
        p["router_w"].T.astype(BF16), p["router_b"].reshape(N_EXPERTS, 1), S)

    gates, slot, counts, tile_cnt, tile_car, tile_lst = _route(logits_t)

    bm = _tiles()["moe_bm"]
    n_tiles = N // _tiles()["moe_t"]
    n_rows = N * TOP_K + N_EXPERTS * bm
    nblk = n_rows // bm
    counts = counts.reshape(N_EXPERTS)
    padded = ((counts + bm - 1) // bm) * bm
    pend = jnp.cumsum(padded)
    pstart = pend - padded
    eids = jnp.arange(N_EXPERTS, dtype=jnp.int32)
    per_run = lambda a: a[:, :n_tiles].T.reshape(n_tiles * N_EXPERTS).astype(jnp.int32)
    tile_dst = per_run(tile_car + pstart[:, None])
    tile_cnt, tile_lst = per_run(tile_cnt), per_run(tile_lst)
    tail_start = jnp.maximum(pend - bm, 0).astype(jnp.int32)
    n_used = (pend[-1] // bm).astype(jnp.int32).reshape(1)
    blk_row0 = jnp.minimum(jnp.arange(nblk, dtype=jnp.int32), n_used[0] - 1) * bm
    blk_e = jnp.sum(blk_row0[:, None] >= pend[None, :], axis=1).astype(jnp.int32)
    mine = blk_e[:, None] == eids[None, :]
    blk_cnt = jnp.sum(jnp.where(mine, counts, 0), axis=1)
    blk_pstart = jnp.sum(jnp.where(mine, pstart, 0), axis=1)
    blk_valid = jnp.clip(blk_cnt - (blk_row0 - blk_pstart), 0, bm).astype(jnp.int32)
    blk_first = (blk_row0 == blk_pstart).astype(jnp.int32)

    xs = _dispatch(h2, slot, tile_cnt, tile_lst, tile_dst, tail_start, n_used, n_rows, bm)
    b1 = p["moe_b1"].reshape(N_EXPERTS, D_FF, 2)
    later = jnp.logical_and(counts[None, :] > 0, eids[None, :] > eids[:, None])
    next_e = jnp.min(jnp.where(later, eids[None, :], N_EXPERTS), axis=1)
    next_e = jnp.where(next_e == N_EXPERTS, -1, next_e).astype(jnp.int32)
    ys = _ffn(xs, blk_e, blk_valid, blk_first, n_used, next_e, p["moe_w1"],
              b1[:, :, 0].reshape(N_EXPERTS, 1, D_FF), b1[:, :, 1].reshape(N_EXPERTS, 1, D_FF),
              p["moe_w2"], p["moe_b2"].reshape(N_EXPERTS, 1, D), bm)
    return _combine(ys, tile_cnt, tile_lst, tile_dst, x1, slot.T, gates.T,
                    row(p["norm_post_ffn"]), g2, S)


def kernel(x, c, w_ada, b_ada, norm_pre_mix, norm_post_mix, norm_pre_ffn, norm_post_ffn, w_in, b_in, conv_w, conv_b, rg_w_a, rg_b_a, rg_w_x, rg_b_x, rg_lambda, attn_sinks, w_o_rnn, w_o_attn, w_out, router_w, router_b, moe_w1, moe_b1, moe_w2, moe_b2):
    B, S, D = x.shape
    params = dict(
        w_ada=w_ada, b_ada=b_ada, norm_pre_mix=norm_pre_mix, norm_post_mix=norm_post_mix,
        norm_pre_ffn=norm_pre_ffn, norm_post_ffn=norm_post_ffn, w_in=w_in, b_in=b_in,
        conv_w=conv_w, conv_b=conv_b, rg_w_a=rg_w_a, rg_b_a=rg_b_a, rg_w_x=rg_w_x, rg_b_x=rg_b_x,
        rg_lambda=rg_lambda, attn_sinks=attn_sinks, w_o_rnn=w_o_rnn, w_o_attn=w_o_attn,
        w_out=w_out, router_w=router_w, router_b=router_b, moe_w1=moe_w1, moe_b1=moe_b1,
        moe_w2=moe_w2, moe_b2=moe_b2)
    x2 = x.reshape(B * S, D)
    for layer in range(w_ada.shape[0]):
        x2 = _layer(x2, c, B, S, {k: v[layer] for k, v in params.items()})
    return x2.reshape(B, S, D)
```

```python
import functools
import math

import jax
import jax.numpy as jnp
import numpy as np
from jax import lax
from jax.experimental import pallas as pl
from jax.experimental.pallas import tpu as pltpu

D_MODEL = 1024
D_RNN = 1024
RNN_BLOCKS = 16
RNN_BW = D_RNN // RNN_BLOCKS
CONV_W = 4
LRU_C = 8.0
N_HEADS = 16
N_KV = 4
HEAD_DIM = 64
GROUP = N_HEADS // N_KV
WINDOW = 128
Q_DIM = N_HEADS * HEAD_DIM
KV_DIM = N_KV * HEAD_DIM
N_EXPERTS = 32
TOP_K = 4
D_FF = 1024
SWIGLU_LIMIT = 7.0
SWIGLU_ALPHA = 1.702
EPS = 1e-6
D_IN = 2 * D_RNN + Q_DIM + 2 * KV_DIM + 2 * D_MODEL
D_REST = D_IN - 2 * D_RNN

V7X_LANES = 128
V7X_SUBLANES = 8
V7X_MXU_DIM = 256
V7X_VMEM_BYTES = 64 * 1024 * 1024

F32 = jnp.float32
BF16 = jnp.bfloat16


def _tiles():
    return dict(
        ada_tn=1024,
        tok=512,
        attn_q=WINDOW,
        moe_t=256,
        moe_bm=256,
        rnn_group=V7X_MXU_DIM,
    )


def _vmem_limit(nbytes):
    return int(min(max(nbytes, 16 * 1024 * 1024), V7X_VMEM_BYTES - 8 * 1024 * 1024))


def _rms(x, g):
    return x * lax.rsqrt(jnp.mean(x * x, axis=-1, keepdims=True) + EPS) * g


def _ada_kernel(ct_ref, w_ref, b_ref, o_ref):
    ct = ct_ref[...]
    sc = ct * jax.nn.sigmoid(ct)
    w = w_ref[...]
    rows = [jnp.sum(w * sc[:, b:b + 1], axis=0, keepdims=True) for b in range(ct.shape[1])]
    o_ref[...] = jnp.concatenate(rows, axis=0) + b_ref[...]


def _ada(c, w_ada, b_ada):
    B, D = c.shape
    n_out = w_ada.shape[1]
    tn = _tiles()["ada_tn"]
    return pl.pallas_call(
        _ada_kernel,
        grid=(n_out // tn,),
        in_specs=[
            pl.BlockSpec((D, B), lambda j: (0, 0)),
            pl.BlockSpec((D, tn), lambda j: (0, j)),
            pl.BlockSpec((1, tn), lambda j: (0, j)),
        ],
        out_specs=pl.BlockSpec((B, tn), lambda j: (0, j)),
        out_shape=jax.ShapeDtypeStruct((B, n_out), F32),
        compiler_params=pltpu.CompilerParams(
            dimension_semantics=("arbitrary",),
            vmem_limit_bytes=_vmem_limit(4 * D * tn * 4)),
        name="ada",
    )(c.T, w_ada, b_ada.reshape(1, n_out))


def _inproj_kernel(x_ref, g_ref, sc_ref, sh_ref, w_ref, b_ref, rnn_ref, rest_ref, *, chunk):
    x = x_ref[...]
    h = _rms(x, g_ref[...]) * (1.0 + sc_ref[0]) + sh_ref[0]
    hb = h.astype(BF16)
    n_rnn = rnn_ref.shape[1]
    for c0 in range(0, w_ref.shape[1], chunk):
        acc = jnp.dot(hb, w_ref[:, c0:c0 + chunk], preferred_element_type=F32)
        acc = acc + b_ref[:, c0:c0 + chunk]
        if c0 < n_rnn:
            rnn_ref[:, c0:c0 + chunk] = acc
        else:
            rest_ref[:, c0 - n_rnn:c0 - n_rnn + chunk] = acc.astype(BF16)


def _inproj(x2, g, sc, sh, w_bf, b_in, S):
    N, D = x2.shape
    tm = _tiles()["tok"]
    per_b = S // tm
    chunk = 2 * V7X_MXU_DIM
    vmem = D * D_IN * 2 + 2 * tm * (D * 4 + 2 * D_RNN * 4 + D_REST * 2) + 4 * tm * chunk * 4
    return pl.pallas_call(
        functools.partial(_inproj_kernel, chunk=chunk),
        grid=(N // tm,),
        in_specs=[
            pl.BlockSpec((tm, D), lambda i: (i, 0)),
            pl.BlockSpec((1, D), lambda i: (0, 0)),
            pl.BlockSpec((1, 1, D), lambda i: (i // per_b, 0, 0)),
            pl.BlockSpec((1, 1, D), lambda i: (i // per_b, 0, 0)),
            pl.BlockSpec((D, D_IN), lambda i: (0, 0), pipeline_mode=pl.Buffered(1)),
            pl.BlockSpec((1, D_IN), lambda i: (0, 0)),
        ],
        out_specs=[
            pl.BlockSpec((tm, 2 * D_RNN), lambda i: (i, 0)),
            pl.BlockSpec((tm, D_REST), lambda i: (i, 0)),
        ],
        out_shape=[
            jax.ShapeDtypeStruct((N, 2 * D_RNN), F32),
            jax.ShapeDtypeStruct((N, D_REST), BF16),
        ],
        compiler_params=pltpu.CompilerParams(
            dimension_semantics=("arbitrary",),
            vmem_limit_bytes=_vmem_limit(vmem + 8 * 1024 * 1024)),
        name="inproj",
    )(x2, g, sc, sh, w_bf, b_in)


def _gelu_tanh(x):
    return 0.5 * x * (1.0 + jnp.tanh(math.sqrt(2.0 / math.pi) * (x + 0.044715 * (x * x * x))))


def _softplus(z):
    return jnp.maximum(z, 0.0) + jnp.log1p(jnp.exp(-jnp.abs(z)))


def _sigmoid(x):
    return 0.5 * jnp.tanh(0.5 * x) + 0.5


def _rnn_kernel(xr_ref, gr_ref, cw_ref, cb_ref, wa_ref, ba_ref, wx_ref, bx_ref, lam_ref,
                y_ref, xbuf, a_s, b_s, h_s, ga_s, gb_s, gc_s, carry, *, per_b, gw):
    t = pl.program_id(0) % per_b
    tt = xr_ref.shape[0]
    halo = V7X_SUBLANES

    ng = tt // V7X_SUBLANES
    n_lt = a_s.shape[0]
    per_g = gw // V7X_LANES
    slab = lambda j, r: (j, pl.ds(r, ng, stride=V7X_SUBLANES), slice(None))

    @pl.when(t == 0)
    def _():
        xbuf[:, 0:halo, :] = jnp.zeros((n_lt, halo, V7X_LANES), F32)
        carry[...] = jnp.zeros_like(carry)

    for j in range(n_lt):
        ls = slice(j * V7X_LANES, (j + 1) * V7X_LANES)
        xbuf[j, halo:halo + tt, :] = xr_ref[:, ls]
        taps = {o: xbuf[slab(j, halo + o)] for o in range(-(CONV_W - 1), V7X_SUBLANES)}
        for r in range(V7X_SUBLANES):
            acc = cb_ref[:, ls] + taps[r - (CONV_W - 1)] * cw_ref[0:1, ls]
            for kk in range(1, CONV_W):
                acc = acc + taps[r - (CONV_W - 1) + kk] * cw_ref[kk:kk + 1, ls]
            h_s[slab(j, r)] = acc
        xbuf[j, 0:halo, :] = xbuf[j, tt:tt + halo, :]

    sp = _softplus(-lam_ref[...])
    reset = jnp.logical_and(t == 0, lax.broadcasted_iota(jnp.int32, (tt, gw), 0) == 0)
    for g in range(n_lt // per_g):
        cs = slice(g * gw, (g + 1) * gw)
        xc = jnp.concatenate([h_s[g * per_g + j] for j in range(per_g)], axis=1)
        xg = xc.astype(BF16)
        gate_r = _sigmoid(jnp.dot(xg, wa_ref[g], preferred_element_type=F32) + ba_ref[:, cs])
        gate_i = _sigmoid(jnp.dot(xg, wx_ref[g], preferred_element_type=F32) + bx_ref[:, cs])
        a = jnp.exp(-LRU_C * gate_r * sp[:, cs])
        mult = jnp.where(reset, 1.0, jnp.sqrt((1.0 - a) * (1.0 + a)))
        bt = (xc * gate_i) * mult
        for j in range(per_g):
            a_s[g * per_g + j] = a[:, j * V7X_LANES:(j + 1) * V7X_LANES]
            b_s[g * per_g + j] = bt[:, j * V7X_LANES:(j + 1) * V7X_LANES]

    for j in range(n_lt):
        ls = slice(j * V7X_LANES, (j + 1) * V7X_LANES)
        acc_a = a_s[slab(j, 0)]
        acc_h = b_s[slab(j, 0)]
        h_s[slab(j, 0)] = acc_h
        for r in range(1, V7X_SUBLANES):
            a_r = a_s[slab(j, r)]
            acc_h = a_r * acc_h + b_s[slab(j, r)]
            acc_a = a_r * acc_a
            h_s[slab(j, r)] = acc_h
            a_s[slab(j, r)] = acc_a
        ga_s[:, ls] = acc_a
        gb_s[:, ls] = acc_h

    def across(gi, h_prev):
        gc_s[pl.ds(gi, 1), :] = h_prev
        return ga_s[pl.ds(gi, 1), :] * h_prev + gb_s[pl.ds(gi, 1), :]

    carry[...] = lax.fori_loop(0, ng, across, carry[...])
    for j in range(n_lt):
        ls = slice(j * V7X_LANES, (j + 1) * V7X_LANES)
        h_in = gc_s[:, ls]
        for r in range(V7X_SUBLANES):
            h_s[slab(j, r)] = a_s[slab(j, r)] * h_in + h_s[slab(j, r)]
        y_ref[:, ls] = (h_s[j] * _gelu_tanh(gr_ref[:, ls])).astype(BF16)


def _block_diag_tiles(w, gw):
    nb, bw, _ = w.shape
    per = gw // bw
    w4 = w.reshape(nb // per, per, bw, bw)
    eye = jnp.eye(per, dtype=w.dtype)
    return jnp.einsum("gpij,pq->gpiqj", w4, eye).reshape(nb // per, gw, gw)


def _rnn(proj_rnn, conv_w, conv_b, wa, ba, wx, bx, lam, S):
    N = proj_rnn.shape[0]
    C = D_RNN
    tt = _tiles()["tok"]
    gw = _tiles()["rnn_group"]
    per_b = S // tt
    vec = lambda: pl.BlockSpec((1, C), lambda i: (0, 0))
    return pl.pallas_call(
        functools.partial(_rnn_kernel, per_b=per_b, gw=gw),
        grid=(N // tt,),
        in_specs=[
            pl.BlockSpec((tt, C), lambda i: (i, 0)),
            pl.BlockSpec((tt, C), lambda i: (i, 1)),
            pl.BlockSpec((CONV_W, C), lambda i: (0, 0)),
            vec(),
            pl.BlockSpec((C // gw, gw, gw), lambda i: (0, 0, 0)),
            vec(),
            pl.BlockSpec((C // gw, gw, gw), lambda i: (0, 0, 0)),
            vec(),
            vec(),
        ],
        out_specs=pl.BlockSpec((tt, C), lambda i: (i, 0)),
        out_shape=jax.ShapeDtypeStruct((N, C), BF16),
        scratch_shapes=[
            pltpu.VMEM((C // V7X_LANES, tt + V7X_SUBLANES, V7X_LANES), F32),
            pltpu.VMEM((C // V7X_LANES, tt, V7X_LANES), F32),
            pltpu.VMEM((C // V7X_LANES, tt, V7X_LANES), F32),
            pltpu.VMEM((C // V7X_LANES, tt, V7X_LANES), F32),
            pltpu.VMEM((tt // V7X_SUBLANES, C), F32),
            pltpu.VMEM((tt // V7X_SUBLANES, C), F32),
            pltpu.VMEM((tt // V7X_SUBLANES, C), F32),
            pltpu.VMEM((1, C), F32),
        ],
        compiler_params=pltpu.CompilerParams(
            dimension_semantics=("arbitrary",),
            vmem_limit_bytes=_vmem_limit(16 * tt * C * 4)),
        name="rnn",
    )(proj_rnn, proj_rnn, conv_w, conv_b, wa, ba, wx, bx, lam)


def _alibi_slopes():
    return [2.0 ** (-8.0 * (h + 1) / N_HEADS) for h in range(N_HEADS)]


def _attn_kernel(sink_ref, q_ref, kp_ref, kc_ref, vp_ref, vc_ref, o_ref, bias_s, *, per_b):
    blk = pl.program_id(0) % per_b
    bq = q_ref.shape[0]
    slopes = _alibi_slopes()
    scale = HEAD_DIM ** -0.5
    assert math.log2(scale).is_integer()

    @pl.when(pl.program_id(0) == 0)
    def _():
        qi = lax.broadcasted_iota(jnp.int32, (bq, 2 * bq), 0)
        ci = lax.broadcasted_iota(jnp.int32, (bq, 2 * bq), 1)
        dist = qi + bq - ci
        valid = (dist >= 0) & (dist < WINDOW)
        distf = dist.astype(F32)
        for h in range(N_HEADS):
            b = jnp.where(valid, -slopes[h] * distf, -jnp.inf)
            bias_s[0, h] = jnp.where(ci >= bq, b, -jnp.inf)
            bias_s[1, h] = b

    table = jnp.minimum(blk, 1)
    outs = []
    for kv in range(N_KV):
        ks = slice(kv * HEAD_DIM, (kv + 1) * HEAD_DIM)
        kb = jnp.concatenate([kp_ref[:, ks], kc_ref[:, ks]], axis=0)
        vb = jnp.concatenate([vp_ref[:, ks], vc_ref[:, ks]], axis=0)
        for g in range(GROUP):
            h = kv * GROUP + g
            qh = q_ref[:, h * HEAD_DIM:(h + 1) * HEAD_DIM] * jnp.asarray(scale, BF16)
            s = lax.dot_general(qh, kb, (((1,), (1,)), ((), ())), preferred_element_type=F32)
            s = s + bias_s[table, h]
            sink = sink_ref[h]
            m = jnp.maximum(jnp.max(s, axis=-1, keepdims=True), sink)
            p = jnp.exp(s - m)
            denom = jnp.sum(p, axis=-1, keepdims=True) + jnp.exp(sink - m)
            o = jnp.dot(p.astype(BF16), vb, preferred_element_type=F32) / denom
            outs.append(o)
            if len(outs) * HEAD_DIM == V7X_LANES:
                c0 = (h + 1) * HEAD_DIM - V7X_LANES
                o_ref[:, c0:c0 + V7X_LANES] = jnp.concatenate(outs, axis=-1).astype(BF16)
                outs = []


def _attn(proj_rest, sinks, S):
    N = proj_rest.shape[0]
    bq = _tiles()["attn_q"]
    per_b = S // bq
    k_col = (Q_DIM + 2 * D_MODEL) // KV_DIM
    v_col = k_col + 1

    def prev(i):
        return jnp.where(i % per_b == 0, i, i - 1)

    return pl.pallas_call(
        functools.partial(_attn_kernel, per_b=per_b),
        grid=(N // bq,),
        in_specs=[
            pl.BlockSpec(memory_space=pltpu.SMEM),
            pl.BlockSpec((bq, Q_DIM), lambda i: (i, 0)),
            pl.BlockSpec((bq, KV_DIM), lambda i: (prev(i), k_col)),
            pl.BlockSpec((bq, KV_DIM), lambda i: (i, k_col)),
            pl.BlockSpec((bq, KV_DIM), lambda i: (prev(i), v_col)),
            pl.BlockSpec((bq, KV_DIM), lambda i: (i, v_col)),
        ],
        out_specs=pl.BlockSpec((bq, Q_DIM), lambda i: (i, 0)),
        out_shape=jax.ShapeDtypeStruct((N, Q_DIM), BF16),
        scratch_shapes=[pltpu.VMEM((2, N_HEADS, bq, 2 * bq), F32)],
        compiler_params=pltpu.CompilerParams(
            dimension_semantics=("arbitrary",),
            vmem_limit_bytes=_vmem_limit(3 * 2 * N_HEADS * bq * 2 * bq * 4)),
        name="attn",
    )(sinks, proj_rest, proj_rest, proj_rest, proj_rest, proj_rest)


def _merge_kernel(x_ref, yr_ref, ya_ref, gr_ref, ga_ref, wr_ref, wa_ref, wo_ref,
                  gpost_ref, g1_ref, gpre_ref, sc2_ref, sh2_ref, rwt_ref, rb_ref,
                  x1_ref, h2_ref, lg_ref):
    r = jnp.dot(yr_ref[...], wr_ref[...], preferred_element_type=F32)
    a = jnp.dot(ya_ref[...], wa_ref[...], preferred_element_type=F32)
    merged = (_sigmoid(gr_ref[...].astype(F32)) * r
              + _sigmoid(ga_ref[...].astype(F32)) * a)
    mix = jnp.dot(merged.astype(BF16), wo_ref[...], preferred_element_type=F32)
    x1 = x_ref[...] + g1_ref[0] * _rms(mix, gpost_ref[...])
    x1_ref[...] = x1
    h2 = _rms(x1, gpre_ref[...]) * (1.0 + sc2_ref[0]) + sh2_ref[0]
    h2_ref[...] = h2
    lg = lax.dot_general(rwt_ref[...], h2.astype(BF16), (((1,), (1,)), ((), ())),
                         preferred_element_type=F32)
    lg_ref[...] = lg + rb_ref[...]


def _merge(x2, y_rnn, y_att, proj_rest, wr, wa, wo, gpost, g1, gpre, sc2, sh2, rwt, rb, S):
    N, D = x2.shape
    tm = _tiles()["tok"]
    per_b = S // tm
    gate_r_col = Q_DIM // D
    mat = lambda: pl.BlockSpec((D, D), lambda i: (0, 0))
    vec = lambda: pl.BlockSpec((1, D), lambda i: (0, 0))
    bvec = lambda: pl.BlockSpec((1, 1, D), lambda i: (i // per_b, 0, 0))
    tile = lambda col=0: pl.BlockSpec((tm, D), lambda i: (i, col))
    return pl.pallas_call(
        _merge_kernel,
        grid=(N // tm,),
        in_specs=[
            tile(), tile(), tile(), tile(gate_r_col), tile(gate_r_col + 1),
            mat(), mat(), mat(),
            vec(), bvec(), vec(), bvec(), bvec(),
            pl.BlockSpec((N_EXPERTS, D), lambda i: (0, 0)),
            pl.BlockSpec((N_EXPERTS, 1), lambda i: (0, 0)),
        ],
        out_specs=[
            tile(), tile(),
            pl.BlockSpec((N_EXPERTS, tm), lambda i: (0, i)),
        ],
        out_shape=[
            jax.ShapeDtypeStruct((N, D), F32),
            jax.ShapeDtypeStruct((N, D), F32),
            jax.ShapeDtypeStruct((N_EXPERTS, N), F32),
        ],
        compiler_params=pltpu.CompilerParams(
            dimension_semantics=("arbitrary",),
            vmem_limit_bytes=_vmem_limit(6 * D * D * 2 + 24 * tm * D * 4)),
        name="merge",
    )(x2, y_rnn, y_att, proj_rest, proj_rest, wr, wa, wo, gpost, g1, gpre, sc2, sh2, rwt, rb)


def _route_kernel(lg_ref, g_ref, slot_ref, cnt_ref, tcnt_ref, tcar_ref, tlst_ref, carry):
    i = pl.program_id(0)

    @pl.when(i == 0)
    def _():
        carry[...] = jnp.zeros_like(carry)
        tcnt_ref[...] = jnp.zeros_like(tcnt_ref)
        tcar_ref[...] = jnp.zeros_like(tcar_ref)
        tlst_ref[...] = jnp.zeros_like(tlst_ref)

    l = lg_ref[...]
    E, T = l.shape
    row = lax.broadcasted_iota(jnp.int32, (E, T), 0).astype(F32)
    vals, idxs = [], []
    for _ in range(TOP_K):
        m = jnp.max(l, axis=0, keepdims=True)
        idx = jnp.min(jnp.where(l == m, row, float(E)), axis=0, keepdims=True)
        vals.append(m)
        idxs.append(idx)
        l = jnp.where(row == idx, -jnp.inf, l)
    ex = [jnp.exp(v - vals[0]) for v in vals]
    tot = ex[0]
    for e in ex[1:]:
        tot = tot + e
    g_ref[...] = jnp.concatenate([e / tot for e in ex], axis=0)

    hot = [row == idx for idx in idxs]
    onehot = jnp.zeros((E, T), F32)
    for hk in hot:
        onehot = onehot + hk.astype(F32)
    tri_t = (lax.broadcasted_iota(jnp.int32, (T, T), 0)
             < lax.broadcasted_iota(jnp.int32, (T, T), 1)).astype(BF16)
    before = jnp.dot(onehot.astype(BF16), tri_t, preferred_element_type=F32)
    cnt = jnp.sum(onehot, axis=1, keepdims=True)
    tri_e = (lax.broadcasted_iota(jnp.int32, (E, E), 1)
             < lax.broadcasted_iota(jnp.int32, (E, E), 0)).astype(BF16)
    lstart = jnp.dot(tri_e, jnp.broadcast_to(cnt, (E, V7X_LANES)).astype(BF16),
                     preferred_element_type=F32)[:, 0:1]
    local = before + lstart
    slots = [jnp.sum(jnp.where(hk, local, 0.0), axis=0, keepdims=True) for hk in hot]
    slot_ref[...] = jnp.concatenate(slots, axis=0).astype(jnp.int32)

    mine = lax.broadcasted_iota(jnp.int32, tcnt_ref.shape, 1) == i
    tcnt_ref[...] = jnp.where(mine, cnt.astype(jnp.int32), tcnt_ref[...])
    tcar_ref[...] = jnp.where(mine, carry[...].astype(jnp.int32), tcar_ref[...])
    tlst_ref[...] = jnp.where(mine, lstart.astype(jnp.int32), tlst_ref[...])
    total = carry[...] + cnt
    carry[...] = total
    cnt_ref[...] = total.astype(jnp.int32)


def _route(logits_t):
    E, N = logits_t.shape
    T = _tiles()["moe_t"]
    assert T <= 2 ** 8 and N // T <= V7X_LANES
    out = lambda: pl.BlockSpec((TOP_K, T), lambda i: (0, i))
    per_tile = lambda: pl.BlockSpec((E, V7X_LANES), lambda i: (0, 0))
    return pl.pallas_call(
        _route_kernel,
        grid=(N // T,),
        in_specs=[pl.BlockSpec((E, T), lambda i: (0, i))],
        out_specs=[out(), out(), pl.BlockSpec((E, 1), lambda i: (0, 0)),
                   per_tile(), per_tile(), per_tile()],
        out_shape=[
            jax.ShapeDtypeStruct((TOP_K, N), F32),
            jax.ShapeDtypeStruct((TOP_K, N), jnp.int32),
            jax.ShapeDtypeStruct((E, 1), jnp.int32),
            jax.ShapeDtypeStruct((E, V7X_LANES), jnp.int32),
            jax.ShapeDtypeStruct((E, V7X_LANES), jnp.int32),
            jax.ShapeDtypeStruct((E, V7X_LANES), jnp.int32),
        ],
        scratch_shapes=[pltpu.VMEM((E, 1), F32)],
        compiler_params=pltpu.CompilerParams(dimension_semantics=("arbitrary",)),
        name="route",
    )(logits_t)


def _run_sizes(T):
    return [2 ** b for b in range(int(math.log2(T)), -1, -1)]


def _copy_plan(tile_cnt, tile_lst, tile_dst, T):
    sizes = jnp.asarray(_run_sizes(T), jnp.int32)[None, :, None]
    cnt, lst, dst = (a[:, None, :] for a in (tile_cnt, tile_lst, tile_dst))
    has = (cnt & sizes) != 0
    off = (cnt // (2 * sizes)) * (2 * sizes)
    place = jnp.cumsum(has.astype(jnp.int32), axis=-1) - has
    hit = has[..., :, None] & (place[..., :, None] == jnp.arange(N_EXPERTS))
    pick = lambda v: jnp.sum(jnp.where(hit, v[..., :, None], 0), axis=-2)
    n_tiles, nb = tile_cnt.shape[0], sizes.shape[1]
    parts = [pick(jnp.broadcast_to(lst + off, has.shape)).reshape(n_tiles, -1),
             pick(jnp.broadcast_to(dst + off, has.shape)).reshape(n_tiles, -1),
             jnp.sum(has, axis=-1).reshape(n_tiles, nb)]
    plan = jnp.concatenate(parts, axis=1).astype(jnp.int32)
    pad = _plan_len(T) - plan.shape[1]
    return jnp.pad(plan, ((0, 0), (0, pad))).reshape(-1)


def _plan_len(T):
    nb = len(_run_sizes(T))
    return max(V7X_LANES, pl.next_power_of_2(2 * nb * N_EXPERTS + nb))


def _run_copies(plan_ref, T, make):
    sizes = _run_sizes(T)
    nb = len(sizes)
    for bi, b in enumerate(sizes):
        def body(j, c, bi=bi, b=b):
            make(plan_ref[bi * N_EXPERTS + j], plan_ref[(nb + bi) * N_EXPERTS + j], b).start()
            return c

        lax.fori_loop(0, plan_ref[2 * nb * N_EXPERTS + bi], body, 0)


def _row_slab(ref, row, nrows, lead=()):
    rows = pl.ds(pl.multiple_of(row * V7X_SUBLANES, V7X_SUBLANES), nrows * V7X_SUBLANES)
    return ref.at[(*lead, rows)]


def _to_slabs(ref, val, lead=()):
    rows = val.shape[0]
    for s in range(V7X_SUBLANES):
        ref[(*lead, pl.ds(s, rows, stride=V7X_SUBLANES), slice(None))] = (
            val[:, s * V7X_LANES:(s + 1) * V7X_LANES])


def _from_slabs(ref, lead=()):
    rows = ref.shape[-2] // V7X_SUBLANES
    return jnp.concatenate(
        [ref[(*lead, pl.ds(s, rows, stride=V7X_SUBLANES), slice(None))]
         for s in range(V7X_SUBLANES)], axis=1)


def _dispatch_kernel(tail_ref, nu_ref, plan_ref, slot_ref, h_ref, xs_ref, buf, zbuf, sem, zsem):
    i = pl.program_id(0)
    T = h_ref.shape[0]
    bm = zbuf.shape[0] // V7X_SUBLANES
    nblk = xs_ref.shape[0] // zbuf.shape[0]
    cur = i % 2

    def zero_copy(row0):
        return pltpu.make_async_copy(zbuf, _row_slab(xs_ref, row0, bm), zsem)

    @pl.when(i == 0)
    def _():
        zbuf[...] = jnp.zeros_like(zbuf)

        def start(e, c):
            zero_copy(tail_ref[e]).start()
            return c

        def wait(e, c):
            zero_copy(tail_ref[e]).wait()
            return c

        def start_unused(j, c):
            zero_copy(j * bm).start()
            return c

        def wait_unused(j, c):
            zero_copy(j * bm).wait()
            return c

        lax.fori_loop(0, N_EXPERTS, start, 0)
        lax.fori_loop(nu_ref[0], nblk, start_unused, 0)
        lax.fori_loop(0, N_EXPERTS, wait, 0)
        lax.fori_loop(nu_ref[0], nblk, wait_unused, 0)

    slot = slot_ref[...]
    r_id = lax.broadcasted_iota(jnp.int32, (TOP_K * T, T), 0)
    perm = r_id == slot[0:1, :]
    for k in range(1, TOP_K):
        perm = jnp.logical_or(perm, r_id == slot[k:k + 1, :])
    grouped = jnp.dot(perm.astype(BF16), h_ref[...].astype(BF16), preferred_element_type=F32)
    _to_slabs(buf, grouped, lead=(cur,))

    def wait_tile(b):
        pltpu.make_async_copy(buf.at[b], _row_slab(xs_ref, 0, TOP_K * T), sem.at[b]).wait()

    @pl.when(i > 0)
    def _():
        wait_tile(1 - cur)

    _run_copies(plan_ref, T, lambda l, d, n: pltpu.make_async_copy(
        _row_slab(buf, l, n, lead=(cur,)), _row_slab(xs_ref, d, n), sem.at[cur]))

    @pl.when(i == pl.num_programs(0) - 1)
    def _():
        wait_tile(cur)


def _dispatch(h2, slot, plan, tail_start, n_used, n_rows, bm):
    N, D = h2.shape
    T = _tiles()["moe_t"]
    S8 = V7X_SUBLANES
    imap = lambda f: (lambda i, *_: f(i))
    return pl.pallas_call(
        _dispatch_kernel,
        grid_spec=pltpu.PrefetchScalarGridSpec(
            num_scalar_prefetch=2,
            grid=(N // T,),
            in_specs=[
                pl.BlockSpec((_plan_len(T),), imap(lambda i: (i,)), memory_space=pltpu.SMEM),
                pl.BlockSpec((TOP_K, T), imap(lambda i: (0, i))),
                pl.BlockSpec((T, D), imap(lambda i: (i, 0))),
            ],
            out_specs=pl.BlockSpec(memory_space=pl.ANY),
            scratch_shapes=[
                pltpu.VMEM((2, TOP_K * T * S8, V7X_LANES), F32),
                pltpu.VMEM((bm * S8, V7X_LANES), F32),
                pltpu.SemaphoreType.DMA((2,)),
                pltpu.SemaphoreType.DMA(()),
            ],
        ),
        out_shape=jax.ShapeDtypeStruct((n_rows * S8, V7X_LANES), F32),
        compiler_params=pltpu.CompilerParams(
            dimension_semantics=("arbitrary",),
            vmem_limit_bytes=_vmem_limit(8 * TOP_K * T * D * 4)),
        name="dispatch",
    )(tail_start, n_used, plan, slot, h2)


def _ffn_kernel(be_ref, bv_ref, bf_ref, nu_ref, nxt_ref, x_ref, w1_hbm, b1g_ref, b1l_ref, w2_hbm,
                b2_ref, y_ref, w1f_s, w2f_s, w1g_s, w1l_s, w2_s, sem):
    i = pl.program_id(0)
    bm, D = x_ref.shape[0] // V7X_SUBLANES, w2_s.shape[1]
    pw = 2 * V7X_LANES

    def fetch(e):
        return (pltpu.make_async_copy(w1_hbm.at[e], w1f_s, sem.at[0]),
                pltpu.make_async_copy(w2_hbm.at[e], w2f_s, sem.at[1]))

    @pl.when(i == 0)
    def _():
        for cp in fetch(be_ref[0]):
            cp.start()

    @pl.when(jnp.logical_and(i < nu_ref[0], bf_ref[i] == 1))
    def _():
        e = be_ref[i]
        for cp in fetch(e):
            cp.wait()
        src = lax.broadcasted_iota(jnp.int32, (pw, pw), 0)
        dst = lax.broadcasted_iota(jnp.int32, (pw, pw), 1)
        want = jnp.where(dst < V7X_LANES, 2 * dst, 2 * (dst - V7X_LANES) + 1)
        perm = (src == want).astype(BF16)
        for c in range(w1f_s.shape[1] // pw):
            wp = jnp.dot(w1f_s[:, c * pw:(c + 1) * pw].astype(BF16), perm,
                         preferred_element_type=F32).astype(BF16)
            w1g_s[:, c * V7X_LANES:(c + 1) * V7X_LANES] = wp[:, :V7X_LANES]
            w1l_s[:, c * V7X_LANES:(c + 1) * V7X_LANES] = wp[:, V7X_LANES:]
        w2_s[...] = w2f_s[...].astype(BF16)

        @pl.when(nxt_ref[e] >= 0)
        def _():
            for cp in fetch(nxt_ref[e]):
                cp.start()

    @pl.when(i < nu_ref[0])
    def _():
        rows = lax.broadcasted_iota(jnp.int32, (bm, D), 0)
        x = jnp.where(rows < bv_ref[i], _from_slabs(x_ref), 0.0).astype(BF16)
        glu = jnp.dot(x, w1g_s[...], preferred_element_type=F32) + b1g_ref[0]
        lin = jnp.dot(x, w1l_s[...], preferred_element_type=F32) + b1l_ref[0]
        glu = jnp.minimum(glu, SWIGLU_LIMIT)
        lin = jnp.clip(lin, -SWIGLU_LIMIT, SWIGLU_LIMIT)
        act = glu * _sigmoid(SWIGLU_ALPHA * glu) * (lin + 1.0)
        _to_slabs(y_ref, jnp.dot(act.astype(BF16), w2_s[...], preferred_element_type=F32) + b2_ref[0])

    @pl.when(i >= nu_ref[0])
    def _():
        y_ref[...] = jnp.zeros_like(y_ref)


def _ffn(xs, blk_e, blk_valid, blk_first, n_used, next_e, w1, b1g, b1l, w2, b2, bm):
    E, D, F2 = w1.shape
    F = F2 // 2
    slab = bm * V7X_SUBLANES
    nblk = xs.shape[0] // slab

    def row_blk(i, be, bv, bf, nu, nxt):
        return (jnp.minimum(i, nu[0] - 1), 0)

    def per_e(i, be, bv, bf, nu, nxt):
        return (be[i], 0, 0)

    vmem = (D * F2 + F * D) * (4 + 2) + 8 * bm * F2 * 4
    return pl.pallas_call(
        _ffn_kernel,
        grid_spec=pltpu.PrefetchScalarGridSpec(
            num_scalar_prefetch=5,
            grid=(nblk,),
            in_specs=[
                pl.BlockSpec((slab, V7X_LANES), row_blk),
                pl.BlockSpec(memory_space=pl.ANY),
                pl.BlockSpec((1, 1, F), per_e),
                pl.BlockSpec((1, 1, F), per_e),
                pl.BlockSpec(memory_space=pl.ANY),
                pl.BlockSpec((1, 1, D), per_e),
            ],
            out_specs=pl.BlockSpec((slab, V7X_LANES), lambda i, be, bv, bf, nu, nxt: (i, 0)),
            scratch_shapes=[
                pltpu.VMEM((D, F2), F32),
                pltpu.VMEM((F, D), F32),
                pltpu.VMEM((D, F), BF16),
                pltpu.VMEM((D, F), BF16),
                pltpu.VMEM((F, D), BF16),
                pltpu.SemaphoreType.DMA((2,)),
            ],
        ),
        out_shape=jax.ShapeDtypeStruct(xs.shape, F32),
        compiler_params=pltpu.CompilerParams(
            dimension_semantics=("arbitrary",),
            vmem_limit_bytes=_vmem_limit(vmem)),
        name="ffn",
    )(blk_e, blk_valid, blk_first, n_used, next_e, xs, w1, b1g, b1l, w2, b2)


def _split_bf16(v):
    hi = v.astype(BF16)
    return hi, (v - hi.astype(F32)).astype(BF16)


def _combine_kernel(plan_ref, plan_next_ref, ys_ref, x1_ref, slot_ref, gate_ref, gpost_ref, g2_ref,
                    o_ref, buf, sem):
    i = pl.program_id(0)
    T = x1_ref.shape[0]
    cur = i % 2

    def gather(plan, b):
        _run_copies(plan, T, lambda l, d, n: pltpu.make_async_copy(
            _row_slab(ys_ref, d, n), _row_slab(buf, l, n, lead=(b,)), sem.at[b]))

    @pl.when(i == 0)
    def _():
        gather(plan_ref, cur)

    @pl.when(i + 1 < pl.num_programs(0))
    def _():
        gather(plan_next_ref, 1 - cur)

    pltpu.make_async_copy(_row_slab(ys_ref, 0, TOP_K * T), buf.at[cur], sem.at[cur]).wait()

    slot = slot_ref[...]
    gates = gate_ref[...]
    r_id = lax.broadcasted_iota(jnp.int32, (T, TOP_K * T), 1)
    w = jnp.zeros((T, TOP_K * T), F32)
    for k in range(TOP_K):
        w = jnp.where(r_id == slot[:, k:k + 1], gates[:, k:k + 1], w)
    w_hi, w_lo = _split_bf16(w)
    y_hi, y_lo = _split_bf16(_from_slabs(buf, lead=(cur,)))
    ff = (jnp.dot(w_hi, y_hi, preferred_element_type=F32)
          + (jnp.dot(w_hi, y_lo, preferred_element_type=F32)
             + jnp.dot(w_lo, y_hi, preferred_element_type=F32)))
    o_ref[...] = x1_ref[...] + g2_ref[0] * _rms(ff, gpost_ref[...])


def _combine(ys, plan, x1, slot_tk, gates_tk, gpost, g2, S):
    N, D = x1.shape
    T = _tiles()["moe_t"]
    per_b = S // T
    n_tiles = N // T
    plan_spec = lambda f: pl.BlockSpec((_plan_len(T),), f, memory_space=pltpu.SMEM)
    return pl.pallas_call(
        _combine_kernel,
        grid=(n_tiles,),
        in_specs=[
            plan_spec(lambda i: (i,)),
            plan_spec(lambda i: (jnp.minimum(i + 1, n_tiles - 1),)),
            pl.BlockSpec(memory_space=pl.ANY),
            pl.BlockSpec((T, D), lambda i: (i, 0)),
            pl.BlockSpec((T, TOP_K), lambda i: (i, 0)),
            pl.BlockSpec((T, TOP_K), lambda i: (i, 0)),
            pl.BlockSpec((1, D), lambda i: (0, 0)),
            pl.BlockSpec((1, 1, D), lambda i: (i // per_b, 0, 0)),
        ],
        out_specs=pl.BlockSpec((T, D), lambda i: (i, 0)),
        out_shape=jax.ShapeDtypeStruct((N, D), F32),
        scratch_shapes=[
            pltpu.VMEM((2, TOP_K * T * V7X_SUBLANES, V7X_LANES), F32),
            pltpu.SemaphoreType.DMA((2,)),
        ],
        compiler_params=pltpu.CompilerParams(
            dimension_semantics=("arbitrary",),
            vmem_limit_bytes=_vmem_limit(10 * TOP_K * T * D * 4)),
        name="combine",
    )(plan, plan, ys, x1, slot_tk, gates_tk, gpost, g2)


def _rest_order(w):
    xr, gr, q, k, v, mg_r, mg_a = jnp.split(
        w, np.cumsum([D_RNN, D_RNN, Q_DIM, KV_DIM, KV_DIM, D_MODEL]).tolist(), axis=-1)
    return jnp.concatenate([xr, gr, q, mg_r, mg_a, k, v], axis=-1)


def _layer(x2, c, B, S, p):
    D = D_MODEL
    N = B * S
    ada = _ada(c, p["w_ada"], p["b_ada"])
    sh1, sc1, g1, sh2, sc2, g2 = [a.reshape(B, 1, D) for a in jnp.split(ada, 6, axis=-1)]
    row = lambda v: v.reshape(1, -1)

    proj_rnn, proj_rest = _inproj(x2, row(p["norm_pre_mix"]), sc1, sh1,
                                  _rest_order(p["w_in"]).astype(BF16),
                                  _rest_order(row(p["b_in"])), S)

    gw = _tiles()["rnn_group"]
    y_rnn = _rnn(proj_rnn, p["conv_w"], row(p["conv_b"]),
                 _block_diag_tiles(p["rg_w_a"], gw).astype(BF16), row(p["rg_b_a"]),
                 _block_diag_tiles(p["rg_w_x"], gw).astype(BF16), row(p["rg_b_x"]),
                 row(p["rg_lambda"]), S)
    y_att = _attn(proj_rest, p["attn_sinks"], S)

    x1, h2, logits_t = _merge(
        x2, y_rnn, y_att, proj_rest,
        p["w_o_rnn"].astype(BF16), p["w_o_attn"].astype(BF16), p["w_out"].astype(BF16),
        row(p["norm_post_mix"]), g1, row(p["norm_pre_ffn"]), sc2, sh2,
        p["router_w"].T.astype(BF16), p["router_b"].reshape(N_EXPERTS, 1), S)

    gates, slot, counts, tile_cnt, tile_car, tile_lst = _route(logits_t)

    bm = _tiles()["moe_bm"]
    n_tiles = N // _tiles()["moe_t"]
    n_rows = N * TOP_K + N_EXPERTS * bm
    nblk = n_rows // bm
    counts = counts.reshape(N_EXPERTS)
    padded = ((counts + bm - 1) // bm) * bm
    pend = jnp.cumsum(padded)
    pstart = pend - padded
    eids = jnp.arange(N_EXPERTS, dtype=jnp.int32)
    per_run = lambda a: a[:, :n_tiles].T.astype(jnp.int32)
    plan = _copy_plan(per_run(tile_cnt), per_run(tile_lst), per_run(tile_car + pstart[:, None]),
                      _tiles()["moe_t"])
    tail_start = jnp.maximum(pend - bm, 0).astype(jnp.int32)
    n_used = (pend[-1] // bm).astype(jnp.int32).reshape(1)
    blk_row0 = jnp.minimum(jnp.arange(nblk, dtype=jnp.int32), n_used[0] - 1) * bm
    blk_e = jnp.sum(blk_row0[:, None] >= pend[None, :], axis=1).astype(jnp.int32)
    mine = blk_e[:, None] == eids[None, :]
    blk_cnt = jnp.sum(jnp.where(mine, counts, 0), axis=1)
    blk_pstart = jnp.sum(jnp.where(mine, pstart, 0), axis=1)
    blk_valid = jnp.clip(blk_cnt - (blk_row0 - blk_pstart), 0, bm).astype(jnp.int32)
    blk_first = (blk_row0 == blk_pstart).astype(jnp.int32)

    xs = _dispatch(h2, slot, plan, tail_start, n_used, n_rows, bm)
    b1 = p["moe_b1"].reshape(N_EXPERTS, D_FF, 2)
    later = jnp.logical_and(counts[None, :] > 0, eids[None, :] > eids[:, None])
    next_e = jnp.min(jnp.where(later, eids[None, :], N_EXPERTS), axis=1)
    next_e = jnp.where(next_e == N_EXPERTS, -1, next_e).astype(jnp.int32)
    ys = _ffn(xs, blk_e, blk_valid, blk_first, n_used, next_e, p["moe_w1"],
              b1[:, :, 0].reshape(N_EXPERTS, 1, D_FF), b1[:, :, 1].reshape(N_EXPERTS, 1, D_FF),
              p["moe_w2"], p["moe_b2"].reshape(N_EXPERTS, 1, D), bm)
    return _combine(ys, plan, x1, slot.T, gates.T, row(p["norm_post_ffn"]), g2, S)


def kernel(x, c, w_ada, b_ada, norm_pre_mix, norm_post_mix, norm_pre_ffn, norm_post_ffn, w_in, b_in, conv_w, conv_b, rg_w_a, rg_b_a, rg_w_x, rg_b_x, rg_lambda, attn_sinks, w_o_rnn, w_o_attn, w_out, router_w, router_b, moe_w1, moe_b1, moe_w2, moe_b2):
    B, S, D = x.shape
    params = dict(
        w_ada=w_ada, b_ada=b_ada, norm_pre_mix=norm_pre_mix, norm_post_mix=norm_post_mix,
        norm_pre_ffn=norm_pre_ffn, norm_post_ffn=norm_post_ffn, w_in=w_in, b_in=b_in,
        conv_w=conv_w, conv_b=conv_b, rg_w_a=rg_w_a, rg_b_a=rg_b_a, rg_w_x=rg_w_x, rg_b_x=rg_b_x,
        rg_lambda=rg_lambda, attn_sinks=attn_sinks, w_o_rnn=w_o_rnn, w_o_attn=w_o_attn,
        w_out=w_out, router_w=router_w, router_b=router_b, moe_w1=moe_w1, moe_b1=moe_b1,
        moe_w2=moe_w2, moe_b2=moe_b2)
    x2 = x.reshape(B * S, D)
    for layer in range(w_ada.shape[0]):
        x2 = _layer(x2, c, B, S, {k: v[layer] for k, v in params.items()})
    return x2.reshape(B, S, D)
```

```python
import functools
import math

import jax
import jax.numpy as jnp
import numpy as np
from jax import lax
from jax.experimental import pallas as pl
from jax.experimental.pallas import tpu as pltpu

D_MODEL = 1024
D_RNN = 1024
RNN_BLOCKS = 16
RNN_BW = D_RNN // RNN_BLOCKS
CONV_W = 4
LRU_C = 8.0
N_HEADS = 16
N_KV = 4
HEAD_DIM = 64
GROUP = N_HEADS // N_KV
WINDOW = 128
Q_DIM = N_HEADS * HEAD_DIM
KV_DIM = N_KV * HEAD_DIM
N_EXPERTS = 32
TOP_K = 4
D_FF = 1024
SWIGLU_LIMIT = 7.0
SWIGLU_ALPHA = 1.702
EPS = 1e-6
D_IN = 2 * D_RNN + Q_DIM + 2 * KV_DIM + 2 * D_MODEL
D_REST = D_IN - 2 * D_RNN

V7X_LANES = 128
V7X_SUBLANES = 8
V7X_MXU_DIM = 256
V7X_VMEM_BYTES = 64 * 1024 * 1024

F32 = jnp.float32
BF16 = jnp.bfloat16


def _tiles():
    return dict(
        ada_tn=1024,
        tok=512,
        attn_q=WINDOW,
        moe_t=256,
        moe_bm=256,
        rnn_group=V7X_MXU_DIM,
    )


def _vmem_limit(nbytes):
    return int(min(max(nbytes, 16 * 1024 * 1024), V7X_VMEM_BYTES - 8 * 1024 * 1024))


def _rms(x, g):
    return x * lax.rsqrt(jnp.mean(x * x, axis=-1, keepdims=True) + EPS) * g


def _ada_kernel(ct_ref, w_ref, b_ref, o_ref):
    ct = ct_ref[...]
    sc = ct * jax.nn.sigmoid(ct)
    w = w_ref[...]
    rows = [jnp.sum(w * sc[:, b:b + 1], axis=0, keepdims=True) for b in range(ct.shape[1])]
    o_ref[...] = jnp.concatenate(rows, axis=0) + b_ref[...]


def _ada(c, w_ada, b_ada):
    B, D = c.shape
    n_out = w_ada.shape[1]
    tn = _tiles()["ada_tn"]
    return pl.pallas_call(
        _ada_kernel,
        grid=(n_out // tn,),
        in_specs=[
            pl.BlockSpec((D, B), lambda j: (0, 0)),
            pl.BlockSpec((D, tn), lambda j: (0, j)),
            pl.BlockSpec((1, tn), lambda j: (0, j)),
        ],
        out_specs=pl.BlockSpec((B, tn), lambda j: (0, j)),
        out_shape=jax.ShapeDtypeStruct((B, n_out), F32),
        compiler_params=pltpu.CompilerParams(
            dimension_semantics=("arbitrary",),
            vmem_limit_bytes=_vmem_limit(4 * D * tn * 4)),
        name="ada",
    )(c.T, w_ada, b_ada.reshape(1, n_out))


def _gelu_tanh(x):
    return 0.5 * x * (1.0 + jnp.tanh(math.sqrt(2.0 / math.pi) * (x + 0.044715 * (x * x * x))))


def _softplus(z):
    return jnp.maximum(z, 0.0) + jnp.log1p(jnp.exp(-jnp.abs(z)))


def _sigmoid(x):
    return 0.5 * jnp.tanh(0.5 * x) + 0.5


def _mixin_kernel(x_ref, g_ref, sc_ref, sh_ref, w_ref, b_ref, cw_ref, cb_ref, wa_ref, ba_ref,
                  wx_ref, bx_ref, lam_ref, rest_ref, y_ref,
                  hb_s, xbuf, gr_s, a_s, b_s, h_s, ga_s, gb_s, gc_s, carry, *, per_b, gw, chunk):
    t = pl.program_id(0) % per_b
    tt = x_ref.shape[0]
    halo = V7X_SUBLANES
    ng = tt // V7X_SUBLANES
    n_lt = a_s.shape[0]
    per_g = gw // V7X_LANES
    lanes = lambda j: slice(j * V7X_LANES, (j + 1) * V7X_LANES)
    slab = lambda j, r: (j, pl.ds(r, ng, stride=V7X_SUBLANES), slice(None))

    @pl.when(t == 0)
    def _():
        xbuf[:, 0:halo, :] = jnp.zeros((n_lt, halo, V7X_LANES), F32)
        carry[...] = jnp.zeros_like(carry)

    hb_s[...] = (_rms(x_ref[...], g_ref[...]) * (1.0 + sc_ref[0]) + sh_ref[0]).astype(BF16)

    def proj(c0):
        return (jnp.dot(hb_s[...], w_ref[:, c0:c0 + chunk], preferred_element_type=F32)
                + b_ref[:, c0:c0 + chunk])

    def proj_rest(c0):
        rest_ref[:, c0 - 2 * D_RNN:c0 - 2 * D_RNN + chunk] = proj(c0).astype(BF16)

    for c0 in range(0, D_RNN, chunk):
        acc = proj(c0)
        for j in range(chunk // V7X_LANES):
            xbuf[c0 // V7X_LANES + j, halo:halo + tt, :] = acc[:, lanes(j)]
    for c0 in range(D_RNN, 2 * D_RNN, chunk):
        gr_s[:, c0 - D_RNN:c0 - D_RNN + chunk] = proj(c0)

    def conv(j):
        ls = lanes(j)
        taps = {o: xbuf[slab(j, halo + o)] for o in range(-(CONV_W - 1), V7X_SUBLANES)}
        for r in range(V7X_SUBLANES):
            acc = cb_ref[:, ls] + taps[r - (CONV_W - 1)] * cw_ref[0:1, ls]
            for kk in range(1, CONV_W):
                acc = acc + taps[r - (CONV_W - 1) + kk] * cw_ref[kk:kk + 1, ls]
            h_s[slab(j, r)] = acc
        xbuf[j, 0:halo, :] = xbuf[j, tt:tt + halo, :]

    def gates(g):
        cs = slice(g * gw, (g + 1) * gw)
        sp = _softplus(-lam_ref[:, cs])
        reset = jnp.logical_and(t == 0, lax.broadcasted_iota(jnp.int32, (tt, gw), 0) == 0)
        xc = jnp.concatenate([h_s[g * per_g + j] for j in range(per_g)], axis=1)
        xg = xc.astype(BF16)
        gate_r = _sigmoid(jnp.dot(xg, wa_ref[g], preferred_element_type=F32) + ba_ref[:, cs])
        gate_i = _sigmoid(jnp.dot(xg, wx_ref[g], preferred_element_type=F32) + bx_ref[:, cs])
        a = jnp.exp(-LRU_C * gate_r * sp)
        mult = jnp.where(reset, 1.0, jnp.sqrt((1.0 - a) * (1.0 + a)))
        bt = (xc * gate_i) * mult
        for j in range(per_g):
            a_s[g * per_g + j] = a[:, lanes(j)]
            b_s[g * per_g + j] = bt[:, lanes(j)]

    def scan_groups(j):
        acc_a = a_s[slab(j, 0)]
        acc_h = b_s[slab(j, 0)]
        h_s[slab(j, 0)] = acc_h
        for r in range(1, V7X_SUBLANES):
            a_r = a_s[slab(j, r)]
            acc_h = a_r * acc_h + b_s[slab(j, r)]
            acc_a = a_r * acc_a
            h_s[slab(j, r)] = acc_h
            a_s[slab(j, r)] = acc_a
        ga_s[:, lanes(j)] = acc_a
        gb_s[:, lanes(j)] = acc_h

    def across(gi, h_prev):
        gc_s[pl.ds(gi, 1), :] = h_prev
        return ga_s[pl.ds(gi, 1), :] * h_prev + gb_s[pl.ds(gi, 1), :]

    def finish(j):
        h_in = gc_s[:, lanes(j)]
        for r in range(V7X_SUBLANES):
            h_s[slab(j, r)] = a_s[slab(j, r)] * h_in + h_s[slab(j, r)]
        y_ref[:, lanes(j)] = (h_s[j] * _gelu_tanh(gr_s[:, lanes(j)])).astype(BF16)

    rest = [functools.partial(proj_rest, c0) for c0 in range(2 * D_RNN, w_ref.shape[1], chunk)]
    before = ([functools.partial(conv, j) for j in range(n_lt)]
              + [functools.partial(gates, g) for g in range(n_lt // per_g)]
              + [functools.partial(scan_groups, j) for j in range(n_lt)])
    after = [functools.partial(finish, j) for j in range(n_lt)]
    n_before = (len(rest) * len(before)) // (len(before) + len(after))

    def interleave(steps, chunks):
        every = -(-len(steps) // max(len(chunks), 1))
        for n, step in enumerate(steps):
            step()
            if (n + 1) % every == 0 and chunks:
                chunks.pop(0)()
        while chunks:
            chunks.pop(0)()

    interleave(before, rest[:n_before])
    carry[...] = lax.fori_loop(0, ng, across, carry[...])
    interleave(after, rest[n_before:])


def _block_diag_tiles(w, gw):
    nb, bw, _ = w.shape
    per = gw // bw
    w4 = w.reshape(nb // per, per, bw, bw)
    eye = jnp.eye(per, dtype=w.dtype)
    return jnp.einsum("gpij,pq->gpiqj", w4, eye).reshape(nb // per, gw, gw)


def _mixin(x2, g, sc, sh, w_bf, b_in, conv_w, conv_b, wa, ba, wx, bx, lam, S):
    N, D = x2.shape
    C = D_RNN
    tt = _tiles()["tok"]
    gw = _tiles()["rnn_group"]
    per_b = S // tt
    chunk = 2 * V7X_MXU_DIM
    vec = lambda: pl.BlockSpec((1, C), lambda i: (0, 0))
    bvec = lambda: pl.BlockSpec((1, 1, D), lambda i: (i // per_b, 0, 0))
    gate_w = lambda: pl.BlockSpec((C // gw, gw, gw), lambda i: (0, 0, 0))
    slabs = lambda rows: pltpu.VMEM((C // V7X_LANES, rows, V7X_LANES), F32)
    groups = lambda: pltpu.VMEM((tt // V7X_SUBLANES, C), F32)
    vmem = D * D_IN * 2 + 2 * tt * (D * 4 + D_REST * 2 + C * 2) + tt * C * (2 + 5 * 4) + 8 * tt * chunk * 4
    return pl.pallas_call(
        functools.partial(_mixin_kernel, per_b=per_b, gw=gw, chunk=chunk),
        grid=(N // tt,),
        in_specs=[
            pl.BlockSpec((tt, D), lambda i: (i, 0)),
            pl.BlockSpec((1, D), lambda i: (0, 0)),
            bvec(), bvec(),
            pl.BlockSpec((D, D_IN), lambda i: (0, 0), pipeline_mode=pl.Buffered(1)),
            pl.BlockSpec((1, D_IN), lambda i: (0, 0)),
            pl.BlockSpec((CONV_W, C), lambda i: (0, 0)),
            vec(), gate_w(), vec(), gate_w(), vec(), vec(),
        ],
        out_specs=[
            pl.BlockSpec((tt, D_REST), lambda i: (i, 0)),
            pl.BlockSpec((tt, C), lambda i: (i, 0)),
        ],
        out_shape=[
            jax.ShapeDtypeStruct((N, D_REST), BF16),
            jax.ShapeDtypeStruct((N, C), BF16),
        ],
        scratch_shapes=[
            pltpu.VMEM((tt, D), BF16),
            slabs(tt + V7X_SUBLANES),
            pltpu.VMEM((tt, C), F32),
            slabs(tt), slabs(tt), slabs(tt),
            groups(), groups(), groups(),
            pltpu.VMEM((1, C), F32),
        ],
        compiler_params=pltpu.CompilerParams(
            dimension_semantics=("arbitrary",),
            vmem_limit_bytes=_vmem_limit(vmem)),
        name="mixin",
    )(x2, g, sc, sh, w_bf, b_in, conv_w, conv_b, wa, ba, wx, bx, lam)


def _alibi_slopes():
    return [2.0 ** (-8.0 * (h + 1) / N_HEADS) for h in range(N_HEADS)]


def _attn_kernel(sink_ref, q_ref, kp_ref, kc_ref, vp_ref, vc_ref, o_ref, bias_s, *, per_b):
    blk = pl.program_id(0) % per_b
    bq = q_ref.shape[0]
    slopes = _alibi_slopes()

    @pl.when(pl.program_id(0) == 0)
    def _():
        qi = lax.broadcasted_iota(jnp.int32, (bq, 2 * bq), 0)
        ci = lax.broadcasted_iota(jnp.int32, (bq, 2 * bq), 1)
        dist = qi + bq - ci
        valid = (dist >= 0) & (dist < WINDOW)
        distf = dist.astype(F32)
        for h in range(N_HEADS):
            b = jnp.where(valid, -slopes[h] * distf, -jnp.inf)
            bias_s[0, h] = jnp.where(ci >= bq, b, -jnp.inf)
            bias_s[1, h] = b

    table = jnp.minimum(blk, 1)
    outs = []
    for kv in range(N_KV):
        ks = slice(kv * HEAD_DIM, (kv + 1) * HEAD_DIM)
        kb = jnp.concatenate([kp_ref[:, ks], kc_ref[:, ks]], axis=0)
        vb = jnp.concatenate([vp_ref[:, ks], vc_ref[:, ks]], axis=0)
        for g in range(GROUP):
            h = kv * GROUP + g
            qh = q_ref[:, h * HEAD_DIM:(h + 1) * HEAD_DIM]
            s = lax.dot_general(qh, kb, (((1,), (1,)), ((), ())), preferred_element_type=F32)
            s = s + bias_s[table, h]
            sink = sink_ref[h]
            m = jnp.maximum(jnp.max(s, axis=-1, keepdims=True), sink)
            p = jnp.exp(s - m)
            denom = jnp.sum(p, axis=-1, keepdims=True) + jnp.exp(sink - m)
            o = jnp.dot(p.astype(BF16), vb, preferred_element_type=F32) / denom
            outs.append(o)
            if len(outs) * HEAD_DIM == V7X_LANES:
                c0 = (h + 1) * HEAD_DIM - V7X_LANES
                o_ref[:, c0:c0 + V7X_LANES] = jnp.concatenate(outs, axis=-1).astype(BF16)
                outs = []


def _attn(proj_rest, sinks, S):
    N = proj_rest.shape[0]
    bq = _tiles()["attn_q"]
    per_b = S // bq
    k_col = (Q_DIM + 2 * D_MODEL) // KV_DIM
    v_col = k_col + 1

    def prev(i):
        return jnp.where(i % per_b == 0, i, i - 1)

    return pl.pallas_call(
        functools.partial(_attn_kernel, per_b=per_b),
        grid=(N // bq,),
        in_specs=[
            pl.BlockSpec(memory_space=pltpu.SMEM),
            pl.BlockSpec((bq, Q_DIM), lambda i: (i, 0)),
            pl.BlockSpec((bq, KV_DIM), lambda i: (prev(i), k_col)),
            pl.BlockSpec((bq, KV_DIM), lambda i: (i, k_col)),
            pl.BlockSpec((bq, KV_DIM), lambda i: (prev(i), v_col)),
            pl.BlockSpec((bq, KV_DIM), lambda i: (i, v_col)),
        ],
        out_specs=pl.BlockSpec((bq, Q_DIM), lambda i: (i, 0)),
        out_shape=jax.ShapeDtypeStruct((N, Q_DIM), BF16),
        scratch_shapes=[pltpu.VMEM((2, N_HEADS, bq, 2 * bq), F32)],
        compiler_params=pltpu.CompilerParams(
            dimension_semantics=("arbitrary",),
            vmem_limit_bytes=_vmem_limit(3 * 2 * N_HEADS * bq * 2 * bq * 4)),
        name="attn",
    )(sinks, proj_rest, proj_rest, proj_rest, proj_rest, proj_rest)


def _merge_kernel(x_ref, yr_ref, ya_ref, gr_ref, ga_ref, wr_ref, wa_ref, wo_ref,
                  gpost_ref, g1_ref, gpre_ref, sc2_ref, sh2_ref, rwt_ref, rb_ref,
                  x1_ref, h2_ref, lg_ref):
    r = jnp.dot(yr_ref[...], wr_ref[...], preferred_element_type=F32)
    a = jnp.dot(ya_ref[...], wa_ref[...], preferred_element_type=F32)
    merged = (_sigmoid(gr_ref[...].astype(F32)) * r
              + _sigmoid(ga_ref[...].astype(F32)) * a)
    mix = jnp.dot(merged.astype(BF16), wo_ref[...], preferred_element_type=F32)
    x1 = x_ref[...] + g1_ref[0] * _rms(mix, gpost_ref[...])
    x1_ref[...] = x1
    h2 = _rms(x1, gpre_ref[...]) * (1.0 + sc2_ref[0]) + sh2_ref[0]
    h2_ref[...] = h2
    lg = lax.dot_general(rwt_ref[...], h2.astype(BF16), (((1,), (1,)), ((), ())),
                         preferred_element_type=F32)
    lg_ref[...] = lg + rb_ref[...]


def _merge(x2, y_rnn, y_att, proj_rest, wr, wa, wo, gpost, g1, gpre, sc2, sh2, rwt, rb, S):
    N, D = x2.shape
    tm = _tiles()["tok"]
    per_b = S // tm
    gate_r_col = Q_DIM // D
    mat = lambda: pl.BlockSpec((D, D), lambda i: (0, 0))
    vec = lambda: pl.BlockSpec((1, D), lambda i: (0, 0))
    bvec = lambda: pl.BlockSpec((1, 1, D), lambda i: (i // per_b, 0, 0))
    tile = lambda col=0: pl.BlockSpec((tm, D), lambda i: (i, col))
    return pl.pallas_call(
        _merge_kernel,
        grid=(N // tm,),
        in_specs=[
            tile(), tile(), tile(), tile(gate_r_col), tile(gate_r_col + 1),
            mat(), mat(), mat(),
            vec(), bvec(), vec(), bvec(), bvec(),
            pl.BlockSpec((N_EXPERTS, D), lambda i: (0, 0)),
            pl.BlockSpec((N_EXPERTS, 1), lambda i: (0, 0)),
        ],
        out_specs=[
            tile(), tile(),
            pl.BlockSpec((N_EXPERTS, tm), lambda i: (0, i)),
        ],
        out_shape=[
            jax.ShapeDtypeStruct((N, D), F32),
            jax.ShapeDtypeStruct((N, D), F32),
            jax.ShapeDtypeStruct((N_EXPERTS, N), F32),
        ],
        compiler_params=pltpu.CompilerParams(
            dimension_semantics=("arbitrary",),
            vmem_limit_bytes=_vmem_limit(6 * D * D * 2 + 24 * tm * D * 4)),
        name="merge",
    )(x2, y_rnn, y_att, proj_rest, proj_rest, wr, wa, wo, gpost, g1, gpre, sc2, sh2, rwt, rb)


def _route_kernel(lg_ref, g_ref, slot_ref, cnt_ref, tcnt_ref, tcar_ref, tlst_ref, carry):
    i = pl.program_id(0)

    @pl.when(i == 0)
    def _():
        carry[...] = jnp.zeros_like(carry)
        tcnt_ref[...] = jnp.zeros_like(tcnt_ref)
        tcar_ref[...] = jnp.zeros_like(tcar_ref)
        tlst_ref[...] = jnp.zeros_like(tlst_ref)

    l = lg_ref[...]
    E, T = l.shape
    row = lax.broadcasted_iota(jnp.int32, (E, T), 0).astype(F32)
    vals, idxs = [], []
    for _ in range(TOP_K):
        m = jnp.max(l, axis=0, keepdims=True)
        idx = jnp.min(jnp.where(l == m, row, float(E)), axis=0, keepdims=True)
        vals.append(m)
        idxs.append(idx)
        l = jnp.where(row == idx, -jnp.inf, l)
    ex = [jnp.exp(v - vals[0]) for v in vals]
    tot = ex[0]
    for e in ex[1:]:
        tot = tot + e
    g_ref[...] = jnp.concatenate([e / tot for e in ex], axis=0)

    hot = [row == idx for idx in idxs]
    onehot = jnp.zeros((E, T), F32)
    for hk in hot:
        onehot = onehot + hk.astype(F32)
    tri_t = (lax.broadcasted_iota(jnp.int32, (T, T), 0)
             < lax.broadcasted_iota(jnp.int32, (T, T), 1)).astype(BF16)
    before = jnp.dot(onehot.astype(BF16), tri_t, preferred_element_type=F32)
    cnt = jnp.sum(onehot, axis=1, keepdims=True)
    tri_e = (lax.broadcasted_iota(jnp.int32, (E, E), 1)
             < lax.broadcasted_iota(jnp.int32, (E, E), 0)).astype(BF16)
    lstart = jnp.dot(tri_e, jnp.broadcast_to(cnt, (E, V7X_LANES)).astype(BF16),
                     preferred_element_type=F32)[:, 0:1]
    local = before + lstart
    slots = [jnp.sum(jnp.where(hk, local, 0.0), axis=0, keepdims=True) for hk in hot]
    slot_ref[...] = jnp.concatenate(slots, axis=0).astype(jnp.int32)

    mine = lax.broadcasted_iota(jnp.int32, tcnt_ref.shape, 1) == i
    tcnt_ref[...] = jnp.where(mine, cnt.astype(jnp.int32), tcnt_ref[...])
    tcar_ref[...] = jnp.where(mine, carry[...].astype(jnp.int32), tcar_ref[...])
    tlst_ref[...] = jnp.where(mine, lstart.astype(jnp.int32), tlst_ref[...])
    total = carry[...] + cnt
    carry[...] = total
    cnt_ref[...] = total.astype(jnp.int32)


def _route(logits_t):
    E, N = logits_t.shape
    T = _tiles()["moe_t"]
    assert T <= 2 ** 8 and N // T <= V7X_LANES
    out = lambda: pl.BlockSpec((TOP_K, T), lambda i: (0, i))
    per_tile = lambda: pl.BlockSpec((E, V7X_LANES), lambda i: (0, 0))
    return pl.pallas_call(
        _route_kernel,
        grid=(N // T,),
        in_specs=[pl.BlockSpec((E, T), lambda i: (0, i))],
        out_specs=[out(), out(), pl.BlockSpec((E, 1), lambda i: (0, 0)),
                   per_tile(), per_tile(), per_tile()],
        out_shape=[
            jax.ShapeDtypeStruct((TOP_K, N), F32),
            jax.ShapeDtypeStruct((TOP_K, N), jnp.int32),
            jax.ShapeDtypeStruct((E, 1), jnp.int32),
            jax.ShapeDtypeStruct((E, V7X_LANES), jnp.int32),
            jax.ShapeDtypeStruct((E, V7X_LANES), jnp.int32),
            jax.ShapeDtypeStruct((E, V7X_LANES), jnp.int32),
        ],
        scratch_shapes=[pltpu.VMEM((E, 1), F32)],
        compiler_params=pltpu.CompilerParams(dimension_semantics=("arbitrary",)),
        name="route",
    )(logits_t)


def _run_sizes(T):
    return [2 ** b for b in range(int(math.log2(T)), -1, -1)]


def _copy_plan(tile_cnt, tile_lst, tile_dst, T):
    sizes = jnp.asarray(_run_sizes(T), jnp.int32)[None, :, None]
    cnt, lst, dst = (a[:, None, :] for a in (tile_cnt, tile_lst, tile_dst))
    has = (cnt & sizes) != 0
    off = (cnt // (2 * sizes)) * (2 * sizes)
    place = jnp.cumsum(has.astype(jnp.int32), axis=-1) - has
    hit = has[..., :, None] & (place[..., :, None] == jnp.arange(N_EXPERTS))
    pick = lambda v: jnp.sum(jnp.where(hit, v[..., :, None], 0), axis=-2)
    n_tiles, nb = tile_cnt.shape[0], sizes.shape[1]
    parts = [pick(jnp.broadcast_to(lst + off, has.shape)).reshape(n_tiles, -1),
             pick(jnp.broadcast_to(dst + off, has.shape)).reshape(n_tiles, -1),
             jnp.sum(has, axis=-1).reshape(n_tiles, nb)]
    plan = jnp.concatenate(parts, axis=1).astype(jnp.int32)
    pad = _plan_len(T) - plan.shape[1]
    return jnp.pad(plan, ((0, 0), (0, pad))).reshape(-1)


def _plan_len(T):
    nb = len(_run_sizes(T))
    return max(V7X_LANES, pl.next_power_of_2(2 * nb * N_EXPERTS + nb))


def _run_copies(plan_ref, T, make):
    sizes = _run_sizes(T)
    nb = len(sizes)
    for bi, b in enumerate(sizes):
        def body(j, c, bi=bi, b=b):
            make(plan_ref[bi * N_EXPERTS + j], plan_ref[(nb + bi) * N_EXPERTS + j], b).start()
            return c

        lax.fori_loop(0, plan_ref[2 * nb * N_EXPERTS + bi], body, 0)


def _row_slab(ref, row, nrows, lead=()):
    rows = pl.ds(pl.multiple_of(row * V7X_SUBLANES, V7X_SUBLANES), nrows * V7X_SUBLANES)
    return ref.at[(*lead, rows)]


def _to_slabs(ref, val, lead=()):
    rows = val.shape[0]
    for s in range(V7X_SUBLANES):
        ref[(*lead, pl.ds(s, rows, stride=V7X_SUBLANES), slice(None))] = (
            val[:, s * V7X_LANES:(s + 1) * V7X_LANES])


def _from_slabs(ref, lead=()):
    rows = ref.shape[-2] // V7X_SUBLANES
    return jnp.concatenate(
        [ref[(*lead, pl.ds(s, rows, stride=V7X_SUBLANES), slice(None))]
         for s in range(V7X_SUBLANES)], axis=1)


def _dispatch_kernel(tail_ref, nu_ref, plan_ref, slot_ref, h_ref, xs_ref, buf, zbuf, sem, zsem):
    i = pl.program_id(0)
    T = h_ref.shape[0]
    bm = zbuf.shape[0] // V7X_SUBLANES
    nblk = xs_ref.shape[0] // zbuf.shape[0]
    cur = i % 2

    def zero_copy(row0):
        return pltpu.make_async_copy(zbuf, _row_slab(xs_ref, row0, bm), zsem)

    @pl.when(i == 0)
    def _():
        zbuf[...] = jnp.zeros_like(zbuf)

        def start(e, c):
            zero_copy(tail_ref[e]).start()
            return c

        def wait(e, c):
            zero_copy(tail_ref[e]).wait()
            return c

        def start_unused(j, c):
            zero_copy(j * bm).start()
            return c

        def wait_unused(j, c):
            zero_copy(j * bm).wait()
            return c

        lax.fori_loop(0, N_EXPERTS, start, 0)
        lax.fori_loop(nu_ref[0], nblk, start_unused, 0)
        lax.fori_loop(0, N_EXPERTS, wait, 0)
        lax.fori_loop(nu_ref[0], nblk, wait_unused, 0)

    slot = slot_ref[...]
    r_id = lax.broadcasted_iota(jnp.int32, (TOP_K * T, T), 0)
    perm = r_id == slot[0:1, :]
    for k in range(1, TOP_K):
        perm = jnp.logical_or(perm, r_id == slot[k:k + 1, :])
    grouped = jnp.dot(perm.astype(BF16), h_ref[...].astype(BF16), preferred_element_type=F32)
    _to_slabs(buf, grouped, lead=(cur,))

    def wait_tile(b):
        pltpu.make_async_copy(buf.at[b], _row_slab(xs_ref, 0, TOP_K * T), sem.at[b]).wait()

    @pl.when(i > 0)
    def _():
        wait_tile(1 - cur)

    _run_copies(plan_ref, T, lambda l, d, n: pltpu.make_async_copy(
        _row_slab(buf, l, n, lead=(cur,)), _row_slab(xs_ref, d, n), sem.at[cur]))

    @pl.when(i == pl.num_programs(0) - 1)
    def _():
        wait_tile(cur)


def _dispatch(h2, slot, plan, tail_start, n_used, n_rows, bm):
    N, D = h2.shape
    T = _tiles()["moe_t"]
    S8 = V7X_SUBLANES
    imap = lambda f: (lambda i, *_: f(i))
    return pl.pallas_call(
        _dispatch_kernel,
        grid_spec=pltpu.PrefetchScalarGridSpec(
            num_scalar_prefetch=2,
            grid=(N // T,),
            in_specs=[
                pl.BlockSpec((_plan_len(T),), imap(lambda i: (i,)), memory_space=pltpu.SMEM),
                pl.BlockSpec((TOP_K, T), imap(lambda i: (0, i))),
                pl.BlockSpec((T, D), imap(lambda i: (i, 0))),
            ],
            out_specs=pl.BlockSpec(memory_space=pl.ANY),
            scratch_shapes=[
                pltpu.VMEM((2, TOP_K * T * S8, V7X_LANES), F32),
                pltpu.VMEM((bm * S8, V7X_LANES), F32),
                pltpu.SemaphoreType.DMA((2,)),
                pltpu.SemaphoreType.DMA(()),
            ],
        ),
        out_shape=jax.ShapeDtypeStruct((n_rows * S8, V7X_LANES), F32),
        compiler_params=pltpu.CompilerParams(
            dimension_semantics=("arbitrary",),
            vmem_limit_bytes=_vmem_limit(8 * TOP_K * T * D * 4)),
        name="dispatch",
    )(tail_start, n_used, plan, slot, h2)


def _ffn_kernel(be_ref, bv_ref, bf_ref, nu_ref, nxt_ref, x_ref, w1_hbm, b1g_ref, b1l_ref, w2_hbm,
                b2_ref, y_ref, w1f_s, w2f_s, w1g_s, w1l_s, w2_s, sem):
    i = pl.program_id(0)
    bm, D = x_ref.shape[0] // V7X_SUBLANES, w2_s.shape[1]
    pw = 2 * V7X_LANES

    def fetch(e):
        return (pltpu.make_async_copy(w1_hbm.at[e], w1f_s, sem.at[0]),
                pltpu.make_async_copy(w2_hbm.at[e], w2f_s, sem.at[1]))

    @pl.when(i == 0)
    def _():
        for cp in fetch(be_ref[0]):
            cp.start()

    @pl.when(jnp.logical_and(i < nu_ref[0], bf_ref[i] == 1))
    def _():
        e = be_ref[i]
        for cp in fetch(e):
            cp.wait()
        src = lax.broadcasted_iota(jnp.int32, (pw, pw), 0)
        dst = lax.broadcasted_iota(jnp.int32, (pw, pw), 1)
        want = jnp.where(dst < V7X_LANES, 2 * dst, 2 * (dst - V7X_LANES) + 1)
        perm = (src == want).astype(BF16)
        for c in range(w1f_s.shape[1] // pw):
            wp = jnp.dot(w1f_s[:, c * pw:(c + 1) * pw].astype(BF16), perm,
                         preferred_element_type=F32).astype(BF16)
            w1g_s[:, c * V7X_LANES:(c + 1) * V7X_LANES] = wp[:, :V7X_LANES]
            w1l_s[:, c * V7X_LANES:(c + 1) * V7X_LANES] = wp[:, V7X_LANES:]
        w2_s[...] = w2f_s[...].astype(BF16)

        @pl.when(nxt_ref[e] >= 0)
        def _():
            for cp in fetch(nxt_ref[e]):
                cp.start()

    @pl.when(i < nu_ref[0])
    def _():
        rows = lax.broadcasted_iota(jnp.int32, (bm, D), 0)
        x = jnp.where(rows < bv_ref[i], _from_slabs(x_ref), 0.0).astype(BF16)
        glu = jnp.dot(x, w1g_s[...], preferred_element_type=F32) + b1g_ref[0]
        lin = jnp.dot(x, w1l_s[...], preferred_element_type=F32) + b1l_ref[0]
        glu = jnp.minimum(glu, SWIGLU_LIMIT)
        lin = jnp.clip(lin, -SWIGLU_LIMIT, SWIGLU_LIMIT)
        act = glu * _sigmoid(SWIGLU_ALPHA * glu) * (lin + 1.0)
        _to_slabs(y_ref, jnp.dot(act.astype(BF16), w2_s[...], preferred_element_type=F32) + b2_ref[0])

    @pl.when(i >= nu_ref[0])
    def _():
        y_ref[...] = jnp.zeros_like(y_ref)


def _ffn(xs, blk_e, blk_valid, blk_first, n_used, next_e, w1, b1g, b1l, w2, b2, bm):
    E, D, F2 = w1.shape
    F = F2 // 2
    slab = bm * V7X_SUBLANES
    nblk = xs.shape[0] // slab

    def row_blk(i, be, bv, bf, nu, nxt):
        return (jnp.minimum(i, nu[0] - 1), 0)

    def per_e(i, be, bv, bf, nu, nxt):
        return (be[i], 0, 0)

    vmem = (D * F2 + F * D) * (4 + 2) + 8 * bm * F2 * 4
    return pl.pallas_call(
        _ffn_kernel,
        grid_spec=pltpu.PrefetchScalarGridSpec(
            num_scalar_prefetch=5,
            grid=(nblk,),
            in_specs=[
                pl.BlockSpec((slab, V7X_LANES), row_blk),
                pl.BlockSpec(memory_space=pl.ANY),
                pl.BlockSpec((1, 1, F), per_e),
                pl.BlockSpec((1, 1, F), per_e),
                pl.BlockSpec(memory_space=pl.ANY),
                pl.BlockSpec((1, 1, D), per_e),
            ],
            out_specs=pl.BlockSpec((slab, V7X_LANES), lambda i, be, bv, bf, nu, nxt: (i, 0)),
            scratch_shapes=[
                pltpu.VMEM((D, F2), F32),
                pltpu.VMEM((F, D), F32),
                pltpu.VMEM((D, F), BF16),
                pltpu.VMEM((D, F), BF16),
                pltpu.VMEM((F, D), BF16),
                pltpu.SemaphoreType.DMA((2,)),
            ],
        ),
        out_shape=jax.ShapeDtypeStruct(xs.shape, F32),
        compiler_params=pltpu.CompilerParams(
            dimension_semantics=("arbitrary",),
            vmem_limit_bytes=_vmem_limit(vmem)),
        name="ffn",
    )(blk_e, blk_valid, blk_first, n_used, next_e, xs, w1, b1g, b1l, w2, b2)


def _split_bf16(v):
    hi = v.astype(BF16)
    return hi, (v - hi.astype(F32)).astype(BF16)


def _combine_kernel(plan_ref, plan_next_ref, ys_ref, x1_ref, slot_ref, gate_ref, gpost_ref, g2_ref,
                    o_ref, buf, sem):
    i = pl.program_id(0)
    T = x1_ref.shape[0]
    cur = i % 2

    def gather(plan, b):
        _run_copies(plan, T, lambda l, d, n: pltpu.make_async_copy(
            _row_slab(ys_ref, d, n), _row_slab(buf, l, n, lead=(b,)), sem.at[b]))

    @pl.when(i == 0)
    def _():
        gather(plan_ref, cur)

    @pl.when(i + 1 < pl.num_programs(0))
    def _():
        gather(plan_next_ref, 1 - cur)

    pltpu.make_async_copy(_row_slab(ys_ref, 0, TOP_K * T), buf.at[cur], sem.at[cur]).wait()

    slot = slot_ref[...]
    gates = gate_ref[...]
    r_id = lax.broadcasted_iota(jnp.int32, (T, TOP_K * T), 1)
    w = jnp.zeros((T, TOP_K * T), F32)
    for k in range(TOP_K):
        w = jnp.where(r_id == slot[:, k:k + 1], gates[:, k:k + 1], w)
    w_hi, w_lo = _split_bf16(w)
    y_hi, y_lo = _split_bf16(_from_slabs(buf, lead=(cur,)))
    ff = (jnp.dot(w_hi, y_hi, preferred_element_type=F32)
          + (jnp.dot(w_hi, y_lo, preferred_element_type=F32)
             + jnp.dot(w_lo, y_hi, preferred_element_type=F32)))
    o_ref[...] = x1_ref[...] + g2_ref[0] * _rms(ff, gpost_ref[...])


def _combine(ys, plan, x1, slot_tk, gates_tk, gpost, g2, S):
    N, D = x1.shape
    T = _tiles()["moe_t"]
    per_b = S // T
    n_tiles = N // T
    plan_spec = lambda f: pl.BlockSpec((_plan_len(T),), f, memory_space=pltpu.SMEM)
    return pl.pallas_call(
        _combine_kernel,
        grid=(n_tiles,),
        in_specs=[
            plan_spec(lambda i: (i,)),
            plan_spec(lambda i: (jnp.minimum(i + 1, n_tiles - 1),)),
            pl.BlockSpec(memory_space=pl.ANY),
            pl.BlockSpec((T, D), lambda i: (i, 0)),
            pl.BlockSpec((T, TOP_K), lambda i: (i, 0)),
            pl.BlockSpec((T, TOP_K), lambda i: (i, 0)),
            pl.BlockSpec((1, D), lambda i: (0, 0)),
            pl.BlockSpec((1, 1, D), lambda i: (i // per_b, 0, 0)),
        ],
        out_specs=pl.BlockSpec((T, D), lambda i: (i, 0)),
        out_shape=jax.ShapeDtypeStruct((N, D), F32),
        scratch_shapes=[
            pltpu.VMEM((2, TOP_K * T * V7X_SUBLANES, V7X_LANES), F32),
            pltpu.SemaphoreType.DMA((2,)),
        ],
        compiler_params=pltpu.CompilerParams(
            dimension_semantics=("arbitrary",),
            vmem_limit_bytes=_vmem_limit(10 * TOP_K * T * D * 4)),
        name="combine",
    )(plan, plan, ys, x1, slot_tk, gates_tk, gpost, g2)


def _rest_order(w):
    scale = HEAD_DIM ** -0.5
    assert math.log2(scale).is_integer()
    xr, gr, q, k, v, mg_r, mg_a = jnp.split(
        w, np.cumsum([D_RNN, D_RNN, Q_DIM, KV_DIM, KV_DIM, D_MODEL]).tolist(), axis=-1)
    return jnp.concatenate([xr, gr, q * scale, mg_r, mg_a, k, v], axis=-1)


def _layer(x2, c, B, S, p):
    D = D_MODEL
    N = B * S
    ada = _ada(c, p["w_ada"], p["b_ada"])
    sh1, sc1, g1, sh2, sc2, g2 = [a.reshape(B, 1, D) for a in jnp.split(ada, 6, axis=-1)]
    row = lambda v: v.reshape(1, -1)

    gw = _tiles()["rnn_group"]
    proj_rest, y_rnn = _mixin(
        x2, row(p["norm_pre_mix"]), sc1, sh1,
        _rest_order(p["w_in"]).astype(BF16), _rest_order(row(p["b_in"])),
        p["conv_w"], row(p["conv_b"]),
        _block_diag_tiles(p["rg_w_a"], gw).astype(BF16), row(p["rg_b_a"]),
        _block_diag_tiles(p["rg_w_x"], gw).astype(BF16), row(p["rg_b_x"]),
        row(p["rg_lambda"]), S)
    y_att = _attn(proj_rest, p["attn_sinks"], S)

    x1, h2, logits_t = _merge(
        x2, y_rnn, y_att, proj_rest,
        p["w_o_rnn"].astype(BF16), p["w_o_attn"].astype(BF16), p["w_out"].astype(BF16),
        row(p["norm_post_mix"]), g1, row(p["norm_pre_ffn"]), sc2, sh2,
        p["router_w"].T.astype(BF16), p["router_b"].reshape(N_EXPERTS, 1), S)

    gates, slot, counts, tile_cnt, tile_car, tile_lst = _route(logits_t)

    bm = _tiles()["moe_bm"]
    n_tiles = N // _tiles()["moe_t"]
    n_rows = N * TOP_K + N_EXPERTS * bm
    nblk = n_rows // bm
    counts = counts.reshape(N_EXPERTS)
    padded = ((counts + bm - 1) // bm) * bm
    pend = jnp.cumsum(padded)
    pstart = pend - padded
    eids = jnp.arange(N_EXPERTS, dtype=jnp.int32)
    per_run = lambda a: a[:, :n_tiles].T.astype(jnp.int32)
    plan = _copy_plan(per_run(tile_cnt), per_run(tile_lst), per_run(tile_car + pstart[:, None]),
                      _tiles()["moe_t"])
    tail_start = jnp.maximum(pend - bm, 0).astype(jnp.int32)
    n_used = (pend[-1] // bm).astype(jnp.int32).reshape(1)
    blk_row0 = jnp.minimum(jnp.arange(nblk, dtype=jnp.int32), n_used[0] - 1) * bm
    blk_e = jnp.sum(blk_row0[:, None] >= pend[None, :], axis=1).astype(jnp.int32)
    mine = blk_e[:, None] == eids[None, :]
    blk_cnt = jnp.sum(jnp.where(mine, counts, 0), axis=1)
    blk_pstart = jnp.sum(jnp.where(mine, pstart, 0), axis=1)
    blk_valid = jnp.clip(blk_cnt - (blk_row0 - blk_pstart), 0, bm).astype(jnp.int32)
    blk_first = (blk_row0 == blk_pstart).astype(jnp.int32)

    xs = _dispatch(h2, slot, plan, tail_start, n_used, n_rows, bm)
    b1 = p["moe_b1"].reshape(N_EXPERTS, D_FF, 2)
    later = jnp.logical_and(counts[None, :] > 0, eids[None, :] > eids[:, None])
    next_e = jnp.min(jnp.where(later, eids[None, :], N_EXPERTS), axis=1)
    next_e = jnp.where(next_e == N_EXPERTS, -1, next_e).astype(jnp.int32)
    ys = _ffn(xs, blk_e, blk_valid, blk_first, n_used, next_e, p["moe_w1"],
              b1[:, :, 0].reshape(N_EXPERTS, 1, D_FF), b1[:, :, 1].reshape(N_EXPERTS, 1, D_FF),
              p["moe_w2"], p["moe_b2"].reshape(N_EXPERTS, 1, D), bm)
    return _combine(ys, plan, x1, slot.T, gates.T, row(p["norm_post_ffn"]), g2, S)


def kernel(x, c, w_ada, b_ada, norm_pre_mix, norm_post_mix, norm_pre_ffn, norm_post_ffn, w_in, b_in, conv_w, conv_b, rg_w_a, rg_b_a, rg_w_x, rg_b_x, rg_lambda, attn_sinks, w_o_rnn, w_o_attn, w_out, router_w, router_b, moe_w1, moe_b1, moe_w2, moe_b2):
    B, S, D = x.shape
    params = dict(
        w_ada=w_ada, b_ada=b_ada, norm_pre_mix=norm_pre_mix, norm_post_mix=norm_post_mix,
        norm_pre_ffn=norm_pre_ffn, norm_post_ffn=norm_post_ffn, w_in=w_in, b_in=b_in,
        conv_w=conv_w, conv_b=conv_b, rg_w_a=rg_w_a, rg_b_a=rg_b_a, rg_w_x=rg_w_x, rg_b_x=rg_b_x,
        rg_lambda=rg_lambda, attn_sinks=attn_sinks, w_o_rnn=w_o_rnn, w_o_attn=w_o_attn,
        w_out=w_out, router_w=router_w, router_b=router_b, moe_w1=moe_w1, moe_b1=moe_b1,
        moe_w2=moe_w2, moe_b2=moe_b2)
    x2 = x.reshape(B * S, D)
    for layer in range(w_ada.shape[0]):
        x2 = _layer(x2, c, B, S, {k: v[layer] for k, v in params.items()})
    return x2.reshape(B, S, D)
```

```python
import functools
import math

import jax
import jax.numpy as jnp
import numpy as np
from jax import lax
from jax.experimental import pallas as pl
from jax.experimental.pallas import tpu as pltpu

D_MODEL = 1024
D_RNN = 1024
RNN_BLOCKS = 16
RNN_BW = D_RNN // RNN_BLOCKS
CONV_W = 4
LRU_C = 8.0
N_HEADS = 16
N_KV = 4
HEAD_DIM = 64
GROUP = N_HEADS // N_KV
WINDOW = 128
Q_DIM = N_HEADS * HEAD_DIM
KV_DIM = N_KV * HEAD_DIM
N_EXPERTS = 32
TOP_K = 4
D_FF = 1024
SWIGLU_LIMIT = 7.0
SWIGLU_ALPHA = 1.702
EPS = 1e-6
D_IN = 2 * D_RNN + Q_DIM + 2 * KV_DIM + 2 * D_MODEL
D_REST = D_IN - 2 * D_RNN

V7X_LANES = 128
V7X_SUBLANES = 8
V7X_MXU_DIM = 256
V7X_VMEM_BYTES = 64 * 1024 * 1024

F32 = jnp.float32
BF16 = jnp.bfloat16


def _tiles():
    return dict(
        ada_tn=1024,
        tok=512,
        attn_q=WINDOW,
        moe_t=256,
        moe_bm=256,
        rnn_group=V7X_MXU_DIM,
    )


def _vmem_limit(nbytes):
    return int(min(max(nbytes, 16 * 1024 * 1024), V7X_VMEM_BYTES - 8 * 1024 * 1024))


def _rms(x, g):
    return x * lax.rsqrt(jnp.mean(x * x, axis=-1, keepdims=True) + EPS) * g


def _ada_kernel(ct_ref, w_ref, b_ref, o_ref):
    ct = ct_ref[...]
    sc = ct * jax.nn.sigmoid(ct)
    w = w_ref[...]
    rows = [jnp.sum(w * sc[:, b:b + 1], axis=0, keepdims=True) for b in range(ct.shape[1])]
    o_ref[...] = jnp.concatenate(rows, axis=0) + b_ref[...]


def _ada(c, w_ada, b_ada):
    B, D = c.shape
    n_out = w_ada.shape[1]
    tn = _tiles()["ada_tn"]
    return pl.pallas_call(
        _ada_kernel,
        grid=(n_out // tn,),
        in_specs=[
            pl.BlockSpec((D, B), lambda j: (0, 0)),
            pl.BlockSpec((D, tn), lambda j: (0, j)),
            pl.BlockSpec((1, tn), lambda j: (0, j)),
        ],
        out_specs=pl.BlockSpec((B, tn), lambda j: (0, j)),
        out_shape=jax.ShapeDtypeStruct((B, n_out), F32),
        compiler_params=pltpu.CompilerParams(
            dimension_semantics=("arbitrary",),
            vmem_limit_bytes=_vmem_limit(4 * D * tn * 4)),
        name="ada",
    )(c.T, w_ada, b_ada.reshape(1, n_out))


def _gelu_tanh(x):
    return 0.5 * x * (1.0 + jnp.tanh(math.sqrt(2.0 / math.pi) * (x + 0.044715 * (x * x * x))))


def _softplus(z):
    return jnp.maximum(z, 0.0) + jnp.log1p(jnp.exp(-jnp.abs(z)))


def _sigmoid(x):
    return 0.5 * jnp.tanh(0.5 * x) + 0.5


def _mixin_kernel(x_ref, g_ref, sc_ref, sh_ref, w_ref, b_ref, cw_ref, cb_ref, wa_ref, ba_ref,
                  wx_ref, bx_ref, lam_ref, rest_ref, y_ref,
                  hb_s, xbuf, gr_s, a_s, b_s, h_s, ga_s, gb_s, gc_s, carry, *, per_b, gw, chunk):
    t = pl.program_id(0) % per_b
    tt = x_ref.shape[0]
    halo = V7X_SUBLANES
    ng = tt // V7X_SUBLANES
    n_lt = a_s.shape[0]
    per_g = gw // V7X_LANES
    lanes = lambda j: slice(j * V7X_LANES, (j + 1) * V7X_LANES)
    slab = lambda j, r: (j, pl.ds(r, ng, stride=V7X_SUBLANES), slice(None))

    @pl.when(t == 0)
    def _():
        xbuf[:, 0:halo, :] = jnp.zeros((n_lt, halo, V7X_LANES), F32)
        carry[...] = jnp.zeros_like(carry)

    hb_s[...] = (_rms(x_ref[...], g_ref[...]) * (1.0 + sc_ref[0]) + sh_ref[0]).astype(BF16)

    def proj(c0):
        return (jnp.dot(hb_s[...], w_ref[:, c0:c0 + chunk], preferred_element_type=F32)
                + b_ref[:, c0:c0 + chunk])

    def proj_rest(c0):
        rest_ref[:, c0 - 2 * D_RNN:c0 - 2 * D_RNN + chunk] = proj(c0).astype(BF16)

    for c0 in range(0, D_RNN, chunk):
        acc = proj(c0)
        for j in range(chunk // V7X_LANES):
            xbuf[c0 // V7X_LANES + j, halo:halo + tt, :] = acc[:, lanes(j)]
    for c0 in range(D_RNN, 2 * D_RNN, chunk):
        gr_s[:, c0 - D_RNN:c0 - D_RNN + chunk] = proj(c0)

    def conv(j):
        ls = lanes(j)
        taps = {o: xbuf[slab(j, halo + o)] for o in range(-(CONV_W - 1), V7X_SUBLANES)}
        for r in range(V7X_SUBLANES):
            acc = cb_ref[:, ls] + taps[r - (CONV_W - 1)] * cw_ref[0:1, ls]
            for kk in range(1, CONV_W):
                acc = acc + taps[r - (CONV_W - 1) + kk] * cw_ref[kk:kk + 1, ls]
            h_s[slab(j, r)] = acc
        xbuf[j, 0:halo, :] = xbuf[j, tt:tt + halo, :]

    def gates(g):
        cs = slice(g * gw, (g + 1) * gw)
        sp = _softplus(-lam_ref[:, cs])
        reset = jnp.logical_and(t == 0, lax.broadcasted_iota(jnp.int32, (tt, gw), 0) == 0)
        xc = jnp.concatenate([h_s[g * per_g + j] for j in range(per_g)], axis=1)
        xg = xc.astype(BF16)
        gate_r = _sigmoid(jnp.dot(xg, wa_ref[g], preferred_element_type=F32) + ba_ref[:, cs])
        gate_i = _sigmoid(jnp.dot(xg, wx_ref[g], preferred_element_type=F32) + bx_ref[:, cs])
        a = jnp.exp(-LRU_C * gate_r * sp)
        mult = jnp.where(reset, 1.0, jnp.sqrt((1.0 - a) * (1.0 + a)))
        bt = (xc * gate_i) * mult
        for j in range(per_g):
            a_s[g * per_g + j] = a[:, lanes(j)]
            b_s[g * per_g + j] = bt[:, lanes(j)]

    def scan_groups(j):
        acc_a = a_s[slab(j, 0)]
        acc_h = b_s[slab(j, 0)]
        h_s[slab(j, 0)] = acc_h
        for r in range(1, V7X_SUBLANES):
            a_r = a_s[slab(j, r)]
            acc_h = a_r * acc_h + b_s[slab(j, r)]
            acc_a = a_r * acc_a
            h_s[slab(j, r)] = acc_h
            a_s[slab(j, r)] = acc_a
        ga_s[:, lanes(j)] = acc_a
        gb_s[:, lanes(j)] = acc_h

    def across(gi, h_prev):
        gc_s[pl.ds(gi, 1), :] = h_prev
        return ga_s[pl.ds(gi, 1), :] * h_prev + gb_s[pl.ds(gi, 1), :]

    def finish(j):
        h_in = gc_s[:, lanes(j)]
        for r in range(V7X_SUBLANES):
            h_s[slab(j, r)] = a_s[slab(j, r)] * h_in + h_s[slab(j, r)]
        y_ref[:, lanes(j)] = (h_s[j] * _gelu_tanh(gr_s[:, lanes(j)])).astype(BF16)

    rest = [functools.partial(proj_rest, c0) for c0 in range(2 * D_RNN, w_ref.shape[1], chunk)]
    before = ([functools.partial(conv, j) for j in range(n_lt)]
              + [functools.partial(gates, g) for g in range(n_lt // per_g)]
              + [functools.partial(scan_groups, j) for j in range(n_lt)])
    after = [functools.partial(finish, j) for j in range(n_lt)]
    n_before = (len(rest) * len(before)) // (len(before) + len(after))

    def interleave(steps, chunks):
        every = -(-len(steps) // max(len(chunks), 1))
        for n, step in enumerate(steps):
            step()
            if (n + 1) % every == 0 and chunks:
                chunks.pop(0)()
        while chunks:
            chunks.pop(0)()

    interleave(before, rest[:n_before])
    carry[...] = lax.fori_loop(0, ng, across, carry[...])
    interleave(after, rest[n_before:])


def _block_diag_tiles(w, gw):
    nb, bw, _ = w.shape
    per = gw // bw
    w4 = w.reshape(nb // per, per, bw, bw)
    eye = jnp.eye(per, dtype=w.dtype)
    return jnp.einsum("gpij,pq->gpiqj", w4, eye).reshape(nb // per, gw, gw)


def _mixin(x2, g, sc, sh, w_bf, b_in, conv_w, conv_b, wa, ba, wx, bx, lam, S):
    N, D = x2.shape
    C = D_RNN
    tt = _tiles()["tok"]
    gw = _tiles()["rnn_group"]
    per_b = S // tt
    chunk = 2 * V7X_MXU_DIM
    vec = lambda: pl.BlockSpec((1, C), lambda i: (0, 0))
    bvec = lambda: pl.BlockSpec((1, 1, D), lambda i: (i // per_b, 0, 0))
    gate_w = lambda: pl.BlockSpec((C // gw, gw, gw), lambda i: (0, 0, 0))
    slabs = lambda rows: pltpu.VMEM((C // V7X_LANES, rows, V7X_LANES), F32)
    groups = lambda: pltpu.VMEM((tt // V7X_SUBLANES, C), F32)
    vmem = D * D_IN * 2 + 2 * tt * (D * 4 + D_REST * 2 + C * 2) + tt * C * (2 + 5 * 4) + 8 * tt * chunk * 4
    return pl.pallas_call(
        functools.partial(_mixin_kernel, per_b=per_b, gw=gw, chunk=chunk),
        grid=(N // tt,),
        in_specs=[
            pl.BlockSpec((tt, D), lambda i: (i, 0)),
            pl.BlockSpec((1, D), lambda i: (0, 0)),
            bvec(), bvec(),
            pl.BlockSpec((D, D_IN), lambda i: (0, 0), pipeline_mode=pl.Buffered(1)),
            pl.BlockSpec((1, D_IN), lambda i: (0, 0)),
            pl.BlockSpec((CONV_W, C), lambda i: (0, 0)),
            vec(), gate_w(), vec(), gate_w(), vec(), vec(),
        ],
        out_specs=[
            pl.BlockSpec((tt, D_REST), lambda i: (i, 0)),
            pl.BlockSpec((tt, C), lambda i: (i, 0)),
        ],
        out_shape=[
            jax.ShapeDtypeStruct((N, D_REST), BF16),
            jax.ShapeDtypeStruct((N, C), BF16),
        ],
        scratch_shapes=[
            pltpu.VMEM((tt, D), BF16),
            slabs(tt + V7X_SUBLANES),
            pltpu.VMEM((tt, C), F32),
            slabs(tt), slabs(tt), slabs(tt),
            groups(), groups(), groups(),
            pltpu.VMEM((1, C), F32),
        ],
        compiler_params=pltpu.CompilerParams(
            dimension_semantics=("arbitrary",),
            vmem_limit_bytes=_vmem_limit(vmem)),
        name="mixin",
    )(x2, g, sc, sh, w_bf, b_in, conv_w, conv_b, wa, ba, wx, bx, lam)


def _alibi_slopes():
    return [2.0 ** (-8.0 * (h + 1) / N_HEADS) for h in range(N_HEADS)]


def _attn_kernel(sink_ref, q_ref, kp_ref, kc_ref, vp_ref, vc_ref, o_ref, bias_s, *, per_b):
    blk = pl.program_id(0) % per_b
    bq = q_ref.shape[0]
    slopes = _alibi_slopes()

    @pl.when(pl.program_id(0) == 0)
    def _():
        qi = lax.broadcasted_iota(jnp.int32, (bq, 2 * bq), 0)
        ci = lax.broadcasted_iota(jnp.int32, (bq, 2 * bq), 1)
        dist = qi + bq - ci
        valid = (dist >= 0) & (dist < WINDOW)
        distf = dist.astype(F32)
        for h in range(N_HEADS):
            b = jnp.where(valid, -slopes[h] * distf, -jnp.inf)
            bias_s[0, h] = jnp.where(ci >= bq, b, -jnp.inf)
            bias_s[1, h] = b

    assert 2 * HEAD_DIM == V7X_LANES and GROUP % 2 == 0
    table = jnp.minimum(blk, 1)
    low = lax.broadcasted_iota(jnp.int32, (bq, V7X_LANES), 1) < HEAD_DIM
    zero = jnp.zeros((bq, V7X_LANES), BF16)
    for kvt in range(N_KV // 2):
        lt = slice(kvt * V7X_LANES, (kvt + 1) * V7X_LANES)
        k_t = jnp.concatenate([kp_ref[:, lt], kc_ref[:, lt]], axis=0)
        v_t = jnp.concatenate([vp_ref[:, lt], vc_ref[:, lt]], axis=0)
        k_swapped = pltpu.roll(k_t, HEAD_DIM, 1)
        for kv_half in range(2):
            kv = 2 * kvt + kv_half
            for pair in range(GROUP // 2):
                h0 = kv * GROUP + 2 * pair
                q_t = q_ref[:, (h0 // 2) * V7X_LANES:(h0 // 2 + 1) * V7X_LANES]
                halves = []
                for q_half in range(2):
                    h = h0 + q_half
                    qm = jnp.where(low if q_half == 0 else ~low, q_t, zero)
                    kk = k_t if q_half == kv_half else k_swapped
                    s = lax.dot_general(qm, kk, (((1,), (1,)), ((), ())), preferred_element_type=F32)
                    s = s + bias_s[table, h]
                    sink = sink_ref[h]
                    m = jnp.maximum(jnp.max(s, axis=-1, keepdims=True), sink)
                    p = jnp.exp(s - m)
                    denom = jnp.sum(p, axis=-1, keepdims=True) + jnp.exp(sink - m)
                    halves.append(jnp.dot(p.astype(BF16), v_t, preferred_element_type=F32) / denom)
                if kv_half == 0:
                    o = jnp.where(low, halves[0], pltpu.roll(halves[1], HEAD_DIM, 1))
                else:
                    o = jnp.where(low, pltpu.roll(halves[0], HEAD_DIM, 1), halves[1])
                o_ref[:, (h0 // 2) * V7X_LANES:(h0 // 2 + 1) * V7X_LANES] = o.astype(BF16)


def _attn(proj_rest, sinks, S):
    N = proj_rest.shape[0]
    bq = _tiles()["attn_q"]
    per_b = S // bq
    k_col = (Q_DIM + 2 * D_MODEL) // KV_DIM
    v_col = k_col + 1

    def prev(i):
        return jnp.where(i % per_b == 0, i, i - 1)

    return pl.pallas_call(
        functools.partial(_attn_kernel, per_b=per_b),
        grid=(N // bq,),
        in_specs=[
            pl.BlockSpec(memory_space=pltpu.SMEM),
            pl.BlockSpec((bq, Q_DIM), lambda i: (i, 0)),
            pl.BlockSpec((bq, KV_DIM), lambda i: (prev(i), k_col)),
            pl.BlockSpec((bq, KV_DIM), lambda i: (i, k_col)),
            pl.BlockSpec((bq, KV_DIM), lambda i: (prev(i), v_col)),
            pl.BlockSpec((bq, KV_DIM), lambda i: (i, v_col)),
        ],
        out_specs=pl.BlockSpec((bq, Q_DIM), lambda i: (i, 0)),
        out_shape=jax.ShapeDtypeStruct((N, Q_DIM), BF16),
        scratch_shapes=[pltpu.VMEM((2, N_HEADS, bq, 2 * bq), F32)],
        compiler_params=pltpu.CompilerParams(
            dimension_semantics=("arbitrary",),
            vmem_limit_bytes=_vmem_limit(3 * 2 * N_HEADS * bq * 2 * bq * 4)),
        name="attn",
    )(sinks, proj_rest, proj_rest, proj_rest, proj_rest, proj_rest)


def _merge_kernel(x_ref, yr_ref, ya_ref, gr_ref, ga_ref, wr_ref, wa_ref, wo_ref,
                  gpost_ref, g1_ref, gpre_ref, sc2_ref, sh2_ref, rwt_ref, rb_ref,
                  x1_ref, h2_ref, lg_ref):
    r = jnp.dot(yr_ref[...], wr_ref[...], preferred_element_type=F32)
    a = jnp.dot(ya_ref[...], wa_ref[...], preferred_element_type=F32)
    merged = (_sigmoid(gr_ref[...].astype(F32)) * r
              + _sigmoid(ga_ref[...].astype(F32)) * a)
    mix = jnp.dot(merged.astype(BF16), wo_ref[...], preferred_element_type=F32)
    x1 = x_ref[...] + g1_ref[0] * _rms(mix, gpost_ref[...])
    x1_ref[...] = x1
    h2 = _rms(x1, gpre_ref[...]) * (1.0 + sc2_ref[0]) + sh2_ref[0]
    h2_ref[...] = h2
    lg = lax.dot_general(rwt_ref[...], h2.astype(BF16), (((1,), (1,)), ((), ())),
                         preferred_element_type=F32)
    lg_ref[...] = lg + rb_ref[...]


def _merge(x2, y_rnn, y_att, proj_rest, wr, wa, wo, gpost, g1, gpre, sc2, sh2, rwt, rb, S):
    N, D = x2.shape
    tm = _tiles()["tok"]
    per_b = S // tm
    gate_r_col = Q_DIM // D
    mat = lambda: pl.BlockSpec((D, D), lambda i: (0, 0))
    vec = lambda: pl.BlockSpec((1, D), lambda i: (0, 0))
    bvec = lambda: pl.BlockSpec((1, 1, D), lambda i: (i // per_b, 0, 0))
    tile = lambda col=0: pl.BlockSpec((tm, D), lambda i: (i, col))
    return pl.pallas_call(
        _merge_kernel,
        grid=(N // tm,),
        in_specs=[
            tile(), tile(), tile(), tile(gate_r_col), tile(gate_r_col + 1),
            mat(), mat(), mat(),
            vec(), bvec(), vec(), bvec(), bvec(),
            pl.BlockSpec((N_EXPERTS, D), lambda i: (0, 0)),
            pl.BlockSpec((N_EXPERTS, 1), lambda i: (0, 0)),
        ],
        out_specs=[
            tile(), tile(),
            pl.BlockSpec((N_EXPERTS, tm), lambda i: (0, i)),
        ],
        out_shape=[
            jax.ShapeDtypeStruct((N, D), F32),
            jax.ShapeDtypeStruct((N, D), F32),
            jax.ShapeDtypeStruct((N_EXPERTS, N), F32),
        ],
        compiler_params=pltpu.CompilerParams(
            dimension_semantics=("arbitrary",),
            vmem_limit_bytes=_vmem_limit(6 * D * D * 2 + 24 * tm * D * 4)),
        name="merge",
    )(x2, y_rnn, y_att, proj_rest, proj_rest, wr, wa, wo, gpost, g1, gpre, sc2, sh2, rwt, rb)


def _route_kernel(lg_ref, g_ref, slot_ref, cnt_ref, tcnt_ref, tcar_ref, tlst_ref, carry):
    i = pl.program_id(0)

    @pl.when(i == 0)
    def _():
        carry[...] = jnp.zeros_like(carry)
        tcnt_ref[...] = jnp.zeros_like(tcnt_ref)
        tcar_ref[...] = jnp.zeros_like(tcar_ref)
        tlst_ref[...] = jnp.zeros_like(tlst_ref)

    l = lg_ref[...]
    E, T = l.shape
    row = lax.broadcasted_iota(jnp.int32, (E, T), 0).astype(F32)
    vals, idxs = [], []
    for _ in range(TOP_K):
        m = jnp.max(l, axis=0, keepdims=True)
        idx = jnp.min(jnp.where(l == m, row, float(E)), axis=0, keepdims=True)
        vals.append(m)
        idxs.append(idx)
        l = jnp.where(row == idx, -jnp.inf, l)
    ex = [jnp.exp(v - vals[0]) for v in vals]
    tot = ex[0]
    for e in ex[1:]:
        tot = tot + e
    g_ref[...] = jnp.concatenate([e / tot for e in ex], axis=0)

    hot = [row == idx for idx in idxs]
    onehot = jnp.zeros((E, T), F32)
    for hk in hot:
        onehot = onehot + hk.astype(F32)
    tri_t = (lax.broadcasted_iota(jnp.int32, (T, T), 0)
             < lax.broadcasted_iota(jnp.int32, (T, T), 1)).astype(BF16)
    before = jnp.dot(onehot.astype(BF16), tri_t, preferred_element_type=F32)
    cnt = jnp.sum(onehot, axis=1, keepdims=True)
    tri_e = (lax.broadcasted_iota(jnp.int32, (E, E), 1)
             < lax.broadcasted_iota(jnp.int32, (E, E), 0)).astype(BF16)
    lstart = jnp.dot(tri_e, jnp.broadcast_to(cnt, (E, V7X_LANES)).astype(BF16),
                     preferred_element_type=F32)[:, 0:1]
    local = before + lstart
    slots = [jnp.sum(jnp.where(hk, local, 0.0), axis=0, keepdims=True) for hk in hot]
    slot_ref[...] = jnp.concatenate(slots, axis=0).astype(jnp.int32)

    mine = lax.broadcasted_iota(jnp.int32, tcnt_ref.shape, 1) == i
    tcnt_ref[...] = jnp.where(mine, cnt.astype(jnp.int32), tcnt_ref[...])
    tcar_ref[...] = jnp.where(mine, carry[...].astype(jnp.int32), tcar_ref[...])
    tlst_ref[...] = jnp.where(mine, lstart.astype(jnp.int32), tlst_ref[...])
    total = carry[...] + cnt
    carry[...] = total
    cnt_ref[...] = total.astype(jnp.int32)


def _route(logits_t):
    E, N = logits_t.shape
    T = _tiles()["moe_t"]
    assert T <= 2 ** 8 and N // T <= V7X_LANES
    out = lambda: pl.BlockSpec((TOP_K, T), lambda i: (0, i))
    per_tile = lambda: pl.BlockSpec((E, V7X_LANES), lambda i: (0, 0))
    return pl.pallas_call(
        _route_kernel,
        grid=(N // T,),
        in_specs=[pl.BlockSpec((E, T), lambda i: (0, i))],
        out_specs=[out(), out(), pl.BlockSpec((E, 1), lambda i: (0, 0)),
                   per_tile(), per_tile(), per_tile()],
        out_shape=[
            jax.ShapeDtypeStruct((TOP_K, N), F32),
            jax.ShapeDtypeStruct((TOP_K, N), jnp.int32),
            jax.ShapeDtypeStruct((E, 1), jnp.int32),
            jax.ShapeDtypeStruct((E, V7X_LANES), jnp.int32),
            jax.ShapeDtypeStruct((E, V7X_LANES), jnp.int32),
            jax.ShapeDtypeStruct((E, V7X_LANES), jnp.int32),
        ],
        scratch_shapes=[pltpu.VMEM((E, 1), F32)],
        compiler_params=pltpu.CompilerParams(dimension_semantics=("arbitrary",)),
        name="route",
    )(logits_t)


def _run_sizes(T):
    return [2 ** b for b in range(int(math.log2(T)), -1, -1)]


def _copy_plan(tile_cnt, tile_lst, tile_dst, T):
    sizes = jnp.asarray(_run_sizes(T), jnp.int32)[None, :, None]
    cnt, lst, dst = (a[:, None, :] for a in (tile_cnt, tile_lst, tile_dst))
    has = (cnt & sizes) != 0
    off = (cnt // (2 * sizes)) * (2 * sizes)
    eids = jnp.arange(N_EXPERTS)
    earlier = eids[:, None] < eids[None, :]
    place = jnp.sum(jnp.where(earlier, has[..., :, None], False), axis=-2)
    hit = has[..., :, None] & (place[..., :, None] == eids)
    pick = lambda v: jnp.sum(jnp.where(hit, v[..., :, None], 0), axis=-2)
    n_tiles, nb = tile_cnt.shape[0], sizes.shape[1]
    parts = [pick(jnp.broadcast_to(lst + off, has.shape)).reshape(n_tiles, -1),
             pick(jnp.broadcast_to(dst + off, has.shape)).reshape(n_tiles, -1),
             jnp.sum(has, axis=-1).reshape(n_tiles, nb)]
    plan = jnp.concatenate(parts, axis=1).astype(jnp.int32)
    pad = _plan_len(T) - plan.shape[1]
    return jnp.pad(plan, ((0, 0), (0, pad))).reshape(-1)


def _plan_len(T):
    nb = len(_run_sizes(T))
    return max(V7X_LANES, pl.next_power_of_2(2 * nb * N_EXPERTS + nb))


def _run_copies(plan_ref, T, make):
    sizes = _run_sizes(T)
    nb = len(sizes)
    for bi, b in enumerate(sizes):
        def body(j, c, bi=bi, b=b):
            make(plan_ref[bi * N_EXPERTS + j], plan_ref[(nb + bi) * N_EXPERTS + j], b).start()
            return c

        lax.fori_loop(0, plan_ref[2 * nb * N_EXPERTS + bi], body, 0)


def _row_slab(ref, row, nrows, lead=()):
    rows = pl.ds(pl.multiple_of(row * V7X_SUBLANES, V7X_SUBLANES), nrows * V7X_SUBLANES)
    return ref.at[(*lead, rows)]


def _to_slabs(ref, val, lead=()):
    rows = val.shape[0]
    for s in range(V7X_SUBLANES):
        ref[(*lead, pl.ds(s, rows, stride=V7X_SUBLANES), slice(None))] = (
            val[:, s * V7X_LANES:(s + 1) * V7X_LANES])


def _from_slabs(ref, lead=()):
    rows = ref.shape[-2] // V7X_SUBLANES
    return jnp.concatenate(
        [ref[(*lead, pl.ds(s, rows, stride=V7X_SUBLANES), slice(None))]
         for s in range(V7X_SUBLANES)], axis=1)


def _dispatch_kernel(tail_ref, nu_ref, plan_ref, slot_ref, h_ref, xs_ref, buf, zbuf, sem, zsem):
    i = pl.program_id(0)
    T = h_ref.shape[0]
    bm = zbuf.shape[0] // V7X_SUBLANES
    nblk = xs_ref.shape[0] // zbuf.shape[0]
    cur = i % 2

    def zero_copy(row0):
        return pltpu.make_async_copy(zbuf, _row_slab(xs_ref, row0, bm), zsem)

    @pl.when(i == 0)
    def _():
        zbuf[...] = jnp.zeros_like(zbuf)

        def start(e, c):
            zero_copy(tail_ref[e]).start()
            return c

        def wait(e, c):
            zero_copy(tail_ref[e]).wait()
            return c

        def start_unused(j, c):
            zero_copy(j * bm).start()
            return c

        def wait_unused(j, c):
            zero_copy(j * bm).wait()
            return c

        lax.fori_loop(0, N_EXPERTS, start, 0)
        lax.fori_loop(nu_ref[0], nblk, start_unused, 0)
        lax.fori_loop(0, N_EXPERTS, wait, 0)
        lax.fori_loop(nu_ref[0], nblk, wait_unused, 0)

    slot = slot_ref[...]
    r_id = lax.broadcasted_iota(jnp.int32, (TOP_K * T, T), 0)
    perm = r_id == slot[0:1, :]
    for k in range(1, TOP_K):
        perm = jnp.logical_or(perm, r_id == slot[k:k + 1, :])
    grouped = jnp.dot(perm.astype(BF16), h_ref[...].astype(BF16), preferred_element_type=F32)
    _to_slabs(buf, grouped, lead=(cur,))

    def wait_tile(b):
        pltpu.make_async_copy(buf.at[b], _row_slab(xs_ref, 0, TOP_K * T), sem.at[b]).wait()

    @pl.when(i > 0)
    def _():
        wait_tile(1 - cur)

    _run_copies(plan_ref, T, lambda l, d, n: pltpu.make_async_copy(
        _row_slab(buf, l, n, lead=(cur,)), _row_slab(xs_ref, d, n), sem.at[cur]))

    @pl.when(i == pl.num_programs(0) - 1)
    def _():
        wait_tile(cur)


def _dispatch(h2, slot, plan, tail_start, n_used, n_rows, bm):
    N, D = h2.shape
    T = _tiles()["moe_t"]
    S8 = V7X_SUBLANES
    imap = lambda f: (lambda i, *_: f(i))
    return pl.pallas_call(
        _dispatch_kernel,
        grid_spec=pltpu.PrefetchScalarGridSpec(
            num_scalar_prefetch=2,
            grid=(N // T,),
            in_specs=[
                pl.BlockSpec((_plan_len(T),), imap(lambda i: (i,)), memory_space=pltpu.SMEM),
                pl.BlockSpec((TOP_K, T), imap(lambda i: (0, i))),
                pl.BlockSpec((T, D), imap(lambda i: (i, 0))),
            ],
            out_specs=pl.BlockSpec(memory_space=pl.ANY),
            scratch_shapes=[
                pltpu.VMEM((2, TOP_K * T * S8, V7X_LANES), F32),
                pltpu.VMEM((bm * S8, V7X_LANES), F32),
                pltpu.SemaphoreType.DMA((2,)),
                pltpu.SemaphoreType.DMA(()),
            ],
        ),
        out_shape=jax.ShapeDtypeStruct((n_rows * S8, V7X_LANES), F32),
        compiler_params=pltpu.CompilerParams(
            dimension_semantics=("arbitrary",),
            vmem_limit_bytes=_vmem_limit(8 * TOP_K * T * D * 4)),
        name="dispatch",
    )(tail_start, n_used, plan, slot, h2)


def _ffn_kernel(be_ref, bv_ref, bf_ref, nu_ref, nxt_ref, x_ref, w1_hbm, b1g_ref, b1l_ref, w2_hbm,
                b2_ref, y_ref, w1f_s, w2f_s, w1g_s, w1l_s, w2_s, sem):
    i = pl.program_id(0)
    bm, D = x_ref.shape[0] // V7X_SUBLANES, w2_s.shape[1]
    pw = 2 * V7X_LANES

    def fetch(e):
        return (pltpu.make_async_copy(w1_hbm.at[e], w1f_s, sem.at[0]),
                pltpu.make_async_copy(w2_hbm.at[e], w2f_s, sem.at[1]))

    @pl.when(i == 0)
    def _():
        for cp in fetch(be_ref[0]):
            cp.start()

    @pl.when(jnp.logical_and(i < nu_ref[0], bf_ref[i] == 1))
    def _():
        e = be_ref[i]
        for cp in fetch(e):
            cp.wait()
        src = lax.broadcasted_iota(jnp.int32, (pw, pw), 0)
        dst = lax.broadcasted_iota(jnp.int32, (pw, pw), 1)
        want = jnp.where(dst < V7X_LANES, 2 * dst, 2 * (dst - V7X_LANES) + 1)
        perm = (src == want).astype(BF16)
        for c in range(w1f_s.shape[1] // pw):
            wp = jnp.dot(w1f_s[:, c * pw:(c + 1) * pw].astype(BF16), perm,
                         preferred_element_type=F32).astype(BF16)
            w1g_s[:, c * V7X_LANES:(c + 1) * V7X_LANES] = wp[:, :V7X_LANES]
            w1l_s[:, c * V7X_LANES:(c + 1) * V7X_LANES] = wp[:, V7X_LANES:]
        w2_s[...] = w2f_s[...].astype(BF16)

        @pl.when(nxt_ref[e] >= 0)
        def _():
            for cp in fetch(nxt_ref[e]):
                cp.start()

    @pl.when(i < nu_ref[0])
    def _():
        rows = lax.broadcasted_iota(jnp.int32, (bm, D), 0)
        x = jnp.where(rows < bv_ref[i], _from_slabs(x_ref), 0.0).astype(BF16)
        glu = jnp.dot(x, w1g_s[...], preferred_element_type=F32) + b1g_ref[0]
        lin = jnp.dot(x, w1l_s[...], preferred_element_type=F32) + b1l_ref[0]
        glu = jnp.minimum(glu, SWIGLU_LIMIT)
        lin = jnp.clip(lin, -SWIGLU_LIMIT, SWIGLU_LIMIT)
        act = glu * _sigmoid(SWIGLU_ALPHA * glu) * (lin + 1.0)
        _to_slabs(y_ref, jnp.dot(act.astype(BF16), w2_s[...], preferred_element_type=F32) + b2_ref[0])

    @pl.when(i >= nu_ref[0])
    def _():
        y_ref[...] = jnp.zeros_like(y_ref)


def _ffn(xs, blk_e, blk_valid, blk_first, n_used, next_e, w1, b1g, b1l, w2, b2, bm):
    E, D, F2 = w1.shape
    F = F2 // 2
    slab = bm * V7X_SUBLANES
    nblk = xs.shape[0] // slab

    def row_blk(i, be, bv, bf, nu, nxt):
        return (jnp.minimum(i, nu[0] - 1), 0)

    def per_e(i, be, bv, bf, nu, nxt):
        return (be[i], 0, 0)

    vmem = (D * F2 + F * D) * (4 + 2) + 8 * bm * F2 * 4
    return pl.pallas_call(
        _ffn_kernel,
        grid_spec=pltpu.PrefetchScalarGridSpec(
            num_scalar_prefetch=5,
            grid=(nblk,),
            in_specs=[
                pl.BlockSpec((slab, V7X_LANES), row_blk),
                pl.BlockSpec(memory_space=pl.ANY),
                pl.BlockSpec((1, 1, F), per_e),
                pl.BlockSpec((1, 1, F), per_e),
                pl.BlockSpec(memory_space=pl.ANY),
                pl.BlockSpec((1, 1, D), per_e),
            ],
            out_specs=pl.BlockSpec((slab, V7X_LANES), lambda i, be, bv, bf, nu, nxt: (i, 0)),
            scratch_shapes=[
                pltpu.VMEM((D, F2), F32),
                pltpu.VMEM((F, D), F32),
                pltpu.VMEM((D, F), BF16),
                pltpu.VMEM((D, F), BF16),
                pltpu.VMEM((F, D), BF16),
                pltpu.SemaphoreType.DMA((2,)),
            ],
        ),
        out_shape=jax.ShapeDtypeStruct(xs.shape, F32),
        compiler_params=pltpu.CompilerParams(
            dimension_semantics=("arbitrary",),
            vmem_limit_bytes=_vmem_limit(vmem)),
        name="ffn",
    )(blk_e, blk_valid, blk_first, n_used, next_e, xs, w1, b1g, b1l, w2, b2)


def _combine_kernel(plan_ref, plan_next_ref, ys_ref, x1_ref, slot_ref, gate_ref, gpost_ref, g2_ref,
                    o_ref, buf, sem):
    i = pl.program_id(0)
    T = x1_ref.shape[0]
    cur = i % 2

    def gather(plan, b):
        _run_copies(plan, T, lambda l, d, n: pltpu.make_async_copy(
            _row_slab(ys_ref, d, n), _row_slab(buf, l, n, lead=(b,)), sem.at[b]))

    @pl.when(i == 0)
    def _():
        gather(plan_ref, cur)

    @pl.when(i + 1 < pl.num_programs(0))
    def _():
        gather(plan_next_ref, 1 - cur)

    pltpu.make_async_copy(_row_slab(ys_ref, 0, TOP_K * T), buf.at[cur], sem.at[cur]).wait()

    slot = slot_ref[...]
    gates = gate_ref[...]
    r_id = lax.broadcasted_iota(jnp.int32, (T, TOP_K * T), 1)
    w = jnp.zeros((T, TOP_K * T), F32)
    for k in range(TOP_K):
        w = jnp.where(r_id == slot[:, k:k + 1], gates[:, k:k + 1], w)
    ff = jnp.dot(w.astype(BF16), _from_slabs(buf, lead=(cur,)).astype(BF16),
                 preferred_element_type=F32)
    o_ref[...] = x1_ref[...] + g2_ref[0] * _rms(ff, gpost_ref[...])


def _combine(ys, plan, x1, slot_tk, gates_tk, gpost, g2, S):
    N, D = x1.shape
    T = _tiles()["moe_t"]
    per_b = S // T
    n_tiles = N // T
    plan_spec = lambda f: pl.BlockSpec((_plan_len(T),), f, memory_space=pltpu.SMEM)
    return pl.pallas_call(
        _combine_kernel,
        grid=(n_tiles,),
        in_specs=[
            plan_spec(lambda i: (i,)),
            plan_spec(lambda i: (jnp.minimum(i + 1, n_tiles - 1),)),
            pl.BlockSpec(memory_space=pl.ANY),
            pl.BlockSpec((T, D), lambda i: (i, 0)),
            pl.BlockSpec((T, TOP_K), lambda i: (i, 0)),
            pl.BlockSpec((T, TOP_K), lambda i: (i, 0)),
            pl.BlockSpec((1, D), lambda i: (0, 0)),
            pl.BlockSpec((1, 1, D), lambda i: (i // per_b, 0, 0)),
        ],
        out_specs=pl.BlockSpec((T, D), lambda i: (i, 0)),
        out_shape=jax.ShapeDtypeStruct((N, D), F32),
        scratch_shapes=[
            pltpu.VMEM((2, TOP_K * T * V7X_SUBLANES, V7X_LANES), F32),
            pltpu.SemaphoreType.DMA((2,)),
        ],
        compiler_params=pltpu.CompilerParams(
            dimension_semantics=("arbitrary",),
            vmem_limit_bytes=_vmem_limit(10 * TOP_K * T * D * 4)),
        name="combine",
    )(plan, plan, ys, x1, slot_tk, gates_tk, gpost, g2)


def _rest_order(w):
    scale = HEAD_DIM ** -0.5
    assert math.log2(scale).is_integer()
    xr, gr, q, k, v, mg_r, mg_a = jnp.split(
        w, np.cumsum([D_RNN, D_RNN, Q_DIM, KV_DIM, KV_DIM, D_MODEL]).tolist(), axis=-1)
    return jnp.concatenate([xr, gr, q * scale, mg_r, mg_a, k, v], axis=-1)


def _layer(x2, c, B, S, p):
    D = D_MODEL
    N = B * S
    ada = _ada(c, p["w_ada"], p["b_ada"])
    sh1, sc1, g1, sh2, sc2, g2 = [a.reshape(B, 1, D) for a in jnp.split(ada, 6, axis=-1)]
    row = lambda v: v.reshape(1, -1)

    gw = _tiles()["rnn_group"]
    proj_rest, y_rnn = _mixin(
        x2, row(p["norm_pre_mix"]), sc1, sh1,
        _rest_order(p["w_in"]).astype(BF16), _rest_order(row(p["b_in"])),
        p["conv_w"], row(p["conv_b"]),
        _block_diag_tiles(p["rg_w_a"], gw).astype(BF16), row(p["rg_b_a"]),
        _block_diag_tiles(p["rg_w_x"], gw).astype(BF16), row(p["rg_b_x"]),
        row(p["rg_lambda"]), S)
    y_att = _attn(proj_rest, p["attn_sinks"], S)

    x1, h2, logits_t = _merge(
        x2, y_rnn, y_att, proj_rest,
        p["w_o_rnn"].astype(BF16), p["w_o_attn"].astype(BF16), p["w_out"].astype(BF16),
        row(p["norm_post_mix"]), g1, row(p["norm_pre_ffn"]), sc2, sh2,
        p["router_w"].T.astype(BF16), p["router_b"].reshape(N_EXPERTS, 1), S)

    gates, slot, counts, tile_cnt, tile_car, tile_lst = _route(logits_t)

    bm = _tiles()["moe_bm"]
    n_tiles = N // _tiles()["moe_t"]
    n_rows = N * TOP_K + N_EXPERTS * bm
    nblk = n_rows // bm
    counts = counts.reshape(N_EXPERTS)
    padded = ((counts + bm - 1) // bm) * bm
    pend = jnp.cumsum(padded)
    pstart = pend - padded
    eids = jnp.arange(N_EXPERTS, dtype=jnp.int32)
    per_run = lambda a: a[:, :n_tiles].T.astype(jnp.int32)
    plan = _copy_plan(per_run(tile_cnt), per_run(tile_lst), per_run(tile_car + pstart[:, None]),
                      _tiles()["moe_t"])
    tail_start = jnp.maximum(pend - bm, 0).astype(jnp.int32)
    n_used = (pend[-1] // bm).astype(jnp.int32).reshape(1)
    blk_row0 = jnp.minimum(jnp.arange(nblk, dtype=jnp.int32), n_used[0] - 1) * bm
    blk_e = jnp.sum(blk_row0[:, None] >= pend[None, :], axis=1).astype(jnp.int32)
    mine = blk_e[:, None] == eids[None, :]
    blk_cnt = jnp.sum(jnp.where(mine, counts, 0), axis=1)
    blk_pstart = jnp.sum(jnp.where(mine, pstart, 0), axis=1)
    blk_valid = jnp.clip(blk_cnt - (blk_row0 - blk_pstart), 0, bm).astype(jnp.int32)
    blk_first = (blk_row0 == blk_pstart).astype(jnp.int32)

    xs = _dispatch(h2, slot, plan, tail_start, n_used, n_rows, bm)
    b1 = p["moe_b1"].reshape(N_EXPERTS, D_FF, 2)
    later = jnp.logical_and(counts[None, :] > 0, eids[None, :] > eids[:, None])
    next_e = jnp.min(jnp.where(later, eids[None, :], N_EXPERTS), axis=1)
    next_e = jnp.where(next_e == N_EXPERTS, -1, next_e).astype(jnp.int32)
    ys = _ffn(xs, blk_e, blk_valid, blk_first, n_used, next_e, p["moe_w1"],
              b1[:, :, 0].reshape(N_EXPERTS, 1, D_FF), b1[:, :, 1].reshape(N_EXPERTS, 1, D_FF),
              p["moe_w2"], p["moe_b2"].reshape(N_EXPERTS, 1, D), bm)
    return _combine(ys, plan, x1, slot.T, gates.T, row(p["norm_post_ffn"]), g2, S)


def kernel(x, c, w_ada, b_ada, norm_pre_mix, norm_post_mix, norm_pre_ffn, norm_post_ffn, w_in, b_in, conv_w, conv_b, rg_w_a, rg_b_a, rg_w_x, rg_b_x, rg_lambda, attn_sinks, w_o_rnn, w_o_attn, w_out, router_w, router_b, moe_w1, moe_b1, moe_w2, moe_b2):
    B, S, D = x.shape
    params = dict(
        w_ada=w_ada, b_ada=b_ada, norm_pre_mix=norm_pre_mix, norm_post_mix=norm_post_mix,
        norm_pre_ffn=norm_pre_ffn, norm_post_ffn=norm_post_ffn, w_in=w_in, b_in=b_in,
        conv_w=conv_w, conv_b=conv_b, rg_w_a=rg_w_a, rg_b_a=rg_b_a, rg_w_x=rg_w_x, rg_b_x=rg_b_x,
        rg_lambda=rg_lambda, attn_sinks=attn_sinks, w_o_rnn=w_o_rnn, w_o_attn=w_o_attn,
        w_out=w_out, router_w=router_w, router_b=router_b, moe_w1=moe_w1, moe_b1=moe_b1,
        moe_w2=moe_w2, moe_b2=moe_b2)
    x2 = x.reshape(B * S, D)
    for layer in range(w_ada.shape[0]):
        x2 = _layer(x2, c, B, S, {k: v[layer] for k, v in params.items()})
    return x2.reshape(B, S, D)
```

```python
import functools
import math

import jax
import jax.numpy as jnp
import numpy as np
from jax import lax
from jax.experimental import pallas as pl
from jax.experimental.pallas import tpu as pltpu

D_MODEL = 1024
D_RNN = 1024
RNN_BLOCKS = 16
RNN_BW = D_RNN // RNN_BLOCKS
CONV_W = 4
LRU_C = 8.0
N_HEADS = 16
N_KV = 4
HEAD_DIM = 64
GROUP = N_HEADS // N_KV
WINDOW = 128
Q_DIM = N_HEADS * HEAD_DIM
KV_DIM = N_KV * HEAD_DIM
N_EXPERTS = 32
TOP_K = 4
D_FF = 1024
SWIGLU_LIMIT = 7.0
SWIGLU_ALPHA = 1.702
EPS = 1e-6
D_IN = 2 * D_RNN + Q_DIM + 2 * KV_DIM + 2 * D_MODEL
D_REST = D_IN - 2 * D_RNN

V7X_LANES = 128
V7X_SUBLANES = 8
V7X_MXU_DIM = 256
V7X_VMEM_BYTES = 64 * 1024 * 1024

F32 = jnp.float32
BF16 = jnp.bfloat16


def _tiles():
    return dict(
        ada_tn=1024,
        tok=512,
        attn_q=WINDOW,
        moe_t=256,
        moe_bm=512,
        rnn_group=V7X_MXU_DIM,
    )


def _vmem_limit(nbytes):
    return int(min(max(nbytes, 16 * 1024 * 1024), V7X_VMEM_BYTES - 8 * 1024 * 1024))


def _rms(x, g):
    return x * lax.rsqrt(jnp.mean(x * x, axis=-1, keepdims=True) + EPS) * g


def _ada_kernel(ct_ref, w_ref, b_ref, o_ref):
    ct = ct_ref[...]
    sc = ct * jax.nn.sigmoid(ct)
    w = w_ref[...]
    rows = [jnp.sum(w * sc[:, b:b + 1], axis=0, keepdims=True) for b in range(ct.shape[1])]
    o_ref[...] = jnp.concatenate(rows, axis=0) + b_ref[...]


def _ada(c, w_ada, b_ada):
    B, D = c.shape
    n_out = w_ada.shape[1]
    tn = _tiles()["ada_tn"]
    return pl.pallas_call(
        _ada_kernel,
        grid=(n_out // tn,),
        in_specs=[
            pl.BlockSpec((D, B), lambda j: (0, 0)),
            pl.BlockSpec((D, tn), lambda j: (0, j)),
            pl.BlockSpec((1, tn), lambda j: (0, j)),
        ],
        out_specs=pl.BlockSpec((B, tn), lambda j: (0, j)),
        out_shape=jax.ShapeDtypeStruct((B, n_out), F32),
        compiler_params=pltpu.CompilerParams(
            dimension_semantics=("arbitrary",),
            vmem_limit_bytes=_vmem_limit(4 * D * tn * 4)),
        name="ada",
    )(c.T, w_ada, b_ada.reshape(1, n_out))


def _gelu_tanh(x):
    return 0.5 * x * (1.0 + jnp.tanh(math.sqrt(2.0 / math.pi) * (x + 0.044715 * (x * x * x))))


def _softplus(z):
    return jnp.maximum(z, 0.0) + jnp.log1p(jnp.exp(-jnp.abs(z)))


def _sigmoid(x):
    return 0.5 * jnp.tanh(0.5 * x) + 0.5


def _rest_col(c):
    c -= 2 * D_RNN
    q_end, k_end, v_end = Q_DIM, Q_DIM + KV_DIM, Q_DIM + 2 * KV_DIM
    if c < q_end:
        return c
    if c < k_end:
        return Q_DIM + 2 * D_MODEL + (c - q_end)
    if c < v_end:
        return Q_DIM + 2 * D_MODEL + KV_DIM + (c - k_end)
    return Q_DIM + (c - v_end)


def _mixin_kernel(x_ref, g_ref, sc_ref, sh_ref, w_ref, b_ref, cw_ref, cb_ref, wa_ref, ba_ref,
                  wx_ref, bx_ref, lam_ref, rest_ref, y_ref,
                  hb_s, xbuf, gr_s, a_s, b_s, h_s, ga_s, gb_s, gc_s, carry, *, per_b, gw, chunk):
    t = pl.program_id(0) % per_b
    tt = x_ref.shape[0]
    halo = V7X_SUBLANES
    ng = tt // V7X_SUBLANES
    n_lt = a_s.shape[0]
    per_g = gw // V7X_LANES
    lanes = lambda j: slice(j * V7X_LANES, (j + 1) * V7X_LANES)
    slab = lambda j, r: (j, pl.ds(r, ng, stride=V7X_SUBLANES), slice(None))

    @pl.when(t == 0)
    def _():
        xbuf[:, 0:halo, :] = jnp.zeros((n_lt, halo, V7X_LANES), F32)
        carry[...] = jnp.zeros_like(carry)

    hb_s[...] = (_rms(x_ref[...], g_ref[...]) * (1.0 + sc_ref[0]) + sh_ref[0]).astype(BF16)

    def proj(c0):
        return (jnp.dot(hb_s[...], w_ref[:, c0:c0 + chunk], preferred_element_type=F32)
                + b_ref[:, c0:c0 + chunk])

    def proj_rest(c0):
        acc = proj(c0).astype(BF16)
        for p0 in range(0, chunk, KV_DIM):
            d0 = _rest_col(c0 + p0)
            rest_ref[:, d0:d0 + KV_DIM] = acc[:, p0:p0 + KV_DIM]

    for c0 in range(0, D_RNN, chunk):
        acc = proj(c0)
        for j in range(chunk // V7X_LANES):
            xbuf[c0 // V7X_LANES + j, halo:halo + tt, :] = acc[:, lanes(j)]
    for c0 in range(D_RNN, 2 * D_RNN, chunk):
        gr_s[:, c0 - D_RNN:c0 - D_RNN + chunk] = proj(c0)

    def conv(j):
        ls = lanes(j)
        taps = {o: xbuf[slab(j, halo + o)] for o in range(-(CONV_W - 1), V7X_SUBLANES)}
        for r in range(V7X_SUBLANES):
            acc = cb_ref[:, ls] + taps[r - (CONV_W - 1)] * cw_ref[0:1, ls]
            for kk in range(1, CONV_W):
                acc = acc + taps[r - (CONV_W - 1) + kk] * cw_ref[kk:kk + 1, ls]
            h_s[slab(j, r)] = acc
        xbuf[j, 0:halo, :] = xbuf[j, tt:tt + halo, :]

    def gates(g):
        cs = slice(g * gw, (g + 1) * gw)
        rate = (-LRU_C * math.log2(math.e)) * _softplus(-lam_ref[:, cs])
        reset = jnp.logical_and(t == 0, lax.broadcasted_iota(jnp.int32, (tt, gw), 0) == 0)
        xc = jnp.concatenate([h_s[g * per_g + j] for j in range(per_g)], axis=1)
        xg = xc.astype(BF16)
        gate_r = 0.5 * jnp.tanh(jnp.dot(xg, wa_ref[g], preferred_element_type=F32) + ba_ref[:, cs]) + 0.5
        gate_i = 0.5 * jnp.tanh(jnp.dot(xg, wx_ref[g], preferred_element_type=F32) + bx_ref[:, cs]) + 0.5
        a = jnp.exp2(gate_r * rate)
        v = (1.0 - a) * (1.0 + a)
        mult = jnp.where(reset, 1.0, jnp.where(v > 0.0, v * lax.rsqrt(v), 0.0))
        bt = (xc * gate_i) * mult
        for j in range(per_g):
            a_s[g * per_g + j] = a[:, lanes(j)]
            b_s[g * per_g + j] = bt[:, lanes(j)]

    def scan_groups(j):
        acc_a = a_s[slab(j, 0)]
        acc_h = b_s[slab(j, 0)]
        h_s[slab(j, 0)] = acc_h
        for r in range(1, V7X_SUBLANES):
            a_r = a_s[slab(j, r)]
            acc_h = a_r * acc_h + b_s[slab(j, r)]
            acc_a = a_r * acc_a
            h_s[slab(j, r)] = acc_h
            a_s[slab(j, r)] = acc_a
        ga_s[:, lanes(j)] = acc_a
        gb_s[:, lanes(j)] = acc_h

    def across(gi, h_prev):
        gc_s[pl.ds(gi, 1), :] = h_prev
        return ga_s[pl.ds(gi, 1), :] * h_prev + gb_s[pl.ds(gi, 1), :]

    def finish(j):
        h_in = gc_s[:, lanes(j)]
        for r in range(V7X_SUBLANES):
            h_s[slab(j, r)] = a_s[slab(j, r)] * h_in + h_s[slab(j, r)]
        y_ref[:, lanes(j)] = (h_s[j] * _gelu_tanh(gr_s[:, lanes(j)])).astype(BF16)

    rest = [functools.partial(proj_rest, c0) for c0 in range(2 * D_RNN, w_ref.shape[1], chunk)]
    before = ([functools.partial(conv, j) for j in range(n_lt)]
              + [functools.partial(gates, g) for g in range(n_lt // per_g)]
              + [functools.partial(scan_groups, j) for j in range(n_lt)])
    after = [functools.partial(finish, j) for j in range(n_lt)]
    n_before = (len(rest) * len(before)) // (len(before) + len(after))

    def interleave(steps, chunks):
        every = -(-len(steps) // max(len(chunks), 1))
        for n, step in enumerate(steps):
            step()
            if (n + 1) % every == 0 and chunks:
                chunks.pop(0)()
        while chunks:
            chunks.pop(0)()

    interleave(before, rest[:n_before])
    carry[...] = lax.fori_loop(0, ng, across, carry[...])
    interleave(after, rest[n_before:])


def _block_diag_tiles(w, gw):
    nb, bw, _ = w.shape
    per = gw // bw
    w4 = w.reshape(nb // per, per, bw, bw)
    eye = jnp.eye(per, dtype=w.dtype)
    return jnp.einsum("gpij,pq->gpiqj", w4, eye).reshape(nb // per, gw, gw)


def _mixin(x2, g, sc, sh, w_bf, b_in, conv_w, conv_b, wa, ba, wx, bx, lam, S):
    N, D = x2.shape
    C = D_RNN
    tt = _tiles()["tok"]
    gw = _tiles()["rnn_group"]
    per_b = S // tt
    chunk = 2 * V7X_MXU_DIM
    vec = lambda: pl.BlockSpec((1, C), lambda i: (0, 0))
    bvec = lambda: pl.BlockSpec((1, 1, D), lambda i: (i // per_b, 0, 0))
    gate_w = lambda: pl.BlockSpec((C // gw, gw, gw), lambda i: (0, 0, 0))
    slabs = lambda rows: pltpu.VMEM((C // V7X_LANES, rows, V7X_LANES), F32)
    groups = lambda: pltpu.VMEM((tt // V7X_SUBLANES, C), F32)
    vmem = D * D_IN * 2 + 2 * tt * (D * 4 + D_REST * 2 + C * 2) + tt * C * (2 + 5 * 4) + 8 * tt * chunk * 4
    return pl.pallas_call(
        functools.partial(_mixin_kernel, per_b=per_b, gw=gw, chunk=chunk),
        grid=(N // tt,),
        in_specs=[
            pl.BlockSpec((tt, D), lambda i: (i, 0)),
            pl.BlockSpec((1, D), lambda i: (0, 0)),
            bvec(), bvec(),
            pl.BlockSpec((D, D_IN), lambda i: (0, 0), pipeline_mode=pl.Buffered(1)),
            pl.BlockSpec((1, D_IN), lambda i: (0, 0)),
            pl.BlockSpec((CONV_W, C), lambda i: (0, 0)),
            vec(), gate_w(), vec(), gate_w(), vec(), vec(),
        ],
        out_specs=[
            pl.BlockSpec((tt, D_REST), lambda i: (i, 0)),
            pl.BlockSpec((tt, C), lambda i: (i, 0)),
        ],
        out_shape=[
            jax.ShapeDtypeStruct((N, D_REST), BF16),
            jax.ShapeDtypeStruct((N, C), BF16),
        ],
        scratch_shapes=[
            pltpu.VMEM((tt, D), BF16),
            slabs(tt + V7X_SUBLANES),
            pltpu.VMEM((tt, C), F32),
            slabs(tt), slabs(tt), slabs(tt),
            groups(), groups(), groups(),
            pltpu.VMEM((1, C), F32),
        ],
        compiler_params=pltpu.CompilerParams(
            dimension_semantics=("arbitrary",),
            vmem_limit_bytes=_vmem_limit(vmem)),
        name="mixin",
    )(x2, g, sc, sh, w_bf, b_in, conv_w, conv_b, wa, ba, wx, bx, lam)


def _alibi_slopes():
    return [2.0 ** (-8.0 * (h + 1) / N_HEADS) for h in range(N_HEADS)]


def _attn_kernel(sink_ref, q_ref, kp_ref, kc_ref, vp_ref, vc_ref, o_ref, bias_s, *, per_b):
    blk = pl.program_id(0) % per_b
    bq = q_ref.shape[0]
    slopes = _alibi_slopes()

    @pl.when(pl.program_id(0) == 0)
    def _():
        qi = lax.broadcasted_iota(jnp.int32, (bq, 2 * bq), 0)
        ci = lax.broadcasted_iota(jnp.int32, (bq, 2 * bq), 1)
        dist = qi + bq - ci
        valid = (dist >= 0) & (dist < WINDOW)
        distf = dist.astype(F32)
        for h in range(N_HEADS):
            b = jnp.where(valid, -slopes[h] * distf, -jnp.inf)
            bias_s[0, h] = jnp.where(ci >= bq, b, -jnp.inf)
            bias_s[1, h] = b

    assert 2 * HEAD_DIM == V7X_LANES and GROUP % 2 == 0
    table = jnp.minimum(blk, 1)
    low = lax.broadcasted_iota(jnp.int32, (bq, V7X_LANES), 1) < HEAD_DIM
    zero = jnp.zeros((bq, V7X_LANES), BF16)
    for kvt in range(N_KV // 2):
        lt = slice(kvt * V7X_LANES, (kvt + 1) * V7X_LANES)
        k_t = jnp.concatenate([kp_ref[:, lt], kc_ref[:, lt]], axis=0)
        v_t = jnp.concatenate([vp_ref[:, lt], vc_ref[:, lt]], axis=0)
        k_swapped = pltpu.roll(k_t, HEAD_DIM, 1)
        for kv_half in range(2):
            kv = 2 * kvt + kv_half
            for pair in range(GROUP // 2):
                h0 = kv * GROUP + 2 * pair
                q_t = q_ref[:, (h0 // 2) * V7X_LANES:(h0 // 2 + 1) * V7X_LANES]
                halves = []
                for q_half in range(2):
                    h = h0 + q_half
                    qm = jnp.where(low if q_half == 0 else ~low, q_t, zero)
                    kk = k_t if q_half == kv_half else k_swapped
                    s = lax.dot_general(qm, kk, (((1,), (1,)), ((), ())), preferred_element_type=F32)
                    s = s + bias_s[table, h]
                    sink = sink_ref[h]
                    m = jnp.maximum(jnp.max(s, axis=-1, keepdims=True), sink)
                    p = jnp.exp(s - m)
                    denom = jnp.sum(p, axis=-1, keepdims=True) + jnp.exp(sink - m)
                    halves.append(jnp.dot(p.astype(BF16), v_t, preferred_element_type=F32) / denom)
                if kv_half == 0:
                    o = jnp.where(low, halves[0], pltpu.roll(halves[1], HEAD_DIM, 1))
                else:
                    o = jnp.where(low, pltpu.roll(halves[0], HEAD_DIM, 1), halves[1])
                o_ref[:, (h0 // 2) * V7X_LANES:(h0 // 2 + 1) * V7X_LANES] = o.astype(BF16)


def _attn(proj_rest, sinks, S):
    N = proj_rest.shape[0]
    bq = _tiles()["attn_q"]
    per_b = S // bq
    k_col = (Q_DIM + 2 * D_MODEL) // KV_DIM
    v_col = k_col + 1

    def prev(i):
        return jnp.where(i % per_b == 0, i, i - 1)

    return pl.pallas_call(
        functools.partial(_attn_kernel, per_b=per_b),
        grid=(N // bq,),
        in_specs=[
            pl.BlockSpec(memory_space=pltpu.SMEM),
            pl.BlockSpec((bq, Q_DIM), lambda i: (i, 0)),
            pl.BlockSpec((bq, KV_DIM), lambda i: (prev(i), k_col)),
            pl.BlockSpec((bq, KV_DIM), lambda i: (i, k_col)),
            pl.BlockSpec((bq, KV_DIM), lambda i: (prev(i), v_col)),
            pl.BlockSpec((bq, KV_DIM), lambda i: (i, v_col)),
        ],
        out_specs=pl.BlockSpec((bq, Q_DIM), lambda i: (i, 0)),
        out_shape=jax.ShapeDtypeStruct((N, Q_DIM), BF16),
        scratch_shapes=[pltpu.VMEM((2, N_HEADS, bq, 2 * bq), F32)],
        compiler_params=pltpu.CompilerParams(
            dimension_semantics=("arbitrary",),
            vmem_limit_bytes=_vmem_limit(3 * 2 * N_HEADS * bq * 2 * bq * 4)),
        name="attn",
    )(sinks, proj_rest, proj_rest, proj_rest, proj_rest, proj_rest)


def _merge_kernel(x_ref, yr_ref, ya_ref, gr_ref, ga_ref, wr_ref, wa_ref, wo_ref,
                  gpost_ref, g1_ref, gpre_ref, sc2_ref, sh2_ref, rwt_ref, rb_ref,
                  x1_ref, h2_ref, lg_ref):
    r = jnp.dot(yr_ref[...], wr_ref[...], preferred_element_type=F32)
    a = jnp.dot(ya_ref[...], wa_ref[...], preferred_element_type=F32)
    merged = (_sigmoid(gr_ref[...].astype(F32)) * r
              + _sigmoid(ga_ref[...].astype(F32)) * a)
    mix = jnp.dot(merged.astype(BF16), wo_ref[...], preferred_element_type=F32)
    x1 = x_ref[...] + g1_ref[0] * _rms(mix, gpost_ref[...])
    x1_ref[...] = x1
    h2 = _rms(x1, gpre_ref[...]) * (1.0 + sc2_ref[0]) + sh2_ref[0]
    h2_ref[...] = h2
    lg = lax.dot_general(rwt_ref[...], h2.astype(BF16), (((1,), (1,)), ((), ())),
                         preferred_element_type=F32)
    lg_ref[...] = lg + rb_ref[...]


def _merge(x2, y_rnn, y_att, proj_rest, wr, wa, wo, gpost, g1, gpre, sc2, sh2, rwt, rb, S):
    N, D = x2.shape
    tm = _tiles()["tok"]
    per_b = S // tm
    gate_r_col = Q_DIM // D
    mat = lambda: pl.BlockSpec((D, D), lambda i: (0, 0))
    vec = lambda: pl.BlockSpec((1, D), lambda i: (0, 0))
    bvec = lambda: pl.BlockSpec((1, 1, D), lambda i: (i // per_b, 0, 0))
    tile = lambda col=0: pl.BlockSpec((tm, D), lambda i: (i, col))
    return pl.pallas_call(
        _merge_kernel,
        grid=(N // tm,),
        in_specs=[
            tile(), tile(), tile(), tile(gate_r_col), tile(gate_r_col + 1),
            mat(), mat(), mat(),
            vec(), bvec(), vec(), bvec(), bvec(),
            pl.BlockSpec((N_EXPERTS, D), lambda i: (0, 0)),
            pl.BlockSpec((N_EXPERTS, 1), lambda i: (0, 0)),
        ],
        out_specs=[
            tile(), tile(),
            pl.BlockSpec((N_EXPERTS, tm), lambda i: (0, i)),
        ],
        out_shape=[
            jax.ShapeDtypeStruct((N, D), F32),
            jax.ShapeDtypeStruct((N, D), F32),
            jax.ShapeDtypeStruct((N_EXPERTS, N), F32),
        ],
        compiler_params=pltpu.CompilerParams(
            dimension_semantics=("arbitrary",),
            vmem_limit_bytes=_vmem_limit(6 * D * D * 2 + 24 * tm * D * 4)),
        name="merge",
    )(x2, y_rnn, y_att, proj_rest, proj_rest, wr, wa, wo, gpost, g1, gpre, sc2, sh2, rwt, rb)


def _route_kernel(lg_ref, g_ref, slot_ref, cnt_ref, tcnt_ref, tcar_ref, tlst_ref, carry):
    i = pl.program_id(0)

    @pl.when(i == 0)
    def _():
        carry[...] = jnp.zeros_like(carry)
        tcnt_ref[...] = jnp.zeros_like(tcnt_ref)
        tcar_ref[...] = jnp.zeros_like(tcar_ref)
        tlst_ref[...] = jnp.zeros_like(tlst_ref)

    l = lg_ref[...]
    E, T = l.shape
    row = lax.broadcasted_iota(jnp.int32, (E, T), 0).astype(F32)
    vals, idxs = [], []
    for _ in range(TOP_K):
        m = jnp.max(l, axis=0, keepdims=True)
        idx = jnp.min(jnp.where(l == m, row, float(E)), axis=0, keepdims=True)
        vals.append(m)
        idxs.append(idx)
        l = jnp.where(row == idx, -jnp.inf, l)
    ex = [jnp.exp(v - vals[0]) for v in vals]
    tot = ex[0]
    for e in ex[1:]:
        tot = tot + e
    g_ref[...] = jnp.concatenate([e / tot for e in ex], axis=0)

    hot = [row == idx for idx in idxs]
    onehot = jnp.zeros((E, T), F32)
    for hk in hot:
        onehot = onehot + hk.astype(F32)
    tri_t = (lax.broadcasted_iota(jnp.int32, (T, T), 0)
             < lax.broadcasted_iota(jnp.int32, (T, T), 1)).astype(BF16)
    before = jnp.dot(onehot.astype(BF16), tri_t, preferred_element_type=F32)
    cnt = jnp.sum(onehot, axis=1, keepdims=True)
    tri_e = (lax.broadcasted_iota(jnp.int32, (E, E), 1)
             < lax.broadcasted_iota(jnp.int32, (E, E), 0)).astype(BF16)
    lstart = jnp.dot(tri_e, jnp.broadcast_to(cnt, (E, V7X_LANES)).astype(BF16),
                     preferred_element_type=F32)[:, 0:1]
    local = before + lstart
    slots = [jnp.sum(jnp.where(hk, local, 0.0), axis=0, keepdims=True) for hk in hot]
    slot_ref[...] = jnp.concatenate(slots, axis=0).astype(jnp.int32)

    mine = lax.broadcasted_iota(jnp.int32, tcnt_ref.shape, 1) == i
    tcnt_ref[...] = jnp.where(mine, cnt.astype(jnp.int32), tcnt_ref[...])
    tcar_ref[...] = jnp.where(mine, carry[...].astype(jnp.int32), tcar_ref[...])
    tlst_ref[...] = jnp.where(mine, lstart.astype(jnp.int32), tlst_ref[...])
    total = carry[...] + cnt
    carry[...] = total
    cnt_ref[...] = total.astype(jnp.int32)


def _route(logits_t):
    E, N = logits_t.shape
    T = _tiles()["moe_t"]
    assert T <= 2 ** 8 and N // T <= V7X_LANES
    out = lambda: pl.BlockSpec((TOP_K, T), lambda i: (0, i))
    per_tile = lambda: pl.BlockSpec((E, V7X_LANES), lambda i: (0, 0))
    return pl.pallas_call(
        _route_kernel,
        grid=(N // T,),
        in_specs=[pl.BlockSpec((E, T), lambda i: (0, i))],
        out_specs=[out(), out(), pl.BlockSpec((E, 1), lambda i: (0, 0)),
                   per_tile(), per_tile(), per_tile()],
        out_shape=[
            jax.ShapeDtypeStruct((TOP_K, N), F32),
            jax.ShapeDtypeStruct((TOP_K, N), jnp.int32),
            jax.ShapeDtypeStruct((E, 1), jnp.int32),
            jax.ShapeDtypeStruct((E, V7X_LANES), jnp.int32),
            jax.ShapeDtypeStruct((E, V7X_LANES), jnp.int32),
            jax.ShapeDtypeStruct((E, V7X_LANES), jnp.int32),
        ],
        scratch_shapes=[pltpu.VMEM((E, 1), F32)],
        compiler_params=pltpu.CompilerParams(dimension_semantics=("arbitrary",)),
        name="route",
    )(logits_t)


def _run_sizes(T):
    return [2 ** b for b in range(int(math.log2(T)), -1, -1)]


def _copy_plan(tile_cnt, tile_lst, tile_dst, T):
    sizes = jnp.asarray(_run_sizes(T), jnp.int32)[None, :, None]
    cnt, lst, dst = (a[:, None, :] for a in (tile_cnt, tile_lst, tile_dst))
    has = (cnt & sizes) != 0
    off = (cnt // (2 * sizes)) * (2 * sizes)
    eids = jnp.arange(N_EXPERTS)
    earlier = eids[:, None] < eids[None, :]
    place = jnp.sum(jnp.where(earlier, has[..., :, None], False), axis=-2)
    hit = has[..., :, None] & (place[..., :, None] == eids)
    pick = lambda v: jnp.sum(jnp.where(hit, v[..., :, None], 0), axis=-2)
    n_tiles, nb = tile_cnt.shape[0], sizes.shape[1]
    parts = [pick(jnp.broadcast_to(lst + off, has.shape)).reshape(n_tiles, -1),
             pick(jnp.broadcast_to(dst + off, has.shape)).reshape(n_tiles, -1),
             jnp.sum(has, axis=-1).reshape(n_tiles, nb)]
    plan = jnp.concatenate(parts, axis=1).astype(jnp.int32)
    pad = _plan_len(T) - plan.shape[1]
    return jnp.pad(plan, ((0, 0), (0, pad))).reshape(-1)


def _plan_len(T):
    nb = len(_run_sizes(T))
    return max(V7X_LANES, pl.next_power_of_2(2 * nb * N_EXPERTS + nb))


def _run_copies(plan_ref, T, make):
    sizes = _run_sizes(T)
    nb = len(sizes)
    for bi, b in enumerate(sizes):
        def body(j, c, bi=bi, b=b):
            make(plan_ref[bi * N_EXPERTS + j], plan_ref[(nb + bi) * N_EXPERTS + j], b).start()
            return c

        lax.fori_loop(0, plan_ref[2 * nb * N_EXPERTS + bi], body, 0)


def _row_slab(ref, row, nrows, lead=()):
    rows = pl.ds(pl.multiple_of(row * V7X_SUBLANES, V7X_SUBLANES), nrows * V7X_SUBLANES)
    return ref.at[(*lead, rows)]


def _to_slabs(ref, val, lead=()):
    rows = val.shape[0]
    for s in range(V7X_SUBLANES):
        ref[(*lead, pl.ds(s, rows, stride=V7X_SUBLANES), slice(None))] = (
            val[:, s * V7X_LANES:(s + 1) * V7X_LANES])


def _from_slabs(ref, lead=(), rows=None):
    rows = ref.shape[-2] // V7X_SUBLANES if rows is None else rows
    return jnp.concatenate(
        [ref[(*lead, pl.ds(s, rows, stride=V7X_SUBLANES), slice(None))]
         for s in range(V7X_SUBLANES)], axis=1)


def _dispatch_kernel(tail_ref, nu_ref, plan_ref, slot_ref, h_ref, xs_ref, buf, zbuf, sem, zsem):
    i = pl.program_id(0)
    T = h_ref.shape[0]
    bm = zbuf.shape[0] // V7X_SUBLANES
    nblk = xs_ref.shape[0] // zbuf.shape[0]
    cur = i % 2

    def zero_copy(row0):
        return pltpu.make_async_copy(zbuf, _row_slab(xs_ref, row0, bm), zsem)

    @pl.when(i == 0)
    def _():
        zbuf[...] = jnp.zeros_like(zbuf)

        def start(e, c):
            zero_copy(tail_ref[e]).start()
            return c

        def wait(e, c):
            zero_copy(tail_ref[e]).wait()
            return c

        def start_unused(j, c):
            zero_copy(j * bm).start()
            return c

        def wait_unused(j, c):
            zero_copy(j * bm).wait()
            return c

        lax.fori_loop(0, N_EXPERTS, start, 0)
        lax.fori_loop(nu_ref[0], nblk, start_unused, 0)
        lax.fori_loop(0, N_EXPERTS, wait, 0)
        lax.fori_loop(nu_ref[0], nblk, wait_unused, 0)

    slot = slot_ref[...]
    r_id = lax.broadcasted_iota(jnp.int32, (TOP_K * T, T), 0)
    perm = r_id == slot[0:1, :]
    for k in range(1, TOP_K):
        perm = jnp.logical_or(perm, r_id == slot[k:k + 1, :])
    grouped = jnp.dot(perm.astype(BF16), h_ref[...].astype(BF16), preferred_element_type=F32)
    _to_slabs(buf, grouped, lead=(cur,))

    def wait_tile(b):
        pltpu.make_async_copy(buf.at[b], _row_slab(xs_ref, 0, TOP_K * T), sem.at[b]).wait()

    @pl.when(i > 0)
    def _():
        wait_tile(1 - cur)

    _run_copies(plan_ref, T, lambda l, d, n: pltpu.make_async_copy(
        _row_slab(buf, l, n, lead=(cur,)), _row_slab(xs_ref, d, n), sem.at[cur]))

    @pl.when(i == pl.num_programs(0) - 1)
    def _():
        wait_tile(cur)


def _dispatch(h2, slot, plan, tail_start, n_used, n_rows, bm):
    N, D = h2.shape
    T = _tiles()["moe_t"]
    S8 = V7X_SUBLANES
    imap = lambda f: (lambda i, *_: f(i))
    return pl.pallas_call(
        _dispatch_kernel,
        grid_spec=pltpu.PrefetchScalarGridSpec(
            num_scalar_prefetch=2,
            grid=(N // T,),
            in_specs=[
                pl.BlockSpec((_plan_len(T),), imap(lambda i: (i,)), memory_space=pltpu.SMEM),
                pl.BlockSpec((TOP_K, T), imap(lambda i: (0, i))),
                pl.BlockSpec((T, D), imap(lambda i: (i, 0))),
            ],
            out_specs=pl.BlockSpec(memory_space=pl.ANY),
            scratch_shapes=[
                pltpu.VMEM((2, TOP_K * T * S8, V7X_LANES), F32),
                pltpu.VMEM((bm * S8, V7X_LANES), F32),
                pltpu.SemaphoreType.DMA((2,)),
                pltpu.SemaphoreType.DMA(()),
            ],
        ),
        out_shape=jax.ShapeDtypeStruct((n_rows * S8, V7X_LANES), F32),
        compiler_params=pltpu.CompilerParams(
            dimension_semantics=("arbitrary",),
            vmem_limit_bytes=_vmem_limit(8 * TOP_K * T * D * 4)),
        name="dispatch",
    )(tail_start, n_used, plan, slot, h2)


def _ffn_kernel(be_ref, bv_ref, bf_ref, nu_ref, nxt_ref, x_ref, w1_hbm, b1g_ref, b1l_ref, w2_hbm,
                b2_ref, y_ref, w1f_s, w2f_s, w1g_s, w1l_s, w2_s, sem):
    i = pl.program_id(0)
    bm, D = x_ref.shape[0] // V7X_SUBLANES, w2_s.shape[1]
    pw = 2 * V7X_LANES

    def fetch(e):
        return (pltpu.make_async_copy(w1_hbm.at[e], w1f_s, sem.at[0]),
                pltpu.make_async_copy(w2_hbm.at[e], w2f_s, sem.at[1]))

    @pl.when(i == 0)
    def _():
        for cp in fetch(be_ref[0]):
            cp.start()

    @pl.when(jnp.logical_and(i < nu_ref[0], bf_ref[i] == 1))
    def _():
        e = be_ref[i]
        for cp in fetch(e):
            cp.wait()
        src = lax.broadcasted_iota(jnp.int32, (pw, pw), 0)
        dst = lax.broadcasted_iota(jnp.int32, (pw, pw), 1)
        want = jnp.where(dst < V7X_LANES, 2 * dst, 2 * (dst - V7X_LANES) + 1)
        perm = (src == want).astype(BF16)
        for c in range(w1f_s.shape[1] // pw):
            wp = jnp.dot(w1f_s[:, c * pw:(c + 1) * pw].astype(BF16), perm,
                         preferred_element_type=F32).astype(BF16)
            w1g_s[:, c * V7X_LANES:(c + 1) * V7X_LANES] = wp[:, :V7X_LANES]
            w1l_s[:, c * V7X_LANES:(c + 1) * V7X_LANES] = wp[:, V7X_LANES:]
        w2_s[...] = w2f_s[...].astype(BF16)

        @pl.when(nxt_ref[e] >= 0)
        def _():
            for cp in fetch(nxt_ref[e]):
                cp.start()

    def mlp(n_rows):
        rows = lax.broadcasted_iota(jnp.int32, (n_rows, D), 0)
        x = jnp.where(rows < bv_ref[i], _from_slabs(x_ref, rows=n_rows), 0.0).astype(BF16)
        glu = jnp.dot(x, w1g_s[...], preferred_element_type=F32) + b1g_ref[0]
        lin = jnp.dot(x, w1l_s[...], preferred_element_type=F32) + b1l_ref[0]
        glu = jnp.minimum(glu, SWIGLU_LIMIT)
        lin = jnp.clip(lin, -SWIGLU_LIMIT, SWIGLU_LIMIT)
        act = glu * _sigmoid(SWIGLU_ALPHA * glu) * (lin + 1.0)
        _to_slabs(y_ref, jnp.dot(act.astype(BF16), w2_s[...], preferred_element_type=F32) + b2_ref[0])
        if n_rows < bm:
            y_ref[n_rows * V7X_SUBLANES:, :] = jnp.zeros(
                ((bm - n_rows) * V7X_SUBLANES, V7X_LANES), F32)

    used = i < nu_ref[0]
    half = bm // 2

    @pl.when(jnp.logical_and(used, bv_ref[i] > half))
    def _():
        mlp(bm)

    @pl.when(jnp.logical_and(used, bv_ref[i] <= half))
    def _():
        mlp(half)

    @pl.when(i >= nu_ref[0])
    def _():
        y_ref[...] = jnp.zeros_like(y_ref)


def _ffn(xs, blk_e, blk_valid, blk_first, n_used, next_e, w1, b1g, b1l, w2, b2, bm):
    E, D, F2 = w1.shape
    F = F2 // 2
    slab = bm * V7X_SUBLANES
    nblk = xs.shape[0] // slab

    def row_blk(i, be, bv, bf, nu, nxt):
        return (jnp.minimum(i, nu[0] - 1), 0)

    def per_e(i, be, bv, bf, nu, nxt):
        return (be[i], 0, 0)

    vmem = (D * F2 + F * D) * (4 + 2) + 8 * bm * F2 * 4
    return pl.pallas_call(
        _ffn_kernel,
        grid_spec=pltpu.PrefetchScalarGridSpec(
            num_scalar_prefetch=5,
            grid=(nblk,),
            in_specs=[
                pl.BlockSpec((slab, V7X_LANES), row_blk),
                pl.BlockSpec(memory_space=pl.ANY),
                pl.BlockSpec((1, 1, F), per_e),
                pl.BlockSpec((1, 1, F), per_e),
                pl.BlockSpec(memory_space=pl.ANY),
                pl.BlockSpec((1, 1, D), per_e),
            ],
            out_specs=pl.BlockSpec((slab, V7X_LANES), lambda i, be, bv, bf, nu, nxt: (i, 0)),
            scratch_shapes=[
                pltpu.VMEM((D, F2), F32),
                pltpu.VMEM((F, D), F32),
                pltpu.VMEM((D, F), BF16),
                pltpu.VMEM((D, F), BF16),
                pltpu.VMEM((F, D), BF16),
                pltpu.SemaphoreType.DMA((2,)),
            ],
        ),
        out_shape=jax.ShapeDtypeStruct(xs.shape, F32),
        compiler_params=pltpu.CompilerParams(
            dimension_semantics=("arbitrary",),
            vmem_limit_bytes=_vmem_limit(vmem)),
        name="ffn",
    )(blk_e, blk_valid, blk_first, n_used, next_e, xs, w1, b1g, b1l, w2, b2)


def _combine_kernel(plan_ref, plan_next_ref, ys_ref, x1_ref, slot_ref, gate_ref, gpost_ref, g2_ref,
                    o_ref, buf, sem):
    i = pl.program_id(0)
    T = x1_ref.shape[0]
    cur = i % 2

    def gather(plan, b):
        _run_copies(plan, T, lambda l, d, n: pltpu.make_async_copy(
            _row_slab(ys_ref, d, n), _row_slab(buf, l, n, lead=(b,)), sem.at[b]))

    @pl.when(i == 0)
    def _():
        gather(plan_ref, cur)

    @pl.when(i + 1 < pl.num_programs(0))
    def _():
        gather(plan_next_ref, 1 - cur)

    pltpu.make_async_copy(_row_slab(ys_ref, 0, TOP_K * T), buf.at[cur], sem.at[cur]).wait()

    slot = slot_ref[...]
    gates = gate_ref[...]
    r_id = lax.broadcasted_iota(jnp.int32, (T, TOP_K * T), 1)
    w = jnp.zeros((T, TOP_K * T), F32)
    for k in range(TOP_K):
        w = jnp.where(r_id == slot[:, k:k + 1], gates[:, k:k + 1], w)
    ff = jnp.dot(w.astype(BF16), _from_slabs(buf, lead=(cur,)).astype(BF16),
                 preferred_element_type=F32)
    o_ref[...] = x1_ref[...] + g2_ref[0] * _rms(ff, gpost_ref[...])


def _combine(ys, plan, x1, slot_tk, gates_tk, gpost, g2, S):
    N, D = x1.shape
    T = _tiles()["moe_t"]
    per_b = S // T
    n_tiles = N // T
    plan_spec = lambda f: pl.BlockSpec((_plan_len(T),), f, memory_space=pltpu.SMEM)
    return pl.pallas_call(
        _combine_kernel,
        grid=(n_tiles,),
        in_specs=[
            plan_spec(lambda i: (i,)),
            plan_spec(lambda i: (jnp.minimum(i + 1, n_tiles - 1),)),
            pl.BlockSpec(memory_space=pl.ANY),
            pl.BlockSpec((T, D), lambda i: (i, 0)),
            pl.BlockSpec((T, TOP_K), lambda i: (i, 0)),
            pl.BlockSpec((T, TOP_K), lambda i: (i, 0)),
            pl.BlockSpec((1, D), lambda i: (0, 0)),
            pl.BlockSpec((1, 1, D), lambda i: (i // per_b, 0, 0)),
        ],
        out_specs=pl.BlockSpec((T, D), lambda i: (i, 0)),
        out_shape=jax.ShapeDtypeStruct((N, D), F32),
        scratch_shapes=[
            pltpu.VMEM((2, TOP_K * T * V7X_SUBLANES, V7X_LANES), F32),
            pltpu.SemaphoreType.DMA((2,)),
        ],
        compiler_params=pltpu.CompilerParams(
            dimension_semantics=("arbitrary",),
            vmem_limit_bytes=_vmem_limit(10 * TOP_K * T * D * 4)),
        name="combine",
    )(plan, plan, ys, x1, slot_tk, gates_tk, gpost, g2)


def _q_scaled(w):
    scale = HEAD_DIM ** -0.5
    assert math.log2(scale).is_integer()
    col = np.arange(D_IN)
    is_q = (col >= 2 * D_RNN) & (col < 2 * D_RNN + Q_DIM)
    return w * jnp.asarray(np.where(is_q, scale, 1.0), w.dtype)


def _layer(x2, c, B, S, p):
    D = D_MODEL
    N = B * S
    ada = _ada(c, p["w_ada"], p["b_ada"])
    sh1, sc1, g1, sh2, sc2, g2 = [a.reshape(B, 1, D) for a in jnp.split(ada, 6, axis=-1)]
    row = lambda v: v.reshape(1, -1)

    gw = _tiles()["rnn_group"]
    proj_rest, y_rnn = _mixin(
        x2, row(p["norm_pre_mix"]), sc1, sh1,
        _q_scaled(p["w_in"]).astype(BF16), _q_scaled(row(p["b_in"])),
        p["conv_w"], row(p["conv_b"]),
        (0.5 * _block_diag_tiles(p["rg_w_a"], gw)).astype(BF16), 0.5 * row(p["rg_b_a"]),
        (0.5 * _block_diag_tiles(p["rg_w_x"], gw)).astype(BF16), 0.5 * row(p["rg_b_x"]),
        row(p["rg_lambda"]), S)
    y_att = _attn(proj_rest, p["attn_sinks"], S)

    x1, h2, logits_t = _merge(
        x2, y_rnn, y_att, proj_rest,
        p["w_o_rnn"].astype(BF16), p["w_o_attn"].astype(BF16), p["w_out"].astype(BF16),
        row(p["norm_post_mix"]), g1, row(p["norm_pre_ffn"]), sc2, sh2,
        p["router_w"].T.astype(BF16), p["router_b"].reshape(N_EXPERTS, 1), S)

    gates, slot, counts, tile_cnt, tile_car, tile_lst = _route(logits_t)

    bm = _tiles()["moe_bm"]
    n_tiles = N // _tiles()["moe_t"]
    n_rows = N * TOP_K + N_EXPERTS * bm
    nblk = n_rows // bm
    counts = counts.reshape(N_EXPERTS)
    padded = ((counts + bm - 1) // bm) * bm
    pend = jnp.cumsum(padded)
    pstart = pend - padded
    eids = jnp.arange(N_EXPERTS, dtype=jnp.int32)
    per_run = lambda a: a[:, :n_tiles].T.astype(jnp.int32)
    plan = _copy_plan(per_run(tile_cnt), per_run(tile_lst), per_run(tile_car + pstart[:, None]),
                      _tiles()["moe_t"])
    tail_start = jnp.maximum(pend - bm, 0).astype(jnp.int32)
    n_used = (pend[-1] // bm).astype(jnp.int32).reshape(1)
    blk_row0 = jnp.minimum(jnp.arange(nblk, dtype=jnp.int32), n_used[0] - 1) * bm
    blk_e = jnp.sum(blk_row0[:, None] >= pend[None, :], axis=1).astype(jnp.int32)
    mine = blk_e[:, None] == eids[None, :]
    blk_cnt = jnp.sum(jnp.where(mine, counts, 0), axis=1)
    blk_pstart = jnp.sum(jnp.where(mine, pstart, 0), axis=1)
    blk_valid = jnp.clip(blk_cnt - (blk_row0 - blk_pstart), 0, bm).astype(jnp.int32)
    blk_first = (blk_row0 == blk_pstart).astype(jnp.int32)

    xs = _dispatch(h2, slot, plan, tail_start, n_used, n_rows, bm)
    b1 = p["moe_b1"].reshape(N_EXPERTS, D_FF, 2)
    later = jnp.logical_and(counts[None, :] > 0, eids[None, :] > eids[:, None])
    next_e = jnp.min(jnp.where(later, eids[None, :], N_EXPERTS), axis=1)
    next_e = jnp.where(next_e == N_EXPERTS, -1, next_e).astype(jnp.int32)
    ys = _ffn(xs, blk_e, blk_valid, blk_first, n_used, next_e, p["moe_w1"],
              b1[:, :, 0].reshape(N_EXPERTS, 1, D_FF), b1[:, :, 1].reshape(N_EXPERTS, 1, D_FF),
              p["moe_w2"], p["moe_b2"].reshape(N_EXPERTS, 1, D), bm)
    return _combine(ys, plan, x1, slot.T, gates.T, row(p["norm_post_ffn"]), g2, S)


def kernel(x, c, w_ada, b_ada, norm_pre_mix, norm_post_mix, norm_pre_ffn, norm_post_ffn, w_in, b_in, conv_w, conv_b, rg_w_a, rg_b_a, rg_w_x, rg_b_x, rg_lambda, attn_sinks, w_o_rnn, w_o_attn, w_out, router_w, router_b, moe_w1, moe_b1, moe_w2, moe_b2):
    B, S, D = x.shape
    params = dict(
        w_ada=w_ada, b_ada=b_ada, norm_pre_mix=norm_pre_mix, norm_post_mix=norm_post_mix,
        norm_pre_ffn=norm_pre_ffn, norm_post_ffn=norm_post_ffn, w_in=w_in, b_in=b_in,
        conv_w=conv_w, conv_b=conv_b, rg_w_a=rg_w_a, rg_b_a=rg_b_a, rg_w_x=rg_w_x, rg_b_x=rg_b_x,
        rg_lambda=rg_lambda, attn_sinks=attn_sinks, w_o_rnn=w_o_rnn, w_o_attn=w_o_attn,
        w_out=w_out, router_w=router_w, router_b=router_b, moe_w1=moe_w1, moe_b1=moe_b1,
        moe_w2=moe_w2, moe_b2=moe_b2)
    x2 = x.reshape(B * S, D)
    for layer in range(w_ada.shape[0]):
        x2 = _layer(x2, c, B, S, {k: v[layer] for k, v in params.items()})
    return x2.reshape(B, S, D)
```

```python
import functools
import math

import jax
import jax.numpy as jnp
import numpy as np
from jax import lax
from jax.experimental import pallas as pl
from jax.experimental.pallas import tpu as pltpu

D_MODEL = 1024
D_RNN = 1024
RNN_BLOCKS = 16
RNN_BW = D_RNN // RNN_BLOCKS
CONV_W = 4
LRU_C = 8.0
N_HEADS = 16
N_KV = 4
HEAD_DIM = 64
GROUP = N_HEADS // N_KV
WINDOW = 128
Q_DIM = N_HEADS * HEAD_DIM
KV_DIM = N_KV * HEAD_DIM
N_EXPERTS = 32
TOP_K = 4
D_FF = 1024
SWIGLU_LIMIT = 7.0
SWIGLU_ALPHA = 1.702
EPS = 1e-6
D_IN = 2 * D_RNN + Q_DIM + 2 * KV_DIM + 2 * D_MODEL
D_REST = D_IN - 2 * D_RNN

V7X_LANES = 128
V7X_SUBLANES = 8
V7X_MXU_DIM = 256
V7X_VMEM_BYTES = 64 * 1024 * 1024

F32 = jnp.float32
BF16 = jnp.bfloat16


def _tiles():
    return dict(
        ada_tn=1024,
        tok=512,
        attn_q=WINDOW,
        moe_t=256,
        moe_bm=512,
        rnn_group=V7X_MXU_DIM,
    )


def _vmem_limit(nbytes):
    return int(min(max(nbytes, 16 * 1024 * 1024), V7X_VMEM_BYTES - 8 * 1024 * 1024))


def _rms(x, g):
    return x * lax.rsqrt(jnp.mean(x * x, axis=-1, keepdims=True) + EPS) * g


def _ada_kernel(ct_ref, w_ref, b_ref, o_ref):
    ct = ct_ref[...]
    sc = ct * jax.nn.sigmoid(ct)
    w = w_ref[...]
    rows = [jnp.sum(w * sc[:, b:b + 1], axis=0, keepdims=True) for b in range(ct.shape[1])]
    o_ref[...] = jnp.concatenate(rows, axis=0) + b_ref[...]


def _ada(c, w_ada, b_ada):
    B, D = c.shape
    n_out = w_ada.shape[1]
    tn = _tiles()["ada_tn"]
    return pl.pallas_call(
        _ada_kernel,
        grid=(n_out // tn,),
        in_specs=[
            pl.BlockSpec((D, B), lambda j: (0, 0)),
            pl.BlockSpec((D, tn), lambda j: (0, j)),
            pl.BlockSpec((1, tn), lambda j: (0, j)),
        ],
        out_specs=pl.BlockSpec((B, tn), lambda j: (0, j)),
        out_shape=jax.ShapeDtypeStruct((B, n_out), F32),
        compiler_params=pltpu.CompilerParams(
            dimension_semantics=("arbitrary",),
            vmem_limit_bytes=_vmem_limit(4 * D * tn * 4)),
        name="ada",
    )(c.T, w_ada, b_ada.reshape(1, n_out))


def _gelu_tanh(x):
    return 0.5 * x * (1.0 + jnp.tanh(math.sqrt(2.0 / math.pi) * (x + 0.044715 * (x * x * x))))


def _softplus(z):
    return jnp.maximum(z, 0.0) + jnp.log1p(jnp.exp(-jnp.abs(z)))


def _sigmoid(x):
    return 0.5 * jnp.tanh(0.5 * x) + 0.5


def _rest_col(c):
    c -= 2 * D_RNN
    q_end, k_end, v_end = Q_DIM, Q_DIM + KV_DIM, Q_DIM + 2 * KV_DIM
    if c < q_end:
        return c
    if c < k_end:
        return Q_DIM + 2 * D_MODEL + (c - q_end)
    if c < v_end:
        return Q_DIM + 2 * D_MODEL + KV_DIM + (c - k_end)
    return Q_DIM + (c - v_end)


def _mixin_kernel(x_ref, g_ref, sc_ref, sh_ref, w_ref, b_ref, cw_ref, cb_ref, wa_ref, ba_ref,
                  wx_ref, bx_ref, lam_ref, rest_ref, y_ref,
                  hb_s, xbuf, gr_s, a_s, b_s, h_s, ga_s, gb_s, gc_s, carry, *, per_b, gw, chunk):
    t = pl.program_id(0) % per_b
    tt = x_ref.shape[0]
    halo = V7X_SUBLANES
    ng = tt // V7X_SUBLANES
    n_lt = a_s.shape[0]
    per_g = gw // V7X_LANES
    lanes = lambda j: slice(j * V7X_LANES, (j + 1) * V7X_LANES)
    slab = lambda j, r: (j, pl.ds(r, ng, stride=V7X_SUBLANES), slice(None))

    @pl.when(t == 0)
    def _():
        xbuf[:, 0:halo, :] = jnp.zeros((n_lt, halo, V7X_LANES), F32)
        carry[...] = jnp.zeros_like(carry)

    hb_s[...] = (_rms(x_ref[...], g_ref[...]) * (1.0 + sc_ref[0]) + sh_ref[0]).astype(BF16)

    def proj(c0):
        return (jnp.dot(hb_s[...], w_ref[:, c0:c0 + chunk], preferred_element_type=F32)
                + b_ref[:, c0:c0 + chunk])

    def proj_rest(c0):
        acc = proj(c0).astype(BF16)
        for p0 in range(0, chunk, KV_DIM):
            d0 = _rest_col(c0 + p0)
            rest_ref[:, d0:d0 + KV_DIM] = acc[:, p0:p0 + KV_DIM]

    for c0 in range(0, D_RNN, chunk):
        acc = proj(c0)
        for j in range(chunk // V7X_LANES):
            xbuf[c0 // V7X_LANES + j, halo:halo + tt, :] = acc[:, lanes(j)]
    for c0 in range(D_RNN, 2 * D_RNN, chunk):
        gr_s[:, c0 - D_RNN:c0 - D_RNN + chunk] = proj(c0)

    def conv(j):
        ls = lanes(j)
        taps = {o: xbuf[slab(j, halo + o)] for o in range(-(CONV_W - 1), V7X_SUBLANES)}
        for r in range(V7X_SUBLANES):
            acc = cb_ref[:, ls] + taps[r - (CONV_W - 1)] * cw_ref[0:1, ls]
            for kk in range(1, CONV_W):
                acc = acc + taps[r - (CONV_W - 1) + kk] * cw_ref[kk:kk + 1, ls]
            h_s[slab(j, r)] = acc
        xbuf[j, 0:halo, :] = xbuf[j, tt:tt + halo, :]

    def gates(g):
        cs = slice(g * gw, (g + 1) * gw)
        rate = (-LRU_C * math.log2(math.e)) * _softplus(-lam_ref[:, cs])
        reset = jnp.logical_and(t == 0, lax.broadcasted_iota(jnp.int32, (tt, gw), 0) == 0)
        xc = jnp.concatenate([h_s[g * per_g + j] for j in range(per_g)], axis=1)
        xg = xc.astype(BF16)
        gate_r = 0.5 * jnp.tanh(jnp.dot(xg, wa_ref[g], preferred_element_type=F32) + ba_ref[:, cs]) + 0.5
        gate_i = 0.5 * jnp.tanh(jnp.dot(xg, wx_ref[g], preferred_element_type=F32) + bx_ref[:, cs]) + 0.5
        a = jnp.exp2(gate_r * rate)
        v = (1.0 - a) * (1.0 + a)
        mult = jnp.where(reset, 1.0, jnp.where(v > 0.0, v * lax.rsqrt(v), 0.0))
        bt = (xc * gate_i) * mult
        for j in range(per_g):
            a_s[g * per_g + j] = a[:, lanes(j)]
            b_s[g * per_g + j] = bt[:, lanes(j)]

    def scan_groups(j):
        acc_a = a_s[slab(j, 0)]
        acc_h = b_s[slab(j, 0)]
        h_s[slab(j, 0)] = acc_h
        for r in range(1, V7X_SUBLANES):
            a_r = a_s[slab(j, r)]
            acc_h = a_r * acc_h + b_s[slab(j, r)]
            acc_a = a_r * acc_a
            h_s[slab(j, r)] = acc_h
            a_s[slab(j, r)] = acc_a
        ga_s[:, lanes(j)] = acc_a
        gb_s[:, lanes(j)] = acc_h

    def across(gi, h_prev):
        gc_s[pl.ds(gi, 1), :] = h_prev
        return ga_s[pl.ds(gi, 1), :] * h_prev + gb_s[pl.ds(gi, 1), :]

    def finish(j):
        h_in = gc_s[:, lanes(j)]
        for r in range(V7X_SUBLANES):
            h_s[slab(j, r)] = a_s[slab(j, r)] * h_in + h_s[slab(j, r)]
        y_ref[:, lanes(j)] = (h_s[j] * _gelu_tanh(gr_s[:, lanes(j)])).astype(BF16)

    rest = [functools.partial(proj_rest, c0) for c0 in range(2 * D_RNN, w_ref.shape[1], chunk)]
    before = ([functools.partial(conv, j) for j in range(n_lt)]
              + [functools.partial(gates, g) for g in range(n_lt // per_g)]
              + [functools.partial(scan_groups, j) for j in range(n_lt)])
    after = [functools.partial(finish, j) for j in range(n_lt)]
    n_before = (len(rest) * len(before)) // (len(before) + len(after))

    def interleave(steps, chunks):
        every = -(-len(steps) // max(len(chunks), 1))
        for n, step in enumerate(steps):
            step()
            if (n + 1) % every == 0 and chunks:
                chunks.pop(0)()
        while chunks:
            chunks.pop(0)()

    interleave(before, rest[:n_before])
    carry[...] = lax.fori_loop(0, ng, across, carry[...])
    interleave(after, rest[n_before:])


def _block_diag_tiles(w, gw):
    nb, bw, _ = w.shape
    per = gw // bw
    w4 = w.reshape(nb // per, per, bw, bw)
    eye = jnp.eye(per, dtype=w.dtype)
    return jnp.einsum("gpij,pq->gpiqj", w4, eye).reshape(nb // per, gw, gw)


def _mixin(x2, g, sc, sh, w_bf, b_in, conv_w, conv_b, wa, ba, wx, bx, lam, S):
    N, D = x2.shape
    C = D_RNN
    tt = _tiles()["tok"]
    gw = _tiles()["rnn_group"]
    per_b = S // tt
    chunk = 2 * V7X_MXU_DIM
    vec = lambda: pl.BlockSpec((1, C), lambda i: (0, 0))
    bvec = lambda: pl.BlockSpec((1, 1, D), lambda i: (i // per_b, 0, 0))
    gate_w = lambda: pl.BlockSpec((C // gw, gw, gw), lambda i: (0, 0, 0))
    slabs = lambda rows: pltpu.VMEM((C // V7X_LANES, rows, V7X_LANES), F32)
    groups = lambda: pltpu.VMEM((tt // V7X_SUBLANES, C), F32)
    vmem = D * D_IN * 2 + 2 * tt * (D * 4 + D_REST * 2 + C * 2) + tt * C * (2 + 5 * 4) + 8 * tt * chunk * 4
    return pl.pallas_call(
        functools.partial(_mixin_kernel, per_b=per_b, gw=gw, chunk=chunk),
        grid=(N // tt,),
        in_specs=[
            pl.BlockSpec((tt, D), lambda i: (i, 0)),
            pl.BlockSpec((1, D), lambda i: (0, 0)),
            bvec(), bvec(),
            pl.BlockSpec((D, D_IN), lambda i: (0, 0), pipeline_mode=pl.Buffered(1)),
            pl.BlockSpec((1, D_IN), lambda i: (0, 0)),
            pl.BlockSpec((CONV_W, C), lambda i: (0, 0)),
            vec(), gate_w(), vec(), gate_w(), vec(), vec(),
        ],
        out_specs=[
            pl.BlockSpec((tt, D_REST), lambda i: (i, 0)),
            pl.BlockSpec((tt, C), lambda i: (i, 0)),
        ],
        out_shape=[
            jax.ShapeDtypeStruct((N, D_REST), BF16),
            jax.ShapeDtypeStruct((N, C), BF16),
        ],
        scratch_shapes=[
            pltpu.VMEM((tt, D), BF16),
            slabs(tt + V7X_SUBLANES),
            pltpu.VMEM((tt, C), F32),
            slabs(tt), slabs(tt), slabs(tt),
            groups(), groups(), groups(),
            pltpu.VMEM((1, C), F32),
        ],
        compiler_params=pltpu.CompilerParams(
            dimension_semantics=("arbitrary",),
            vmem_limit_bytes=_vmem_limit(vmem)),
        name="mixin",
    )(x2, g, sc, sh, w_bf, b_in, conv_w, conv_b, wa, ba, wx, bx, lam)


def _alibi_slopes():
    return [2.0 ** (-8.0 * (h + 1) / N_HEADS) for h in range(N_HEADS)]


def _attn_kernel(sink_ref, q_ref, kp_ref, kc_ref, vp_ref, vc_ref, o_ref, bias_s, *, per_b):
    blk = pl.program_id(0) % per_b
    bq = q_ref.shape[0]
    slopes = _alibi_slopes()

    @pl.when(pl.program_id(0) == 0)
    def _():
        qi = lax.broadcasted_iota(jnp.int32, (bq, 2 * bq), 0)
        ci = lax.broadcasted_iota(jnp.int32, (bq, 2 * bq), 1)
        dist = qi + bq - ci
        valid = (dist >= 0) & (dist < WINDOW)
        distf = dist.astype(F32)
        for h in range(N_HEADS):
            b = jnp.where(valid, -slopes[h] * distf, -jnp.inf)
            bias_s[0, h] = jnp.where(ci >= bq, b, -jnp.inf)
            bias_s[1, h] = b

    assert 2 * HEAD_DIM == V7X_LANES and GROUP % 2 == 0
    table = jnp.minimum(blk, 1)
    low = lax.broadcasted_iota(jnp.int32, (bq, V7X_LANES), 1) < HEAD_DIM
    zero = jnp.zeros((bq, V7X_LANES), BF16)
    for kvt in range(N_KV // 2):
        lt = slice(kvt * V7X_LANES, (kvt + 1) * V7X_LANES)
        k_t = jnp.concatenate([kp_ref[:, lt], kc_ref[:, lt]], axis=0)
        v_t = jnp.concatenate([vp_ref[:, lt], vc_ref[:, lt]], axis=0)
        k_swapped = pltpu.roll(k_t, HEAD_DIM, 1)
        for kv_half in range(2):
            kv = 2 * kvt + kv_half
            for pair in range(GROUP // 2):
                h0 = kv * GROUP + 2 * pair
                q_t = q_ref[:, (h0 // 2) * V7X_LANES:(h0 // 2 + 1) * V7X_LANES]
                halves = []
                for q_half in range(2):
                    h = h0 + q_half
                    qm = jnp.where(low if q_half == 0 else ~low, q_t, zero)
                    kk = k_t if q_half == kv_half else k_swapped
                    s = lax.dot_general(qm, kk, (((1,), (1,)), ((), ())), preferred_element_type=F32)
                    s = s + bias_s[table, h]
                    sink = sink_ref[h]
                    m = jnp.maximum(jnp.max(s, axis=-1, keepdims=True), sink)
                    p = jnp.exp(s - m)
                    denom = jnp.sum(p, axis=-1, keepdims=True) + jnp.exp(sink - m)
                    halves.append(jnp.dot(p.astype(BF16), v_t, preferred_element_type=F32) / denom)
                if kv_half == 0:
                    o = jnp.where(low, halves[0], pltpu.roll(halves[1], HEAD_DIM, 1))
                else:
                    o = jnp.where(low, pltpu.roll(halves[0], HEAD_DIM, 1), halves[1])
                o_ref[:, (h0 // 2) * V7X_LANES:(h0 // 2 + 1) * V7X_LANES] = o.astype(BF16)


def _attn(proj_rest, sinks, S):
    N = proj_rest.shape[0]
    bq = _tiles()["attn_q"]
    per_b = S // bq
    k_col = (Q_DIM + 2 * D_MODEL) // KV_DIM
    v_col = k_col + 1

    def prev(i):
        return jnp.where(i % per_b == 0, i, i - 1)

    return pl.pallas_call(
        functools.partial(_attn_kernel, per_b=per_b),
        grid=(N // bq,),
        in_specs=[
            pl.BlockSpec(memory_space=pltpu.SMEM),
            pl.BlockSpec((bq, Q_DIM), lambda i: (i, 0)),
            pl.BlockSpec((bq, KV_DIM), lambda i: (prev(i), k_col)),
            pl.BlockSpec((bq, KV_DIM), lambda i: (i, k_col)),
            pl.BlockSpec((bq, KV_DIM), lambda i: (prev(i), v_col)),
            pl.BlockSpec((bq, KV_DIM), lambda i: (i, v_col)),
        ],
        out_specs=pl.BlockSpec((bq, Q_DIM), lambda i: (i, 0)),
        out_shape=jax.ShapeDtypeStruct((N, Q_DIM), BF16),
        scratch_shapes=[pltpu.VMEM((2, N_HEADS, bq, 2 * bq), F32)],
        compiler_params=pltpu.CompilerParams(
            dimension_semantics=("arbitrary",),
            vmem_limit_bytes=_vmem_limit(3 * 2 * N_HEADS * bq * 2 * bq * 4)),
        name="attn",
    )(sinks, proj_rest, proj_rest, proj_rest, proj_rest, proj_rest)


def _merge_kernel(x_ref, yr_ref, ya_ref, gr_ref, ga_ref, wr_ref, wa_ref, wo_ref,
                  gpost_ref, g1_ref, gpre_ref, sc2_ref, sh2_ref, rwt_ref, rb_ref,
                  x1_ref, h2_ref, lg_ref):
    r = jnp.dot(yr_ref[...], wr_ref[...], preferred_element_type=F32)
    a = jnp.dot(ya_ref[...], wa_ref[...], preferred_element_type=F32)
    merged = (_sigmoid(gr_ref[...].astype(F32)) * r
              + _sigmoid(ga_ref[...].astype(F32)) * a)
    mix = jnp.dot(merged.astype(BF16), wo_ref[...], preferred_element_type=F32)
    x1 = x_ref[...] + g1_ref[0] * _rms(mix, gpost_ref[...])
    x1_ref[...] = x1
    h2 = _rms(x1, gpre_ref[...]) * (1.0 + sc2_ref[0]) + sh2_ref[0]
    h2_ref[...] = h2
    lg = lax.dot_general(rwt_ref[...], h2.astype(BF16), (((1,), (1,)), ((), ())),
                         preferred_element_type=F32)
    lg_ref[...] = lg + rb_ref[...]


def _merge(x2, y_rnn, y_att, proj_rest, wr, wa, wo, gpost, g1, gpre, sc2, sh2, rwt, rb, S):
    N, D = x2.shape
    tm = _tiles()["tok"]
    per_b = S // tm
    gate_r_col = Q_DIM // D
    mat = lambda: pl.BlockSpec((D, D), lambda i: (0, 0))
    vec = lambda: pl.BlockSpec((1, D), lambda i: (0, 0))
    bvec = lambda: pl.BlockSpec((1, 1, D), lambda i: (i // per_b, 0, 0))
    tile = lambda col=0: pl.BlockSpec((tm, D), lambda i: (i, col))
    return pl.pallas_call(
        _merge_kernel,
        grid=(N // tm,),
        in_specs=[
            tile(), tile(), tile(), tile(gate_r_col), tile(gate_r_col + 1),
            mat(), mat(), mat(),
            vec(), bvec(), vec(), bvec(), bvec(),
            pl.BlockSpec((N_EXPERTS, D), lambda i: (0, 0)),
            pl.BlockSpec((N_EXPERTS, 1), lambda i: (0, 0)),
        ],
        out_specs=[
            tile(), tile(),
            pl.BlockSpec((N_EXPERTS, tm), lambda i: (0, i)),
        ],
        out_shape=[
            jax.ShapeDtypeStruct((N, D), F32),
            jax.ShapeDtypeStruct((N, D), F32),
            jax.ShapeDtypeStruct((N_EXPERTS, N), F32),
        ],
        compiler_params=pltpu.CompilerParams(
            dimension_semantics=("arbitrary",),
            vmem_limit_bytes=_vmem_limit(6 * D * D * 2 + 24 * tm * D * 4)),
        name="merge",
    )(x2, y_rnn, y_att, proj_rest, proj_rest, wr, wa, wo, gpost, g1, gpre, sc2, sh2, rwt, rb)


def _route_kernel(lg_ref, g_ref, slot_ref, cnt_ref, tcnt_ref, tcar_ref, tlst_ref, carry):
    i = pl.program_id(0)

    @pl.when(i == 0)
    def _():
        carry[...] = jnp.zeros_like(carry)
        tcnt_ref[...] = jnp.zeros_like(tcnt_ref)
        tcar_ref[...] = jnp.zeros_like(tcar_ref)
        tlst_ref[...] = jnp.zeros_like(tlst_ref)

    l = lg_ref[...]
    E, T = l.shape
    row = lax.broadcasted_iota(jnp.int32, (E, T), 0).astype(F32)
    vals, idxs = [], []
    for _ in range(TOP_K):
        m = jnp.max(l, axis=0, keepdims=True)
        idx = jnp.min(jnp.where(l == m, row, float(E)), axis=0, keepdims=True)
        vals.append(m)
        idxs.append(idx)
        l = jnp.where(row == idx, -jnp.inf, l)
    ex = [jnp.exp(v - vals[0]) for v in vals]
    tot = ex[0]
    for e in ex[1:]:
        tot = tot + e
    g_ref[...] = jnp.concatenate([e / tot for e in ex], axis=0)

    hot = [row == idx for idx in idxs]
    onehot = jnp.zeros((E, T), F32)
    for hk in hot:
        onehot = onehot + hk.astype(F32)
    tri_t = (lax.broadcasted_iota(jnp.int32, (T, T), 0)
             < lax.broadcasted_iota(jnp.int32, (T, T), 1)).astype(BF16)
    before = jnp.dot(onehot.astype(BF16), tri_t, preferred_element_type=F32)
    cnt = jnp.sum(onehot, axis=1, keepdims=True)
    tri_e = (lax.broadcasted_iota(jnp.int32, (E, E), 1)
             < lax.broadcasted_iota(jnp.int32, (E, E), 0)).astype(BF16)
    lstart = jnp.dot(tri_e, jnp.broadcast_to(cnt, (E, V7X_LANES)).astype(BF16),
                     preferred_element_type=F32)[:, 0:1]
    local = before + lstart
    slots = [jnp.sum(jnp.where(hk, local, 0.0), axis=0, keepdims=True) for hk in hot]
    slot_ref[...] = jnp.concatenate(slots, axis=0).astype(jnp.int32)

    mine = lax.broadcasted_iota(jnp.int32, tcnt_ref.shape, 1) == i
    tcnt_ref[...] = jnp.where(mine, cnt.astype(jnp.int32), tcnt_ref[...])
    tcar_ref[...] = jnp.where(mine, carry[...].astype(jnp.int32), tcar_ref[...])
    tlst_ref[...] = jnp.where(mine, lstart.astype(jnp.int32), tlst_ref[...])
    total = carry[...] + cnt
    carry[...] = total
    cnt_ref[...] = total.astype(jnp.int32)


def _route(logits_t):
    E, N = logits_t.shape
    T = _tiles()["moe_t"]
    assert T <= 2 ** 8 and N // T <= V7X_LANES
    out = lambda: pl.BlockSpec((TOP_K, T), lambda i: (0, i))
    per_tile = lambda: pl.BlockSpec((E, V7X_LANES), lambda i: (0, 0))
    return pl.pallas_call(
        _route_kernel,
        grid=(N // T,),
        in_specs=[pl.BlockSpec((E, T), lambda i: (0, i))],
        out_specs=[out(), out(), pl.BlockSpec((E, 1), lambda i: (0, 0)),
                   per_tile(), per_tile(), per_tile()],
        out_shape=[
            jax.ShapeDtypeStruct((TOP_K, N), F32),
            jax.ShapeDtypeStruct((TOP_K, N), jnp.int32),
            jax.ShapeDtypeStruct((E, 1), jnp.int32),
            jax.ShapeDtypeStruct((E, V7X_LANES), jnp.int32),
            jax.ShapeDtypeStruct((E, V7X_LANES), jnp.int32),
            jax.ShapeDtypeStruct((E, V7X_LANES), jnp.int32),
        ],
        scratch_shapes=[pltpu.VMEM((E, 1), F32)],
        compiler_params=pltpu.CompilerParams(dimension_semantics=("arbitrary",)),
        name="route",
    )(logits_t)


def _run_sizes(T):
    return [2 ** b for b in range(int(math.log2(T)), -1, -1)]


def _copy_plan(tile_cnt, tile_lst, tile_dst, T):
    sizes = jnp.asarray(_run_sizes(T), jnp.int32)[None, :, None]
    cnt, lst, dst = (a[:, None, :] for a in (tile_cnt, tile_lst, tile_dst))
    has = (cnt & sizes) != 0
    off = (cnt // (2 * sizes)) * (2 * sizes)
    eids = jnp.arange(N_EXPERTS)
    earlier = eids[:, None] < eids[None, :]
    place = jnp.sum(jnp.where(earlier, has[..., :, None], False), axis=-2)
    hit = has[..., :, None] & (place[..., :, None] == eids)
    pick = lambda v: jnp.sum(jnp.where(hit, v[..., :, None], 0), axis=-2)
    n_tiles, nb = tile_cnt.shape[0], sizes.shape[1]
    parts = [pick(jnp.broadcast_to(lst + off, has.shape)).reshape(n_tiles, -1),
             pick(jnp.broadcast_to(dst + off, has.shape)).reshape(n_tiles, -1),
             jnp.sum(has, axis=-1).reshape(n_tiles, nb)]
    plan = jnp.concatenate(parts, axis=1).astype(jnp.int32)
    pad = _plan_len(T) - plan.shape[1]
    return jnp.pad(plan, ((0, 0), (0, pad))).reshape(-1)


def _plan_len(T):
    nb = len(_run_sizes(T))
    return max(V7X_LANES, pl.next_power_of_2(2 * nb * N_EXPERTS + nb))


def _run_copies(plan_ref, T, make):
    sizes = _run_sizes(T)
    nb = len(sizes)
    for bi, b in enumerate(sizes):
        def body(j, c, bi=bi, b=b):
            make(plan_ref[bi * N_EXPERTS + j], plan_ref[(nb + bi) * N_EXPERTS + j], b).start()
            return c

        lax.fori_loop(0, plan_ref[2 * nb * N_EXPERTS + bi], body, 0)


def _row_slab(ref, row, nrows, lead=()):
    rows = pl.ds(pl.multiple_of(row * V7X_SUBLANES, V7X_SUBLANES), nrows * V7X_SUBLANES)
    return ref.at[(*lead, rows)]


def _to_slabs(ref, val, lead=()):
    rows = val.shape[0]
    for s in range(V7X_SUBLANES):
        ref[(*lead, pl.ds(s, rows, stride=V7X_SUBLANES), slice(None))] = (
            val[:, s * V7X_LANES:(s + 1) * V7X_LANES])


def _from_slabs(ref, lead=(), rows=None):
    rows = ref.shape[-2] // V7X_SUBLANES if rows is None else rows
    return jnp.concatenate(
        [ref[(*lead, pl.ds(s, rows, stride=V7X_SUBLANES), slice(None))]
         for s in range(V7X_SUBLANES)], axis=1)


def _dispatch_kernel(tail_ref, nu_ref, plan_ref, slot_ref, h_ref, xs_ref, buf, zbuf, sem, zsem):
    i = pl.program_id(0)
    T = h_ref.shape[0]
    bm = zbuf.shape[0] // V7X_SUBLANES
    nblk = xs_ref.shape[0] // zbuf.shape[0]
    cur = i % 2

    def zero_copy(row0):
        return pltpu.make_async_copy(zbuf, _row_slab(xs_ref, row0, bm), zsem)

    @pl.when(i == 0)
    def _():
        zbuf[...] = jnp.zeros_like(zbuf)

        def start(e, c):
            zero_copy(tail_ref[e]).start()
            return c

        def wait(e, c):
            zero_copy(tail_ref[e]).wait()
            return c

        def start_unused(j, c):
            zero_copy(j * bm).start()
            return c

        def wait_unused(j, c):
            zero_copy(j * bm).wait()
            return c

        lax.fori_loop(0, N_EXPERTS, start, 0)
        lax.fori_loop(nu_ref[0], nblk, start_unused, 0)
        lax.fori_loop(0, N_EXPERTS, wait, 0)
        lax.fori_loop(nu_ref[0], nblk, wait_unused, 0)

    slot = slot_ref[...]
    r_id = lax.broadcasted_iota(jnp.int32, (TOP_K * T, T), 0)
    perm = r_id == slot[0:1, :]
    for k in range(1, TOP_K):
        perm = jnp.logical_or(perm, r_id == slot[k:k + 1, :])
    grouped = jnp.dot(perm.astype(BF16), h_ref[...].astype(BF16), preferred_element_type=F32)
    _to_slabs(buf, grouped, lead=(cur,))

    def wait_tile(b):
        pltpu.make_async_copy(buf.at[b], _row_slab(xs_ref, 0, TOP_K * T), sem.at[b]).wait()

    @pl.when(i > 0)
    def _():
        wait_tile(1 - cur)

    _run_copies(plan_ref, T, lambda l, d, n: pltpu.make_async_copy(
        _row_slab(buf, l, n, lead=(cur,)), _row_slab(xs_ref, d, n), sem.at[cur]))

    @pl.when(i == pl.num_programs(0) - 1)
    def _():
        wait_tile(cur)


def _dispatch(h2, slot, plan, tail_start, n_used, n_rows, bm):
    N, D = h2.shape
    T = _tiles()["moe_t"]
    S8 = V7X_SUBLANES
    imap = lambda f: (lambda i, *_: f(i))
    return pl.pallas_call(
        _dispatch_kernel,
        grid_spec=pltpu.PrefetchScalarGridSpec(
            num_scalar_prefetch=2,
            grid=(N // T,),
            in_specs=[
                pl.BlockSpec((_plan_len(T),), imap(lambda i: (i,)), memory_space=pltpu.SMEM),
                pl.BlockSpec((TOP_K, T), imap(lambda i: (0, i))),
                pl.BlockSpec((T, D), imap(lambda i: (i, 0))),
            ],
            out_specs=pl.BlockSpec(memory_space=pl.ANY),
            scratch_shapes=[
                pltpu.VMEM((2, TOP_K * T * S8, V7X_LANES), F32),
                pltpu.VMEM((bm * S8, V7X_LANES), F32),
                pltpu.SemaphoreType.DMA((2,)),
                pltpu.SemaphoreType.DMA(()),
            ],
        ),
        out_shape=jax.ShapeDtypeStruct((n_rows * S8, V7X_LANES), F32),
        compiler_params=pltpu.CompilerParams(
            dimension_semantics=("arbitrary",),
            vmem_limit_bytes=_vmem_limit(8 * TOP_K * T * D * 4)),
        name="dispatch",
    )(tail_start, n_used, plan, slot, h2)


def _ffn_kernel(be_ref, bv_ref, bf_ref, nu_ref, nxt_ref, x_ref, w1_hbm, b1g_ref, b1l_ref, w2_hbm,
                b2_ref, y_ref, w1f_s, w2f_s, w1g_s, w1l_s, w2_s, sem):
    i = pl.program_id(0)
    bm, D = x_ref.shape[0] // V7X_SUBLANES, w2_s.shape[1]
    pw = 2 * V7X_LANES

    def fetch(e):
        return (pltpu.make_async_copy(w1_hbm.at[e], w1f_s, sem.at[0]),
                pltpu.make_async_copy(w2_hbm.at[e], w2f_s, sem.at[1]))

    @pl.when(i == 0)
    def _():
        for cp in fetch(be_ref[0]):
            cp.start()

    @pl.when(jnp.logical_and(i < nu_ref[0], bf_ref[i] == 1))
    def _():
        e = be_ref[i]
        for cp in fetch(e):
            cp.wait()
        src = lax.broadcasted_iota(jnp.int32, (pw, pw), 0)
        dst = lax.broadcasted_iota(jnp.int32, (pw, pw), 1)
        want = jnp.where(dst < V7X_LANES, 2 * dst, 2 * (dst - V7X_LANES) + 1)
        perm = (src == want).astype(BF16)
        for c in range(w1f_s.shape[1] // pw):
            wp = jnp.dot(w1f_s[:, c * pw:(c + 1) * pw].astype(BF16), perm,
                         preferred_element_type=F32).astype(BF16)
            w1g_s[:, c * V7X_LANES:(c + 1) * V7X_LANES] = wp[:, :V7X_LANES]
            w1l_s[:, c * V7X_LANES:(c + 1) * V7X_LANES] = wp[:, V7X_LANES:]
        w2_s[...] = w2f_s[...].astype(BF16)

        @pl.when(nxt_ref[e] >= 0)
        def _():
            for cp in fetch(nxt_ref[e]):
                cp.start()

    def mlp(n_rows):
        rows = lax.broadcasted_iota(jnp.int32, (n_rows, D), 0)
        x = jnp.where(rows < bv_ref[i], _from_slabs(x_ref, rows=n_rows), 0.0).astype(BF16)
        glu = jnp.dot(x, w1g_s[...], preferred_element_type=F32) + b1g_ref[0]
        lin = jnp.dot(x, w1l_s[...], preferred_element_type=F32) + b1l_ref[0]
        glu = jnp.minimum(glu, SWIGLU_LIMIT)
        lin = jnp.clip(lin, -SWIGLU_LIMIT, SWIGLU_LIMIT)
        act = glu * _sigmoid(SWIGLU_ALPHA * glu) * (lin + 1.0)
        _to_slabs(y_ref, jnp.dot(act.astype(BF16), w2_s[...], preferred_element_type=F32) + b2_ref[0])
        if n_rows < bm:
            y_ref[n_rows * V7X_SUBLANES:, :] = jnp.zeros(
                ((bm - n_rows) * V7X_SUBLANES, V7X_LANES), F32)

    used = i < nu_ref[0]
    half = bm // 2

    @pl.when(jnp.logical_and(used, bv_ref[i] > half))
    def _():
        mlp(bm)

    @pl.when(jnp.logical_and(used, bv_ref[i] <= half))
    def _():
        mlp(half)

    @pl.when(i >= nu_ref[0])
    def _():
        y_ref[...] = jnp.zeros_like(y_ref)


def _ffn(xs, blk_e, blk_valid, blk_first, n_used, next_e, w1, b1g, b1l, w2, b2, bm):
    E, D, F2 = w1.shape
    F = F2 // 2
    slab = bm * V7X_SUBLANES
    nblk = xs.shape[0] // slab

    def row_blk(i, be, bv, bf, nu, nxt):
        return (jnp.minimum(i, nu[0] - 1), 0)

    def per_e(i, be, bv, bf, nu, nxt):
        return (be[i], 0, 0)

    vmem = (D * F2 + F * D) * (4 + 2) + 8 * bm * F2 * 4
    return pl.pallas_call(
        _ffn_kernel,
        grid_spec=pltpu.PrefetchScalarGridSpec(
            num_scalar_prefetch=5,
            grid=(nblk,),
            in_specs=[
                pl.BlockSpec((slab, V7X_LANES), row_blk),
                pl.BlockSpec(memory_space=pl.ANY),
                pl.BlockSpec((1, 1, F), per_e),
                pl.BlockSpec((1, 1, F), per_e),
                pl.BlockSpec(memory_space=pl.ANY),
                pl.BlockSpec((1, 1, D), per_e),
            ],
            out_specs=pl.BlockSpec((slab, V7X_LANES), lambda i, be, bv, bf, nu, nxt: (i, 0)),
            scratch_shapes=[
                pltpu.VMEM((D, F2), F32),
                pltpu.VMEM((F, D), F32),
                pltpu.VMEM((D, F), BF16),
                pltpu.VMEM((D, F), BF16),
                pltpu.VMEM((F, D), BF16),
                pltpu.SemaphoreType.DMA((2,)),
            ],
        ),
        out_shape=jax.ShapeDtypeStruct(xs.shape, F32),
        compiler_params=pltpu.CompilerParams(
            dimension_semantics=("arbitrary",),
            vmem_limit_bytes=_vmem_limit(vmem)),
        name="ffn",
    )(blk_e, blk_valid, blk_first, n_used, next_e, xs, w1, b1g, b1l, w2, b2)


def _combine_kernel(plan_ref, plan_next_ref, ys_ref, x1_ref, slot_ref, gate_ref, gpost_ref, g2_ref,
                    o_ref, buf, sem):
    i = pl.program_id(0)
    T = x1_ref.shape[0]
    cur = i % 2

    def gather(plan, b):
        _run_copies(plan, T, lambda l, d, n: pltpu.make_async_copy(
            _row_slab(ys_ref, d, n), _row_slab(buf, l, n, lead=(b,)), sem.at[b]))

    @pl.when(i == 0)
    def _():
        gather(plan_ref, cur)

    @pl.when(i + 1 < pl.num_programs(0))
    def _():
        gather(plan_next_ref, 1 - cur)

    pltpu.make_async_copy(_row_slab(ys_ref, 0, TOP_K * T), buf.at[cur], sem.at[cur]).wait()

    slot = slot_ref[...]
    gates = gate_ref[...]
    r_id = lax.broadcasted_iota(jnp.int32, (T, TOP_K * T), 1)
    w = jnp.zeros((T, TOP_K * T), F32)
    for k in range(TOP_K):
        w = jnp.where(r_id == slot[:, k:k + 1], gates[:, k:k + 1], w)
    ff = jnp.dot(w.astype(BF16), _from_slabs(buf, lead=(cur,)).astype(BF16),
                 preferred_element_type=F32)
    o_ref[...] = x1_ref[...] + g2_ref[0] * _rms(ff, gpost_ref[...])


def _combine(ys, plan, x1, slot_tk, gates_tk, gpost, g2, S):
    N, D = x1.shape
    T = _tiles()["moe_t"]
    per_b = S // T
    n_tiles = N // T
    plan_spec = lambda f: pl.BlockSpec((_plan_len(T),), f, memory_space=pltpu.SMEM)
    return pl.pallas_call(
        _combine_kernel,
        grid=(n_tiles,),
        in_specs=[
            plan_spec(lambda i: (i,)),
            plan_spec(lambda i: (jnp.minimum(i + 1, n_tiles - 1),)),
            pl.BlockSpec(memory_space=pl.ANY),
            pl.BlockSpec((T, D), lambda i: (i, 0)),
            pl.BlockSpec((T, TOP_K), lambda i: (i, 0)),
            pl.BlockSpec((T, TOP_K), lambda i: (i, 0)),
            pl.BlockSpec((1, D), lambda i: (0, 0)),
            pl.BlockSpec((1, 1, D), lambda i: (i // per_b, 0, 0)),
        ],
        out_specs=pl.BlockSpec((T, D), lambda i: (i, 0)),
        out_shape=jax.ShapeDtypeStruct((N, D), F32),
        scratch_shapes=[
            pltpu.VMEM((2, TOP_K * T * V7X_SUBLANES, V7X_LANES), F32),
            pltpu.SemaphoreType.DMA((2,)),
        ],
        compiler_params=pltpu.CompilerParams(
            dimension_semantics=("arbitrary",),
            vmem_limit_bytes=_vmem_limit(10 * TOP_K * T * D * 4)),
        name="combine",
    )(plan, plan, ys, x1, slot_tk, gates_tk, gpost, g2)


def _q_scaled(w):
    scale = HEAD_DIM ** -0.5
    assert math.log2(scale).is_integer()
    col = np.arange(D_IN)
    is_q = (col >= 2 * D_RNN) & (col < 2 * D_RNN + Q_DIM)
    return w * jnp.asarray(np.where(is_q, scale, 1.0), w.dtype)


def _layer(x2, c, B, S, p):
    D = D_MODEL
    N = B * S
    ada = _ada(c, p["w_ada"], p["b_ada"])
    sh1, sc1, g1, sh2, sc2, g2 = [a.reshape(B, 1, D) for a in jnp.split(ada, 6, axis=-1)]
    row = lambda v: v.reshape(1, -1)

    gw = _tiles()["rnn_group"]
    proj_rest, y_rnn = _mixin(
        x2, row(p["norm_pre_mix"]), sc1, sh1,
        _q_scaled(p["w_in"]).astype(BF16), _q_scaled(row(p["b_in"])),
        p["conv_w"], row(p["conv_b"]),
        (0.5 * _block_diag_tiles(p["rg_w_a"], gw)).astype(BF16), 0.5 * row(p["rg_b_a"]),
        (0.5 * _block_diag_tiles(p["rg_w_x"], gw)).astype(BF16), 0.5 * row(p["rg_b_x"]),
        row(p["rg_lambda"]), S)
    y_att = _attn(proj_rest, p["attn_sinks"], S)

    x1, h2, logits_t = _merge(
        x2, y_rnn, y_att, proj_rest,
        p["w_o_rnn"].astype(BF16), p["w_o_attn"].astype(BF16), p["w_out"].astype(BF16),
        row(p["norm_post_mix"]), g1, row(p["norm_pre_ffn"]), sc2, sh2,
        p["router_w"].T.astype(BF16), p["router_b"].reshape(N_EXPERTS, 1), S)

    gates, slot, counts, tile_cnt, tile_car, tile_lst = _route(logits_t)

    bm = _tiles()["moe_bm"]
    n_tiles = N // _tiles()["moe_t"]
    n_rows = N * TOP_K + N_EXPERTS * bm
    nblk = n_rows // bm
    counts = counts.reshape(N_EXPERTS)
    padded = ((counts + bm - 1) // bm) * bm
    pend = jnp.cumsum(padded)
    pstart = pend - padded
    eids = jnp.arange(N_EXPERTS, dtype=jnp.int32)
    per_run = lambda a: a[:, :n_tiles].T.astype(jnp.int32)
    plan = _copy_plan(per_run(tile_cnt), per_run(tile_lst), per_run(tile_car + pstart[:, None]),
                      _tiles()["moe_t"])
    zr = bm // 2
    tail_start = (jnp.maximum(pstart + counts - 1, 0) // zr * zr).astype(jnp.int32)
    n_used = (pend[-1] // bm).astype(jnp.int32).reshape(1)
    blk_row0 = jnp.minimum(jnp.arange(nblk, dtype=jnp.int32), n_used[0] - 1) * bm
    blk_e = jnp.sum(blk_row0[:, None] >= pend[None, :], axis=1).astype(jnp.int32)
    mine = blk_e[:, None] == eids[None, :]
    blk_cnt = jnp.sum(jnp.where(mine, counts, 0), axis=1)
    blk_pstart = jnp.sum(jnp.where(mine, pstart, 0), axis=1)
    blk_valid = jnp.clip(blk_cnt - (blk_row0 - blk_pstart), 0, bm).astype(jnp.int32)
    blk_first = (blk_row0 == blk_pstart).astype(jnp.int32)

    xs = _dispatch(h2, slot, plan, tail_start, n_used * (bm // zr), n_rows, zr)
    b1 = p["moe_b1"].reshape(N_EXPERTS, D_FF, 2)
    later = jnp.logical_and(counts[None, :] > 0, eids[None, :] > eids[:, None])
    next_e = jnp.min(jnp.where(later, eids[None, :], N_EXPERTS), axis=1)
    next_e = jnp.where(next_e == N_EXPERTS, -1, next_e).astype(jnp.int32)
    ys = _ffn(xs, blk_e, blk_valid, blk_first, n_used, next_e, p["moe_w1"],
              b1[:, :, 0].reshape(N_EXPERTS, 1, D_FF), b1[:, :, 1].reshape(N_EXPERTS, 1, D_FF),
              p["moe_w2"], p["moe_b2"].reshape(N_EXPERTS, 1, D), bm)
    return _combine(ys, plan, x1, slot.T, gates.T, row(p["norm_post_ffn"]), g2, S)


def kernel(x, c, w_ada, b_ada, norm_pre_mix, norm_post_mix, norm_pre_ffn, norm_post_ffn, w_in, b_in, conv_w, conv_b, rg_w_a, rg_b_a, rg_w_x, rg_b_x, rg_lambda, attn_sinks, w_o_rnn, w_o_attn, w_out, router_w, router_b, moe_w1, moe_b1, moe_w2, moe_b2):
    B, S, D = x.shape
    params = dict(
        w_ada=w_ada, b_ada=b_ada, norm_pre_mix=norm_pre_mix, norm_post_mix=norm_post_mix,
        norm_pre_ffn=norm_pre_ffn, norm_post_ffn=norm_post_ffn, w_in=w_in, b_in=b_in,
        conv_w=conv_w, conv_b=conv_b, rg_w_a=rg_w_a, rg_b_a=rg_b_a, rg_w_x=rg_w_x, rg_b_x=rg_b_x,
        rg_lambda=rg_lambda, attn_sinks=attn_sinks, w_o_rnn=w_o_rnn, w_o_attn=w_o_attn,
        w_out=w_out, router_w=router_w, router_b=router_b, moe_w1=moe_w1, moe_b1=moe_b1,
        moe_w2=moe_w2, moe_b2=moe_b2)
    x2 = x.reshape(B * S, D)
    for layer in range(w_ada.shape[0]):
        x2 = _layer(x2, c, B, S, {k: v[layer] for k, v in params.items()})
    return x2.reshape(B, S, D)
```

```python
import functools
import math

import jax
import jax.numpy as jnp
import numpy as np
from jax import lax
from jax.experimental import pallas as pl
from jax.experimental.pallas import tpu as pltpu

D_MODEL = 1024
D_RNN = 1024
RNN_BLOCKS = 16
RNN_BW = D_RNN // RNN_BLOCKS
CONV_W = 4
LRU_C = 8.0
N_HEADS = 16
N_KV = 4
HEAD_DIM = 64
GROUP = N_HEADS // N_KV
WINDOW = 128
Q_DIM = N_HEADS * HEAD_DIM
KV_DIM = N_KV * HEAD_DIM
N_EXPERTS = 32
TOP_K = 4
D_FF = 1024
SWIGLU_LIMIT = 7.0
SWIGLU_ALPHA = 1.702
EPS = 1e-6
D_IN = 2 * D_RNN + Q_DIM + 2 * KV_DIM + 2 * D_MODEL
D_REST = D_IN - 2 * D_RNN

V7X_LANES = 128
V7X_SUBLANES = 8
V7X_MXU_DIM = 256
V7X_VMEM_BYTES = 64 * 1024 * 1024

F32 = jnp.float32
BF16 = jnp.bfloat16


def _tiles():
    return dict(
        ada_tn=1024,
        tok=512,
        attn_q=WINDOW,
        attn_blocks=2,
        moe_t=256,
        moe_bm=512,
        rnn_group=V7X_MXU_DIM,
    )


def _vmem_limit(nbytes):
    return int(min(max(nbytes, 16 * 1024 * 1024), V7X_VMEM_BYTES - 8 * 1024 * 1024))


def _rms(x, g):
    return x * lax.rsqrt(jnp.mean(x * x, axis=-1, keepdims=True) + EPS) * g


def _ada_kernel(ct_ref, w_ref, b_ref, o_ref):
    ct = ct_ref[...]
    sc = ct * jax.nn.sigmoid(ct)
    w = w_ref[...]
    rows = [jnp.sum(w * sc[:, b:b + 1], axis=0, keepdims=True) for b in range(ct.shape[1])]
    o_ref[...] = jnp.concatenate(rows, axis=0) + b_ref[...]


def _ada(c, w_ada, b_ada):
    B, D = c.shape
    n_out = w_ada.shape[1]
    tn = _tiles()["ada_tn"]
    return pl.pallas_call(
        _ada_kernel,
        grid=(n_out // tn,),
        in_specs=[
            pl.BlockSpec((D, B), lambda j: (0, 0)),
            pl.BlockSpec((D, tn), lambda j: (0, j)),
            pl.BlockSpec((1, tn), lambda j: (0, j)),
        ],
        out_specs=pl.BlockSpec((B, tn), lambda j: (0, j)),
        out_shape=jax.ShapeDtypeStruct((B, n_out), F32),
        compiler_params=pltpu.CompilerParams(
            dimension_semantics=("arbitrary",),
            vmem_limit_bytes=_vmem_limit(4 * D * tn * 4)),
        name="ada",
    )(c.T, w_ada, b_ada.reshape(1, n_out))


def _gelu_tanh(x):
    return 0.5 * x * (1.0 + jnp.tanh(math.sqrt(2.0 / math.pi) * (x + 0.044715 * (x * x * x))))


def _softplus(z):
    return jnp.maximum(z, 0.0) + jnp.log1p(jnp.exp(-jnp.abs(z)))


def _sigmoid(x):
    return 0.5 * jnp.tanh(0.5 * x) + 0.5


def _rest_col(c):
    c -= 2 * D_RNN
    q_end, k_end, v_end = Q_DIM, Q_DIM + KV_DIM, Q_DIM + 2 * KV_DIM
    if c < q_end:
        return c
    if c < k_end:
        return Q_DIM + 2 * D_MODEL + (c - q_end)
    if c < v_end:
        return Q_DIM + 2 * D_MODEL + KV_DIM + (c - k_end)
    return Q_DIM + (c - v_end)


def _mixin_kernel(x_ref, g_ref, sc_ref, sh_ref, w_ref, b_ref, cw_ref, cb_ref, wa_ref, ba_ref,
                  wx_ref, bx_ref, lam_ref, rest_ref, y_ref,
                  hb_s, xbuf, gr_s, a_s, b_s, h_s, ga_s, gb_s, gc_s, carry, *, per_b, gw, chunk):
    t = pl.program_id(0) % per_b
    tt = x_ref.shape[0]
    halo = V7X_SUBLANES
    ng = tt // V7X_SUBLANES
    n_lt = a_s.shape[0]
    per_g = gw // V7X_LANES
    lanes = lambda j: slice(j * V7X_LANES, (j + 1) * V7X_LANES)
    slab = lambda j, r: (j, pl.ds(r, ng, stride=V7X_SUBLANES), slice(None))

    @pl.when(t == 0)
    def _():
        xbuf[:, 0:halo, :] = jnp.zeros((n_lt, halo, V7X_LANES), F32)
        carry[...] = jnp.zeros_like(carry)

    hb_s[...] = (_rms(x_ref[...], g_ref[...]) * (1.0 + sc_ref[0]) + sh_ref[0]).astype(BF16)

    def proj(c0):
        return (jnp.dot(hb_s[...], w_ref[:, c0:c0 + chunk], preferred_element_type=F32)
                + b_ref[:, c0:c0 + chunk])

    def proj_rest(c0):
        acc = proj(c0).astype(BF16)
        for p0 in range(0, chunk, KV_DIM):
            d0 = _rest_col(c0 + p0)
            rest_ref[:, d0:d0 + KV_DIM] = acc[:, p0:p0 + KV_DIM]

    for c0 in range(0, D_RNN, chunk):
        acc = proj(c0)
        for j in range(chunk // V7X_LANES):
            xbuf[c0 // V7X_LANES + j, halo:halo + tt, :] = acc[:, lanes(j)]
    for c0 in range(D_RNN, 2 * D_RNN, chunk):
        gr_s[:, c0 - D_RNN:c0 - D_RNN + chunk] = proj(c0)

    def conv(j):
        ls = lanes(j)
        taps = {o: xbuf[slab(j, halo + o)] for o in range(-(CONV_W - 1), V7X_SUBLANES)}
        for r in range(V7X_SUBLANES):
            acc = cb_ref[:, ls] + taps[r - (CONV_W - 1)] * cw_ref[0:1, ls]
            for kk in range(1, CONV_W):
                acc = acc + taps[r - (CONV_W - 1) + kk] * cw_ref[kk:kk + 1, ls]
            h_s[slab(j, r)] = acc
        xbuf[j, 0:halo, :] = xbuf[j, tt:tt + halo, :]

    def gates(g):
        cs = slice(g * gw, (g + 1) * gw)
        rate = (-LRU_C * math.log2(math.e)) * _softplus(-lam_ref[:, cs])
        reset = jnp.logical_and(t == 0, lax.broadcasted_iota(jnp.int32, (tt, gw), 0) == 0)
        xc = jnp.concatenate([h_s[g * per_g + j] for j in range(per_g)], axis=1)
        xg = xc.astype(BF16)
        gate_r = 0.5 * jnp.tanh(jnp.dot(xg, wa_ref[g], preferred_element_type=F32) + ba_ref[:, cs]) + 0.5
        gate_i = 0.5 * jnp.tanh(jnp.dot(xg, wx_ref[g], preferred_element_type=F32) + bx_ref[:, cs]) + 0.5
        a = jnp.exp2(gate_r * rate)
        v = (1.0 - a) * (1.0 + a)
        mult = jnp.where(reset, 1.0, jnp.where(v > 0.0, v * lax.rsqrt(v), 0.0))
        bt = (xc * gate_i) * mult
        for j in range(per_g):
            a_s[g * per_g + j] = a[:, lanes(j)]
            b_s[g * per_g + j] = bt[:, lanes(j)]

    def scan_groups(j):
        acc_a = a_s[slab(j, 0)]
        acc_h = b_s[slab(j, 0)]
        h_s[slab(j, 0)] = acc_h
        for r in range(1, V7X_SUBLANES):
            a_r = a_s[slab(j, r)]
            acc_h = a_r * acc_h + b_s[slab(j, r)]
            acc_a = a_r * acc_a
            h_s[slab(j, r)] = acc_h
            a_s[slab(j, r)] = acc_a
        ga_s[:, lanes(j)] = acc_a
        gb_s[:, lanes(j)] = acc_h

    def across(gi, h_prev):
        gc_s[pl.ds(gi, 1), :] = h_prev
        return ga_s[pl.ds(gi, 1), :] * h_prev + gb_s[pl.ds(gi, 1), :]

    def finish(j):
        h_in = gc_s[:, lanes(j)]
        for r in range(V7X_SUBLANES):
            h_s[slab(j, r)] = a_s[slab(j, r)] * h_in + h_s[slab(j, r)]
        y_ref[:, lanes(j)] = (h_s[j] * _gelu_tanh(gr_s[:, lanes(j)])).astype(BF16)

    rest = [functools.partial(proj_rest, c0) for c0 in range(2 * D_RNN, w_ref.shape[1], chunk)]
    before = ([functools.partial(conv, j) for j in range(n_lt)]
              + [functools.partial(gates, g) for g in range(n_lt // per_g)]
              + [functools.partial(scan_groups, j) for j in range(n_lt)])
    after = [functools.partial(finish, j) for j in range(n_lt)]
    n_before = (len(rest) * len(before)) // (len(before) + len(after))

    def interleave(steps, chunks):
        every = -(-len(steps) // max(len(chunks), 1))
        for n, step in enumerate(steps):
            step()
            if (n + 1) % every == 0 and chunks:
                chunks.pop(0)()
        while chunks:
            chunks.pop(0)()

    interleave(before, rest[:n_before])
    carry[...] = lax.fori_loop(0, ng, across, carry[...])
    interleave(after, rest[n_before:])


def _block_diag_tiles(w, gw):
    nb, bw, _ = w.shape
    per = gw // bw
    w4 = w.reshape(nb // per, per, bw, bw)
    eye = jnp.eye(per, dtype=w.dtype)
    return jnp.einsum("gpij,pq->gpiqj", w4, eye).reshape(nb // per, gw, gw)


def _mixin(x2, g, sc, sh, w_bf, b_in, conv_w, conv_b, wa, ba, wx, bx, lam, S):
    N, D = x2.shape
    C = D_RNN
    tt = _tiles()["tok"]
    gw = _tiles()["rnn_group"]
    per_b = S // tt
    chunk = 2 * V7X_MXU_DIM
    vec = lambda: pl.BlockSpec((1, C), lambda i: (0, 0))
    bvec = lambda: pl.BlockSpec((1, 1, D), lambda i: (i // per_b, 0, 0))
    gate_w = lambda: pl.BlockSpec((C // gw, gw, gw), lambda i: (0, 0, 0))
    slabs = lambda rows: pltpu.VMEM((C // V7X_LANES, rows, V7X_LANES), F32)
    groups = lambda: pltpu.VMEM((tt // V7X_SUBLANES, C), F32)
    vmem = D * D_IN * 2 + 2 * tt * (D * 4 + D_REST * 2 + C * 2) + tt * C * (2 + 5 * 4) + 8 * tt * chunk * 4
    return pl.pallas_call(
        functools.partial(_mixin_kernel, per_b=per_b, gw=gw, chunk=chunk),
        grid=(N // tt,),
        in_specs=[
            pl.BlockSpec((tt, D), lambda i: (i, 0)),
            pl.BlockSpec((1, D), lambda i: (0, 0)),
            bvec(), bvec(),
            pl.BlockSpec((D, D_IN), lambda i: (0, 0), pipeline_mode=pl.Buffered(1)),
            pl.BlockSpec((1, D_IN), lambda i: (0, 0)),
            pl.BlockSpec((CONV_W, C), lambda i: (0, 0)),
            vec(), gate_w(), vec(), gate_w(), vec(), vec(),
        ],
        out_specs=[
            pl.BlockSpec((tt, D_REST), lambda i: (i, 0)),
            pl.BlockSpec((tt, C), lambda i: (i, 0)),
        ],
        out_shape=[
            jax.ShapeDtypeStruct((N, D_REST), BF16),
            jax.ShapeDtypeStruct((N, C), BF16),
        ],
        scratch_shapes=[
            pltpu.VMEM((tt, D), BF16),
            slabs(tt + V7X_SUBLANES),
            pltpu.VMEM((tt, C), F32),
            slabs(tt), slabs(tt), slabs(tt),
            groups(), groups(), groups(),
            pltpu.VMEM((1, C), F32),
        ],
        compiler_params=pltpu.CompilerParams(
            dimension_semantics=("arbitrary",),
            vmem_limit_bytes=_vmem_limit(vmem)),
        name="mixin",
    )(x2, g, sc, sh, w_bf, b_in, conv_w, conv_b, wa, ba, wx, bx, lam)


def _alibi_slopes():
    return [2.0 ** (-8.0 * (h + 1) / N_HEADS) for h in range(N_HEADS)]


def _attn_kernel(sink_ref, q_ref, kp_ref, kc_ref, vp_ref, vc_ref, o_ref, bias_s, *, per_b):
    first = pl.program_id(0) % per_b == 0
    bq = kp_ref.shape[0]
    slopes = _alibi_slopes()

    @pl.when(pl.program_id(0) == 0)
    def _():
        qi = lax.broadcasted_iota(jnp.int32, (bq, 2 * bq), 0)
        ci = lax.broadcasted_iota(jnp.int32, (bq, 2 * bq), 1)
        dist = qi + bq - ci
        valid = (dist >= 0) & (dist < WINDOW)
        distf = dist.astype(F32)
        for h in range(N_HEADS):
            b = jnp.where(valid, -slopes[h] * distf, -jnp.inf)
            bias_s[0, h] = jnp.where(ci >= bq, b, -jnp.inf)
            bias_s[1, h] = b

    assert 2 * HEAD_DIM == V7X_LANES and GROUP % 2 == 0
    low = lax.broadcasted_iota(jnp.int32, (bq, V7X_LANES), 1) < HEAD_DIM
    zero = jnp.zeros((bq, V7X_LANES), BF16)
    for sub in range(q_ref.shape[0] // bq):
        rows = slice(sub * bq, (sub + 1) * bq)
        before = slice((sub - 1) * bq, sub * bq)
        table = jnp.where(first, 0, 1) if sub == 0 else 1
        for kvt in range(N_KV // 2):
            lt = slice(kvt * V7X_LANES, (kvt + 1) * V7X_LANES)
            k_prev, v_prev = (kp_ref[:, lt], vp_ref[:, lt]) if sub == 0 else (kc_ref[before, lt], vc_ref[before, lt])
            k_t = jnp.concatenate([k_prev, kc_ref[rows, lt]], axis=0)
            v_t = jnp.concatenate([v_prev, vc_ref[rows, lt]], axis=0)
            k_swapped = pltpu.roll(k_t, HEAD_DIM, 1)
            for kv_half in range(2):
                kv = 2 * kvt + kv_half
                for pair in range(GROUP // 2):
                    h0 = kv * GROUP + 2 * pair
                    tile = slice((h0 // 2) * V7X_LANES, (h0 // 2 + 1) * V7X_LANES)
                    q_t = q_ref[rows, tile]
                    halves = []
                    for q_half in range(2):
                        h = h0 + q_half
                        qm = jnp.where(low if q_half == 0 else ~low, q_t, zero)
                        kk = k_t if q_half == kv_half else k_swapped
                        s = lax.dot_general(qm, kk, (((1,), (1,)), ((), ())),
                                            preferred_element_type=F32)
                        s = s + bias_s[table, h]
                        sink = sink_ref[h]
                        m = jnp.maximum(jnp.max(s, axis=-1, keepdims=True), sink)
                        p = jnp.exp(s - m)
                        denom = jnp.sum(p, axis=-1, keepdims=True) + jnp.exp(sink - m)
                        halves.append(jnp.dot(p.astype(BF16), v_t, preferred_element_type=F32) / denom)
                    if kv_half == 0:
                        o = jnp.where(low, halves[0], pltpu.roll(halves[1], HEAD_DIM, 1))
                    else:
                        o = jnp.where(low, pltpu.roll(halves[0], HEAD_DIM, 1), halves[1])
                    o_ref[rows, tile] = o.astype(BF16)


def _attn(proj_rest, sinks, S):
    N = proj_rest.shape[0]
    bq = _tiles()["attn_q"]
    nsub = _tiles()["attn_blocks"]
    tq = nsub * bq
    per_b = S // tq
    k_col = (Q_DIM + 2 * D_MODEL) // KV_DIM
    v_col = k_col + 1

    def prev(i):
        return jnp.where(i % per_b == 0, i * nsub, i * nsub - 1)

    return pl.pallas_call(
        functools.partial(_attn_kernel, per_b=per_b),
        grid=(N // tq,),
        in_specs=[
            pl.BlockSpec(memory_space=pltpu.SMEM),
            pl.BlockSpec((tq, Q_DIM), lambda i: (i, 0)),
            pl.BlockSpec((bq, KV_DIM), lambda i: (prev(i), k_col)),
            pl.BlockSpec((tq, KV_DIM), lambda i: (i, k_col)),
            pl.BlockSpec((bq, KV_DIM), lambda i: (prev(i), v_col)),
            pl.BlockSpec((tq, KV_DIM), lambda i: (i, v_col)),
        ],
        out_specs=pl.BlockSpec((tq, Q_DIM), lambda i: (i, 0)),
        out_shape=jax.ShapeDtypeStruct((N, Q_DIM), BF16),
        scratch_shapes=[pltpu.VMEM((2, N_HEADS, bq, 2 * bq), F32)],
        compiler_params=pltpu.CompilerParams(
            dimension_semantics=("arbitrary",),
            vmem_limit_bytes=_vmem_limit(3 * 2 * N_HEADS * bq * 2 * bq * 4)),
        name="attn",
    )(sinks, proj_rest, proj_rest, proj_rest, proj_rest, proj_rest)


def _merge_kernel(x_ref, yr_ref, ya_ref, gr_ref, ga_ref, wr_ref, wa_ref, wo_ref,
                  gpost_ref, g1_ref, gpre_ref, sc2_ref, sh2_ref, rwt_ref, rb_ref,
                  x1_ref, h2_ref, lg_ref):
    r = jnp.dot(yr_ref[...], wr_ref[...], preferred_element_type=F32)
    a = jnp.dot(ya_ref[...], wa_ref[...], preferred_element_type=F32)
    merged = ((0.5 * jnp.tanh(gr_ref[...].astype(F32)) + 0.5) * r
              + (0.5 * jnp.tanh(ga_ref[...].astype(F32)) + 0.5) * a)
    mix = jnp.dot(merged.astype(BF16), wo_ref[...], preferred_element_type=F32)
    x1 = x_ref[...] + g1_ref[0] * _rms(mix, gpost_ref[...])
    x1_ref[...] = x1
    h2 = _rms(x1, gpre_ref[...]) * (1.0 + sc2_ref[0]) + sh2_ref[0]
    h2_ref[...] = h2
    lg = lax.dot_general(rwt_ref[...], h2.astype(BF16), (((1,), (1,)), ((), ())),
                         preferred_element_type=F32)
    lg_ref[...] = lg + rb_ref[...]


def _merge(x2, y_rnn, y_att, proj_rest, wr, wa, wo, gpost, g1, gpre, sc2, sh2, rwt, rb, S):
    N, D = x2.shape
    tm = _tiles()["tok"]
    per_b = S // tm
    gate_r_col = Q_DIM // D
    mat = lambda: pl.BlockSpec((D, D), lambda i: (0, 0))
    vec = lambda: pl.BlockSpec((1, D), lambda i: (0, 0))
    bvec = lambda: pl.BlockSpec((1, 1, D), lambda i: (i // per_b, 0, 0))
    tile = lambda col=0: pl.BlockSpec((tm, D), lambda i: (i, col))
    return pl.pallas_call(
        _merge_kernel,
        grid=(N // tm,),
        in_specs=[
            tile(), tile(), tile(), tile(gate_r_col), tile(gate_r_col + 1),
            mat(), mat(), mat(),
            vec(), bvec(), vec(), bvec(), bvec(),
            pl.BlockSpec((N_EXPERTS, D), lambda i: (0, 0)),
            pl.BlockSpec((N_EXPERTS, 1), lambda i: (0, 0)),
        ],
        out_specs=[
            tile(), tile(),
            pl.BlockSpec((N_EXPERTS, tm), lambda i: (0, i)),
        ],
        out_shape=[
            jax.ShapeDtypeStruct((N, D), F32),
            jax.ShapeDtypeStruct((N, D), F32),
            jax.ShapeDtypeStruct((N_EXPERTS, N), F32),
        ],
        compiler_params=pltpu.CompilerParams(
            dimension_semantics=("arbitrary",),
            vmem_limit_bytes=_vmem_limit(6 * D * D * 2 + 24 * tm * D * 4)),
        name="merge",
    )(x2, y_rnn, y_att, proj_rest, proj_rest, wr, wa, wo, gpost, g1, gpre, sc2, sh2, rwt, rb)


def _route_kernel(lg_ref, g_ref, slot_ref, cnt_ref, tcnt_ref, tcar_ref, tlst_ref, carry):
    i = pl.program_id(0)

    @pl.when(i == 0)
    def _():
        carry[...] = jnp.zeros_like(carry)
        tcnt_ref[...] = jnp.zeros_like(tcnt_ref)
        tcar_ref[...] = jnp.zeros_like(tcar_ref)
        tlst_ref[...] = jnp.zeros_like(tlst_ref)

    l = lg_ref[...]
    E, T = l.shape
    row = lax.broadcasted_iota(jnp.int32, (E, T), 0).astype(F32)
    vals, idxs = [], []
    for _ in range(TOP_K):
        m = jnp.max(l, axis=0, keepdims=True)
        idx = jnp.min(jnp.where(l == m, row, float(E)), axis=0, keepdims=True)
        vals.append(m)
        idxs.append(idx)
        l = jnp.where(row == idx, -jnp.inf, l)
    ex = [jnp.exp(v - vals[0]) for v in vals]
    tot = ex[0]
    for e in ex[1:]:
        tot = tot + e
    g_ref[...] = jnp.concatenate([e / tot for e in ex], axis=0)

    hot = [row == idx for idx in idxs]
    onehot = jnp.zeros((E, T), F32)
    for hk in hot:
        onehot = onehot + hk.astype(F32)
    tri_t = (lax.broadcasted_iota(jnp.int32, (T, T), 0)
             < lax.broadcasted_iota(jnp.int32, (T, T), 1)).astype(BF16)
    before = jnp.dot(onehot.astype(BF16), tri_t, preferred_element_type=F32)
    cnt = jnp.sum(onehot, axis=1, keepdims=True)
    tri_e = (lax.broadcasted_iota(jnp.int32, (E, E), 1)
             < lax.broadcasted_iota(jnp.int32, (E, E), 0)).astype(BF16)
    lstart = jnp.dot(tri_e, jnp.broadcast_to(cnt, (E, V7X_LANES)).astype(BF16),
                     preferred_element_type=F32)[:, 0:1]
    local = before + lstart
    slots = [jnp.sum(jnp.where(hk, local, 0.0), axis=0, keepdims=True) for hk in hot]
    slot_ref[...] = jnp.concatenate(slots, axis=0).astype(jnp.int32)

    mine = lax.broadcasted_iota(jnp.int32, tcnt_ref.shape, 1) == i
    tcnt_ref[...] = jnp.where(mine, cnt.astype(jnp.int32), tcnt_ref[...])
    tcar_ref[...] = jnp.where(mine, carry[...].astype(jnp.int32), tcar_ref[...])
    tlst_ref[...] = jnp.where(mine, lstart.astype(jnp.int32), tlst_ref[...])
    total = carry[...] + cnt
    carry[...] = total
    cnt_ref[...] = total.astype(jnp.int32)


def _route(logits_t):
    E, N = logits_t.shape
    T = _tiles()["moe_t"]
    assert T <= 2 ** 8 and N // T <= V7X_LANES
    out = lambda: pl.BlockSpec((TOP_K, T), lambda i: (0, i))
    per_tile = lambda: pl.BlockSpec((E, V7X_LANES), lambda i: (0, 0))
    return pl.pallas_call(
        _route_kernel,
        grid=(N // T,),
        in_specs=[pl.BlockSpec((E, T), lambda i: (0, i))],
        out_specs=[out(), out(), pl.BlockSpec((E, 1), lambda i: (0, 0)),
                   per_tile(), per_tile(), per_tile()],
        out_shape=[
            jax.ShapeDtypeStruct((TOP_K, N), F32),
            jax.ShapeDtypeStruct((TOP_K, N), jnp.int32),
            jax.ShapeDtypeStruct((E, 1), jnp.int32),
            jax.ShapeDtypeStruct((E, V7X_LANES), jnp.int32),
            jax.ShapeDtypeStruct((E, V7X_LANES), jnp.int32),
            jax.ShapeDtypeStruct((E, V7X_LANES), jnp.int32),
        ],
        scratch_shapes=[pltpu.VMEM((E, 1), F32)],
        compiler_params=pltpu.CompilerParams(dimension_semantics=("arbitrary",)),
        name="route",
    )(logits_t)


def _run_sizes(T):
    return [2 ** b for b in range(int(math.log2(T)), -1, -1)]


def _copy_plan(tile_cnt, tile_lst, tile_dst, T):
    sizes = jnp.asarray(_run_sizes(T), jnp.int32)[None, :, None]
    cnt, lst, dst = (a[:, None, :] for a in (tile_cnt, tile_lst, tile_dst))
    has = (cnt & sizes) != 0
    off = (cnt // (2 * sizes)) * (2 * sizes)
    eids = jnp.arange(N_EXPERTS)
    earlier = eids[:, None] < eids[None, :]
    place = jnp.sum(jnp.where(earlier, has[..., :, None], False), axis=-2)
    hit = has[..., :, None] & (place[..., :, None] == eids)
    pick = lambda v: jnp.sum(jnp.where(hit, v[..., :, None], 0), axis=-2)
    n_tiles, nb = tile_cnt.shape[0], sizes.shape[1]
    parts = [pick(jnp.broadcast_to(lst + off, has.shape)).reshape(n_tiles, -1),
             pick(jnp.broadcast_to(dst + off, has.shape)).reshape(n_tiles, -1),
             jnp.sum(has, axis=-1).reshape(n_tiles, nb)]
    plan = jnp.concatenate(parts, axis=1).astype(jnp.int32)
    pad = _plan_len(T) - plan.shape[1]
    return jnp.pad(plan, ((0, 0), (0, pad))).reshape(-1)


def _plan_len(T):
    nb = len(_run_sizes(T))
    return max(V7X_LANES, pl.next_power_of_2(2 * nb * N_EXPERTS + nb))


def _run_copies(plan_ref, T, make):
    sizes = _run_sizes(T)
    nb = len(sizes)
    for bi, b in enumerate(sizes):
        def body(j, c, bi=bi, b=b):
            make(plan_ref[bi * N_EXPERTS + j], plan_ref[(nb + bi) * N_EXPERTS + j], b).start()
            return c

        lax.fori_loop(0, plan_ref[2 * nb * N_EXPERTS + bi], body, 0)


def _row_slab(ref, row, nrows, lead=()):
    rows = pl.ds(pl.multiple_of(row * V7X_SUBLANES, V7X_SUBLANES), nrows * V7X_SUBLANES)
    return ref.at[(*lead, rows)]


def _to_slabs(ref, val, lead=()):
    rows = val.shape[0]
    for s in range(V7X_SUBLANES):
        ref[(*lead, pl.ds(s, rows, stride=V7X_SUBLANES), slice(None))] = (
            val[:, s * V7X_LANES:(s + 1) * V7X_LANES])


def _from_slabs(ref, lead=(), rows=None):
    rows = ref.shape[-2] // V7X_SUBLANES if rows is None else rows
    return jnp.concatenate(
        [ref[(*lead, pl.ds(s, rows, stride=V7X_SUBLANES), slice(None))]
         for s in range(V7X_SUBLANES)], axis=1)


def _dispatch_kernel(tail_ref, nu_ref, plan_ref, slot_ref, h_ref, xs_ref, buf, zbuf, sem, zsem):
    i = pl.program_id(0)
    T = h_ref.shape[0]
    bm = zbuf.shape[0] // V7X_SUBLANES
    nblk = xs_ref.shape[0] // zbuf.shape[0]
    cur = i % 2

    def zero_copy(row0):
        return pltpu.make_async_copy(zbuf, _row_slab(xs_ref, row0, bm), zsem)

    @pl.when(i == 0)
    def _():
        zbuf[...] = jnp.zeros_like(zbuf)

        def tail_chunks(e, fn):
            @pl.when(tail_ref[e] >= 0)
            def _():
                fn(zero_copy(tail_ref[e]))

            @pl.when(jnp.logical_and(tail_ref[e] >= 0, tail_ref[e] % (2 * bm) == 0))
            def _():
                fn(zero_copy(tail_ref[e] + bm))

        def start(e, c):
            tail_chunks(e, lambda cp: cp.start())
            return c

        def wait(e, c):
            tail_chunks(e, lambda cp: cp.wait())
            return c

        def start_unused(j, c):
            zero_copy(j * bm).start()
            return c

        def wait_unused(j, c):
            zero_copy(j * bm).wait()
            return c

        lax.fori_loop(0, N_EXPERTS, start, 0)
        lax.fori_loop(nu_ref[0], nblk, start_unused, 0)
        lax.fori_loop(0, N_EXPERTS, wait, 0)
        lax.fori_loop(nu_ref[0], nblk, wait_unused, 0)

    slot = slot_ref[...]
    r_id = lax.broadcasted_iota(jnp.int32, (TOP_K * T, T), 0)
    perm = r_id == slot[0:1, :]
    for k in range(1, TOP_K):
        perm = jnp.logical_or(perm, r_id == slot[k:k + 1, :])
    grouped = jnp.dot(perm.astype(BF16), h_ref[...].astype(BF16), preferred_element_type=F32)
    _to_slabs(buf, grouped, lead=(cur,))

    def wait_tile(b):
        pltpu.make_async_copy(buf.at[b], _row_slab(xs_ref, 0, TOP_K * T), sem.at[b]).wait()

    @pl.when(i > 0)
    def _():
        wait_tile(1 - cur)

    _run_copies(plan_ref, T, lambda l, d, n: pltpu.make_async_copy(
        _row_slab(buf, l, n, lead=(cur,)), _row_slab(xs_ref, d, n), sem.at[cur]))

    @pl.when(i == pl.num_programs(0) - 1)
    def _():
        wait_tile(cur)


def _dispatch(h2, slot, plan, tail_start, n_used, n_rows, bm):
    N, D = h2.shape
    T = _tiles()["moe_t"]
    S8 = V7X_SUBLANES
    imap = lambda f: (lambda i, *_: f(i))
    return pl.pallas_call(
        _dispatch_kernel,
        grid_spec=pltpu.PrefetchScalarGridSpec(
            num_scalar_prefetch=2,
            grid=(N // T,),
            in_specs=[
                pl.BlockSpec((_plan_len(T),), imap(lambda i: (i,)), memory_space=pltpu.SMEM),
                pl.BlockSpec((TOP_K, T), imap(lambda i: (0, i))),
                pl.BlockSpec((T, D), imap(lambda i: (i, 0))),
            ],
            out_specs=pl.BlockSpec(memory_space=pl.ANY),
            scratch_shapes=[
                pltpu.VMEM((2, TOP_K * T * S8, V7X_LANES), F32),
                pltpu.VMEM((bm * S8, V7X_LANES), F32),
                pltpu.SemaphoreType.DMA((2,)),
                pltpu.SemaphoreType.DMA(()),
            ],
        ),
        out_shape=jax.ShapeDtypeStruct((n_rows * S8, V7X_LANES), F32),
        compiler_params=pltpu.CompilerParams(
            dimension_semantics=("arbitrary",),
            vmem_limit_bytes=_vmem_limit(8 * TOP_K * T * D * 4)),
        name="dispatch",
    )(tail_start, n_used, plan, slot, h2)


def _ffn_kernel(be_ref, bv_ref, bf_ref, nu_ref, nxt_ref, x_ref, w1_hbm, b1g_ref, b1l_ref, w2_hbm,
                b2_ref, y_ref, w1f_s, w2f_s, w1g_s, w1l_s, w2_s, sem):
    i = pl.program_id(0)
    bm, D = x_ref.shape[0] // V7X_SUBLANES, w2_s.shape[1]
    pw = 2 * V7X_LANES

    def fetch(e):
        return (pltpu.make_async_copy(w1_hbm.at[e], w1f_s, sem.at[0]),
                pltpu.make_async_copy(w2_hbm.at[e], w2f_s, sem.at[1]))

    @pl.when(i == 0)
    def _():
        for cp in fetch(be_ref[0]):
            cp.start()

    @pl.when(jnp.logical_and(i < nu_ref[0], bf_ref[i] == 1))
    def _():
        e = be_ref[i]
        for cp in fetch(e):
            cp.wait()
        src = lax.broadcasted_iota(jnp.int32, (pw, pw), 0)
        dst = lax.broadcasted_iota(jnp.int32, (pw, pw), 1)
        want = jnp.where(dst < V7X_LANES, 2 * dst, 2 * (dst - V7X_LANES) + 1)
        perm = (src == want).astype(BF16)
        for c in range(w1f_s.shape[1] // pw):
            wp = jnp.dot(w1f_s[:, c * pw:(c + 1) * pw].astype(BF16), perm,
                         preferred_element_type=F32).astype(BF16)
            w1g_s[:, c * V7X_LANES:(c + 1) * V7X_LANES] = wp[:, :V7X_LANES]
            w1l_s[:, c * V7X_LANES:(c + 1) * V7X_LANES] = wp[:, V7X_LANES:]
        w2_s[...] = w2f_s[...].astype(BF16)

        @pl.when(nxt_ref[e] >= 0)
        def _():
            for cp in fetch(nxt_ref[e]):
                cp.start()

    def mlp(n_rows):
        rows = lax.broadcasted_iota(jnp.int32, (n_rows, D), 0)
        x = jnp.where(rows < bv_ref[i], _from_slabs(x_ref, rows=n_rows), 0.0).astype(BF16)
        glu = jnp.dot(x, w1g_s[...], preferred_element_type=F32) + b1g_ref[0]
        lin = jnp.dot(x, w1l_s[...], preferred_element_type=F32) + b1l_ref[0]
        glu = jnp.minimum(glu, SWIGLU_LIMIT)
        lin = jnp.clip(lin, -SWIGLU_LIMIT, SWIGLU_LIMIT)
        act = glu * _sigmoid(SWIGLU_ALPHA * glu) * (lin + 1.0)
        _to_slabs(y_ref, jnp.dot(act.astype(BF16), w2_s[...], preferred_element_type=F32) + b2_ref[0])
        if n_rows < bm:
            y_ref[n_rows * V7X_SUBLANES:, :] = jnp.zeros(
                ((bm - n_rows) * V7X_SUBLANES, V7X_LANES), F32)

    used = i < nu_ref[0]
    half = bm // 2

    @pl.when(jnp.logical_and(used, bv_ref[i] > half))
    def _():
        mlp(bm)

    @pl.when(jnp.logical_and(used, bv_ref[i] <= half))
    def _():
        mlp(half)

    @pl.when(i >= nu_ref[0])
    def _():
        y_ref[...] = jnp.zeros_like(y_ref)


def _ffn(xs, blk_e, blk_valid, blk_first, n_used, next_e, w1, b1g, b1l, w2, b2, bm):
    E, D, F2 = w1.shape
    F = F2 // 2
    slab = bm * V7X_SUBLANES
    nblk = xs.shape[0] // slab

    def row_blk(i, be, bv, bf, nu, nxt):
        return (jnp.minimum(i, nu[0] - 1), 0)

    def per_e(i, be, bv, bf, nu, nxt):
        return (be[i], 0, 0)

    vmem = (D * F2 + F * D) * (4 + 2) + 8 * bm * F2 * 4
    return pl.pallas_call(
        _ffn_kernel,
        grid_spec=pltpu.PrefetchScalarGridSpec(
            num_scalar_prefetch=5,
            grid=(nblk,),
            in_specs=[
                pl.BlockSpec((slab, V7X_LANES), row_blk),
                pl.BlockSpec(memory_space=pl.ANY),
                pl.BlockSpec((1, 1, F), per_e),
                pl.BlockSpec((1, 1, F), per_e),
                pl.BlockSpec(memory_space=pl.ANY),
                pl.BlockSpec((1, 1, D), per_e),
            ],
            out_specs=pl.BlockSpec((slab, V7X_LANES), lambda i, be, bv, bf, nu, nxt: (i, 0)),
            scratch_shapes=[
                pltpu.VMEM((D, F2), F32),
                pltpu.VMEM((F, D), F32),
                pltpu.VMEM((D, F), BF16),
                pltpu.VMEM((D, F), BF16),
                pltpu.VMEM((F, D), BF16),
                pltpu.SemaphoreType.DMA((2,)),
            ],
        ),
        out_shape=jax.ShapeDtypeStruct(xs.shape, F32),
        compiler_params=pltpu.CompilerParams(
            dimension_semantics=("arbitrary",),
            vmem_limit_bytes=_vmem_limit(vmem)),
        name="ffn",
    )(blk_e, blk_valid, blk_first, n_used, next_e, xs, w1, b1g, b1l, w2, b2)


def _combine_kernel(plan_ref, plan_next_ref, ys_ref, x1_ref, slot_ref, gate_ref, gpost_ref, g2_ref,
                    o_ref, buf, sem):
    i = pl.program_id(0)
    T = x1_ref.shape[0]
    cur = i % 2

    def gather(plan, b):
        _run_copies(plan, T, lambda l, d, n: pltpu.make_async_copy(
            _row_slab(ys_ref, d, n), _row_slab(buf, l, n, lead=(b,)), sem.at[b]))

    @pl.when(i == 0)
    def _():
        gather(plan_ref, cur)

    @pl.when(i + 1 < pl.num_programs(0))
    def _():
        gather(plan_next_ref, 1 - cur)

    pltpu.make_async_copy(_row_slab(ys_ref, 0, TOP_K * T), buf.at[cur], sem.at[cur]).wait()

    slot = slot_ref[...]
    gates = gate_ref[...]
    r_id = lax.broadcasted_iota(jnp.int32, (T, TOP_K * T), 1)
    w = jnp.zeros((T, TOP_K * T), F32)
    for k in range(TOP_K):
        w = jnp.where(r_id == slot[:, k:k + 1], gates[:, k:k + 1], w)
    ff = jnp.dot(w.astype(BF16), _from_slabs(buf, lead=(cur,)).astype(BF16),
                 preferred_element_type=F32)
    o_ref[...] = x1_ref[...] + g2_ref[0] * _rms(ff, gpost_ref[...])


def _combine(ys, plan, x1, slot_tk, gates_tk, gpost, g2, S):
    N, D = x1.shape
    T = _tiles()["moe_t"]
    per_b = S // T
    n_tiles = N // T
    plan_spec = lambda f: pl.BlockSpec((_plan_len(T),), f, memory_space=pltpu.SMEM)
    return pl.pallas_call(
        _combine_kernel,
        grid=(n_tiles,),
        in_specs=[
            plan_spec(lambda i: (i,)),
            plan_spec(lambda i: (jnp.minimum(i + 1, n_tiles - 1),)),
            pl.BlockSpec(memory_space=pl.ANY),
            pl.BlockSpec((T, D), lambda i: (i, 0)),
            pl.BlockSpec((T, TOP_K), lambda i: (i, 0)),
            pl.BlockSpec((T, TOP_K), lambda i: (i, 0)),
            pl.BlockSpec((1, D), lambda i: (0, 0)),
            pl.BlockSpec((1, 1, D), lambda i: (i // per_b, 0, 0)),
        ],
        out_specs=pl.BlockSpec((T, D), lambda i: (i, 0)),
        out_shape=jax.ShapeDtypeStruct((N, D), F32),
        scratch_shapes=[
            pltpu.VMEM((2, TOP_K * T * V7X_SUBLANES, V7X_LANES), F32),
            pltpu.SemaphoreType.DMA((2,)),
        ],
        compiler_params=pltpu.CompilerParams(
            dimension_semantics=("arbitrary",),
            vmem_limit_bytes=_vmem_limit(10 * TOP_K * T * D * 4)),
        name="combine",
    )(plan, plan, ys, x1, slot_tk, gates_tk, gpost, g2)


def _col_scaled(w):
    scale = HEAD_DIM ** -0.5
    assert math.log2(scale).is_integer()
    col = np.arange(D_IN)
    q0, g0 = 2 * D_RNN, 2 * D_RNN + Q_DIM + 2 * KV_DIM
    factor = np.where((col >= q0) & (col < q0 + Q_DIM), scale, np.where(col >= g0, 0.5, 1.0))
    return w * jnp.asarray(factor, w.dtype)


def _layer(x2, c, B, S, p):
    D = D_MODEL
    N = B * S
    ada = _ada(c, p["w_ada"], p["b_ada"])
    sh1, sc1, g1, sh2, sc2, g2 = [a.reshape(B, 1, D) for a in jnp.split(ada, 6, axis=-1)]
    row = lambda v: v.reshape(1, -1)

    gw = _tiles()["rnn_group"]
    proj_rest, y_rnn = _mixin(
        x2, row(p["norm_pre_mix"]), sc1, sh1,
        _col_scaled(p["w_in"]).astype(BF16), _col_scaled(row(p["b_in"])),
        p["conv_w"], row(p["conv_b"]),
        (0.5 * _block_diag_tiles(p["rg_w_a"], gw)).astype(BF16), 0.5 * row(p["rg_b_a"]),
        (0.5 * _block_diag_tiles(p["rg_w_x"], gw)).astype(BF16), 0.5 * row(p["rg_b_x"]),
        row(p["rg_lambda"]), S)
    y_att = _attn(proj_rest, p["attn_sinks"], S)

    x1, h2, logits_t = _merge(
        x2, y_rnn, y_att, proj_rest,
        p["w_o_rnn"].astype(BF16), p["w_o_attn"].astype(BF16), p["w_out"].astype(BF16),
        row(p["norm_post_mix"]), g1, row(p["norm_pre_ffn"]), sc2, sh2,
        p["router_w"].T.astype(BF16), p["router_b"].reshape(N_EXPERTS, 1), S)

    gates, slot, counts, tile_cnt, tile_car, tile_lst = _route(logits_t)

    bm = _tiles()["moe_bm"]
    n_tiles = N // _tiles()["moe_t"]
    n_rows = N * TOP_K + N_EXPERTS * bm
    nblk = n_rows // bm
    counts = counts.reshape(N_EXPERTS)
    padded = ((counts + bm - 1) // bm) * bm
    pend = jnp.cumsum(padded)
    pstart = pend - padded
    eids = jnp.arange(N_EXPERTS, dtype=jnp.int32)
    per_run = lambda a: a[:, :n_tiles].T.astype(jnp.int32)
    plan = _copy_plan(per_run(tile_cnt), per_run(tile_lst), per_run(tile_car + pstart[:, None]),
                      _tiles()["moe_t"])
    zr = bm // 2
    tail_start = jnp.where(counts > 0, (pstart + counts - 1) // zr * zr, -1).astype(jnp.int32)
    n_used = (pend[-1] // bm).astype(jnp.int32).reshape(1)
    blk_row0 = jnp.minimum(jnp.arange(nblk, dtype=jnp.int32), n_used[0] - 1) * bm
    blk_e = jnp.sum(blk_row0[:, None] >= pend[None, :], axis=1).astype(jnp.int32)
    mine = blk_e[:, None] == eids[None, :]
    blk_cnt = jnp.sum(jnp.where(mine, counts, 0), axis=1)
    blk_pstart = jnp.sum(jnp.where(mine, pstart, 0), axis=1)
    blk_valid = jnp.clip(blk_cnt - (blk_row0 - blk_pstart), 0, bm).astype(jnp.int32)
    blk_first = (blk_row0 == blk_pstart).astype(jnp.int32)

    xs = _dispatch(h2, slot, plan, tail_start, n_used * (bm // zr), n_rows, zr)
    b1 = p["moe_b1"].reshape(N_EXPERTS, D_FF, 2)
    later = jnp.logical_and(counts[None, :] > 0, eids[None, :] > eids[:, None])
    next_e = jnp.min(jnp.where(later, eids[None, :], N_EXPERTS), axis=1)
    next_e = jnp.where(next_e == N_EXPERTS, -1, next_e).astype(jnp.int32)
    ys = _ffn(xs, blk_e, blk_valid, blk_first, n_used, next_e, p["moe_w1"],
              b1[:, :, 0].reshape(N_EXPERTS, 1, D_FF), b1[:, :, 1].reshape(N_EXPERTS, 1, D_FF),
              p["moe_w2"], p["moe_b2"].reshape(N_EXPERTS, 1, D), bm)
    return _combine(ys, plan, x1, slot.T, gates.T, row(p["norm_post_ffn"]), g2, S)


def kernel(x, c, w_ada, b_ada, norm_pre_mix, norm_post_mix, norm_pre_ffn, norm_post_ffn, w_in, b_in, conv_w, conv_b, rg_w_a, rg_b_a, rg_w_x, rg_b_x, rg_lambda, attn_sinks, w_o_rnn, w_o_attn, w_out, router_w, router_b, moe_w1, moe_b1, moe_w2, moe_b2):
    B, S, D = x.shape
    params = dict(
        w_ada=w_ada, b_ada=b_ada, norm_pre_mix=norm_pre_mix, norm_post_mix=norm_post_mix,
        norm_pre_ffn=norm_pre_ffn, norm_post_ffn=norm_post_ffn, w_in=w_in, b_in=b_in,
        conv_w=conv_w, conv_b=conv_b, rg_w_a=rg_w_a, rg_b_a=rg_b_a, rg_w_x=rg_w_x, rg_b_x=rg_b_x,
        rg_lambda=rg_lambda, attn_sinks=attn_sinks, w_o_rnn=w_o_rnn, w_o_attn=w_o_attn,
        w_out=w_out, router_w=router_w, router_b=router_b, moe_w1=moe_w1, moe_b1=moe_b1,
        moe_w2=moe_w2, moe_b2=moe_b2)
    x2 = x.reshape(B * S, D)
    for layer in range(w_ada.shape[0]):
        x2 = _layer(x2, c, B, S, {k: v[layer] for k, v in params.items()})
    return x2.reshape(B, S, D)
```

```python
import functools
import math

import jax
import jax.numpy as jnp
import numpy as np
from jax import lax
from jax.experimental import pallas as pl
from jax.experimental.pallas import tpu as pltpu

D_MODEL = 1024
D_RNN = 1024
RNN_BLOCKS = 16
RNN_BW = D_RNN // RNN_BLOCKS
CONV_W = 4
LRU_C = 8.0
N_HEADS = 16
N_KV = 4
HEAD_DIM = 64
GROUP = N_HEADS // N_KV
WINDOW = 128
Q_DIM = N_HEADS * HEAD_DIM
KV_DIM = N_KV * HEAD_DIM
N_EXPERTS = 32
TOP_K = 4
D_FF = 1024
SWIGLU_LIMIT = 7.0
SWIGLU_ALPHA = 1.702
EPS = 1e-6
D_IN = 2 * D_RNN + Q_DIM + 2 * KV_DIM + 2 * D_MODEL
D_REST = D_IN - 2 * D_RNN

V7X_LANES = 128
V7X_SUBLANES = 8
V7X_MXU_DIM = 256
V7X_VMEM_BYTES = 64 * 1024 * 1024

F32 = jnp.float32
BF16 = jnp.bfloat16


def _tiles():
    return dict(
        ada_tn=4 * V7X_MXU_DIM,
        tok=2 * V7X_MXU_DIM,
        attn_q=WINDOW,
        attn_blocks=2,
        moe_t=V7X_MXU_DIM,
        route_tiles=4,
        moe_bm=2 * V7X_MXU_DIM,
        rnn_group=V7X_MXU_DIM,
    )


V7X_VMEM_MIN_LIMIT = 16 * 1024 * 1024
V7X_VMEM_HEADROOM = 8 * 1024 * 1024


def _vmem_limit(nbytes):
    return int(min(max(nbytes, V7X_VMEM_MIN_LIMIT), V7X_VMEM_BYTES - V7X_VMEM_HEADROOM))


def _rms(x, g):
    return x * lax.rsqrt(jnp.mean(x * x, axis=-1, keepdims=True) + EPS) * g


def _ada_kernel(ct_ref, w_ref, b_ref, o_ref):
    ct = ct_ref[...]
    sc = ct * jax.nn.sigmoid(ct)
    w = w_ref[...]
    rows = [jnp.sum(w * sc[:, b:b + 1], axis=0, keepdims=True) for b in range(ct.shape[1])]
    o_ref[...] = jnp.concatenate(rows, axis=0) + b_ref[...]


def _ada(c, w_ada, b_ada):
    B, D = c.shape
    n_out = w_ada.shape[1]
    tn = _tiles()["ada_tn"]
    return pl.pallas_call(
        _ada_kernel,
        grid=(n_out // tn,),
        in_specs=[
            pl.BlockSpec((D, B), lambda j: (0, 0)),
            pl.BlockSpec((D, tn), lambda j: (0, j)),
            pl.BlockSpec((1, tn), lambda j: (0, j)),
        ],
        out_specs=pl.BlockSpec((B, tn), lambda j: (0, j)),
        out_shape=jax.ShapeDtypeStruct((B, n_out), F32),
        compiler_params=pltpu.CompilerParams(
            dimension_semantics=("arbitrary",),
            vmem_limit_bytes=_vmem_limit(4 * D * tn * 4)),
        name="ada",
    )(c.T, w_ada, b_ada.reshape(1, n_out))


def _gelu_tanh(x):
    return 0.5 * x * (1.0 + jnp.tanh(math.sqrt(2.0 / math.pi) * (x + 0.044715 * (x * x * x))))


def _softplus(z):
    return jnp.maximum(z, 0.0) + jnp.log1p(jnp.exp(-jnp.abs(z)))


def _sigmoid(x):
    return 0.5 * jnp.tanh(0.5 * x) + 0.5


def _rest_col(c):
    c -= 2 * D_RNN
    q_end, k_end, v_end = Q_DIM, Q_DIM + KV_DIM, Q_DIM + 2 * KV_DIM
    if c < q_end:
        return c
    if c < k_end:
        return Q_DIM + 2 * D_MODEL + (c - q_end)
    if c < v_end:
        return Q_DIM + 2 * D_MODEL + KV_DIM + (c - k_end)
    return Q_DIM + (c - v_end)


def _mixin_kernel(x_ref, g_ref, sc_ref, sh_ref, w_ref, b_ref, cw_ref, cb_ref, wa_ref, ba_ref,
                  wx_ref, bx_ref, lam_ref, rest_ref, y_ref,
                  hb_s, xbuf, gr_s, a_s, b_s, h_s, ga_s, gb_s, gc_s, carry, *, per_b, gw, chunk):
    t = pl.program_id(0) % per_b
    tt = x_ref.shape[0]
    halo = V7X_SUBLANES
    ng = tt // V7X_SUBLANES
    n_lt = a_s.shape[0]
    per_g = gw // V7X_LANES
    lanes = lambda j: slice(j * V7X_LANES, (j + 1) * V7X_LANES)
    slab = lambda j, r: (j, pl.ds(r, ng, stride=V7X_SUBLANES), slice(None))

    @pl.when(t == 0)
    def _():
        xbuf[:, 0:halo, :] = jnp.zeros((n_lt, halo, V7X_LANES), F32)
        carry[...] = jnp.zeros_like(carry)

    hb_s[...] = (_rms(x_ref[...], g_ref[...]) * (1.0 + sc_ref[0]) + sh_ref[0]).astype(BF16)

    def proj(c0):
        return (jnp.dot(hb_s[...], w_ref[:, c0:c0 + chunk], preferred_element_type=F32)
                + b_ref[:, c0:c0 + chunk])

    def proj_rest(c0):
        acc = proj(c0).astype(BF16)
        for p0 in range(0, chunk, KV_DIM):
            d0 = _rest_col(c0 + p0)
            rest_ref[:, d0:d0 + KV_DIM] = acc[:, p0:p0 + KV_DIM]

    for c0 in range(0, D_RNN, chunk):
        acc = proj(c0)
        for j in range(chunk // V7X_LANES):
            xbuf[c0 // V7X_LANES + j, halo:halo + tt, :] = acc[:, lanes(j)]
    for c0 in range(D_RNN, 2 * D_RNN, chunk):
        gr_s[:, c0 - D_RNN:c0 - D_RNN + chunk] = proj(c0)

    def conv(j):
        ls = lanes(j)
        taps = {o: xbuf[slab(j, halo + o)] for o in range(-(CONV_W - 1), V7X_SUBLANES)}
        for r in range(V7X_SUBLANES):
            acc = cb_ref[:, ls] + taps[r - (CONV_W - 1)] * cw_ref[0:1, ls]
            for kk in range(1, CONV_W):
                acc = acc + taps[r - (CONV_W - 1) + kk] * cw_ref[kk:kk + 1, ls]
            h_s[slab(j, r)] = acc
        xbuf[j, 0:halo, :] = xbuf[j, tt:tt + halo, :]

    def gates(g):
        cs = slice(g * gw, (g + 1) * gw)
        rate = (-LRU_C * math.log2(math.e)) * _softplus(-lam_ref[:, cs])
        reset = jnp.logical_and(t == 0, lax.broadcasted_iota(jnp.int32, (tt, gw), 0) == 0)
        xc = jnp.concatenate([h_s[g * per_g + j] for j in range(per_g)], axis=1)
        xg = xc.astype(BF16)
        gate_r = 0.5 * jnp.tanh(jnp.dot(xg, wa_ref[g], preferred_element_type=F32) + ba_ref[:, cs]) + 0.5
        gate_i = 0.5 * jnp.tanh(jnp.dot(xg, wx_ref[g], preferred_element_type=F32) + bx_ref[:, cs]) + 0.5
        a = jnp.exp2(gate_r * rate)
        v = (1.0 - a) * (1.0 + a)
        mult = jnp.where(reset, 1.0, jnp.where(v > 0.0, v * lax.rsqrt(v), 0.0))
        bt = (xc * gate_i) * mult
        for j in range(per_g):
            a_s[g * per_g + j] = a[:, lanes(j)]
            b_s[g * per_g + j] = bt[:, lanes(j)]

    def scan_groups(j):
        acc_a = a_s[slab(j, 0)]
        acc_h = b_s[slab(j, 0)]
        h_s[slab(j, 0)] = acc_h
        for r in range(1, V7X_SUBLANES):
            a_r = a_s[slab(j, r)]
            acc_h = a_r * acc_h + b_s[slab(j, r)]
            acc_a = a_r * acc_a
            h_s[slab(j, r)] = acc_h
            a_s[slab(j, r)] = acc_a
        ga_s[:, lanes(j)] = acc_a
        gb_s[:, lanes(j)] = acc_h

    def across(gi, h_prev):
        gc_s[pl.ds(gi, 1), :] = h_prev
        return ga_s[pl.ds(gi, 1), :] * h_prev + gb_s[pl.ds(gi, 1), :]

    def finish(j):
        h_in = gc_s[:, lanes(j)]
        for r in range(V7X_SUBLANES):
            h_s[slab(j, r)] = a_s[slab(j, r)] * h_in + h_s[slab(j, r)]
        y_ref[:, lanes(j)] = (h_s[j] * _gelu_tanh(gr_s[:, lanes(j)])).astype(BF16)

    rest = [functools.partial(proj_rest, c0) for c0 in range(2 * D_RNN, w_ref.shape[1], chunk)]
    before = ([functools.partial(conv, j) for j in range(n_lt)]
              + [functools.partial(gates, g) for g in range(n_lt // per_g)]
              + [functools.partial(scan_groups, j) for j in range(n_lt)])
    after = [functools.partial(finish, j) for j in range(n_lt)]
    n_before = (len(rest) * len(before)) // (len(before) + len(after))

    def interleave(steps, chunks):
        every = -(-len(steps) // max(len(chunks), 1))
        for n, step in enumerate(steps):
            step()
            if (n + 1) % every == 0 and chunks:
                chunks.pop(0)()
        while chunks:
            chunks.pop(0)()

    interleave(before, rest[:n_before])
    carry[...] = lax.fori_loop(0, ng, across, carry[...])
    interleave(after, rest[n_before:])


def _block_diag_tiles(w, gw):
    nb, bw, _ = w.shape
    per = gw // bw
    w4 = w.reshape(nb // per, per, bw, bw)
    eye = jnp.eye(per, dtype=w.dtype)
    return jnp.einsum("gpij,pq->gpiqj", w4, eye).reshape(nb // per, gw, gw)


def _mixin(x2, g, sc, sh, w_bf, b_in, conv_w, conv_b, wa, ba, wx, bx, lam, S):
    N, D = x2.shape
    C = D_RNN
    tt = _tiles()["tok"]
    gw = _tiles()["rnn_group"]
    per_b = S // tt
    chunk = 2 * V7X_MXU_DIM
    vec = lambda: pl.BlockSpec((1, C), lambda i: (0, 0))
    bvec = lambda: pl.BlockSpec((1, 1, D), lambda i: (i // per_b, 0, 0))
    gate_w = lambda: pl.BlockSpec((C // gw, gw, gw), lambda i: (0, 0, 0))
    slabs = lambda rows: pltpu.VMEM((C // V7X_LANES, rows, V7X_LANES), F32)
    groups = lambda: pltpu.VMEM((tt // V7X_SUBLANES, C), F32)
    vmem = D * D_IN * 2 + 2 * tt * (D * 4 + D_REST * 2 + C * 2) + tt * C * (2 + 5 * 4) + 8 * tt * chunk * 4
    return pl.pallas_call(
        functools.partial(_mixin_kernel, per_b=per_b, gw=gw, chunk=chunk),
        grid=(N // tt,),
        in_specs=[
            pl.BlockSpec((tt, D), lambda i: (i, 0)),
            pl.BlockSpec((1, D), lambda i: (0, 0)),
            bvec(), bvec(),
            pl.BlockSpec((D, D_IN), lambda i: (0, 0), pipeline_mode=pl.Buffered(1)),
            pl.BlockSpec((1, D_IN), lambda i: (0, 0)),
            pl.BlockSpec((CONV_W, C), lambda i: (0, 0)),
            vec(), gate_w(), vec(), gate_w(), vec(), vec(),
        ],
        out_specs=[
            pl.BlockSpec((tt, D_REST), lambda i: (i, 0)),
            pl.BlockSpec((tt, C), lambda i: (i, 0)),
        ],
        out_shape=[
            jax.ShapeDtypeStruct((N, D_REST), BF16),
            jax.ShapeDtypeStruct((N, C), BF16),
        ],
        scratch_shapes=[
            pltpu.VMEM((tt, D), BF16),
            slabs(tt + V7X_SUBLANES),
            pltpu.VMEM((tt, C), F32),
            slabs(tt), slabs(tt), slabs(tt),
            groups(), groups(), groups(),
            pltpu.VMEM((1, C), F32),
        ],
        compiler_params=pltpu.CompilerParams(
            dimension_semantics=("arbitrary",),
            vmem_limit_bytes=_vmem_limit(vmem)),
        name="mixin",
    )(x2, g, sc, sh, w_bf, b_in, conv_w, conv_b, wa, ba, wx, bx, lam)


def _alibi_slopes():
    return [2.0 ** (-8.0 * (h + 1) / N_HEADS) for h in range(N_HEADS)]


def _attn_kernel(sink_ref, q_ref, kp_ref, kc_ref, vp_ref, vc_ref, o_ref, bias_s, *, per_b):
    first = pl.program_id(0) % per_b == 0
    bq = kp_ref.shape[0]
    slopes = _alibi_slopes()

    @pl.when(pl.program_id(0) == 0)
    def _():
        qi = lax.broadcasted_iota(jnp.int32, (bq, 2 * bq), 0)
        ci = lax.broadcasted_iota(jnp.int32, (bq, 2 * bq), 1)
        dist = qi + bq - ci
        valid = (dist >= 0) & (dist < WINDOW)
        distf = dist.astype(F32)
        for h in range(N_HEADS):
            b = jnp.where(valid, -slopes[h] * distf, -jnp.inf)
            bias_s[0, h] = jnp.where(ci >= bq, b, -jnp.inf)
            bias_s[1, h] = b

    assert 2 * HEAD_DIM == V7X_LANES and GROUP % 2 == 0
    low = lax.broadcasted_iota(jnp.int32, (bq, V7X_LANES), 1) < HEAD_DIM
    zero = jnp.zeros((bq, V7X_LANES), BF16)
    for sub in range(q_ref.shape[0] // bq):
        rows = slice(sub * bq, (sub + 1) * bq)
        before = slice((sub - 1) * bq, sub * bq)
        table = jnp.where(first, 0, 1) if sub == 0 else 1
        for kvt in range(N_KV // 2):
            lt = slice(kvt * V7X_LANES, (kvt + 1) * V7X_LANES)
            k_prev, v_prev = (kp_ref[:, lt], vp_ref[:, lt]) if sub == 0 else (kc_ref[before, lt], vc_ref[before, lt])
            k_t = jnp.concatenate([k_prev, kc_ref[rows, lt]], axis=0)
            v_t = jnp.concatenate([v_prev, vc_ref[rows, lt]], axis=0)
            k_swapped = pltpu.roll(k_t, HEAD_DIM, 1)
            for kv_half in range(2):
                kv = 2 * kvt + kv_half
                for pair in range(GROUP // 2):
                    h0 = kv * GROUP + 2 * pair
                    tile = slice((h0 // 2) * V7X_LANES, (h0 // 2 + 1) * V7X_LANES)
                    q_t = q_ref[rows, tile]
                    halves = []
                    for q_half in range(2):
                        h = h0 + q_half
                        qm = jnp.where(low if q_half == 0 else ~low, q_t, zero)
                        kk = k_t if q_half == kv_half else k_swapped
                        s = lax.dot_general(qm, kk, (((1,), (1,)), ((), ())),
                                            preferred_element_type=F32)
                        s = s + bias_s[table, h]
                        sink = sink_ref[h]
                        m = jnp.maximum(jnp.max(s, axis=-1, keepdims=True), sink)
                        p = jnp.exp(s - m)
                        denom = jnp.sum(p, axis=-1, keepdims=True) + jnp.exp(sink - m)
                        halves.append(jnp.dot(p.astype(BF16), v_t, preferred_element_type=F32) / denom)
                    if kv_half == 0:
                        o = jnp.where(low, halves[0], pltpu.roll(halves[1], HEAD_DIM, 1))
                    else:
                        o = jnp.where(low, pltpu.roll(halves[0], HEAD_DIM, 1), halves[1])
                    o_ref[rows, tile] = o.astype(BF16)


def _attn(proj_rest, sinks, S):
    N = proj_rest.shape[0]
    bq = _tiles()["attn_q"]
    nsub = _tiles()["attn_blocks"]
    tq = nsub * bq
    per_b = S // tq
    k_col = (Q_DIM + 2 * D_MODEL) // KV_DIM
    v_col = k_col + 1

    def prev(i):
        return jnp.where(i % per_b == 0, i * nsub, i * nsub - 1)

    return pl.pallas_call(
        functools.partial(_attn_kernel, per_b=per_b),
        grid=(N // tq,),
        in_specs=[
            pl.BlockSpec(memory_space=pltpu.SMEM),
            pl.BlockSpec((tq, Q_DIM), lambda i: (i, 0)),
            pl.BlockSpec((bq, KV_DIM), lambda i: (prev(i), k_col)),
            pl.BlockSpec((tq, KV_DIM), lambda i: (i, k_col)),
            pl.BlockSpec((bq, KV_DIM), lambda i: (prev(i), v_col)),
            pl.BlockSpec((tq, KV_DIM), lambda i: (i, v_col)),
        ],
        out_specs=pl.BlockSpec((tq, Q_DIM), lambda i: (i, 0)),
        out_shape=jax.ShapeDtypeStruct((N, Q_DIM), BF16),
        scratch_shapes=[pltpu.VMEM((2, N_HEADS, bq, 2 * bq), F32)],
        compiler_params=pltpu.CompilerParams(
            dimension_semantics=("arbitrary",),
            vmem_limit_bytes=_vmem_limit(3 * 2 * N_HEADS * bq * 2 * bq * 4)),
        name="attn",
    )(sinks, proj_rest, proj_rest, proj_rest, proj_rest, proj_rest)


def _merge_kernel(x_ref, yr_ref, ya_ref, gr_ref, ga_ref, wr_ref, wa_ref, wo_ref,
                  gpost_ref, g1_ref, gpre_ref, sc2_ref, sh2_ref, rwt_ref, rb_ref,
                  x1_ref, h2_ref, lg_ref):
    r = jnp.dot(yr_ref[...], wr_ref[...], preferred_element_type=F32)
    a = jnp.dot(ya_ref[...], wa_ref[...], preferred_element_type=F32)
    merged = ((0.5 * jnp.tanh(gr_ref[...].astype(F32)) + 0.5) * r
              + (0.5 * jnp.tanh(ga_ref[...].astype(F32)) + 0.5) * a)
    mix = jnp.dot(merged.astype(BF16), wo_ref[...], preferred_element_type=F32)
    x1 = x_ref[...] + g1_ref[0] * _rms(mix, gpost_ref[...])
    x1_ref[...] = x1
    h2 = (_rms(x1, gpre_ref[...]) * (1.0 + sc2_ref[0]) + sh2_ref[0]).astype(BF16)
    h2_ref[...] = h2
    lg = lax.dot_general(rwt_ref[...], h2, (((1,), (1,)), ((), ())),
                         preferred_element_type=F32)
    lg_ref[...] = lg + rb_ref[...]


def _merge(x2, y_rnn, y_att, proj_rest, wr, wa, wo, gpost, g1, gpre, sc2, sh2, rwt, rb, S):
    N, D = x2.shape
    tm = _tiles()["tok"]
    per_b = S // tm
    gate_r_col = Q_DIM // D
    mat = lambda: pl.BlockSpec((D, D), lambda i: (0, 0))
    vec = lambda: pl.BlockSpec((1, D), lambda i: (0, 0))
    bvec = lambda: pl.BlockSpec((1, 1, D), lambda i: (i // per_b, 0, 0))
    tile = lambda col=0: pl.BlockSpec((tm, D), lambda i: (i, col))
    return pl.pallas_call(
        _merge_kernel,
        grid=(N // tm,),
        in_specs=[
            tile(), tile(), tile(), tile(gate_r_col), tile(gate_r_col + 1),
            mat(), mat(), mat(),
            vec(), bvec(), vec(), bvec(), bvec(),
            pl.BlockSpec((N_EXPERTS, D), lambda i: (0, 0)),
            pl.BlockSpec((N_EXPERTS, 1), lambda i: (0, 0)),
        ],
        out_specs=[
            tile(), tile(),
            pl.BlockSpec((N_EXPERTS, tm), lambda i: (0, i)),
        ],
        out_shape=[
            jax.ShapeDtypeStruct((N, D), F32),
            jax.ShapeDtypeStruct((N, D), BF16),
            jax.ShapeDtypeStruct((N_EXPERTS, N), F32),
        ],
        compiler_params=pltpu.CompilerParams(
            dimension_semantics=("arbitrary",),
            vmem_limit_bytes=_vmem_limit(6 * D * D * 2 + 24 * tm * D * 4)),
        name="merge",
    )(x2, y_rnn, y_att, proj_rest, proj_rest, wr, wa, wo, gpost, g1, gpre, sc2, sh2, rwt, rb)


def _route_kernel(lg_ref, g_ref, slot_ref, cnt_ref, tcnt_ref, tcar_ref, tlst_ref, carry, *, T):
    i = pl.program_id(0)
    E = lg_ref.shape[0]
    n_sub = lg_ref.shape[1] // T

    @pl.when(i == 0)
    def _():
        carry[...] = jnp.zeros_like(carry)
        tcnt_ref[...] = jnp.zeros_like(tcnt_ref)
        tcar_ref[...] = jnp.zeros_like(tcar_ref)
        tlst_ref[...] = jnp.zeros_like(tlst_ref)

    row = lax.broadcasted_iota(jnp.int32, (E, T), 0).astype(F32)
    tri_t = (lax.broadcasted_iota(jnp.int32, (T, T), 0)
             < lax.broadcasted_iota(jnp.int32, (T, T), 1)).astype(BF16)
    tri_e = (lax.broadcasted_iota(jnp.int32, (E, E), 1)
             < lax.broadcasted_iota(jnp.int32, (E, E), 0)).astype(BF16)
    tile_lane = lax.broadcasted_iota(jnp.int32, tcnt_ref.shape, 1)
    running = carry[...]
    for sub in range(n_sub):
        cols = slice(sub * T, (sub + 1) * T)
        l = lg_ref[:, cols]
        vals, idxs = [], []
        for _ in range(TOP_K):
            m = jnp.max(l, axis=0, keepdims=True)
            idx = jnp.min(jnp.where(l == m, row, float(E)), axis=0, keepdims=True)
            vals.append(m)
            idxs.append(idx)
            l = jnp.where(row == idx, -jnp.inf, l)
        ex = [jnp.exp(v - vals[0]) for v in vals]
        tot = ex[0]
        for e in ex[1:]:
            tot = tot + e
        g_ref[:, cols] = jnp.concatenate([e / tot for e in ex], axis=0)

        hot = [row == idx for idx in idxs]
        onehot = jnp.zeros((E, T), F32)
        for hk in hot:
            onehot = onehot + hk.astype(F32)
        before = jnp.dot(onehot.astype(BF16), tri_t, preferred_element_type=F32)
        cnt = jnp.sum(onehot, axis=1, keepdims=True)
        lstart = jnp.dot(tri_e, jnp.broadcast_to(cnt, (E, V7X_LANES)).astype(BF16),
                         preferred_element_type=F32)[:, 0:1]
        local = before + lstart
        slots = [jnp.sum(jnp.where(hk, local, 0.0), axis=0, keepdims=True) for hk in hot]
        slot_ref[:, cols] = jnp.concatenate(slots, axis=0).astype(jnp.int32)

        mine = tile_lane == i * n_sub + sub
        tcnt_ref[...] = jnp.where(mine, cnt.astype(jnp.int32), tcnt_ref[...])
        tcar_ref[...] = jnp.where(mine, running.astype(jnp.int32), tcar_ref[...])
        tlst_ref[...] = jnp.where(mine, lstart.astype(jnp.int32), tlst_ref[...])
        running = running + cnt
    carry[...] = running
    cnt_ref[...] = running.astype(jnp.int32)


def _route(logits_t):
    E, N = logits_t.shape
    T = _tiles()["moe_t"]
    TS = T * _tiles()["route_tiles"]
    assert T <= 2 ** 8 and N // T <= V7X_LANES
    out = lambda: pl.BlockSpec((TOP_K, TS), lambda i: (0, i))
    per_tile = lambda: pl.BlockSpec((E, V7X_LANES), lambda i: (0, 0))
    return pl.pallas_call(
        functools.partial(_route_kernel, T=T),
        grid=(N // TS,),
        in_specs=[pl.BlockSpec((E, TS), lambda i: (0, i))],
        out_specs=[out(), out(), pl.BlockSpec((E, 1), lambda i: (0, 0)),
                   per_tile(), per_tile(), per_tile()],
        out_shape=[
            jax.ShapeDtypeStruct((TOP_K, N), F32),
            jax.ShapeDtypeStruct((TOP_K, N), jnp.int32),
            jax.ShapeDtypeStruct((E, 1), jnp.int32),
            jax.ShapeDtypeStruct((E, V7X_LANES), jnp.int32),
            jax.ShapeDtypeStruct((E, V7X_LANES), jnp.int32),
            jax.ShapeDtypeStruct((E, V7X_LANES), jnp.int32),
        ],
        scratch_shapes=[pltpu.VMEM((E, 1), F32)],
        compiler_params=pltpu.CompilerParams(dimension_semantics=("arbitrary",)),
        name="route",
    )(logits_t)


def _run_sizes(T):
    return [2 ** b for b in range(int(math.log2(T)), -1, -1)]


def _copy_plan(tile_cnt, tile_lst, tile_dst, T):
    sizes = jnp.asarray(_run_sizes(T), jnp.int32)[None, :, None]
    cnt, lst, dst = (a[:, None, :] for a in (tile_cnt, tile_lst, tile_dst))
    has = (cnt & sizes) != 0
    off = (cnt // (2 * sizes)) * (2 * sizes)
    eids = jnp.arange(N_EXPERTS)
    earlier = eids[:, None] < eids[None, :]
    place = jnp.sum(jnp.where(earlier, has[..., :, None], False), axis=-2)
    hit = has[..., :, None] & (place[..., :, None] == eids)
    pick = lambda v: jnp.sum(jnp.where(hit, v[..., :, None], 0), axis=-2)
    n_tiles, nb = tile_cnt.shape[0], sizes.shape[1]
    parts = [pick(jnp.broadcast_to(lst + off, has.shape)).reshape(n_tiles, -1),
             pick(jnp.broadcast_to(dst + off, has.shape)).reshape(n_tiles, -1),
             jnp.sum(has, axis=-1).reshape(n_tiles, nb)]
    plan = jnp.concatenate(parts, axis=1).astype(jnp.int32)
    pad = _plan_len(T) - plan.shape[1]
    return jnp.pad(plan, ((0, 0), (0, pad))).reshape(-1)


def _plan_len(T):
    nb = len(_run_sizes(T))
    return max(V7X_LANES, pl.next_power_of_2(2 * nb * N_EXPERTS + nb))


def _run_copies(plan_ref, T, make):
    sizes = _run_sizes(T)
    nb = len(sizes)
    for bi, b in enumerate(sizes):
        def body(j, c, bi=bi, b=b):
            make(plan_ref[bi * N_EXPERTS + j], plan_ref[(nb + bi) * N_EXPERTS + j], b).start()
            return c

        lax.fori_loop(0, plan_ref[2 * nb * N_EXPERTS + bi], body, 0)


def _row_slab(ref, row, nrows, lead=()):
    rows = pl.ds(pl.multiple_of(row * V7X_SUBLANES, V7X_SUBLANES), nrows * V7X_SUBLANES)
    return ref.at[(*lead, rows)]


def _to_slabs(ref, val, lead=()):
    rows = val.shape[0]
    for s in range(V7X_SUBLANES):
        ref[(*lead, pl.ds(s, rows, stride=V7X_SUBLANES), slice(None))] = (
            val[:, s * V7X_LANES:(s + 1) * V7X_LANES])


def _from_slabs(ref, lead=(), rows=None):
    rows = ref.shape[-2] // V7X_SUBLANES if rows is None else rows
    return jnp.concatenate(
        [ref[(*lead, pl.ds(s, rows, stride=V7X_SUBLANES), slice(None))]
         for s in range(V7X_SUBLANES)], axis=1)


def _dispatch_kernel(tail_ref, nu_ref, plan_ref, slot_ref, h_ref, xs_ref, buf, zbuf, sem, zsem):
    i = pl.program_id(0)
    T = h_ref.shape[0]
    bm = zbuf.shape[0] // V7X_SUBLANES
    nblk = xs_ref.shape[0] // zbuf.shape[0]
    cur = i % 2

    def zero_copy(row0):
        return pltpu.make_async_copy(zbuf, _row_slab(xs_ref, row0, bm), zsem)

    @pl.when(i == 0)
    def _():
        zbuf[...] = jnp.zeros_like(zbuf)

        def tail_chunks(e, fn):
            @pl.when(tail_ref[e] >= 0)
            def _():
                fn(zero_copy(tail_ref[e]))

            @pl.when(jnp.logical_and(tail_ref[e] >= 0, tail_ref[e] % (2 * bm) == 0))
            def _():
                fn(zero_copy(tail_ref[e] + bm))

        def start(e, c):
            tail_chunks(e, lambda cp: cp.start())
            return c

        def wait(e, c):
            tail_chunks(e, lambda cp: cp.wait())
            return c

        def start_unused(j, c):
            zero_copy(j * bm).start()
            return c

        def wait_unused(j, c):
            zero_copy(j * bm).wait()
            return c

        lax.fori_loop(0, N_EXPERTS, start, 0)
        lax.fori_loop(nu_ref[0], nblk, start_unused, 0)
        lax.fori_loop(0, N_EXPERTS, wait, 0)
        lax.fori_loop(nu_ref[0], nblk, wait_unused, 0)

    slot = slot_ref[...]
    r_id = lax.broadcasted_iota(jnp.int32, (TOP_K * T, T), 0)
    perm = r_id == slot[0:1, :]
    for k in range(1, TOP_K):
        perm = jnp.logical_or(perm, r_id == slot[k:k + 1, :])
    grouped = jnp.dot(perm.astype(BF16), h_ref[...], preferred_element_type=F32)
    _to_slabs(buf, grouped, lead=(cur,))

    def wait_tile(b):
        pltpu.make_async_copy(buf.at[b], _row_slab(xs_ref, 0, TOP_K * T), sem.at[b]).wait()

    @pl.when(i > 0)
    def _():
        wait_tile(1 - cur)

    _run_copies(plan_ref, T, lambda l, d, n: pltpu.make_async_copy(
        _row_slab(buf, l, n, lead=(cur,)), _row_slab(xs_ref, d, n), sem.at[cur]))

    @pl.when(i == pl.num_programs(0) - 1)
    def _():
        wait_tile(cur)


def _dispatch(h2, slot, plan, tail_start, n_used, n_rows, bm):
    N, D = h2.shape
    T = _tiles()["moe_t"]
    S8 = V7X_SUBLANES
    imap = lambda f: (lambda i, *_: f(i))
    return pl.pallas_call(
        _dispatch_kernel,
        grid_spec=pltpu.PrefetchScalarGridSpec(
            num_scalar_prefetch=2,
            grid=(N // T,),
            in_specs=[
                pl.BlockSpec((_plan_len(T),), imap(lambda i: (i,)), memory_space=pltpu.SMEM),
                pl.BlockSpec((TOP_K, T), imap(lambda i: (0, i))),
                pl.BlockSpec((T, D), imap(lambda i: (i, 0))),
            ],
            out_specs=pl.BlockSpec(memory_space=pl.ANY),
            scratch_shapes=[
                pltpu.VMEM((2, TOP_K * T * S8, V7X_LANES), F32),
                pltpu.VMEM((bm * S8, V7X_LANES), F32),
                pltpu.SemaphoreType.DMA((2,)),
                pltpu.SemaphoreType.DMA(()),
            ],
        ),
        out_shape=jax.ShapeDtypeStruct((n_rows * S8, V7X_LANES), F32),
        compiler_params=pltpu.CompilerParams(
            dimension_semantics=("arbitrary",),
            vmem_limit_bytes=_vmem_limit(8 * TOP_K * T * D * 4)),
        name="dispatch",
    )(tail_start, n_used, plan, slot, h2)


def _ffn_kernel(be_ref, bv_ref, bf_ref, nu_ref, nxt_ref, x_ref, w1_hbm, b1g_ref, b1l_ref, w2_hbm,
                b2_ref, y_ref, w1f_s, w2f_s, w1_s, w2_s, sem):
    i = pl.program_id(0)
    bm, D = x_ref.shape[0] // V7X_SUBLANES, w2_s.shape[1]
    n_ff = w2_s.shape[0]
    pw = 2 * V7X_LANES

    def fetch(e):
        return (pltpu.make_async_copy(w1_hbm.at[e], w1f_s, sem.at[0]),
                pltpu.make_async_copy(w2_hbm.at[e], w2f_s, sem.at[1]))

    @pl.when(i == 0)
    def _():
        for cp in fetch(be_ref[0]):
            cp.start()

    @pl.when(jnp.logical_and(i < nu_ref[0], bf_ref[i] == 1))
    def _():
        e = be_ref[i]
        for cp in fetch(e):
            cp.wait()
        src = lax.broadcasted_iota(jnp.int32, (pw, pw), 0)
        dst = lax.broadcasted_iota(jnp.int32, (pw, pw), 1)
        want = jnp.where(dst < V7X_LANES, 2 * dst, 2 * (dst - V7X_LANES) + 1)
        perm = (src == want).astype(BF16)
        for c in range(w1f_s.shape[1] // pw):
            wp = jnp.dot(w1f_s[:, c * pw:(c + 1) * pw].astype(BF16), perm,
                         preferred_element_type=F32).astype(BF16)
            w1_s[:, c * V7X_LANES:(c + 1) * V7X_LANES] = wp[:, :V7X_LANES]
            w1_s[:, n_ff + c * V7X_LANES:n_ff + (c + 1) * V7X_LANES] = wp[:, V7X_LANES:]
        w2_s[...] = w2f_s[...].astype(BF16)

        @pl.when(nxt_ref[e] >= 0)
        def _():
            for cp in fetch(nxt_ref[e]):
                cp.start()

    def mlp(n_rows):
        x = _from_slabs(x_ref, rows=n_rows).astype(BF16)
        hid = jnp.dot(x, w1_s[...], preferred_element_type=F32)
        glu = hid[:, :n_ff] + b1g_ref[0]
        lin = hid[:, n_ff:] + b1l_ref[0]
        glu = jnp.minimum(glu, SWIGLU_LIMIT)
        lin = jnp.clip(lin, -SWIGLU_LIMIT, SWIGLU_LIMIT)
        act = glu * _sigmoid(SWIGLU_ALPHA * glu) * (lin + 1.0)
        _to_slabs(y_ref, jnp.dot(act.astype(BF16), w2_s[...], preferred_element_type=F32) + b2_ref[0])
        if n_rows < bm:
            y_ref[n_rows * V7X_SUBLANES:, :] = jnp.zeros(
                ((bm - n_rows) * V7X_SUBLANES, V7X_LANES), F32)

    used = i < nu_ref[0]
    half = bm // 2

    @pl.when(jnp.logical_and(used, bv_ref[i] > half))
    def _():
        mlp(bm)

    @pl.when(jnp.logical_and(used, bv_ref[i] <= half))
    def _():
        mlp(half)

    @pl.when(i >= nu_ref[0])
    def _():
        y_ref[...] = jnp.zeros_like(y_ref)


def _ffn(xs, blk_e, blk_valid, blk_first, n_used, next_e, w1, b1g, b1l, w2, b2, bm):
    E, D, F2 = w1.shape
    F = F2 // 2
    slab = bm * V7X_SUBLANES
    nblk = xs.shape[0] // slab

    def row_blk(i, be, bv, bf, nu, nxt):
        return (jnp.minimum(i, nu[0] - 1), 0)

    def per_e(i, be, bv, bf, nu, nxt):
        return (be[i], 0, 0)

    vmem = (D * F2 + F * D) * (4 + 2) + 8 * bm * F2 * 4
    return pl.pallas_call(
        _ffn_kernel,
        grid_spec=pltpu.PrefetchScalarGridSpec(
            num_scalar_prefetch=5,
            grid=(nblk,),
            in_specs=[
                pl.BlockSpec((slab, V7X_LANES), row_blk),
                pl.BlockSpec(memory_space=pl.ANY),
                pl.BlockSpec((1, 1, F), per_e),
                pl.BlockSpec((1, 1, F), per_e),
                pl.BlockSpec(memory_space=pl.ANY),
                pl.BlockSpec((1, 1, D), per_e),
            ],
            out_specs=pl.BlockSpec((slab, V7X_LANES), lambda i, be, bv, bf, nu, nxt: (i, 0)),
            scratch_shapes=[
                pltpu.VMEM((D, F2), F32),
                pltpu.VMEM((F, D), F32),
                pltpu.VMEM((D, F2), BF16),
                pltpu.VMEM((F, D), BF16),
                pltpu.SemaphoreType.DMA((2,)),
            ],
        ),
        out_shape=jax.ShapeDtypeStruct(xs.shape, F32),
        compiler_params=pltpu.CompilerParams(
            dimension_semantics=("arbitrary",),
            vmem_limit_bytes=_vmem_limit(vmem)),
        name="ffn",
    )(blk_e, blk_valid, blk_first, n_used, next_e, xs, w1, b1g, b1l, w2, b2)


def _combine_kernel(plan_ref, plan_next_ref, ys_ref, x1_ref, slot_ref, gate_ref, gpost_ref, g2_ref,
                    o_ref, buf, sem):
    i = pl.program_id(0)
    T = x1_ref.shape[0]
    cur = i % 2

    def gather(plan, b):
        _run_copies(plan, T, lambda l, d, n: pltpu.make_async_copy(
            _row_slab(ys_ref, d, n), _row_slab(buf, l, n, lead=(b,)), sem.at[b]))

    @pl.when(i == 0)
    def _():
        gather(plan_ref, cur)

    @pl.when(i + 1 < pl.num_programs(0))
    def _():
        gather(plan_next_ref, 1 - cur)

    pltpu.make_async_copy(_row_slab(ys_ref, 0, TOP_K * T), buf.at[cur], sem.at[cur]).wait()

    slot = slot_ref[...]
    gates = gate_ref[...]
    r_id = lax.broadcasted_iota(jnp.int32, (T, TOP_K * T), 1)
    w = jnp.zeros((T, TOP_K * T), F32)
    for k in range(TOP_K):
        w = jnp.where(r_id == slot[:, k:k + 1], gates[:, k:k + 1], w)
    ff = jnp.dot(w.astype(BF16), _from_slabs(buf, lead=(cur,)).astype(BF16),
                 preferred_element_type=F32)
    o_ref[...] = x1_ref[...] + g2_ref[0] * _rms(ff, gpost_ref[...])


def _combine(ys, plan, x1, slot_tk, gates_tk, gpost, g2, S):
    N, D = x1.shape
    T = _tiles()["moe_t"]
    per_b = S // T
    n_tiles = N // T
    plan_spec = lambda f: pl.BlockSpec((_plan_len(T),), f, memory_space=pltpu.SMEM)
    return pl.pallas_call(
        _combine_kernel,
        grid=(n_tiles,),
        in_specs=[
            plan_spec(lambda i: (i,)),
            plan_spec(lambda i: (jnp.minimum(i + 1, n_tiles - 1),)),
            pl.BlockSpec(memory_space=pl.ANY),
            pl.BlockSpec((T, D), lambda i: (i, 0)),
            pl.BlockSpec((T, TOP_K), lambda i: (i, 0)),
            pl.BlockSpec((T, TOP_K), lambda i: (i, 0)),
            pl.BlockSpec((1, D), lambda i: (0, 0)),
            pl.BlockSpec((1, 1, D), lambda i: (i // per_b, 0, 0)),
        ],
        out_specs=pl.BlockSpec((T, D), lambda i: (i, 0)),
        out_shape=jax.ShapeDtypeStruct((N, D), F32),
        scratch_shapes=[
            pltpu.VMEM((2, TOP_K * T * V7X_SUBLANES, V7X_LANES), F32),
            pltpu.SemaphoreType.DMA((2,)),
        ],
        compiler_params=pltpu.CompilerParams(
            dimension_semantics=("arbitrary",),
            vmem_limit_bytes=_vmem_limit(10 * TOP_K * T * D * 4)),
        name="combine",
    )(plan, plan, ys, x1, slot_tk, gates_tk, gpost, g2)


def _col_scaled(w):
    scale = HEAD_DIM ** -0.5
    assert math.log2(scale).is_integer()
    col = np.arange(D_IN)
    q0, g0 = 2 * D_RNN, 2 * D_RNN + Q_DIM + 2 * KV_DIM
    factor = np.where((col >= q0) & (col < q0 + Q_DIM), scale, np.where(col >= g0, 0.5, 1.0))
    return w * jnp.asarray(factor, w.dtype)


def _layer(x2, c, B, S, p):
    D = D_MODEL
    N = B * S
    ada = _ada(c, p["w_ada"], p["b_ada"])
    sh1, sc1, g1, sh2, sc2, g2 = [a.reshape(B, 1, D) for a in jnp.split(ada, 6, axis=-1)]
    row = lambda v: v.reshape(1, -1)

    gw = _tiles()["rnn_group"]
    proj_rest, y_rnn = _mixin(
        x2, row(p["norm_pre_mix"]), sc1, sh1,
        _col_scaled(p["w_in"]).astype(BF16), _col_scaled(row(p["b_in"])),
        p["conv_w"], row(p["conv_b"]),
        (0.5 * _block_diag_tiles(p["rg_w_a"], gw)).astype(BF16), 0.5 * row(p["rg_b_a"]),
        (0.5 * _block_diag_tiles(p["rg_w_x"], gw)).astype(BF16), 0.5 * row(p["rg_b_x"]),
        row(p["rg_lambda"]), S)
    y_att = _attn(proj_rest, p["attn_sinks"], S)

    x1, h2, logits_t = _merge(
        x2, y_rnn, y_att, proj_rest,
        p["w_o_rnn"].astype(BF16), p["w_o_attn"].astype(BF16), p["w_out"].astype(BF16),
        row(p["norm_post_mix"]), g1, row(p["norm_pre_ffn"]), sc2, sh2,
        p["router_w"].T.astype(BF16), p["router_b"].reshape(N_EXPERTS, 1), S)

    gates, slot, counts, tile_cnt, tile_car, tile_lst = _route(logits_t)

    bm = _tiles()["moe_bm"]
    n_tiles = N // _tiles()["moe_t"]
    n_rows = N * TOP_K + N_EXPERTS * bm
    nblk = n_rows // bm
    counts = counts.reshape(N_EXPERTS)
    padded = ((counts + bm - 1) // bm) * bm
    pend = jnp.cumsum(padded)
    pstart = pend - padded
    eids = jnp.arange(N_EXPERTS, dtype=jnp.int32)
    per_run = lambda a: a[:, :n_tiles].T.astype(jnp.int32)
    plan = _copy_plan(per_run(tile_cnt), per_run(tile_lst), per_run(tile_car + pstart[:, None]),
                      _tiles()["moe_t"])
    zr = bm // 2
    tail_start = jnp.where(counts > 0, (pstart + counts - 1) // zr * zr, -1).astype(jnp.int32)
    n_used = (pend[-1] // bm).astype(jnp.int32).reshape(1)
    blk_row0 = jnp.minimum(jnp.arange(nblk, dtype=jnp.int32), n_used[0] - 1) * bm
    blk_e = jnp.sum(blk_row0[:, None] >= pend[None, :], axis=1).astype(jnp.int32)
    mine = blk_e[:, None] == eids[None, :]
    blk_cnt = jnp.sum(jnp.where(mine, counts, 0), axis=1)
    blk_pstart = jnp.sum(jnp.where(mine, pstart, 0), axis=1)
    blk_valid = jnp.clip(blk_cnt - (blk_row0 - blk_pstart), 0, bm).astype(jnp.int32)
    blk_first = (blk_row0 == blk_pstart).astype(jnp.int32)

    xs = _dispatch(h2, slot, plan, tail_start, n_used * (bm // zr), n_rows, zr)
    b1 = p["moe_b1"].reshape(N_EXPERTS, D_FF, 2)
    later = jnp.logical_and(counts[None, :] > 0, eids[None, :] > eids[:, None])
    next_e = jnp.min(jnp.where(later, eids[None, :], N_EXPERTS), axis=1)
    next_e = jnp.where(next_e == N_EXPERTS, -1, next_e).astype(jnp.int32)
    ys = _ffn(xs, blk_e, blk_valid, blk_first, n_used, next_e, p["moe_w1"],
              b1[:, :, 0].reshape(N_EXPERTS, 1, D_FF), b1[:, :, 1].reshape(N_EXPERTS, 1, D_FF),
              p["moe_w2"], p["moe_b2"].reshape(N_EXPERTS, 1, D), bm)
    return _combine(ys, plan, x1, slot.T, gates.T, row(p["norm_post_ffn"]), g2, S)


def kernel(x, c, w_ada, b_ada, norm_pre_mix, norm_post_mix, norm_pre_ffn, norm_post_ffn, w_in, b_in, conv_w, conv_b, rg_w_a, rg_b_a, rg_w_x, rg_b_x, rg_lambda, attn_sinks, w_o_rnn, w_o_attn, w_out, router_w, router_b, moe_w1, moe_b1, moe_w2, moe_b2):
    B, S, D = x.shape
    params = dict(
        w_ada=w_ada, b_ada=b_ada, norm_pre_mix=norm_pre_mix, norm_post_mix=norm_post_mix,
        norm_pre_ffn=norm_pre_ffn, norm_post_ffn=norm_post_ffn, w_in=w_in, b_in=b_in,
        conv_w=conv_w, conv_b=conv_b, rg_w_a=rg_w_a, rg_b_a=rg_b_a, rg_w_x=rg_w_x, rg_b_x=rg_b_x,
        rg_lambda=rg_lambda, attn_sinks=attn_sinks, w_o_rnn=w_o_rnn, w_o_attn=w_o_attn,
        w_out=w_out, router_w=router_w, router_b=router_b, moe_w1=moe_w1, moe_b1=moe_b1,
        moe_w2=moe_w2, moe_b2=moe_b2)
    x2 = x.reshape(B * S, D)
    for layer in range(w_ada.shape[0]):
        x2 = _layer(x2, c, B, S, {k: v[layer] for k, v in params.items()})
    return x2.reshape(B, S, D)
```

```python
import functools
import math

import jax
import jax.numpy as jnp
import numpy as np
from jax import lax
from jax.experimental import pallas as pl
from jax.experimental.pallas import tpu as pltpu

D_MODEL = 1024
D_RNN = 1024
RNN_BLOCKS = 16
RNN_BW = D_RNN // RNN_BLOCKS
CONV_W = 4
LRU_C = 8.0
N_HEADS = 16
N_KV = 4
HEAD_DIM = 64
GROUP = N_HEADS // N_KV
WINDOW = 128
Q_DIM = N_HEADS * HEAD_DIM
KV_DIM = N_KV * HEAD_DIM
N_EXPERTS = 32
TOP_K = 4
D_FF = 1024
SWIGLU_LIMIT = 7.0
SWIGLU_ALPHA = 1.702
EPS = 1e-6
D_IN = 2 * D_RNN + Q_DIM + 2 * KV_DIM + 2 * D_MODEL
D_REST = D_IN - 2 * D_RNN

V7X_LANES = 128
V7X_SUBLANES = 8
V7X_MXU_DIM = 256
V7X_VMEM_BYTES = 64 * 1024 * 1024

F32 = jnp.float32
BF16 = jnp.bfloat16


def _tiles():
    return dict(
        ada_tn=4 * V7X_MXU_DIM,
        tok=2 * V7X_MXU_DIM,
        attn_q=WINDOW,
        attn_blocks=2,
        moe_t=V7X_MXU_DIM,
        route_tiles=4,
        moe_bm=2 * V7X_MXU_DIM,
        rnn_group=V7X_MXU_DIM,
    )


V7X_VMEM_MIN_LIMIT = 16 * 1024 * 1024
V7X_VMEM_HEADROOM = 8 * 1024 * 1024


def _vmem_limit(nbytes):
    return int(min(max(nbytes, V7X_VMEM_MIN_LIMIT), V7X_VMEM_BYTES - V7X_VMEM_HEADROOM))


def _rms(x, g):
    return x * lax.rsqrt(jnp.mean(x * x, axis=-1, keepdims=True) + EPS) * g


def _ada_kernel(ct_ref, w_ref, b_ref, o_ref):
    ct = ct_ref[...]
    sc = ct * jax.nn.sigmoid(ct)
    w = w_ref[...]
    rows = [jnp.sum(w * sc[:, b:b + 1], axis=0, keepdims=True) for b in range(ct.shape[1])]
    o_ref[...] = jnp.concatenate(rows, axis=0) + b_ref[...]


def _ada(c, w_ada, b_ada):
    B, D = c.shape
    n_out = w_ada.shape[1]
    tn = _tiles()["ada_tn"]
    return pl.pallas_call(
        _ada_kernel,
        grid=(n_out // tn,),
        in_specs=[
            pl.BlockSpec((D, B), lambda j: (0, 0)),
            pl.BlockSpec((D, tn), lambda j: (0, j)),
            pl.BlockSpec((1, tn), lambda j: (0, j)),
        ],
        out_specs=pl.BlockSpec((B, tn), lambda j: (0, j)),
        out_shape=jax.ShapeDtypeStruct((B, n_out), F32),
        compiler_params=pltpu.CompilerParams(
            dimension_semantics=("arbitrary",),
            vmem_limit_bytes=_vmem_limit(4 * D * tn * 4)),
        name="ada",
    )(c.T, w_ada, b_ada.reshape(1, n_out))


def _gelu_tanh(x):
    return 0.5 * x * (1.0 + jnp.tanh(math.sqrt(2.0 / math.pi) * (x + 0.044715 * (x * x * x))))


def _softplus(z):
    return jnp.maximum(z, 0.0) + jnp.log1p(jnp.exp(-jnp.abs(z)))


def _sigmoid(x):
    return 0.5 * jnp.tanh(0.5 * x) + 0.5


def _rest_col(c):
    c -= 2 * D_RNN
    q_end, k_end, v_end = Q_DIM, Q_DIM + KV_DIM, Q_DIM + 2 * KV_DIM
    if c < q_end:
        return c
    if c < k_end:
        return Q_DIM + 2 * D_MODEL + (c - q_end)
    if c < v_end:
        return Q_DIM + 2 * D_MODEL + KV_DIM + (c - k_end)
    return Q_DIM + (c - v_end)


def _mixin_kernel(x_ref, g_ref, sc_ref, sh_ref, w_ref, b_ref, cw_ref, cb_ref, wa_ref, ba_ref,
                  wx_ref, bx_ref, lam_ref, rest_ref, y_ref,
                  hb_s, xbuf, gr_s, a_s, b_s, h_s, ga_s, gb_s, gc_s, carry, *, per_b, gw, chunk):
    t = pl.program_id(0) % per_b
    tt = x_ref.shape[0]
    halo = V7X_SUBLANES
    ng = tt // V7X_SUBLANES
    n_lt = a_s.shape[0]
    per_g = gw // V7X_LANES
    lanes = lambda j: slice(j * V7X_LANES, (j + 1) * V7X_LANES)
    slab = lambda j, r: (j, pl.ds(r, ng, stride=V7X_SUBLANES), slice(None))

    @pl.when(t == 0)
    def _():
        xbuf[:, 0:halo, :] = jnp.zeros((n_lt, halo, V7X_LANES), F32)
        carry[...] = jnp.zeros_like(carry)

    hb_s[...] = (_rms(x_ref[...], g_ref[...]) * (1.0 + sc_ref[0]) + sh_ref[0]).astype(BF16)

    def proj(c0):
        return (jnp.dot(hb_s[...], w_ref[:, c0:c0 + chunk], preferred_element_type=F32)
                + b_ref[:, c0:c0 + chunk])

    def proj_rest(c0):
        acc = proj(c0).astype(BF16)
        for p0 in range(0, chunk, KV_DIM):
            d0 = _rest_col(c0 + p0)
            rest_ref[:, d0:d0 + KV_DIM] = acc[:, p0:p0 + KV_DIM]

    for c0 in range(0, D_RNN, chunk):
        acc = proj(c0)
        for j in range(chunk // V7X_LANES):
            xbuf[c0 // V7X_LANES + j, halo:halo + tt, :] = acc[:, lanes(j)]
    for c0 in range(D_RNN, 2 * D_RNN, chunk):
        gr_s[:, c0 - D_RNN:c0 - D_RNN + chunk] = proj(c0)

    def conv(j):
        ls = lanes(j)
        taps = {o: xbuf[slab(j, halo + o)] for o in range(-(CONV_W - 1), V7X_SUBLANES)}
        for r in range(V7X_SUBLANES):
            acc = cb_ref[:, ls] + taps[r - (CONV_W - 1)] * cw_ref[0:1, ls]
            for kk in range(1, CONV_W):
                acc = acc + taps[r - (CONV_W - 1) + kk] * cw_ref[kk:kk + 1, ls]
            h_s[slab(j, r)] = acc
        xbuf[j, 0:halo, :] = xbuf[j, tt:tt + halo, :]

    def gates(g):
        cs = slice(g * gw, (g + 1) * gw)
        rate = (-LRU_C * math.log2(math.e)) * _softplus(-lam_ref[:, cs])
        reset = jnp.logical_and(t == 0, lax.broadcasted_iota(jnp.int32, (tt, gw), 0) == 0)
        xc = jnp.concatenate([h_s[g * per_g + j] for j in range(per_g)], axis=1)
        xg = xc.astype(BF16)
        gate_r = 0.5 * jnp.tanh(jnp.dot(xg, wa_ref[g], preferred_element_type=F32) + ba_ref[:, cs]) + 0.5
        gate_i = 0.5 * jnp.tanh(jnp.dot(xg, wx_ref[g], preferred_element_type=F32) + bx_ref[:, cs]) + 0.5
        a = jnp.exp2(gate_r * rate)
        v = (1.0 - a) * (1.0 + a)
        mult = jnp.where(reset, 1.0, jnp.where(v > 0.0, v * lax.rsqrt(v), 0.0))
        bt = (xc * gate_i) * mult
        for j in range(per_g):
            a_s[g * per_g + j] = a[:, lanes(j)]
            b_s[g * per_g + j] = bt[:, lanes(j)]

    def scan_groups(j):
        acc_a = a_s[slab(j, 0)]
        acc_h = b_s[slab(j, 0)]
        h_s[slab(j, 0)] = acc_h
        for r in range(1, V7X_SUBLANES):
            a_r = a_s[slab(j, r)]
            acc_h = a_r * acc_h + b_s[slab(j, r)]
            acc_a = a_r * acc_a
            h_s[slab(j, r)] = acc_h
            a_s[slab(j, r)] = acc_a
        ga_s[:, lanes(j)] = acc_a
        gb_s[:, lanes(j)] = acc_h

    def across(gi, h_prev):
        gc_s[pl.ds(gi, 1), :] = h_prev
        return ga_s[pl.ds(gi, 1), :] * h_prev + gb_s[pl.ds(gi, 1), :]

    def finish(j):
        h_in = gc_s[:, lanes(j)]
        for r in range(V7X_SUBLANES):
            h_s[slab(j, r)] = a_s[slab(j, r)] * h_in + h_s[slab(j, r)]
        y_ref[:, lanes(j)] = (h_s[j] * _gelu_tanh(gr_s[:, lanes(j)])).astype(BF16)

    rest = [functools.partial(proj_rest, c0) for c0 in range(2 * D_RNN, w_ref.shape[1], chunk)]
    before = ([functools.partial(conv, j) for j in range(n_lt)]
              + [functools.partial(gates, g) for g in range(n_lt // per_g)]
              + [functools.partial(scan_groups, j) for j in range(n_lt)])
    after = [functools.partial(finish, j) for j in range(n_lt)]
    n_before = (len(rest) * len(before)) // (len(before) + len(after))

    def interleave(steps, chunks):
        every = -(-len(steps) // max(len(chunks), 1))
        for n, step in enumerate(steps):
            step()
            if (n + 1) % every == 0 and chunks:
                chunks.pop(0)()
        while chunks:
            chunks.pop(0)()

    interleave(before, rest[:n_before])
    carry[...] = lax.fori_loop(0, ng, across, carry[...])
    interleave(after, rest[n_before:])


def _block_diag_tiles(w, gw):
    nb, bw, _ = w.shape
    per = gw // bw
    w4 = w.reshape(nb // per, per, bw, bw)
    eye = jnp.eye(per, dtype=w.dtype)
    return jnp.einsum("gpij,pq->gpiqj", w4, eye).reshape(nb // per, gw, gw)


def _mixin(x2, g, sc, sh, w_bf, b_in, conv_w, conv_b, wa, ba, wx, bx, lam, S):
    N, D = x2.shape
    C = D_RNN
    tt = _tiles()["tok"]
    gw = _tiles()["rnn_group"]
    per_b = S // tt
    chunk = 2 * V7X_MXU_DIM
    vec = lambda: pl.BlockSpec((1, C), lambda i: (0, 0))
    bvec = lambda: pl.BlockSpec((1, 1, D), lambda i: (i // per_b, 0, 0))
    gate_w = lambda: pl.BlockSpec((C // gw, gw, gw), lambda i: (0, 0, 0))
    slabs = lambda rows: pltpu.VMEM((C // V7X_LANES, rows, V7X_LANES), F32)
    groups = lambda: pltpu.VMEM((tt // V7X_SUBLANES, C), F32)
    vmem = D * D_IN * 2 + 2 * tt * (D * 4 + D_REST * 2 + C * 2) + tt * C * (2 + 5 * 4) + 8 * tt * chunk * 4
    return pl.pallas_call(
        functools.partial(_mixin_kernel, per_b=per_b, gw=gw, chunk=chunk),
        grid=(N // tt,),
        in_specs=[
            pl.BlockSpec((tt, D), lambda i: (i, 0)),
            pl.BlockSpec((1, D), lambda i: (0, 0)),
            bvec(), bvec(),
            pl.BlockSpec((D, D_IN), lambda i: (0, 0), pipeline_mode=pl.Buffered(1)),
            pl.BlockSpec((1, D_IN), lambda i: (0, 0)),
            pl.BlockSpec((CONV_W, C), lambda i: (0, 0)),
            vec(), gate_w(), vec(), gate_w(), vec(), vec(),
        ],
        out_specs=[
            pl.BlockSpec((tt, D_REST), lambda i: (i, 0)),
            pl.BlockSpec((tt, C), lambda i: (i, 0)),
        ],
        out_shape=[
            jax.ShapeDtypeStruct((N, D_REST), BF16),
            jax.ShapeDtypeStruct((N, C), BF16),
        ],
        scratch_shapes=[
            pltpu.VMEM((tt, D), BF16),
            slabs(tt + V7X_SUBLANES),
            pltpu.VMEM((tt, C), F32),
            slabs(tt), slabs(tt), slabs(tt),
            groups(), groups(), groups(),
            pltpu.VMEM((1, C), F32),
        ],
        compiler_params=pltpu.CompilerParams(
            dimension_semantics=("arbitrary",),
            vmem_limit_bytes=_vmem_limit(vmem)),
        name="mixin",
    )(x2, g, sc, sh, w_bf, b_in, conv_w, conv_b, wa, ba, wx, bx, lam)


def _alibi_slopes():
    return [2.0 ** (-8.0 * (h + 1) / N_HEADS) for h in range(N_HEADS)]


def _attn_kernel(sink_ref, q_ref, kp_ref, kc_ref, vp_ref, vc_ref, o_ref, bias_s, *, per_b):
    first = pl.program_id(0) % per_b == 0
    bq = kp_ref.shape[0]
    slopes = _alibi_slopes()

    @pl.when(pl.program_id(0) == 0)
    def _():
        qi = lax.broadcasted_iota(jnp.int32, (bq, 2 * bq), 0)
        ci = lax.broadcasted_iota(jnp.int32, (bq, 2 * bq), 1)
        dist = qi + bq - ci
        valid = (dist >= 0) & (dist < WINDOW)
        distf = dist.astype(F32)
        for h in range(N_HEADS):
            b = jnp.where(valid, -slopes[h] * distf, -jnp.inf)
            bias_s[0, h] = jnp.where(ci >= bq, b, -jnp.inf)
            bias_s[1, h] = b

    assert 2 * HEAD_DIM == V7X_LANES and GROUP % 2 == 0
    low = lax.broadcasted_iota(jnp.int32, (bq, V7X_LANES), 1) < HEAD_DIM
    zero = jnp.zeros((bq, V7X_LANES), BF16)
    for sub in range(q_ref.shape[0] // bq):
        rows = slice(sub * bq, (sub + 1) * bq)
        before = slice((sub - 1) * bq, sub * bq)
        table = jnp.where(first, 0, 1) if sub == 0 else 1
        for kvt in range(N_KV // 2):
            lt = slice(kvt * V7X_LANES, (kvt + 1) * V7X_LANES)
            k_prev, v_prev = (kp_ref[:, lt], vp_ref[:, lt]) if sub == 0 else (kc_ref[before, lt], vc_ref[before, lt])
            k_t = jnp.concatenate([k_prev, kc_ref[rows, lt]], axis=0)
            v_t = jnp.concatenate([v_prev, vc_ref[rows, lt]], axis=0)
            k_swapped = pltpu.roll(k_t, HEAD_DIM, 1)
            for kv_half in range(2):
                kv = 2 * kvt + kv_half
                for pair in range(GROUP // 2):
                    h0 = kv * GROUP + 2 * pair
                    tile = slice((h0 // 2) * V7X_LANES, (h0 // 2 + 1) * V7X_LANES)
                    q_t = q_ref[rows, tile]
                    halves = []
                    for q_half in range(2):
                        h = h0 + q_half
                        qm = jnp.where(low if q_half == 0 else ~low, q_t, zero)
                        kk = k_t if q_half == kv_half else k_swapped
                        s = lax.dot_general(qm, kk, (((1,), (1,)), ((), ())),
                                            preferred_element_type=F32)
                        s = s + bias_s[table, h]
                        sink = sink_ref[h]
                        m = jnp.maximum(jnp.max(s, axis=-1, keepdims=True), sink)
                        p = jnp.exp(s - m)
                        denom = jnp.sum(p, axis=-1, keepdims=True) + jnp.exp(sink - m)
                        halves.append(jnp.dot(p.astype(BF16), v_t, preferred_element_type=F32) / denom)
                    if kv_half == 0:
                        o = jnp.where(low, halves[0], pltpu.roll(halves[1], HEAD_DIM, 1))
                    else:
                        o = jnp.where(low, pltpu.roll(halves[0], HEAD_DIM, 1), halves[1])
                    o_ref[rows, tile] = o.astype(BF16)


def _attn(proj_rest, sinks, S):
    N = proj_rest.shape[0]
    bq = _tiles()["attn_q"]
    nsub = _tiles()["attn_blocks"]
    tq = nsub * bq
    per_b = S // tq
    k_col = (Q_DIM + 2 * D_MODEL) // KV_DIM
    v_col = k_col + 1

    def prev(i):
        return jnp.where(i % per_b == 0, i * nsub, i * nsub - 1)

    return pl.pallas_call(
        functools.partial(_attn_kernel, per_b=per_b),
        grid=(N // tq,),
        in_specs=[
            pl.BlockSpec(memory_space=pltpu.SMEM),
            pl.BlockSpec((tq, Q_DIM), lambda i: (i, 0)),
            pl.BlockSpec((bq, KV_DIM), lambda i: (prev(i), k_col)),
            pl.BlockSpec((tq, KV_DIM), lambda i: (i, k_col)),
            pl.BlockSpec((bq, KV_DIM), lambda i: (prev(i), v_col)),
            pl.BlockSpec((tq, KV_DIM), lambda i: (i, v_col)),
        ],
        out_specs=pl.BlockSpec((tq, Q_DIM), lambda i: (i, 0)),
        out_shape=jax.ShapeDtypeStruct((N, Q_DIM), BF16),
        scratch_shapes=[pltpu.VMEM((2, N_HEADS, bq, 2 * bq), F32)],
        compiler_params=pltpu.CompilerParams(
            dimension_semantics=("arbitrary",),
            vmem_limit_bytes=_vmem_limit(3 * 2 * N_HEADS * bq * 2 * bq * 4)),
        name="attn",
    )(sinks, proj_rest, proj_rest, proj_rest, proj_rest, proj_rest)


def _merge_kernel(x_ref, yr_ref, ya_ref, gr_ref, ga_ref, wr_ref, wa_ref, wo_ref,
                  gpost_ref, g1_ref, gpre_ref, sc2_ref, sh2_ref, rwt_ref, rb_ref,
                  x1_ref, h2_ref, lg_ref):
    r = jnp.dot(yr_ref[...], wr_ref[...], preferred_element_type=F32)
    a = jnp.dot(ya_ref[...], wa_ref[...], preferred_element_type=F32)
    merged = ((0.5 * jnp.tanh(gr_ref[...].astype(F32)) + 0.5) * r
              + (0.5 * jnp.tanh(ga_ref[...].astype(F32)) + 0.5) * a)
    mix = jnp.dot(merged.astype(BF16), wo_ref[...], preferred_element_type=F32)
    x1 = x_ref[...] + g1_ref[0] * _rms(mix, gpost_ref[...])
    x1_ref[...] = x1
    h2 = (_rms(x1, gpre_ref[...]) * (1.0 + sc2_ref[0]) + sh2_ref[0]).astype(BF16)
    h2_ref[...] = h2
    lg = lax.dot_general(rwt_ref[...], h2, (((1,), (1,)), ((), ())),
                         preferred_element_type=F32)
    lg_ref[...] = lg + rb_ref[...]


def _merge(x2, y_rnn, y_att, proj_rest, wr, wa, wo, gpost, g1, gpre, sc2, sh2, rwt, rb, S):
    N, D = x2.shape
    tm = _tiles()["tok"]
    per_b = S // tm
    gate_r_col = Q_DIM // D
    mat = lambda: pl.BlockSpec((D, D), lambda i: (0, 0))
    vec = lambda: pl.BlockSpec((1, D), lambda i: (0, 0))
    bvec = lambda: pl.BlockSpec((1, 1, D), lambda i: (i // per_b, 0, 0))
    tile = lambda col=0: pl.BlockSpec((tm, D), lambda i: (i, col))
    return pl.pallas_call(
        _merge_kernel,
        grid=(N // tm,),
        in_specs=[
            tile(), tile(), tile(), tile(gate_r_col), tile(gate_r_col + 1),
            mat(), mat(), mat(),
            vec(), bvec(), vec(), bvec(), bvec(),
            pl.BlockSpec((N_EXPERTS, D), lambda i: (0, 0)),
            pl.BlockSpec((N_EXPERTS, 1), lambda i: (0, 0)),
        ],
        out_specs=[
            tile(), tile(),
            pl.BlockSpec((N_EXPERTS, tm), lambda i: (0, i)),
        ],
        out_shape=[
            jax.ShapeDtypeStruct((N, D), F32),
            jax.ShapeDtypeStruct((N, D), BF16),
            jax.ShapeDtypeStruct((N_EXPERTS, N), F32),
        ],
        compiler_params=pltpu.CompilerParams(
            dimension_semantics=("arbitrary",),
            vmem_limit_bytes=_vmem_limit(6 * D * D * 2 + 24 * tm * D * 4)),
        name="merge",
    )(x2, y_rnn, y_att, proj_rest, proj_rest, wr, wa, wo, gpost, g1, gpre, sc2, sh2, rwt, rb)


def _route_kernel(lg_ref, g_ref, slot_ref, cnt_ref, tcnt_ref, tcar_ref, tlst_ref, carry, *, T):
    i = pl.program_id(0)
    E = lg_ref.shape[0]
    n_sub = lg_ref.shape[1] // T

    @pl.when(i == 0)
    def _():
        carry[...] = jnp.zeros_like(carry)
        tcnt_ref[...] = jnp.zeros_like(tcnt_ref)
        tcar_ref[...] = jnp.zeros_like(tcar_ref)
        tlst_ref[...] = jnp.zeros_like(tlst_ref)

    row = lax.broadcasted_iota(jnp.int32, (E, T), 0).astype(F32)
    tri_t = (lax.broadcasted_iota(jnp.int32, (T, T), 0)
             < lax.broadcasted_iota(jnp.int32, (T, T), 1)).astype(BF16)
    tri_e = (lax.broadcasted_iota(jnp.int32, (E, E), 1)
             < lax.broadcasted_iota(jnp.int32, (E, E), 0)).astype(BF16)
    tile_lane = lax.broadcasted_iota(jnp.int32, tcnt_ref.shape, 1)
    running = carry[...]
    for sub in range(n_sub):
        cols = slice(sub * T, (sub + 1) * T)
        l = lg_ref[:, cols]
        vals, idxs = [], []
        for _ in range(TOP_K):
            m = jnp.max(l, axis=0, keepdims=True)
            idx = jnp.min(jnp.where(l == m, row, float(E)), axis=0, keepdims=True)
            vals.append(m)
            idxs.append(idx)
            l = jnp.where(row == idx, -jnp.inf, l)
        ex = [jnp.exp(v - vals[0]) for v in vals]
        tot = ex[0]
        for e in ex[1:]:
            tot = tot + e
        g_ref[:, cols] = jnp.concatenate([e / tot for e in ex], axis=0)

        hot = [row == idx for idx in idxs]
        onehot = jnp.zeros((E, T), F32)
        for hk in hot:
            onehot = onehot + hk.astype(F32)
        before = jnp.dot(onehot.astype(BF16), tri_t, preferred_element_type=F32)
        cnt = jnp.sum(onehot, axis=1, keepdims=True)
        lstart = jnp.dot(tri_e, jnp.broadcast_to(cnt, (E, V7X_LANES)).astype(BF16),
                         preferred_element_type=F32)[:, 0:1]
        local = before + lstart
        slots = [jnp.sum(jnp.where(hk, local, 0.0), axis=0, keepdims=True) for hk in hot]
        slot_ref[:, cols] = jnp.concatenate(slots, axis=0).astype(jnp.int32)

        mine = tile_lane == i * n_sub + sub
        tcnt_ref[...] = jnp.where(mine, cnt.astype(jnp.int32), tcnt_ref[...])
        tcar_ref[...] = jnp.where(mine, running.astype(jnp.int32), tcar_ref[...])
        tlst_ref[...] = jnp.where(mine, lstart.astype(jnp.int32), tlst_ref[...])
        running = running + cnt
    carry[...] = running
    cnt_ref[...] = running.astype(jnp.int32)


def _route(logits_t):
    E, N = logits_t.shape
    T = _tiles()["moe_t"]
    TS = T * _tiles()["route_tiles"]
    assert T <= 2 ** 8 and N // T <= V7X_LANES
    out = lambda: pl.BlockSpec((TOP_K, TS), lambda i: (0, i))
    per_tile = lambda: pl.BlockSpec((E, V7X_LANES), lambda i: (0, 0))
    return pl.pallas_call(
        functools.partial(_route_kernel, T=T),
        grid=(N // TS,),
        in_specs=[pl.BlockSpec((E, TS), lambda i: (0, i))],
        out_specs=[out(), out(), pl.BlockSpec((E, 1), lambda i: (0, 0)),
                   per_tile(), per_tile(), per_tile()],
        out_shape=[
            jax.ShapeDtypeStruct((TOP_K, N), F32),
            jax.ShapeDtypeStruct((TOP_K, N), jnp.int32),
            jax.ShapeDtypeStruct((E, 1), jnp.int32),
            jax.ShapeDtypeStruct((E, V7X_LANES), jnp.int32),
            jax.ShapeDtypeStruct((E, V7X_LANES), jnp.int32),
            jax.ShapeDtypeStruct((E, V7X_LANES), jnp.int32),
        ],
        scratch_shapes=[pltpu.VMEM((E, 1), F32)],
        compiler_params=pltpu.CompilerParams(dimension_semantics=("arbitrary",)),
        name="route",
    )(logits_t)


def _run_sizes(T):
    return [2 ** b for b in range(int(math.log2(T)), -1, -1)]


def _copy_plan(tile_cnt, tile_lst, tile_dst, T):
    sizes = jnp.asarray(_run_sizes(T), jnp.int32)[None, :, None]
    cnt, lst, dst = (a[:, None, :] for a in (tile_cnt, tile_lst, tile_dst))
    has = (cnt & sizes) != 0
    off = (cnt // (2 * sizes)) * (2 * sizes)
    eids = jnp.arange(N_EXPERTS)
    earlier = eids[:, None] < eids[None, :]
    place = jnp.sum(jnp.where(earlier, has[..., :, None], False), axis=-2)
    hit = has[..., :, None] & (place[..., :, None] == eids)
    pick = lambda v: jnp.sum(jnp.where(hit, v[..., :, None], 0), axis=-2)
    n_tiles, nb = tile_cnt.shape[0], sizes.shape[1]
    parts = [pick(jnp.broadcast_to(lst + off, has.shape)).reshape(n_tiles, -1),
             pick(jnp.broadcast_to(dst + off, has.shape)).reshape(n_tiles, -1),
             jnp.sum(has, axis=-1).reshape(n_tiles, nb)]
    plan = jnp.concatenate(parts, axis=1).astype(jnp.int32)
    pad = _plan_len(T) - plan.shape[1]
    return jnp.pad(plan, ((0, 0), (0, pad))).reshape(-1)


def _plan_len(T):
    nb = len(_run_sizes(T))
    return max(V7X_LANES, pl.next_power_of_2(2 * nb * N_EXPERTS + nb))


def _run_copies(plan_ref, T, make):
    sizes = _run_sizes(T)
    nb = len(sizes)
    for bi, b in enumerate(sizes):
        def body(j, c, bi=bi, b=b):
            make(plan_ref[bi * N_EXPERTS + j], plan_ref[(nb + bi) * N_EXPERTS + j], b).start()
            return c

        lax.fori_loop(0, plan_ref[2 * nb * N_EXPERTS + bi], body, 0)


def _row_slab(ref, row, nrows, lead=()):
    rows = pl.ds(pl.multiple_of(row * V7X_SUBLANES, V7X_SUBLANES), nrows * V7X_SUBLANES)
    return ref.at[(*lead, rows)]


def _to_slabs(ref, val, lead=()):
    rows = val.shape[0]
    for s in range(V7X_SUBLANES):
        ref[(*lead, pl.ds(s, rows, stride=V7X_SUBLANES), slice(None))] = (
            val[:, s * V7X_LANES:(s + 1) * V7X_LANES])


def _from_slabs(ref, lead=(), rows=None):
    rows = ref.shape[-2] // V7X_SUBLANES if rows is None else rows
    return jnp.concatenate(
        [ref[(*lead, pl.ds(s, rows, stride=V7X_SUBLANES), slice(None))]
         for s in range(V7X_SUBLANES)], axis=1)


def _dispatch_kernel(tail_ref, nu_ref, plan_ref, slot_ref, h_ref, xs_ref, buf, zbuf, sem, zsem):
    i = pl.program_id(0)
    T = h_ref.shape[0]
    bm = zbuf.shape[0] // V7X_SUBLANES
    nblk = xs_ref.shape[0] // zbuf.shape[0]
    cur = i % 2

    def zero_copy(row0, s):
        return pltpu.make_async_copy(zbuf, _row_slab(xs_ref, row0, bm), zsem.at[s])

    def read_chunk(e, fn):
        @pl.when(tail_ref[e] >= 0)
        def _():
            fn(zero_copy(tail_ref[e], 0))

    def unread_chunks(fn):
        def per_expert(e, c):
            @pl.when(jnp.logical_and(tail_ref[e] >= 0, tail_ref[e] % (2 * bm) == 0))
            def _():
                fn(zero_copy(tail_ref[e] + bm, 1))
            return c

        def per_unused(j, c):
            fn(zero_copy(j * bm, 1))
            return c

        lax.fori_loop(0, N_EXPERTS, per_expert, 0)
        lax.fori_loop(nu_ref[0], nblk, per_unused, 0)

    @pl.when(i == 0)
    def _():
        zbuf[...] = jnp.zeros_like(zbuf)

        def start(e, c):
            read_chunk(e, lambda cp: cp.start())
            return c

        def wait(e, c):
            read_chunk(e, lambda cp: cp.wait())
            return c

        lax.fori_loop(0, N_EXPERTS, start, 0)
        unread_chunks(lambda cp: cp.start())
        lax.fori_loop(0, N_EXPERTS, wait, 0)

    slot = slot_ref[...]
    r_id = lax.broadcasted_iota(jnp.int32, (TOP_K * T, T), 0)
    perm = r_id == slot[0:1, :]
    for k in range(1, TOP_K):
        perm = jnp.logical_or(perm, r_id == slot[k:k + 1, :])
    grouped = jnp.dot(perm.astype(BF16), h_ref[...], preferred_element_type=F32)
    _to_slabs(buf, grouped, lead=(cur,))

    def wait_tile(b):
        pltpu.make_async_copy(buf.at[b], _row_slab(xs_ref, 0, TOP_K * T), sem.at[b]).wait()

    @pl.when(i > 0)
    def _():
        wait_tile(1 - cur)

    _run_copies(plan_ref, T, lambda l, d, n: pltpu.make_async_copy(
        _row_slab(buf, l, n, lead=(cur,)), _row_slab(xs_ref, d, n), sem.at[cur]))

    @pl.when(i == pl.num_programs(0) - 1)
    def _():
        wait_tile(cur)
        unread_chunks(lambda cp: cp.wait())


def _dispatch(h2, slot, plan, tail_start, n_used, n_rows, bm):
    N, D = h2.shape
    T = _tiles()["moe_t"]
    S8 = V7X_SUBLANES
    imap = lambda f: (lambda i, *_: f(i))
    return pl.pallas_call(
        _dispatch_kernel,
        grid_spec=pltpu.PrefetchScalarGridSpec(
            num_scalar_prefetch=2,
            grid=(N // T,),
            in_specs=[
                pl.BlockSpec((_plan_len(T),), imap(lambda i: (i,)), memory_space=pltpu.SMEM),
                pl.BlockSpec((TOP_K, T), imap(lambda i: (0, i))),
                pl.BlockSpec((T, D), imap(lambda i: (i, 0))),
            ],
            out_specs=pl.BlockSpec(memory_space=pl.ANY),
            scratch_shapes=[
                pltpu.VMEM((2, TOP_K * T * S8, V7X_LANES), F32),
                pltpu.VMEM((bm * S8, V7X_LANES), F32),
                pltpu.SemaphoreType.DMA((2,)),
                pltpu.SemaphoreType.DMA((2,)),
            ],
        ),
        out_shape=jax.ShapeDtypeStruct((n_rows * S8, V7X_LANES), F32),
        compiler_params=pltpu.CompilerParams(
            dimension_semantics=("arbitrary",),
            vmem_limit_bytes=_vmem_limit(8 * TOP_K * T * D * 4)),
        name="dispatch",
    )(tail_start, n_used, plan, slot, h2)


def _ffn_kernel(be_ref, bv_ref, bf_ref, nu_ref, nxt_ref, x_ref, w1_hbm, b1g_ref, b1l_ref, w2_hbm,
                b2_ref, y_ref, w1f_s, w2f_s, w1_s, w2_s, sem):
    i = pl.program_id(0)
    bm, D = x_ref.shape[0] // V7X_SUBLANES, w2_s.shape[1]
    n_ff = w2_s.shape[0]
    pw = 2 * V7X_LANES

    def fetch(e):
        return (pltpu.make_async_copy(w1_hbm.at[e], w1f_s, sem.at[0]),
                pltpu.make_async_copy(w2_hbm.at[e], w2f_s, sem.at[1]))

    @pl.when(i == 0)
    def _():
        for cp in fetch(be_ref[0]):
            cp.start()

    @pl.when(jnp.logical_and(i < nu_ref[0], bf_ref[i] == 1))
    def _():
        e = be_ref[i]
        for cp in fetch(e):
            cp.wait()
        src = lax.broadcasted_iota(jnp.int32, (pw, pw), 0)
        dst = lax.broadcasted_iota(jnp.int32, (pw, pw), 1)
        want = jnp.where(dst < V7X_LANES, 2 * dst, 2 * (dst - V7X_LANES) + 1)
        perm = (src == want).astype(BF16)
        for c in range(w1f_s.shape[1] // pw):
            wp = jnp.dot(w1f_s[:, c * pw:(c + 1) * pw].astype(BF16), perm,
                         preferred_element_type=F32).astype(BF16)
            w1_s[:, c * V7X_LANES:(c + 1) * V7X_LANES] = wp[:, :V7X_LANES]
            w1_s[:, n_ff + c * V7X_LANES:n_ff + (c + 1) * V7X_LANES] = wp[:, V7X_LANES:]
        w2_s[...] = w2f_s[...].astype(BF16)

        @pl.when(nxt_ref[e] >= 0)
        def _():
            for cp in fetch(nxt_ref[e]):
                cp.start()

    def mlp(n_rows):
        x = _from_slabs(x_ref, rows=n_rows).astype(BF16)
        hid = jnp.dot(x, w1_s[...], preferred_element_type=F32)
        glu = hid[:, :n_ff] + b1g_ref[0]
        lin = hid[:, n_ff:] + b1l_ref[0]
        glu = jnp.minimum(glu, SWIGLU_LIMIT)
        lin = jnp.clip(lin, -SWIGLU_LIMIT, SWIGLU_LIMIT)
        act = glu * _sigmoid(SWIGLU_ALPHA * glu) * (lin + 1.0)
        _to_slabs(y_ref, jnp.dot(act.astype(BF16), w2_s[...], preferred_element_type=F32) + b2_ref[0])
        if n_rows < bm:
            y_ref[n_rows * V7X_SUBLANES:, :] = jnp.zeros(
                ((bm - n_rows) * V7X_SUBLANES, V7X_LANES), F32)

    used = i < nu_ref[0]
    half = bm // 2

    @pl.when(jnp.logical_and(used, bv_ref[i] > half))
    def _():
        mlp(bm)

    @pl.when(jnp.logical_and(used, bv_ref[i] <= half))
    def _():
        mlp(half)

    @pl.when(i >= nu_ref[0])
    def _():
        y_ref[...] = jnp.zeros_like(y_ref)


def _ffn(xs, blk_e, blk_valid, blk_first, n_used, next_e, w1, b1g, b1l, w2, b2, bm):
    E, D, F2 = w1.shape
    F = F2 // 2
    slab = bm * V7X_SUBLANES
    nblk = xs.shape[0] // slab

    def row_blk(i, be, bv, bf, nu, nxt):
        return (jnp.minimum(i, nu[0] - 1), 0)

    def per_e(i, be, bv, bf, nu, nxt):
        return (be[i], 0, 0)

    vmem = (D * F2 + F * D) * (4 + 2) + 8 * bm * F2 * 4
    return pl.pallas_call(
        _ffn_kernel,
        grid_spec=pltpu.PrefetchScalarGridSpec(
            num_scalar_prefetch=5,
            grid=(nblk,),
            in_specs=[
                pl.BlockSpec((slab, V7X_LANES), row_blk),
                pl.BlockSpec(memory_space=pl.ANY),
                pl.BlockSpec((1, 1, F), per_e),
                pl.BlockSpec((1, 1, F), per_e),
                pl.BlockSpec(memory_space=pl.ANY),
                pl.BlockSpec((1, 1, D), per_e),
            ],
            out_specs=pl.BlockSpec((slab, V7X_LANES), lambda i, be, bv, bf, nu, nxt: (i, 0)),
            scratch_shapes=[
                pltpu.VMEM((D, F2), F32),
                pltpu.VMEM((F, D), F32),
                pltpu.VMEM((D, F2), BF16),
                pltpu.VMEM((F, D), BF16),
                pltpu.SemaphoreType.DMA((2,)),
            ],
        ),
        out_shape=jax.ShapeDtypeStruct(xs.shape, F32),
        compiler_params=pltpu.CompilerParams(
            dimension_semantics=("arbitrary",),
            vmem_limit_bytes=_vmem_limit(vmem)),
        name="ffn",
    )(blk_e, blk_valid, blk_first, n_used, next_e, xs, w1, b1g, b1l, w2, b2)


def _combine_kernel(plan_ref, plan_next_ref, ys_ref, x1_ref, slot_ref, gate_ref, gpost_ref, g2_ref,
                    o_ref, buf, sem):
    i = pl.program_id(0)
    T = x1_ref.shape[0]
    cur = i % 2

    def gather(plan, b):
        _run_copies(plan, T, lambda l, d, n: pltpu.make_async_copy(
            _row_slab(ys_ref, d, n), _row_slab(buf, l, n, lead=(b,)), sem.at[b]))

    @pl.when(i == 0)
    def _():
        gather(plan_ref, cur)

    @pl.when(i + 1 < pl.num_programs(0))
    def _():
        gather(plan_next_ref, 1 - cur)

    pltpu.make_async_copy(_row_slab(ys_ref, 0, TOP_K * T), buf.at[cur], sem.at[cur]).wait()

    slot = slot_ref[...]
    gates = gate_ref[...]
    r_id = lax.broadcasted_iota(jnp.int32, (T, TOP_K * T), 1)
    w = jnp.zeros((T, TOP_K * T), F32)
    for k in range(TOP_K):
        w = jnp.where(r_id == slot[:, k:k + 1], gates[:, k:k + 1], w)
    ff = jnp.dot(w.astype(BF16), _from_slabs(buf, lead=(cur,)).astype(BF16),
                 preferred_element_type=F32)
    o_ref[...] = x1_ref[...] + g2_ref[0] * _rms(ff, gpost_ref[...])


def _combine(ys, plan, x1, slot_tk, gates_tk, gpost, g2, S):
    N, D = x1.shape
    T = _tiles()["moe_t"]
    per_b = S // T
    n_tiles = N // T
    plan_spec = lambda f: pl.BlockSpec((_plan_len(T),), f, memory_space=pltpu.SMEM)
    return pl.pallas_call(
        _combine_kernel,
        grid=(n_tiles,),
        in_specs=[
            plan_spec(lambda i: (i,)),
            plan_spec(lambda i: (jnp.minimum(i + 1, n_tiles - 1),)),
            pl.BlockSpec(memory_space=pl.ANY),
            pl.BlockSpec((T, D), lambda i: (i, 0)),
            pl.BlockSpec((T, TOP_K), lambda i: (i, 0)),
            pl.BlockSpec((T, TOP_K), lambda i: (i, 0)),
            pl.BlockSpec((1, D), lambda i: (0, 0)),
            pl.BlockSpec((1, 1, D), lambda i: (i // per_b, 0, 0)),
        ],
        out_specs=pl.BlockSpec((T, D), lambda i: (i, 0)),
        out_shape=jax.ShapeDtypeStruct((N, D), F32),
        scratch_shapes=[
            pltpu.VMEM((2, TOP_K * T * V7X_SUBLANES, V7X_LANES), F32),
            pltpu.SemaphoreType.DMA((2,)),
        ],
        compiler_params=pltpu.CompilerParams(
            dimension_semantics=("arbitrary",),
            vmem_limit_bytes=_vmem_limit(10 * TOP_K * T * D * 4)),
        name="combine",
    )(plan, plan, ys, x1, slot_tk, gates_tk, gpost, g2)


def _col_scaled(w):
    scale = HEAD_DIM ** -0.5
    assert math.log2(scale).is_integer()
    col = np.arange(D_IN)
    q0, g0 = 2 * D_RNN, 2 * D_RNN + Q_DIM + 2 * KV_DIM
    factor = np.where((col >= q0) & (col < q0 + Q_DIM), scale, np.where(col >= g0, 0.5, 1.0))
    return w * jnp.asarray(factor, w.dtype)


def _layer(x2, c, B, S, p):
    D = D_MODEL
    N = B * S
    ada = _ada(c, p["w_ada"], p["b_ada"])
    sh1, sc1, g1, sh2, sc2, g2 = [a.reshape(B, 1, D) for a in jnp.split(ada, 6, axis=-1)]
    row = lambda v: v.reshape(1, -1)

    gw = _tiles()["rnn_group"]
    proj_rest, y_rnn = _mixin(
        x2, row(p["norm_pre_mix"]), sc1, sh1,
        _col_scaled(p["w_in"]).astype(BF16), _col_scaled(row(p["b_in"])),
        p["conv_w"], row(p["conv_b"]),
        (0.5 * _block_diag_tiles(p["rg_w_a"], gw)).astype(BF16), 0.5 * row(p["rg_b_a"]),
        (0.5 * _block_diag_tiles(p["rg_w_x"], gw)).astype(BF16), 0.5 * row(p["rg_b_x"]),
        row(p["rg_lambda"]), S)
    y_att = _attn(proj_rest, p["attn_sinks"], S)

    x1, h2, logits_t = _merge(
        x2, y_rnn, y_att, proj_rest,
        p["w_o_rnn"].astype(BF16), p["w_o_attn"].astype(BF16), p["w_out"].astype(BF16),
        row(p["norm_post_mix"]), g1, row(p["norm_pre_ffn"]), sc2, sh2,
        p["router_w"].T.astype(BF16), p["router_b"].reshape(N_EXPERTS, 1), S)

    gates, slot, counts, tile_cnt, tile_car, tile_lst = _route(logits_t)

    bm = _tiles()["moe_bm"]
    n_tiles = N // _tiles()["moe_t"]
    n_rows = N * TOP_K + N_EXPERTS * bm
    nblk = n_rows // bm
    counts = counts.reshape(N_EXPERTS)
    padded = ((counts + bm - 1) // bm) * bm
    pend = jnp.cumsum(padded)
    pstart = pend - padded
    eids = jnp.arange(N_EXPERTS, dtype=jnp.int32)
    per_run = lambda a: a[:, :n_tiles].T.astype(jnp.int32)
    plan = _copy_plan(per_run(tile_cnt), per_run(tile_lst), per_run(tile_car + pstart[:, None]),
                      _tiles()["moe_t"])
    zr = bm // 2
    tail_start = jnp.where(counts > 0, (pstart + counts - 1) // zr * zr, -1).astype(jnp.int32)
    n_used = (pend[-1] // bm).astype(jnp.int32).reshape(1)
    blk_row0 = jnp.minimum(jnp.arange(nblk, dtype=jnp.int32), n_used[0] - 1) * bm
    blk_e = jnp.sum(blk_row0[:, None] >= pend[None, :], axis=1).astype(jnp.int32)
    mine = blk_e[:, None] == eids[None, :]
    blk_cnt = jnp.sum(jnp.where(mine, counts, 0), axis=1)
    blk_pstart = jnp.sum(jnp.where(mine, pstart, 0), axis=1)
    blk_valid = jnp.clip(blk_cnt - (blk_row0 - blk_pstart), 0, bm).astype(jnp.int32)
    blk_first = (blk_row0 == blk_pstart).astype(jnp.int32)

    xs = _dispatch(h2, slot, plan, tail_start, n_used * (bm // zr), n_rows, zr)
    b1 = p["moe_b1"].reshape(N_EXPERTS, D_FF, 2)
    later = jnp.logical_and(counts[None, :] > 0, eids[None, :] > eids[:, None])
    next_e = jnp.min(jnp.where(later, eids[None, :], N_EXPERTS), axis=1)
    next_e = jnp.where(next_e == N_EXPERTS, -1, next_e).astype(jnp.int32)
    ys = _ffn(xs, blk_e, blk_valid, blk_first, n_used, next_e, p["moe_w1"],
              b1[:, :, 0].reshape(N_EXPERTS, 1, D_FF), b1[:, :, 1].reshape(N_EXPERTS, 1, D_FF),
              p["moe_w2"], p["moe_b2"].reshape(N_EXPERTS, 1, D), bm)
    return _combine(ys, plan, x1, slot.T, gates.T, row(p["norm_post_ffn"]), g2, S)


def kernel(x, c, w_ada, b_ada, norm_pre_mix, norm_post_mix, norm_pre_ffn, norm_post_ffn, w_in, b_in, conv_w, conv_b, rg_w_a, rg_b_a, rg_w_x, rg_b_x, rg_lambda, attn_sinks, w_o_rnn, w_o_attn, w_out, router_w, router_b, moe_w1, moe_b1, moe_w2, moe_b2):
    B, S, D = x.shape
    params = dict(
        w_ada=w_ada, b_ada=b_ada, norm_pre_mix=norm_pre_mix, norm_post_mix=norm_post_mix,
        norm_pre_ffn=norm_pre_ffn, norm_post_ffn=norm_post_ffn, w_in=w_in, b_in=b_in,
        conv_w=conv_w, conv_b=conv_b, rg_w_a=rg_w_a, rg_b_a=rg_b_a, rg_w_x=rg_w_x, rg_b_x=rg_b_x,
        rg_lambda=rg_lambda, attn_sinks=attn_sinks, w_o_rnn=w_o_rnn, w_o_attn=w_o_attn,
        w_out=w_out, router_w=router_w, router_b=router_b, moe_w1=moe_w1, moe_b1=moe_b1,
        moe_w2=moe_w2, moe_b2=moe_b2)
    x2 = x.reshape(B * S, D)
    for layer in range(w_ada.shape[0]):
        x2 = _layer(x2, c, B, S, {k: v[layer] for k, v in params.items()})
    return x2.reshape(B, S, D)
```

```python
import functools
import math

import jax
import jax.numpy as jnp
import numpy as np
from jax import lax
from jax.experimental import pallas as pl
from jax.experimental.pallas import tpu as pltpu

D_MODEL = 1024
D_RNN = 1024
RNN_BLOCKS = 16
RNN_BW = D_RNN // RNN_BLOCKS
CONV_W = 4
LRU_C = 8.0
N_HEADS = 16
N_KV = 4
HEAD_DIM = 64
GROUP = N_HEADS // N_KV
WINDOW = 128
Q_DIM = N_HEADS * HEAD_DIM
KV_DIM = N_KV * HEAD_DIM
N_EXPERTS = 32
TOP_K = 4
D_FF = 1024
SWIGLU_LIMIT = 7.0
SWIGLU_ALPHA = 1.702
EPS = 1e-6
D_IN = 2 * D_RNN + Q_DIM + 2 * KV_DIM + 2 * D_MODEL
D_REST = D_IN - 2 * D_RNN

V7X_LANES = 128
V7X_SUBLANES = 8
V7X_MXU_DIM = 256
V7X_VMEM_BYTES = 64 * 1024 * 1024

F32 = jnp.float32
BF16 = jnp.bfloat16


def _tiles():
    return dict(
        ada_tn=4 * V7X_MXU_DIM,
        tok=2 * V7X_MXU_DIM,
        attn_q=WINDOW,
        attn_blocks=2,
        moe_t=V7X_MXU_DIM,
        route_tiles=4,
        moe_bm=2 * V7X_MXU_DIM,
        rnn_group=V7X_MXU_DIM,
    )


V7X_VMEM_MIN_LIMIT = 16 * 1024 * 1024
V7X_VMEM_HEADROOM = 8 * 1024 * 1024


def _vmem_limit(nbytes):
    return int(min(max(nbytes, V7X_VMEM_MIN_LIMIT), V7X_VMEM_BYTES - V7X_VMEM_HEADROOM))


def _rms(x, g):
    return x * lax.rsqrt(jnp.mean(x * x, axis=-1, keepdims=True) + EPS) * g


def _ada_kernel(ct_ref, w_ref, b_ref, o_ref):
    ct = ct_ref[...]
    sc = ct * jax.nn.sigmoid(ct)
    w = w_ref[...]
    rows = [jnp.sum(w * sc[:, b:b + 1], axis=0, keepdims=True) for b in range(ct.shape[1])]
    o_ref[...] = jnp.concatenate(rows, axis=0) + b_ref[...]


def _ada(c, w_ada, b_ada):
    B, D = c.shape
    n_out = w_ada.shape[1]
    tn = _tiles()["ada_tn"]
    return pl.pallas_call(
        _ada_kernel,
        grid=(n_out // tn,),
        in_specs=[
            pl.BlockSpec((D, B), lambda j: (0, 0)),
            pl.BlockSpec((D, tn), lambda j: (0, j)),
            pl.BlockSpec((1, tn), lambda j: (0, j)),
        ],
        out_specs=pl.BlockSpec((B, tn), lambda j: (0, j)),
        out_shape=jax.ShapeDtypeStruct((B, n_out), F32),
        compiler_params=pltpu.CompilerParams(
            dimension_semantics=("arbitrary",),
            vmem_limit_bytes=_vmem_limit(4 * D * tn * 4)),
        name="ada",
    )(c.T, w_ada, b_ada.reshape(1, n_out))


def _gelu_tanh(x):
    return 0.5 * x * (1.0 + jnp.tanh(math.sqrt(2.0 / math.pi) * (x + 0.044715 * (x * x * x))))


def _softplus(z):
    return jnp.maximum(z, 0.0) + jnp.log1p(jnp.exp(-jnp.abs(z)))


def _sigmoid(x):
    return 0.5 * jnp.tanh(0.5 * x) + 0.5


def _rest_col(c):
    c -= 2 * D_RNN
    q_end, k_end, v_end = Q_DIM, Q_DIM + KV_DIM, Q_DIM + 2 * KV_DIM
    if c < q_end:
        return c
    if c < k_end:
        return Q_DIM + 2 * D_MODEL + (c - q_end)
    if c < v_end:
        return Q_DIM + 2 * D_MODEL + KV_DIM + (c - k_end)
    return Q_DIM + (c - v_end)


def _mixin_kernel(x_ref, g_ref, sc_ref, sh_ref, w_ref, b_ref, cw_ref, cb_ref, wa_ref, ba_ref,
                  wx_ref, bx_ref, lam_ref, rest_ref, y_ref,
                  hb_s, xbuf, gr_s, a_s, b_s, h_s, ga_s, gb_s, gc_s, carry, *, per_b, gw, chunk):
    t = pl.program_id(0) % per_b
    tt = x_ref.shape[0]
    halo = V7X_SUBLANES
    ng = tt // V7X_SUBLANES
    n_lt = a_s.shape[0]
    per_g = gw // V7X_LANES
    lanes = lambda j: slice(j * V7X_LANES, (j + 1) * V7X_LANES)
    slab = lambda j, r: (j, pl.ds(r, ng, stride=V7X_SUBLANES), slice(None))

    @pl.when(t == 0)
    def _():
        xbuf[:, 0:halo, :] = jnp.zeros((n_lt, halo, V7X_LANES), F32)
        carry[...] = jnp.zeros_like(carry)

    hb_s[...] = (_rms(x_ref[...], g_ref[...]) * (1.0 + sc_ref[0]) + sh_ref[0]).astype(BF16)

    def proj(c0):
        return (jnp.dot(hb_s[...], w_ref[:, c0:c0 + chunk], preferred_element_type=F32)
                + b_ref[:, c0:c0 + chunk])

    def proj_rest(c0):
        acc = proj(c0).astype(BF16)
        for p0 in range(0, chunk, KV_DIM):
            d0 = _rest_col(c0 + p0)
            rest_ref[:, d0:d0 + KV_DIM] = acc[:, p0:p0 + KV_DIM]

    for c0 in range(0, D_RNN, chunk):
        acc = proj(c0)
        for j in range(chunk // V7X_LANES):
            xbuf[c0 // V7X_LANES + j, halo:halo + tt, :] = acc[:, lanes(j)]
    for c0 in range(D_RNN, 2 * D_RNN, chunk):
        gr_s[:, c0 - D_RNN:c0 - D_RNN + chunk] = proj(c0)

    def conv(j):
        ls = lanes(j)
        taps = {o: xbuf[slab(j, halo + o)] for o in range(-(CONV_W - 1), V7X_SUBLANES)}
        for r in range(V7X_SUBLANES):
            acc = cb_ref[:, ls] + taps[r - (CONV_W - 1)] * cw_ref[0:1, ls]
            for kk in range(1, CONV_W):
                acc = acc + taps[r - (CONV_W - 1) + kk] * cw_ref[kk:kk + 1, ls]
            h_s[slab(j, r)] = acc
        xbuf[j, 0:halo, :] = xbuf[j, tt:tt + halo, :]

    def gates(g):
        cs = slice(g * gw, (g + 1) * gw)
        rate = (-LRU_C * math.log2(math.e)) * _softplus(-lam_ref[:, cs])
        reset = jnp.logical_and(t == 0, lax.broadcasted_iota(jnp.int32, (tt, gw), 0) == 0)
        xc = jnp.concatenate([h_s[g * per_g + j] for j in range(per_g)], axis=1)
        xg = xc.astype(BF16)
        gate_r = 0.5 * jnp.tanh(jnp.dot(xg, wa_ref[g], preferred_element_type=F32) + ba_ref[:, cs]) + 0.5
        gate_i = 0.5 * jnp.tanh(jnp.dot(xg, wx_ref[g], preferred_element_type=F32) + bx_ref[:, cs]) + 0.5
        a = jnp.exp2(gate_r * rate)
        v = (1.0 - a) * (1.0 + a)
        mult = jnp.where(reset, 1.0, jnp.where(v > 0.0, v * lax.rsqrt(v), 0.0))
        bt = (xc * gate_i) * mult
        for j in range(per_g):
            a_s[g * per_g + j] = a[:, lanes(j)]
            b_s[g * per_g + j] = bt[:, lanes(j)]

    def scan_groups(j):
        acc_a = a_s[slab(j, 0)]
        acc_h = b_s[slab(j, 0)]
        h_s[slab(j, 0)] = acc_h
        for r in range(1, V7X_SUBLANES):
            a_r = a_s[slab(j, r)]
            acc_h = a_r * acc_h + b_s[slab(j, r)]
            acc_a = a_r * acc_a
            h_s[slab(j, r)] = acc_h
            a_s[slab(j, r)] = acc_a
        ga_s[:, lanes(j)] = acc_a
        gb_s[:, lanes(j)] = acc_h

    def across(gi, h_prev):
        gc_s[pl.ds(gi, 1), :] = h_prev
        return ga_s[pl.ds(gi, 1), :] * h_prev + gb_s[pl.ds(gi, 1), :]

    def finish(j):
        h_in = gc_s[:, lanes(j)]
        for r in range(V7X_SUBLANES):
            h_s[slab(j, r)] = a_s[slab(j, r)] * h_in + h_s[slab(j, r)]
        y_ref[:, lanes(j)] = (h_s[j] * _gelu_tanh(gr_s[:, lanes(j)])).astype(BF16)

    rest = [functools.partial(proj_rest, c0) for c0 in range(2 * D_RNN, w_ref.shape[1], chunk)]
    before = ([functools.partial(conv, j) for j in range(n_lt)]
              + [functools.partial(gates, g) for g in range(n_lt // per_g)]
              + [functools.partial(scan_groups, j) for j in range(n_lt)])
    after = [functools.partial(finish, j) for j in range(n_lt)]
    n_before = (len(rest) * len(before)) // (len(before) + len(after))

    def interleave(steps, chunks):
        every = -(-len(steps) // max(len(chunks), 1))
        for n, step in enumerate(steps):
            step()
            if (n + 1) % every == 0 and chunks:
                chunks.pop(0)()
        while chunks:
            chunks.pop(0)()

    interleave(before, rest[:n_before])
    carry[...] = lax.fori_loop(0, ng, across, carry[...])
    interleave(after, rest[n_before:])


def _block_diag_tiles(w, gw):
    nb, bw, _ = w.shape
    per = gw // bw
    w4 = w.reshape(nb // per, per, bw, bw)
    eye = jnp.eye(per, dtype=w.dtype)
    return jnp.einsum("gpij,pq->gpiqj", w4, eye).reshape(nb // per, gw, gw)


def _mixin(x2, g, sc, sh, w_bf, b_in, conv_w, conv_b, wa, ba, wx, bx, lam, S):
    N, D = x2.shape
    C = D_RNN
    tt = _tiles()["tok"]
    gw = _tiles()["rnn_group"]
    per_b = S // tt
    chunk = 2 * V7X_MXU_DIM
    vec = lambda: pl.BlockSpec((1, C), lambda i: (0, 0))
    bvec = lambda: pl.BlockSpec((1, 1, D), lambda i: (i // per_b, 0, 0))
    gate_w = lambda: pl.BlockSpec((C // gw, gw, gw), lambda i: (0, 0, 0))
    slabs = lambda rows: pltpu.VMEM((C // V7X_LANES, rows, V7X_LANES), F32)
    groups = lambda: pltpu.VMEM((tt // V7X_SUBLANES, C), F32)
    vmem = D * D_IN * 2 + 2 * tt * (D * 4 + D_REST * 2 + C * 2) + tt * C * (2 + 5 * 4) + 8 * tt * chunk * 4
    return pl.pallas_call(
        functools.partial(_mixin_kernel, per_b=per_b, gw=gw, chunk=chunk),
        grid=(N // tt,),
        in_specs=[
            pl.BlockSpec((tt, D), lambda i: (i, 0)),
            pl.BlockSpec((1, D), lambda i: (0, 0)),
            bvec(), bvec(),
            pl.BlockSpec((D, D_IN), lambda i: (0, 0), pipeline_mode=pl.Buffered(1)),
            pl.BlockSpec((1, D_IN), lambda i: (0, 0)),
            pl.BlockSpec((CONV_W, C), lambda i: (0, 0)),
            vec(), gate_w(), vec(), gate_w(), vec(), vec(),
        ],
        out_specs=[
            pl.BlockSpec((tt, D_REST), lambda i: (i, 0)),
            pl.BlockSpec((tt, C), lambda i: (i, 0)),
        ],
        out_shape=[
            jax.ShapeDtypeStruct((N, D_REST), BF16),
            jax.ShapeDtypeStruct((N, C), BF16),
        ],
        scratch_shapes=[
            pltpu.VMEM((tt, D), BF16),
            slabs(tt + V7X_SUBLANES),
            pltpu.VMEM((tt, C), F32),
            slabs(tt), slabs(tt), slabs(tt),
            groups(), groups(), groups(),
            pltpu.VMEM((1, C), F32),
        ],
        compiler_params=pltpu.CompilerParams(
            dimension_semantics=("arbitrary",),
            vmem_limit_bytes=_vmem_limit(vmem)),
        name="mixin",
    )(x2, g, sc, sh, w_bf, b_in, conv_w, conv_b, wa, ba, wx, bx, lam)


def _alibi_slopes():
    return [2.0 ** (-8.0 * (h + 1) / N_HEADS) for h in range(N_HEADS)]


def _attn_kernel(sink_ref, q_ref, kp_ref, kc_ref, vp_ref, vc_ref, o_ref, bias_s, *, per_b):
    first = pl.program_id(0) % per_b == 0
    bq = kp_ref.shape[0]
    slopes = _alibi_slopes()

    @pl.when(pl.program_id(0) == 0)
    def _():
        qi = lax.broadcasted_iota(jnp.int32, (bq, 2 * bq), 0)
        ci = lax.broadcasted_iota(jnp.int32, (bq, 2 * bq), 1)
        dist = qi + bq - ci
        valid = (dist >= 0) & (dist < WINDOW)
        distf = dist.astype(F32)
        for h in range(N_HEADS):
            b = jnp.where(valid, -slopes[h] * distf, -jnp.inf)
            bias_s[0, h] = jnp.where(ci >= bq, b, -jnp.inf)
            bias_s[1, h] = b

    assert 2 * HEAD_DIM == V7X_LANES and GROUP % 2 == 0
    low = lax.broadcasted_iota(jnp.int32, (bq, V7X_LANES), 1) < HEAD_DIM
    zero = jnp.zeros((bq, V7X_LANES), BF16)
    for sub in range(q_ref.shape[0] // bq):
        rows = slice(sub * bq, (sub + 1) * bq)
        before = slice((sub - 1) * bq, sub * bq)
        table = jnp.where(first, 0, 1) if sub == 0 else 1
        for kvt in range(N_KV // 2):
            lt = slice(kvt * V7X_LANES, (kvt + 1) * V7X_LANES)
            k_prev, v_prev = (kp_ref[:, lt], vp_ref[:, lt]) if sub == 0 else (kc_ref[before, lt], vc_ref[before, lt])
            k_t = jnp.concatenate([k_prev, kc_ref[rows, lt]], axis=0)
            v_t = jnp.concatenate([v_prev, vc_ref[rows, lt]], axis=0)
            k_swapped = pltpu.roll(k_t, HEAD_DIM, 1)
            for kv_half in range(2):
                kv = 2 * kvt + kv_half
                for pair in range(GROUP // 2):
                    h0 = kv * GROUP + 2 * pair
                    tile = slice((h0 // 2) * V7X_LANES, (h0 // 2 + 1) * V7X_LANES)
                    q_t = q_ref[rows, tile]
                    halves = []
                    for q_half in range(2):
                        h = h0 + q_half
                        qm = jnp.where(low if q_half == 0 else ~low, q_t, zero)
                        kk = k_t if q_half == kv_half else k_swapped
                        s = lax.dot_general(qm, kk, (((1,), (1,)), ((), ())),
                                            preferred_element_type=F32)
                        s = s + bias_s[table, h]
                        sink = sink_ref[h]
                        m = jnp.maximum(jnp.max(s, axis=-1, keepdims=True), sink)
                        p = jnp.exp(s - m)
                        denom = jnp.sum(p, axis=-1, keepdims=True) + jnp.exp(sink - m)
                        halves.append(jnp.dot(p.astype(BF16), v_t, preferred_element_type=F32) / denom)
                    if kv_half == 0:
                        o = jnp.where(low, halves[0], pltpu.roll(halves[1], HEAD_DIM, 1))
                    else:
                        o = jnp.where(low, pltpu.roll(halves[0], HEAD_DIM, 1), halves[1])
                    o_ref[rows, tile] = o.astype(BF16)


def _attn(proj_rest, sinks, S):
    N = proj_rest.shape[0]
    bq = _tiles()["attn_q"]
    nsub = _tiles()["attn_blocks"]
    tq = nsub * bq
    per_b = S // tq
    k_col = (Q_DIM + 2 * D_MODEL) // KV_DIM
    v_col = k_col + 1

    def prev(i):
        return jnp.where(i % per_b == 0, i * nsub, i * nsub - 1)

    return pl.pallas_call(
        functools.partial(_attn_kernel, per_b=per_b),
        grid=(N // tq,),
        in_specs=[
            pl.BlockSpec(memory_space=pltpu.SMEM),
            pl.BlockSpec((tq, Q_DIM), lambda i: (i, 0)),
            pl.BlockSpec((bq, KV_DIM), lambda i: (prev(i), k_col)),
            pl.BlockSpec((tq, KV_DIM), lambda i: (i, k_col)),
            pl.BlockSpec((bq, KV_DIM), lambda i: (prev(i), v_col)),
            pl.BlockSpec((tq, KV_DIM), lambda i: (i, v_col)),
        ],
        out_specs=pl.BlockSpec((tq, Q_DIM), lambda i: (i, 0)),
        out_shape=jax.ShapeDtypeStruct((N, Q_DIM), BF16),
        scratch_shapes=[pltpu.VMEM((2, N_HEADS, bq, 2 * bq), F32)],
        compiler_params=pltpu.CompilerParams(
            dimension_semantics=("arbitrary",),
            vmem_limit_bytes=_vmem_limit(3 * 2 * N_HEADS * bq * 2 * bq * 4)),
        name="attn",
    )(sinks, proj_rest, proj_rest, proj_rest, proj_rest, proj_rest)


def _merge_kernel(x_ref, yr_ref, ya_ref, gr_ref, ga_ref, wr_ref, wa_ref, wo_ref,
                  gpost_ref, g1_ref, gpre_ref, sc2_ref, sh2_ref, rwt_ref, rb_ref,
                  x1_ref, h2_ref, lg_ref):
    r = jnp.dot(yr_ref[...], wr_ref[...], preferred_element_type=F32)
    a = jnp.dot(ya_ref[...], wa_ref[...], preferred_element_type=F32)
    merged = ((0.5 * jnp.tanh(gr_ref[...].astype(F32)) + 0.5) * r
              + (0.5 * jnp.tanh(ga_ref[...].astype(F32)) + 0.5) * a)
    mix = jnp.dot(merged.astype(BF16), wo_ref[...], preferred_element_type=F32)
    x1 = x_ref[...] + g1_ref[0] * _rms(mix, gpost_ref[...])
    x1_ref[...] = x1
    h2 = (_rms(x1, gpre_ref[...]) * (1.0 + sc2_ref[0]) + sh2_ref[0]).astype(BF16)
    h2_ref[...] = h2
    lg = lax.dot_general(rwt_ref[...], h2, (((1,), (1,)), ((), ())),
                         preferred_element_type=F32)
    lg_ref[...] = lg + rb_ref[...]


def _merge(x2, y_rnn, y_att, proj_rest, wr, wa, wo, gpost, g1, gpre, sc2, sh2, rwt, rb, S):
    N, D = x2.shape
    tm = _tiles()["tok"]
    per_b = S // tm
    gate_r_col = Q_DIM // D
    mat = lambda: pl.BlockSpec((D, D), lambda i: (0, 0))
    vec = lambda: pl.BlockSpec((1, D), lambda i: (0, 0))
    bvec = lambda: pl.BlockSpec((1, 1, D), lambda i: (i // per_b, 0, 0))
    tile = lambda col=0: pl.BlockSpec((tm, D), lambda i: (i, col))
    return pl.pallas_call(
        _merge_kernel,
        grid=(N // tm,),
        in_specs=[
            tile(), tile(), tile(), tile(gate_r_col), tile(gate_r_col + 1),
            mat(), mat(), mat(),
            vec(), bvec(), vec(), bvec(), bvec(),
            pl.BlockSpec((N_EXPERTS, D), lambda i: (0, 0)),
            pl.BlockSpec((N_EXPERTS, 1), lambda i: (0, 0)),
        ],
        out_specs=[
            tile(), tile(),
            pl.BlockSpec((N_EXPERTS, tm), lambda i: (0, i)),
        ],
        out_shape=[
            jax.ShapeDtypeStruct((N, D), F32),
            jax.ShapeDtypeStruct((N, D), BF16),
            jax.ShapeDtypeStruct((N_EXPERTS, N), F32),
        ],
        compiler_params=pltpu.CompilerParams(
            dimension_semantics=("arbitrary",),
            vmem_limit_bytes=_vmem_limit(6 * D * D * 2 + 24 * tm * D * 4)),
        name="merge",
    )(x2, y_rnn, y_att, proj_rest, proj_rest, wr, wa, wo, gpost, g1, gpre, sc2, sh2, rwt, rb)


def _route_kernel(lg_ref, g_ref, slot_ref, cnt_ref, tcnt_ref, tcar_ref, tlst_ref, carry, *, T):
    i = pl.program_id(0)
    E = lg_ref.shape[0]
    n_sub = lg_ref.shape[1] // T

    @pl.when(i == 0)
    def _():
        carry[...] = jnp.zeros_like(carry)
        tcnt_ref[...] = jnp.zeros_like(tcnt_ref)
        tcar_ref[...] = jnp.zeros_like(tcar_ref)
        tlst_ref[...] = jnp.zeros_like(tlst_ref)

    row = lax.broadcasted_iota(jnp.int32, (E, T), 0).astype(F32)
    tri_t = (lax.broadcasted_iota(jnp.int32, (T, T), 0)
             < lax.broadcasted_iota(jnp.int32, (T, T), 1)).astype(BF16)
    tri_e = (lax.broadcasted_iota(jnp.int32, (E, E), 1)
             < lax.broadcasted_iota(jnp.int32, (E, E), 0)).astype(BF16)
    tile_lane = lax.broadcasted_iota(jnp.int32, tcnt_ref.shape, 1)
    running = carry[...]
    for sub in range(n_sub):
        cols = slice(sub * T, (sub + 1) * T)
        l = lg_ref[:, cols]
        vals, idxs = [], []
        for _ in range(TOP_K):
            m = jnp.max(l, axis=0, keepdims=True)
            idx = jnp.min(jnp.where(l == m, row, float(E)), axis=0, keepdims=True)
            vals.append(m)
            idxs.append(idx)
            l = jnp.where(row == idx, -jnp.inf, l)
        ex = [jnp.exp(v - vals[0]) for v in vals]
        tot = ex[0]
        for e in ex[1:]:
            tot = tot + e
        g_ref[:, cols] = jnp.concatenate([e / tot for e in ex], axis=0)

        hot = [row == idx for idx in idxs]
        onehot = jnp.zeros((E, T), F32)
        for hk in hot:
            onehot = onehot + hk.astype(F32)
        before = jnp.dot(onehot.astype(BF16), tri_t, preferred_element_type=F32)
        cnt = jnp.sum(onehot, axis=1, keepdims=True)
        lstart = jnp.dot(tri_e, jnp.broadcast_to(cnt, (E, V7X_LANES)).astype(BF16),
                         preferred_element_type=F32)[:, 0:1]
        local = before + lstart
        slots = [jnp.sum(jnp.where(hk, local, 0.0), axis=0, keepdims=True) for hk in hot]
        slot_ref[:, cols] = jnp.concatenate(slots, axis=0).astype(jnp.int32)

        mine = tile_lane == i * n_sub + sub
        tcnt_ref[...] = jnp.where(mine, cnt.astype(jnp.int32), tcnt_ref[...])
        tcar_ref[...] = jnp.where(mine, running.astype(jnp.int32), tcar_ref[...])
        tlst_ref[...] = jnp.where(mine, lstart.astype(jnp.int32), tlst_ref[...])
        running = running + cnt
    carry[...] = running
    cnt_ref[...] = running.astype(jnp.int32)


def _route(logits_t):
    E, N = logits_t.shape
    T = _tiles()["moe_t"]
    TS = T * _tiles()["route_tiles"]
    assert T <= 2 ** 8 and N // T <= V7X_LANES
    out = lambda: pl.BlockSpec((TOP_K, TS), lambda i: (0, i))
    per_tile = lambda: pl.BlockSpec((E, V7X_LANES), lambda i: (0, 0))
    return pl.pallas_call(
        functools.partial(_route_kernel, T=T),
        grid=(N // TS,),
        in_specs=[pl.BlockSpec((E, TS), lambda i: (0, i))],
        out_specs=[out(), out(), pl.BlockSpec((E, 1), lambda i: (0, 0)),
                   per_tile(), per_tile(), per_tile()],
        out_shape=[
            jax.ShapeDtypeStruct((TOP_K, N), F32),
            jax.ShapeDtypeStruct((TOP_K, N), jnp.int32),
            jax.ShapeDtypeStruct((E, 1), jnp.int32),
            jax.ShapeDtypeStruct((E, V7X_LANES), jnp.int32),
            jax.ShapeDtypeStruct((E, V7X_LANES), jnp.int32),
            jax.ShapeDtypeStruct((E, V7X_LANES), jnp.int32),
        ],
        scratch_shapes=[pltpu.VMEM((E, 1), F32)],
        compiler_params=pltpu.CompilerParams(dimension_semantics=("arbitrary",)),
        name="route",
    )(logits_t)


def _run_sizes(T):
    return [2 ** b for b in range(int(math.log2(T)), -1, -1)]


def _copy_plan(tile_cnt, tile_lst, tile_dst, T):
    sizes = jnp.asarray(_run_sizes(T), jnp.int32)[None, :, None]
    cnt, lst, dst = (a[:, None, :] for a in (tile_cnt, tile_lst, tile_dst))
    has = (cnt & sizes) != 0
    off = (cnt // (2 * sizes)) * (2 * sizes)
    eids = jnp.arange(N_EXPERTS)
    earlier = eids[:, None] < eids[None, :]
    place = jnp.sum(jnp.where(earlier, has[..., :, None], False), axis=-2)
    hit = has[..., :, None] & (place[..., :, None] == eids)
    pick = lambda v: jnp.sum(jnp.where(hit, v[..., :, None], 0), axis=-2)
    n_tiles, nb = tile_cnt.shape[0], sizes.shape[1]
    parts = [pick(jnp.broadcast_to(lst + off, has.shape)).reshape(n_tiles, -1),
             pick(jnp.broadcast_to(dst + off, has.shape)).reshape(n_tiles, -1),
             jnp.sum(has, axis=-1).reshape(n_tiles, nb)]
    plan = jnp.concatenate(parts, axis=1).astype(jnp.int32)
    pad = _plan_len(T) - plan.shape[1]
    return jnp.pad(plan, ((0, 0), (0, pad))).reshape(-1)


def _plan_len(T):
    nb = len(_run_sizes(T))
    return max(V7X_LANES, pl.next_power_of_2(2 * nb * N_EXPERTS + nb))


def _run_copies(plan_ref, T, make):
    sizes = _run_sizes(T)
    nb = len(sizes)
    for bi, b in enumerate(sizes):
        def body(j, c, bi=bi, b=b):
            make(plan_ref[bi * N_EXPERTS + j], plan_ref[(nb + bi) * N_EXPERTS + j], b).start()
            return c

        lax.fori_loop(0, plan_ref[2 * nb * N_EXPERTS + bi], body, 0)


def _row_slab(ref, row, nrows, lead=()):
    rows = pl.ds(pl.multiple_of(row * V7X_SUBLANES, V7X_SUBLANES), nrows * V7X_SUBLANES)
    return ref.at[(*lead, rows)]


def _to_slabs(ref, val, lead=()):
    rows = val.shape[0]
    for s in range(V7X_SUBLANES):
        ref[(*lead, pl.ds(s, rows, stride=V7X_SUBLANES), slice(None))] = (
            val[:, s * V7X_LANES:(s + 1) * V7X_LANES])


def _from_slabs(ref, lead=(), rows=None):
    rows = ref.shape[-2] // V7X_SUBLANES if rows is None else rows
    return jnp.concatenate(
        [ref[(*lead, pl.ds(s, rows, stride=V7X_SUBLANES), slice(None))]
         for s in range(V7X_SUBLANES)], axis=1)


def _dispatch_kernel(tail_ref, nu_ref, plan_ref, slot_ref, h_ref, xs_ref, buf, zbuf, sem, zsem):
    i = pl.program_id(0)
    T = h_ref.shape[0]
    bm = zbuf.shape[0] // V7X_SUBLANES
    nblk = xs_ref.shape[0] // zbuf.shape[0]
    depth = buf.shape[0]
    cur = i % depth

    def zero_copy(row0, s):
        return pltpu.make_async_copy(zbuf, _row_slab(xs_ref, row0, bm), zsem.at[s])

    def read_chunk(e, fn):
        @pl.when(tail_ref[e] >= 0)
        def _():
            fn(zero_copy(tail_ref[e], 0))

    def unread_chunks(fn):
        def per_expert(e, c):
            @pl.when(jnp.logical_and(tail_ref[e] >= 0, tail_ref[e] % (2 * bm) == 0))
            def _():
                fn(zero_copy(tail_ref[e] + bm, 1))
            return c

        def per_unused(j, c):
            fn(zero_copy(j * bm, 1))
            return c

        lax.fori_loop(0, N_EXPERTS, per_expert, 0)
        lax.fori_loop(nu_ref[0], nblk, per_unused, 0)

    @pl.when(i == 0)
    def _():
        zbuf[...] = jnp.zeros_like(zbuf)

        def start(e, c):
            read_chunk(e, lambda cp: cp.start())
            return c

        def wait(e, c):
            read_chunk(e, lambda cp: cp.wait())
            return c

        lax.fori_loop(0, N_EXPERTS, start, 0)
        unread_chunks(lambda cp: cp.start())
        lax.fori_loop(0, N_EXPERTS, wait, 0)

    slot = slot_ref[...]
    r_id = lax.broadcasted_iota(jnp.int32, (TOP_K * T, T), 0)
    perm = r_id == slot[0:1, :]
    for k in range(1, TOP_K):
        perm = jnp.logical_or(perm, r_id == slot[k:k + 1, :])
    grouped = jnp.dot(perm.astype(BF16), h_ref[...], preferred_element_type=F32)
    _to_slabs(buf, grouped, lead=(cur,))

    def wait_tile(b):
        pltpu.make_async_copy(buf.at[b], _row_slab(xs_ref, 0, TOP_K * T), sem.at[b]).wait()

    @pl.when(i >= depth - 1)
    def _():
        wait_tile((i + 1) % depth)

    _run_copies(plan_ref, T, lambda l, d, n: pltpu.make_async_copy(
        _row_slab(buf, l, n, lead=(cur,)), _row_slab(xs_ref, d, n), sem.at[cur]))

    @pl.when(i == pl.num_programs(0) - 1)
    def _():
        for back in range(depth - 1):
            @pl.when(i >= back)
            def _(back=back):
                wait_tile((i - back) % depth)
        unread_chunks(lambda cp: cp.wait())


def _dispatch(h2, slot, plan, tail_start, n_used, n_rows, bm):
    N, D = h2.shape
    T = _tiles()["moe_t"]
    ring = 3
    S8 = V7X_SUBLANES
    imap = lambda f: (lambda i, *_: f(i))
    return pl.pallas_call(
        _dispatch_kernel,
        grid_spec=pltpu.PrefetchScalarGridSpec(
            num_scalar_prefetch=2,
            grid=(N // T,),
            in_specs=[
                pl.BlockSpec((_plan_len(T),), imap(lambda i: (i,)), memory_space=pltpu.SMEM),
                pl.BlockSpec((TOP_K, T), imap(lambda i: (0, i))),
                pl.BlockSpec((T, D), imap(lambda i: (i, 0))),
            ],
            out_specs=pl.BlockSpec(memory_space=pl.ANY),
            scratch_shapes=[
                pltpu.VMEM((ring, TOP_K * T * S8, V7X_LANES), F32),
                pltpu.VMEM((bm * S8, V7X_LANES), F32),
                pltpu.SemaphoreType.DMA((ring,)),
                pltpu.SemaphoreType.DMA((2,)),
            ],
        ),
        out_shape=jax.ShapeDtypeStruct((n_rows * S8, V7X_LANES), F32),
        compiler_params=pltpu.CompilerParams(
            dimension_semantics=("arbitrary",),
            vmem_limit_bytes=_vmem_limit(8 * TOP_K * T * D * 4)),
        name="dispatch",
    )(tail_start, n_used, plan, slot, h2)


def _ffn_kernel(be_ref, bv_ref, bf_ref, nu_ref, nxt_ref, x_ref, w1_hbm, b1g_ref, b1l_ref, w2_hbm,
                b2_ref, y_ref, w1f_s, w2f_s, w1_s, w2_s, sem):
    i = pl.program_id(0)
    bm, D = x_ref.shape[0] // V7X_SUBLANES, w2_s.shape[1]
    n_ff = w2_s.shape[0]
    pw = 2 * V7X_LANES

    def fetch(e):
        return (pltpu.make_async_copy(w1_hbm.at[e], w1f_s, sem.at[0]),
                pltpu.make_async_copy(w2_hbm.at[e], w2f_s, sem.at[1]))

    @pl.when(i == 0)
    def _():
        for cp in fetch(be_ref[0]):
            cp.start()

    @pl.when(jnp.logical_and(i < nu_ref[0], bf_ref[i] == 1))
    def _():
        e = be_ref[i]
        for cp in fetch(e):
            cp.wait()
        src = lax.broadcasted_iota(jnp.int32, (pw, pw), 0)
        dst = lax.broadcasted_iota(jnp.int32, (pw, pw), 1)
        want = jnp.where(dst < V7X_LANES, 2 * dst, 2 * (dst - V7X_LANES) + 1)
        perm = (src == want).astype(BF16)
        for c in range(w1f_s.shape[1] // pw):
            wp = jnp.dot(w1f_s[:, c * pw:(c + 1) * pw].astype(BF16), perm,
                         preferred_element_type=F32).astype(BF16)
            w1_s[:, c * V7X_LANES:(c + 1) * V7X_LANES] = wp[:, :V7X_LANES]
            w1_s[:, n_ff + c * V7X_LANES:n_ff + (c + 1) * V7X_LANES] = wp[:, V7X_LANES:]
        w2_s[...] = w2f_s[...].astype(BF16)

        @pl.when(nxt_ref[e] >= 0)
        def _():
            for cp in fetch(nxt_ref[e]):
                cp.start()

    def mlp(n_rows):
        x = _from_slabs(x_ref, rows=n_rows).astype(BF16)
        hid = jnp.dot(x, w1_s[...], preferred_element_type=F32)
        glu = hid[:, :n_ff] + b1g_ref[0]
        lin = hid[:, n_ff:] + b1l_ref[0]
        glu = jnp.minimum(glu, SWIGLU_LIMIT)
        lin = jnp.clip(lin, -SWIGLU_LIMIT, SWIGLU_LIMIT)
        act = glu * _sigmoid(SWIGLU_ALPHA * glu) * (lin + 1.0)
        _to_slabs(y_ref, jnp.dot(act.astype(BF16), w2_s[...], preferred_element_type=F32) + b2_ref[0])
        if n_rows < bm:
            y_ref[n_rows * V7X_SUBLANES:, :] = jnp.zeros(
                ((bm - n_rows) * V7X_SUBLANES, V7X_LANES), F32)

    used = i < nu_ref[0]
    half = bm // 2

    @pl.when(jnp.logical_and(used, bv_ref[i] > half))
    def _():
        mlp(bm)

    @pl.when(jnp.logical_and(used, bv_ref[i] <= half))
    def _():
        mlp(half)

    @pl.when(i >= nu_ref[0])
    def _():
        y_ref[...] = jnp.zeros_like(y_ref)


def _ffn(xs, blk_e, blk_valid, blk_first, n_used, next_e, w1, b1g, b1l, w2, b2, bm):
    E, D, F2 = w1.shape
    F = F2 // 2
    slab = bm * V7X_SUBLANES
    nblk = xs.shape[0] // slab

    def row_blk(i, be, bv, bf, nu, nxt):
        return (jnp.minimum(i, nu[0] - 1), 0)

    def per_e(i, be, bv, bf, nu, nxt):
        return (be[i], 0, 0)

    vmem = (D * F2 + F * D) * (4 + 2) + 8 * bm * F2 * 4
    return pl.pallas_call(
        _ffn_kernel,
        grid_spec=pltpu.PrefetchScalarGridSpec(
            num_scalar_prefetch=5,
            grid=(nblk,),
            in_specs=[
                pl.BlockSpec((slab, V7X_LANES), row_blk),
                pl.BlockSpec(memory_space=pl.ANY),
                pl.BlockSpec((1, 1, F), per_e),
                pl.BlockSpec((1, 1, F), per_e),
                pl.BlockSpec(memory_space=pl.ANY),
                pl.BlockSpec((1, 1, D), per_e),
            ],
            out_specs=pl.BlockSpec((slab, V7X_LANES), lambda i, be, bv, bf, nu, nxt: (i, 0)),
            scratch_shapes=[
                pltpu.VMEM((D, F2), F32),
                pltpu.VMEM((F, D), F32),
                pltpu.VMEM((D, F2), BF16),
                pltpu.VMEM((F, D), BF16),
                pltpu.SemaphoreType.DMA((2,)),
            ],
        ),
        out_shape=jax.ShapeDtypeStruct(xs.shape, F32),
        compiler_params=pltpu.CompilerParams(
            dimension_semantics=("arbitrary",),
            vmem_limit_bytes=_vmem_limit(vmem)),
        name="ffn",
    )(blk_e, blk_valid, blk_first, n_used, next_e, xs, w1, b1g, b1l, w2, b2)


def _combine_kernel(plan_ref, plan_next_ref, ys_ref, x1_ref, slot_ref, gate_ref, gpost_ref, g2_ref,
                    o_ref, buf, sem):
    i = pl.program_id(0)
    T = x1_ref.shape[0]
    cur = i % 2

    def gather(plan, b):
        _run_copies(plan, T, lambda l, d, n: pltpu.make_async_copy(
            _row_slab(ys_ref, d, n), _row_slab(buf, l, n, lead=(b,)), sem.at[b]))

    @pl.when(i == 0)
    def _():
        gather(plan_ref, cur)

    @pl.when(i + 1 < pl.num_programs(0))
    def _():
        gather(plan_next_ref, 1 - cur)

    pltpu.make_async_copy(_row_slab(ys_ref, 0, TOP_K * T), buf.at[cur], sem.at[cur]).wait()

    slot = slot_ref[...]
    gates = gate_ref[...]
    r_id = lax.broadcasted_iota(jnp.int32, (TOP_K * T, T), 0)
    wt = jnp.zeros((TOP_K * T, T), F32)
    for k in range(TOP_K):
        wt = jnp.where(r_id == slot[k:k + 1, :], gates[k:k + 1, :], wt)
    ff = lax.dot_general(wt.astype(BF16), _from_slabs(buf, lead=(cur,)).astype(BF16),
                         (((0,), (0,)), ((), ())), preferred_element_type=F32)
    o_ref[...] = x1_ref[...] + g2_ref[0] * _rms(ff, gpost_ref[...])


def _combine(ys, plan, x1, slot_tk, gates_tk, gpost, g2, S):
    N, D = x1.shape
    T = _tiles()["moe_t"]
    per_b = S // T
    n_tiles = N // T
    plan_spec = lambda f: pl.BlockSpec((_plan_len(T),), f, memory_space=pltpu.SMEM)
    return pl.pallas_call(
        _combine_kernel,
        grid=(n_tiles,),
        in_specs=[
            plan_spec(lambda i: (i,)),
            plan_spec(lambda i: (jnp.minimum(i + 1, n_tiles - 1),)),
            pl.BlockSpec(memory_space=pl.ANY),
            pl.BlockSpec((T, D), lambda i: (i, 0)),
            pl.BlockSpec((TOP_K, T), lambda i: (0, i)),
            pl.BlockSpec((TOP_K, T), lambda i: (0, i)),
            pl.BlockSpec((1, D), lambda i: (0, 0)),
            pl.BlockSpec((1, 1, D), lambda i: (i // per_b, 0, 0)),
        ],
        out_specs=pl.BlockSpec((T, D), lambda i: (i, 0)),
        out_shape=jax.ShapeDtypeStruct((N, D), F32),
        scratch_shapes=[
            pltpu.VMEM((2, TOP_K * T * V7X_SUBLANES, V7X_LANES), F32),
            pltpu.SemaphoreType.DMA((2,)),
        ],
        compiler_params=pltpu.CompilerParams(
            dimension_semantics=("arbitrary",),
            vmem_limit_bytes=_vmem_limit(10 * TOP_K * T * D * 4)),
        name="combine",
    )(plan, plan, ys, x1, slot_tk, gates_tk, gpost, g2)


def _col_scaled(w):
    scale = HEAD_DIM ** -0.5
    assert math.log2(scale).is_integer()
    col = np.arange(D_IN)
    q0, g0 = 2 * D_RNN, 2 * D_RNN + Q_DIM + 2 * KV_DIM
    factor = np.where((col >= q0) & (col < q0 + Q_DIM), scale, np.where(col >= g0, 0.5, 1.0))
    return w * jnp.asarray(factor, w.dtype)


def _layer(x2, c, B, S, p):
    D = D_MODEL
    N = B * S
    ada = _ada(c, p["w_ada"], p["b_ada"])
    sh1, sc1, g1, sh2, sc2, g2 = [a.reshape(B, 1, D) for a in jnp.split(ada, 6, axis=-1)]
    row = lambda v: v.reshape(1, -1)

    gw = _tiles()["rnn_group"]
    proj_rest, y_rnn = _mixin(
        x2, row(p["norm_pre_mix"]), sc1, sh1,
        _col_scaled(p["w_in"]).astype(BF16), _col_scaled(row(p["b_in"])),
        p["conv_w"], row(p["conv_b"]),
        (0.5 * _block_diag_tiles(p["rg_w_a"], gw)).astype(BF16), 0.5 * row(p["rg_b_a"]),
        (0.5 * _block_diag_tiles(p["rg_w_x"], gw)).astype(BF16), 0.5 * row(p["rg_b_x"]),
        row(p["rg_lambda"]), S)
    y_att = _attn(proj_rest, p["attn_sinks"], S)

    x1, h2, logits_t = _merge(
        x2, y_rnn, y_att, proj_rest,
        p["w_o_rnn"].astype(BF16), p["w_o_attn"].astype(BF16), p["w_out"].astype(BF16),
        row(p["norm_post_mix"]), g1, row(p["norm_pre_ffn"]), sc2, sh2,
        p["router_w"].T.astype(BF16), p["router_b"].reshape(N_EXPERTS, 1), S)

    gates, slot, counts, tile_cnt, tile_car, tile_lst = _route(logits_t)

    bm = _tiles()["moe_bm"]
    n_tiles = N // _tiles()["moe_t"]
    n_rows = N * TOP_K + N_EXPERTS * bm
    nblk = n_rows // bm
    counts = counts.reshape(N_EXPERTS)
    padded = ((counts + bm - 1) // bm) * bm
    pend = jnp.cumsum(padded)
    pstart = pend - padded
    eids = jnp.arange(N_EXPERTS, dtype=jnp.int32)
    per_run = lambda a: a[:, :n_tiles].T.astype(jnp.int32)
    plan = _copy_plan(per_run(tile_cnt), per_run(tile_lst), per_run(tile_car + pstart[:, None]),
                      _tiles()["moe_t"])
    zr = bm // 2
    tail_start = jnp.where(counts > 0, (pstart + counts - 1) // zr * zr, -1).astype(jnp.int32)
    n_used = (pend[-1] // bm).astype(jnp.int32).reshape(1)
    blk_row0 = jnp.minimum(jnp.arange(nblk, dtype=jnp.int32), n_used[0] - 1) * bm
    blk_e = jnp.sum(blk_row0[:, None] >= pend[None, :], axis=1).astype(jnp.int32)
    mine = blk_e[:, None] == eids[None, :]
    blk_cnt = jnp.sum(jnp.where(mine, counts, 0), axis=1)
    blk_pstart = jnp.sum(jnp.where(mine, pstart, 0), axis=1)
    blk_valid = jnp.clip(blk_cnt - (blk_row0 - blk_pstart), 0, bm).astype(jnp.int32)
    blk_first = (blk_row0 == blk_pstart).astype(jnp.int32)

    xs = _dispatch(h2, slot, plan, tail_start, n_used * (bm // zr), n_rows, zr)
    b1 = p["moe_b1"].reshape(N_EXPERTS, D_FF, 2)
    later = jnp.logical_and(counts[None, :] > 0, eids[None, :] > eids[:, None])
    next_e = jnp.min(jnp.where(later, eids[None, :], N_EXPERTS), axis=1)
    next_e = jnp.where(next_e == N_EXPERTS, -1, next_e).astype(jnp.int32)
    ys = _ffn(xs, blk_e, blk_valid, blk_first, n_used, next_e, p["moe_w1"],
              b1[:, :, 0].reshape(N_EXPERTS, 1, D_FF), b1[:, :, 1].reshape(N_EXPERTS, 1, D_FF),
              p["moe_w2"], p["moe_b2"].reshape(N_EXPERTS, 1, D), bm)
    return _combine(ys, plan, x1, slot, gates, row(p["norm_post_ffn"]), g2, S)


def kernel(x, c, w_ada, b_ada, norm_pre_mix, norm_post_mix, norm_pre_ffn, norm_post_ffn, w_in, b_in, conv_w, conv_b, rg_w_a, rg_b_a, rg_w_x, rg_b_x, rg_lambda, attn_sinks, w_o_rnn, w_o_attn, w_out, router_w, router_b, moe_w1, moe_b1, moe_w2, moe_b2):
    B, S, D = x.shape
    params = dict(
        w_ada=w_ada, b_ada=b_ada, norm_pre_mix=norm_pre_mix, norm_post_mix=norm_post_mix,
        norm_pre_ffn=norm_pre_ffn, norm_post_ffn=norm_post_ffn, w_in=w_in, b_in=b_in,
        conv_w=conv_w, conv_b=conv_b, rg_w_a=rg_w_a, rg_b_a=rg_b_a, rg_w_x=rg_w_x, rg_b_x=rg_b_x,
        rg_lambda=rg_lambda, attn_sinks=attn_sinks, w_o_rnn=w_o_rnn, w_o_attn=w_o_attn,
        w_out=w_out, router_w=router_w, router_b=router_b, moe_w1=moe_w1, moe_b1=moe_b1,
        moe_w2=moe_w2, moe_b2=moe_b2)
    x2 = x.reshape(B * S, D)
    for layer in range(w_ada.shape[0]):
        x2 = _layer(x2, c, B, S, {k: v[layer] for k, v in params.items()})
    return x2.reshape(B, S, D)
```

```python
import functools
import math

import jax
import jax.numpy as jnp
import numpy as np
from jax import lax
from jax.experimental import pallas as pl
from jax.experimental.pallas import tpu as pltpu

D_MODEL = 1024
D_RNN = 1024
RNN_BLOCKS = 16
RNN_BW = D_RNN // RNN_BLOCKS
CONV_W = 4
LRU_C = 8.0
N_HEADS = 16
N_KV = 4
HEAD_DIM = 64
GROUP = N_HEADS // N_KV
WINDOW = 128
Q_DIM = N_HEADS * HEAD_DIM
KV_DIM = N_KV * HEAD_DIM
N_EXPERTS = 32
TOP_K = 4
D_FF = 1024
SWIGLU_LIMIT = 7.0
SWIGLU_ALPHA = 1.702
EPS = 1e-6
D_IN = 2 * D_RNN + Q_DIM + 2 * KV_DIM + 2 * D_MODEL
D_REST = D_IN - 2 * D_RNN

V7X_LANES = 128
V7X_SUBLANES = 8
V7X_MXU_DIM = 256
V7X_VMEM_BYTES = 64 * 1024 * 1024

F32 = jnp.float32
BF16 = jnp.bfloat16


def _tiles():
    return dict(
        ada_tn=4 * V7X_MXU_DIM,
        tok=2 * V7X_MXU_DIM,
        attn_q=WINDOW,
        attn_blocks=2,
        moe_t=V7X_MXU_DIM,
        route_tiles=4,
        moe_bm=2 * V7X_MXU_DIM,
        rnn_group=V7X_MXU_DIM,
    )


V7X_VMEM_MIN_LIMIT = 16 * 1024 * 1024
V7X_VMEM_HEADROOM = 8 * 1024 * 1024


def _vmem_limit(nbytes):
    return int(min(max(nbytes, V7X_VMEM_MIN_LIMIT), V7X_VMEM_BYTES - V7X_VMEM_HEADROOM))


def _rms(x, g):
    return x * lax.rsqrt(jnp.mean(x * x, axis=-1, keepdims=True) + EPS) * g


def _ada_kernel(ct_ref, w_ref, b_ref, o_ref):
    ct = ct_ref[...]
    sc = ct * jax.nn.sigmoid(ct)
    w = w_ref[...]
    rows = [jnp.sum(w * sc[:, b:b + 1], axis=0, keepdims=True) for b in range(ct.shape[1])]
    o_ref[...] = jnp.concatenate(rows, axis=0) + b_ref[...]


def _ada(c, w_ada, b_ada):
    B, D = c.shape
    n_out = w_ada.shape[1]
    tn = _tiles()["ada_tn"]
    return pl.pallas_call(
        _ada_kernel,
        grid=(n_out // tn,),
        in_specs=[
            pl.BlockSpec((D, B), lambda j: (0, 0)),
            pl.BlockSpec((D, tn), lambda j: (0, j)),
            pl.BlockSpec((1, tn), lambda j: (0, j)),
        ],
        out_specs=pl.BlockSpec((B, tn), lambda j: (0, j)),
        out_shape=jax.ShapeDtypeStruct((B, n_out), F32),
        compiler_params=pltpu.CompilerParams(
            dimension_semantics=("arbitrary",),
            vmem_limit_bytes=_vmem_limit(4 * D * tn * 4)),
        name="ada",
    )(c.T, w_ada, b_ada.reshape(1, n_out))


def _gelu_tanh(x):
    return 0.5 * x * (1.0 + jnp.tanh(math.sqrt(2.0 / math.pi) * (x + 0.044715 * (x * x * x))))


def _softplus(z):
    return jnp.maximum(z, 0.0) + jnp.log1p(jnp.exp(-jnp.abs(z)))


def _sigmoid(x):
    return 0.5 * jnp.tanh(0.5 * x) + 0.5


def _rest_col(c):
    c -= 2 * D_RNN
    q_end, k_end, v_end = Q_DIM, Q_DIM + KV_DIM, Q_DIM + 2 * KV_DIM
    if c < q_end:
        return c
    if c < k_end:
        return Q_DIM + 2 * D_MODEL + (c - q_end)
    if c < v_end:
        return Q_DIM + 2 * D_MODEL + KV_DIM + (c - k_end)
    return Q_DIM + (c - v_end)


def _mixin_kernel(x_ref, g_ref, sc_ref, sh_ref, w_ref, b_ref, cw_ref, cb_ref, wa_ref, ba_ref,
                  wx_ref, bx_ref, lam_ref, rest_ref, y_ref,
                  hb_s, xbuf, gr_s, a_s, b_s, h_s, ga_s, gb_s, gc_s, carry, *, per_b, gw, chunk):
    t = pl.program_id(0) % per_b
    tt = x_ref.shape[0]
    halo = V7X_SUBLANES
    ng = tt // V7X_SUBLANES
    n_lt = a_s.shape[0]
    per_g = gw // V7X_LANES
    lanes = lambda j: slice(j * V7X_LANES, (j + 1) * V7X_LANES)
    slab = lambda j, r: (j, pl.ds(r, ng, stride=V7X_SUBLANES), slice(None))

    @pl.when(t == 0)
    def _():
        xbuf[:, 0:halo, :] = jnp.zeros((n_lt, halo, V7X_LANES), F32)
        carry[...] = jnp.zeros_like(carry)

    hb_s[...] = (_rms(x_ref[...], g_ref[...]) * (1.0 + sc_ref[0]) + sh_ref[0]).astype(BF16)

    def proj(c0):
        return (jnp.dot(hb_s[...], w_ref[:, c0:c0 + chunk], preferred_element_type=F32)
                + b_ref[:, c0:c0 + chunk])

    def proj_rest(c0):
        acc = proj(c0).astype(BF16)
        for p0 in range(0, chunk, KV_DIM):
            d0 = _rest_col(c0 + p0)
            rest_ref[:, d0:d0 + KV_DIM] = acc[:, p0:p0 + KV_DIM]

    for c0 in range(0, D_RNN, chunk):
        acc = proj(c0)
        for j in range(chunk // V7X_LANES):
            xbuf[c0 // V7X_LANES + j, halo:halo + tt, :] = acc[:, lanes(j)]
    for c0 in range(D_RNN, 2 * D_RNN, chunk):
        gr_s[:, c0 - D_RNN:c0 - D_RNN + chunk] = proj(c0)

    def conv(j):
        ls = lanes(j)
        taps = {o: xbuf[slab(j, halo + o)] for o in range(-(CONV_W - 1), V7X_SUBLANES)}
        for r in range(V7X_SUBLANES):
            acc = cb_ref[:, ls] + taps[r - (CONV_W - 1)] * cw_ref[0:1, ls]
            for kk in range(1, CONV_W):
                acc = acc + taps[r - (CONV_W - 1) + kk] * cw_ref[kk:kk + 1, ls]
            h_s[slab(j, r)] = acc
        xbuf[j, 0:halo, :] = xbuf[j, tt:tt + halo, :]

    def gates(g):
        cs = slice(g * gw, (g + 1) * gw)
        rate = (-LRU_C * math.log2(math.e)) * _softplus(-lam_ref[:, cs])
        reset = jnp.logical_and(t == 0, lax.broadcasted_iota(jnp.int32, (tt, gw), 0) == 0)
        xc = jnp.concatenate([h_s[g * per_g + j] for j in range(per_g)], axis=1)
        xg = xc.astype(BF16)
        gate_r = 0.5 * jnp.tanh(jnp.dot(xg, wa_ref[g], preferred_element_type=F32) + ba_ref[:, cs]) + 0.5
        gate_i = 0.5 * jnp.tanh(jnp.dot(xg, wx_ref[g], preferred_element_type=F32) + bx_ref[:, cs]) + 0.5
        a = jnp.exp2(gate_r * rate)
        v = (1.0 - a) * (1.0 + a)
        mult = jnp.where(reset, 1.0, jnp.where(v > 0.0, v * lax.rsqrt(v), 0.0))
        bt = (xc * gate_i) * mult
        for j in range(per_g):
            a_s[g * per_g + j] = a[:, lanes(j)]
            b_s[g * per_g + j] = bt[:, lanes(j)]

    def scan_groups(j):
        acc_a = a_s[slab(j, 0)]
        acc_h = b_s[slab(j, 0)]
        h_s[slab(j, 0)] = acc_h
        for r in range(1, V7X_SUBLANES):
            a_r = a_s[slab(j, r)]
            acc_h = a_r * acc_h + b_s[slab(j, r)]
            acc_a = a_r * acc_a
            h_s[slab(j, r)] = acc_h
            a_s[slab(j, r)] = acc_a
        ga_s[:, lanes(j)] = acc_a
        gb_s[:, lanes(j)] = acc_h

    def across(gi, h_prev):
        gc_s[pl.ds(gi, 1), :] = h_prev
        return ga_s[pl.ds(gi, 1), :] * h_prev + gb_s[pl.ds(gi, 1), :]

    def finish(j):
        h_in = gc_s[:, lanes(j)]
        for r in range(V7X_SUBLANES):
            h_s[slab(j, r)] = a_s[slab(j, r)] * h_in + h_s[slab(j, r)]
        y_ref[:, lanes(j)] = (h_s[j] * _gelu_tanh(gr_s[:, lanes(j)])).astype(BF16)

    rest = [functools.partial(proj_rest, c0) for c0 in range(2 * D_RNN, w_ref.shape[1], chunk)]
    before = ([functools.partial(conv, j) for j in range(n_lt)]
              + [functools.partial(gates, g) for g in range(n_lt // per_g)]
              + [functools.partial(scan_groups, j) for j in range(n_lt)])
    after = [functools.partial(finish, j) for j in range(n_lt)]
    n_before = (len(rest) * len(before)) // (len(before) + len(after))

    def interleave(steps, chunks):
        every = -(-len(steps) // max(len(chunks), 1))
        for n, step in enumerate(steps):
            step()
            if (n + 1) % every == 0 and chunks:
                chunks.pop(0)()
        while chunks:
            chunks.pop(0)()

    interleave(before, rest[:n_before])
    carry[...] = lax.fori_loop(0, ng, across, carry[...])
    interleave(after, rest[n_before:])


def _block_diag_tiles(w, gw):
    nb, bw, _ = w.shape
    per = gw // bw
    w4 = w.reshape(nb // per, per, bw, bw)
    eye = jnp.eye(per, dtype=w.dtype)
    return jnp.einsum("gpij,pq->gpiqj", w4, eye).reshape(nb // per, gw, gw)


def _mixin(x2, g, sc, sh, w_bf, b_in, conv_w, conv_b, wa, ba, wx, bx, lam, S):
    N, D = x2.shape
    C = D_RNN
    tt = _tiles()["tok"]
    gw = _tiles()["rnn_group"]
    per_b = S // tt
    chunk = 2 * V7X_MXU_DIM
    vec = lambda: pl.BlockSpec((1, C), lambda i: (0, 0))
    bvec = lambda: pl.BlockSpec((1, 1, D), lambda i: (i // per_b, 0, 0))
    gate_w = lambda: pl.BlockSpec((C // gw, gw, gw), lambda i: (0, 0, 0))
    slabs = lambda rows: pltpu.VMEM((C // V7X_LANES, rows, V7X_LANES), F32)
    groups = lambda: pltpu.VMEM((tt // V7X_SUBLANES, C), F32)
    vmem = D * D_IN * 2 + 2 * tt * (D * 4 + D_REST * 2 + C * 2) + tt * C * (2 + 5 * 4) + 8 * tt * chunk * 4
    return pl.pallas_call(
        functools.partial(_mixin_kernel, per_b=per_b, gw=gw, chunk=chunk),
        grid=(N // tt,),
        in_specs=[
            pl.BlockSpec((tt, D), lambda i: (i, 0)),
            pl.BlockSpec((1, D), lambda i: (0, 0)),
            bvec(), bvec(),
            pl.BlockSpec((D, D_IN), lambda i: (0, 0), pipeline_mode=pl.Buffered(1)),
            pl.BlockSpec((1, D_IN), lambda i: (0, 0)),
            pl.BlockSpec((CONV_W, C), lambda i: (0, 0)),
            vec(), gate_w(), vec(), gate_w(), vec(), vec(),
        ],
        out_specs=[
            pl.BlockSpec((tt, D_REST), lambda i: (i, 0)),
            pl.BlockSpec((tt, C), lambda i: (i, 0)),
        ],
        out_shape=[
            jax.ShapeDtypeStruct((N, D_REST), BF16),
            jax.ShapeDtypeStruct((N, C), BF16),
        ],
        scratch_shapes=[
            pltpu.VMEM((tt, D), BF16),
            slabs(tt + V7X_SUBLANES),
            pltpu.VMEM((tt, C), F32),
            slabs(tt), slabs(tt), slabs(tt),
            groups(), groups(), groups(),
            pltpu.VMEM((1, C), F32),
        ],
        compiler_params=pltpu.CompilerParams(
            dimension_semantics=("arbitrary",),
            vmem_limit_bytes=_vmem_limit(vmem)),
        name="mixin",
    )(x2, g, sc, sh, w_bf, b_in, conv_w, conv_b, wa, ba, wx, bx, lam)


def _alibi_slopes():
    return [2.0 ** (-8.0 * (h + 1) / N_HEADS) for h in range(N_HEADS)]


def _attn_kernel(sink_ref, q_ref, kp_ref, kc_ref, vp_ref, vc_ref, o_ref, bias_s, *, per_b):
    first = pl.program_id(0) % per_b == 0
    bq = kp_ref.shape[0]
    slopes = _alibi_slopes()

    @pl.when(pl.program_id(0) == 0)
    def _():
        qi = lax.broadcasted_iota(jnp.int32, (bq, 2 * bq), 0)
        ci = lax.broadcasted_iota(jnp.int32, (bq, 2 * bq), 1)
        dist = qi + bq - ci
        valid = (dist >= 0) & (dist < WINDOW)
        distf = dist.astype(F32)
        for h in range(N_HEADS):
            b = jnp.where(valid, -slopes[h] * distf, -jnp.inf)
            bias_s[0, h] = jnp.where(ci >= bq, b, -jnp.inf)
            bias_s[1, h] = b

    assert 2 * HEAD_DIM == V7X_LANES and GROUP % 2 == 0
    low = lax.broadcasted_iota(jnp.int32, (bq, V7X_LANES), 1) < HEAD_DIM
    zero = jnp.zeros((bq, V7X_LANES), BF16)
    for sub in range(q_ref.shape[0] // bq):
        rows = slice(sub * bq, (sub + 1) * bq)
        before = slice((sub - 1) * bq, sub * bq)
        table = jnp.where(first, 0, 1) if sub == 0 else 1
        for kvt in range(N_KV // 2):
            lt = slice(kvt * V7X_LANES, (kvt + 1) * V7X_LANES)
            k_prev, v_prev = (kp_ref[:, lt], vp_ref[:, lt]) if sub == 0 else (kc_ref[before, lt], vc_ref[before, lt])
            k_t = jnp.concatenate([k_prev, kc_ref[rows, lt]], axis=0)
            v_t = jnp.concatenate([v_prev, vc_ref[rows, lt]], axis=0)
            k_swapped = pltpu.roll(k_t, HEAD_DIM, 1)
            for kv_half in range(2):
                kv = 2 * kvt + kv_half
                for pair in range(GROUP // 2):
                    h0 = kv * GROUP + 2 * pair
                    tile = slice((h0 // 2) * V7X_LANES, (h0 // 2 + 1) * V7X_LANES)
                    q_t = q_ref[rows, tile]
                    halves = []
                    for q_half in range(2):
                        h = h0 + q_half
                        qm = jnp.where(low if q_half == 0 else ~low, q_t, zero)
                        kk = k_t if q_half == kv_half else k_swapped
                        s = lax.dot_general(qm, kk, (((1,), (1,)), ((), ())),
                                            preferred_element_type=F32)
                        s = s + bias_s[table, h]
                        sink = sink_ref[h]
                        m = jnp.maximum(jnp.max(s, axis=-1, keepdims=True), sink)
                        p = jnp.exp(s - m)
                        denom = jnp.sum(p, axis=-1, keepdims=True) + jnp.exp(sink - m)
                        halves.append(jnp.dot(p.astype(BF16), v_t, preferred_element_type=F32) / denom)
                    if kv_half == 0:
                        o = jnp.where(low, halves[0], pltpu.roll(halves[1], HEAD_DIM, 1))
                    else:
                        o = jnp.where(low, pltpu.roll(halves[0], HEAD_DIM, 1), halves[1])
                    o_ref[rows, tile] = o.astype(BF16)


def _attn(proj_rest, sinks, S):
    N = proj_rest.shape[0]
    bq = _tiles()["attn_q"]
    nsub = _tiles()["attn_blocks"]
    tq = nsub * bq
    per_b = S // tq
    k_col = (Q_DIM + 2 * D_MODEL) // KV_DIM
    v_col = k_col + 1

    def prev(i):
        return jnp.where(i % per_b == 0, i * nsub, i * nsub - 1)

    return pl.pallas_call(
        functools.partial(_attn_kernel, per_b=per_b),
        grid=(N // tq,),
        in_specs=[
            pl.BlockSpec(memory_space=pltpu.SMEM),
            pl.BlockSpec((tq, Q_DIM), lambda i: (i, 0)),
            pl.BlockSpec((bq, KV_DIM), lambda i: (prev(i), k_col)),
            pl.BlockSpec((tq, KV_DIM), lambda i: (i, k_col)),
            pl.BlockSpec((bq, KV_DIM), lambda i: (prev(i), v_col)),
            pl.BlockSpec((tq, KV_DIM), lambda i: (i, v_col)),
        ],
        out_specs=pl.BlockSpec((tq, Q_DIM), lambda i: (i, 0)),
        out_shape=jax.ShapeDtypeStruct((N, Q_DIM), BF16),
        scratch_shapes=[pltpu.VMEM((2, N_HEADS, bq, 2 * bq), F32)],
        compiler_params=pltpu.CompilerParams(
            dimension_semantics=("arbitrary",),
            vmem_limit_bytes=_vmem_limit(3 * 2 * N_HEADS * bq * 2 * bq * 4)),
        name="attn",
    )(sinks, proj_rest, proj_rest, proj_rest, proj_rest, proj_rest)


def _merge_kernel(x_ref, yr_ref, ya_ref, gr_ref, ga_ref, wr_ref, wa_ref, wo_ref,
                  gpost_ref, g1_ref, gpre_ref, sc2_ref, sh2_ref, rwt_ref, rb_ref,
                  x1_ref, h2_ref, lg_ref):
    r = jnp.dot(yr_ref[...], wr_ref[...], preferred_element_type=F32)
    a = jnp.dot(ya_ref[...], wa_ref[...], preferred_element_type=F32)
    merged = ((0.5 * jnp.tanh(gr_ref[...].astype(F32)) + 0.5) * r
              + (0.5 * jnp.tanh(ga_ref[...].astype(F32)) + 0.5) * a)
    mix = jnp.dot(merged.astype(BF16), wo_ref[...], preferred_element_type=F32)
    x1 = x_ref[...] + g1_ref[0] * _rms(mix, gpost_ref[...])
    x1_ref[...] = x1
    h2 = (_rms(x1, gpre_ref[...]) * (1.0 + sc2_ref[0]) + sh2_ref[0]).astype(BF16)
    h2_ref[...] = h2
    lg = lax.dot_general(rwt_ref[...], h2, (((1,), (1,)), ((), ())),
                         preferred_element_type=F32)
    lg_ref[...] = lg + rb_ref[...]


def _merge(x2, y_rnn, y_att, proj_rest, wr, wa, wo, gpost, g1, gpre, sc2, sh2, rwt, rb, S):
    N, D = x2.shape
    tm = _tiles()["tok"]
    per_b = S // tm
    gate_r_col = Q_DIM // D
    mat = lambda: pl.BlockSpec((D, D), lambda i: (0, 0))
    vec = lambda: pl.BlockSpec((1, D), lambda i: (0, 0))
    bvec = lambda: pl.BlockSpec((1, 1, D), lambda i: (i // per_b, 0, 0))
    tile = lambda col=0: pl.BlockSpec((tm, D), lambda i: (i, col))
    return pl.pallas_call(
        _merge_kernel,
        grid=(N // tm,),
        in_specs=[
            tile(), tile(), tile(), tile(gate_r_col), tile(gate_r_col + 1),
            mat(), mat(), mat(),
            vec(), bvec(), vec(), bvec(), bvec(),
            pl.BlockSpec((N_EXPERTS, D), lambda i: (0, 0)),
            pl.BlockSpec((N_EXPERTS, 1), lambda i: (0, 0)),
        ],
        out_specs=[
            tile(), tile(),
            pl.BlockSpec((N_EXPERTS, tm), lambda i: (0, i)),
        ],
        out_shape=[
            jax.ShapeDtypeStruct((N, D), F32),
            jax.ShapeDtypeStruct((N, D), BF16),
            jax.ShapeDtypeStruct((N_EXPERTS, N), F32),
        ],
        compiler_params=pltpu.CompilerParams(
            dimension_semantics=("arbitrary",),
            vmem_limit_bytes=_vmem_limit(6 * D * D * 2 + 24 * tm * D * 4)),
        name="merge",
    )(x2, y_rnn, y_att, proj_rest, proj_rest, wr, wa, wo, gpost, g1, gpre, sc2, sh2, rwt, rb)


def _route_kernel(lg_ref, g_ref, slot_ref, cnt_ref, tcnt_ref, tcar_ref, tlst_ref, carry, *, T):
    i = pl.program_id(0)
    E = lg_ref.shape[0]
    n_sub = lg_ref.shape[1] // T

    @pl.when(i == 0)
    def _():
        carry[...] = jnp.zeros_like(carry)
        tcnt_ref[...] = jnp.zeros_like(tcnt_ref)
        tcar_ref[...] = jnp.zeros_like(tcar_ref)
        tlst_ref[...] = jnp.zeros_like(tlst_ref)

    row = lax.broadcasted_iota(jnp.int32, (E, T), 0).astype(F32)
    tri_t = (lax.broadcasted_iota(jnp.int32, (T, T), 0)
             < lax.broadcasted_iota(jnp.int32, (T, T), 1)).astype(BF16)
    tri_e = (lax.broadcasted_iota(jnp.int32, (E, E), 1)
             < lax.broadcasted_iota(jnp.int32, (E, E), 0)).astype(BF16)
    tile_lane = lax.broadcasted_iota(jnp.int32, tcnt_ref.shape, 1)
    running = carry[...]
    for sub in range(n_sub):
        cols = slice(sub * T, (sub + 1) * T)
        l = lg_ref[:, cols]
        vals, idxs = [], []
        for _ in range(TOP_K):
            m = jnp.max(l, axis=0, keepdims=True)
            idx = jnp.min(jnp.where(l == m, row, float(E)), axis=0, keepdims=True)
            vals.append(m)
            idxs.append(idx)
            l = jnp.where(row == idx, -jnp.inf, l)
        ex = [jnp.exp(v - vals[0]) for v in vals]
        tot = ex[0]
        for e in ex[1:]:
            tot = tot + e
        g_ref[:, cols] = jnp.concatenate([e / tot for e in ex], axis=0)

        hot = [row == idx for idx in idxs]
        onehot = jnp.zeros((E, T), F32)
        for hk in hot:
            onehot = onehot + hk.astype(F32)
        before = jnp.dot(onehot.astype(BF16), tri_t, preferred_element_type=F32)
        cnt = jnp.sum(onehot, axis=1, keepdims=True)
        lstart = jnp.dot(tri_e, jnp.broadcast_to(cnt, (E, V7X_LANES)).astype(BF16),
                         preferred_element_type=F32)[:, 0:1]
        local = before + lstart
        slots = [jnp.sum(jnp.where(hk, local, 0.0), axis=0, keepdims=True) for hk in hot]
        slot_ref[:, cols] = jnp.concatenate(slots, axis=0).astype(jnp.int32)

        mine = tile_lane == i * n_sub + sub
        tcnt_ref[...] = jnp.where(mine, cnt.astype(jnp.int32), tcnt_ref[...])
        tcar_ref[...] = jnp.where(mine, running.astype(jnp.int32), tcar_ref[...])
        tlst_ref[...] = jnp.where(mine, lstart.astype(jnp.int32), tlst_ref[...])
        running = running + cnt
    carry[...] = running
    cnt_ref[...] = running.astype(jnp.int32)


def _route(logits_t):
    E, N = logits_t.shape
    T = _tiles()["moe_t"]
    TS = T * _tiles()["route_tiles"]
    assert T <= 2 ** 8 and N // T <= V7X_LANES
    out = lambda: pl.BlockSpec((TOP_K, TS), lambda i: (0, i))
    per_tile = lambda: pl.BlockSpec((E, V7X_LANES), lambda i: (0, 0))
    return pl.pallas_call(
        functools.partial(_route_kernel, T=T),
        grid=(N // TS,),
        in_specs=[pl.BlockSpec((E, TS), lambda i: (0, i))],
        out_specs=[out(), out(), pl.BlockSpec((E, 1), lambda i: (0, 0)),
                   per_tile(), per_tile(), per_tile()],
        out_shape=[
            jax.ShapeDtypeStruct((TOP_K, N), F32),
            jax.ShapeDtypeStruct((TOP_K, N), jnp.int32),
            jax.ShapeDtypeStruct((E, 1), jnp.int32),
            jax.ShapeDtypeStruct((E, V7X_LANES), jnp.int32),
            jax.ShapeDtypeStruct((E, V7X_LANES), jnp.int32),
            jax.ShapeDtypeStruct((E, V7X_LANES), jnp.int32),
        ],
        scratch_shapes=[pltpu.VMEM((E, 1), F32)],
        compiler_params=pltpu.CompilerParams(dimension_semantics=("arbitrary",)),
        name="route",
    )(logits_t)


def _run_sizes(T):
    return [2 ** b for b in range(int(math.log2(T)), -1, -1)]


def _copy_plan(tile_cnt, tile_lst, tile_dst, T):
    sizes = jnp.asarray(_run_sizes(T), jnp.int32)[None, :, None]
    cnt, lst, dst = (a[:, None, :] for a in (tile_cnt, tile_lst, tile_dst))
    has = (cnt & sizes) != 0
    off = (cnt // (2 * sizes)) * (2 * sizes)
    eids = jnp.arange(N_EXPERTS)
    earlier = eids[:, None] < eids[None, :]
    place = jnp.sum(jnp.where(earlier, has[..., :, None], False), axis=-2)
    hit = has[..., :, None] & (place[..., :, None] == eids)
    pick = lambda v: jnp.sum(jnp.where(hit, v[..., :, None], 0), axis=-2)
    n_tiles, nb = tile_cnt.shape[0], sizes.shape[1]
    parts = [pick(jnp.broadcast_to(lst + off, has.shape)).reshape(n_tiles, -1),
             pick(jnp.broadcast_to(dst + off, has.shape)).reshape(n_tiles, -1),
             jnp.sum(has, axis=-1).reshape(n_tiles, nb)]
    plan = jnp.concatenate(parts, axis=1).astype(jnp.int32)
    pad = _plan_len(T) - plan.shape[1]
    return jnp.pad(plan, ((0, 0), (0, pad))).reshape(-1)


def _plan_len(T):
    nb = len(_run_sizes(T))
    return max(V7X_LANES, pl.next_power_of_2(2 * nb * N_EXPERTS + nb))


def _run_copies(plan_ref, T, make):
    sizes = _run_sizes(T)
    nb = len(sizes)
    for bi, b in enumerate(sizes):
        def body(j, c, bi=bi, b=b):
            make(plan_ref[bi * N_EXPERTS + j], plan_ref[(nb + bi) * N_EXPERTS + j], b).start()
            return c

        lax.fori_loop(0, plan_ref[2 * nb * N_EXPERTS + bi], body, 0)


def _row_slab(ref, row, nrows, lead=()):
    rows = pl.ds(pl.multiple_of(row * V7X_SUBLANES, V7X_SUBLANES), nrows * V7X_SUBLANES)
    return ref.at[(*lead, rows)]


def _to_slabs(ref, val, lead=()):
    rows = val.shape[0]
    for s in range(V7X_SUBLANES):
        ref[(*lead, pl.ds(s, rows, stride=V7X_SUBLANES), slice(None))] = (
            val[:, s * V7X_LANES:(s + 1) * V7X_LANES])


def _from_slabs(ref, lead=(), rows=None):
    rows = ref.shape[-2] // V7X_SUBLANES if rows is None else rows
    return jnp.concatenate(
        [ref[(*lead, pl.ds(s, rows, stride=V7X_SUBLANES), slice(None))]
         for s in range(V7X_SUBLANES)], axis=1)


def _dispatch_kernel(tail_ref, nu_ref, plan_ref, slot_ref, h_ref, xs_ref, buf, zbuf, sem, zsem):
    i = pl.program_id(0)
    T = h_ref.shape[0]
    bm = zbuf.shape[0] // V7X_SUBLANES
    nblk = xs_ref.shape[0] // zbuf.shape[0]
    depth = buf.shape[0]
    cur = i % depth

    def zero_copy(row0, s):
        return pltpu.make_async_copy(zbuf, _row_slab(xs_ref, row0, bm), zsem.at[s])

    def read_chunk(e, fn):
        @pl.when(tail_ref[e] >= 0)
        def _():
            fn(zero_copy(tail_ref[e], 0))

    def unread_chunks(fn):
        def per_expert(e, c):
            @pl.when(jnp.logical_and(tail_ref[e] >= 0, tail_ref[e] % (2 * bm) == 0))
            def _():
                fn(zero_copy(tail_ref[e] + bm, 1))
            return c

        def per_unused(j, c):
            fn(zero_copy(j * bm, 1))
            return c

        lax.fori_loop(0, N_EXPERTS, per_expert, 0)
        lax.fori_loop(nu_ref[0], nblk, per_unused, 0)

    @pl.when(i == 0)
    def _():
        zbuf[...] = jnp.zeros_like(zbuf)

        def start(e, c):
            read_chunk(e, lambda cp: cp.start())
            return c

        def wait(e, c):
            read_chunk(e, lambda cp: cp.wait())
            return c

        lax.fori_loop(0, N_EXPERTS, start, 0)
        unread_chunks(lambda cp: cp.start())
        lax.fori_loop(0, N_EXPERTS, wait, 0)

    slot = slot_ref[...]
    r_id = lax.broadcasted_iota(jnp.int32, (TOP_K * T, T), 0)
    perm = r_id == slot[0:1, :]
    for k in range(1, TOP_K):
        perm = jnp.logical_or(perm, r_id == slot[k:k + 1, :])
    grouped = jnp.dot(perm.astype(BF16), h_ref[...], preferred_element_type=F32)
    _to_slabs(buf, grouped, lead=(cur,))

    def wait_tile(b):
        pltpu.make_async_copy(buf.at[b], _row_slab(xs_ref, 0, TOP_K * T), sem.at[b]).wait()

    @pl.when(i >= depth - 1)
    def _():
        wait_tile((i + 1) % depth)

    _run_copies(plan_ref, T, lambda l, d, n: pltpu.make_async_copy(
        _row_slab(buf, l, n, lead=(cur,)), _row_slab(xs_ref, d, n), sem.at[cur]))

    @pl.when(i == pl.num_programs(0) - 1)
    def _():
        for back in range(depth - 1):
            @pl.when(i >= back)
            def _(back=back):
                wait_tile((i - back) % depth)
        unread_chunks(lambda cp: cp.wait())


def _dispatch(h2, slot, plan, tail_start, n_used, n_rows, bm):
    N, D = h2.shape
    T = _tiles()["moe_t"]
    ring = 3
    S8 = V7X_SUBLANES
    imap = lambda f: (lambda i, *_: f(i))
    return pl.pallas_call(
        _dispatch_kernel,
        grid_spec=pltpu.PrefetchScalarGridSpec(
            num_scalar_prefetch=2,
            grid=(N // T,),
            in_specs=[
                pl.BlockSpec((_plan_len(T),), imap(lambda i: (i,)), memory_space=pltpu.SMEM),
                pl.BlockSpec((TOP_K, T), imap(lambda i: (0, i))),
                pl.BlockSpec((T, D), imap(lambda i: (i, 0))),
            ],
            out_specs=pl.BlockSpec(memory_space=pl.ANY),
            scratch_shapes=[
                pltpu.VMEM((ring, TOP_K * T * S8, V7X_LANES), F32),
                pltpu.VMEM((bm * S8, V7X_LANES), F32),
                pltpu.SemaphoreType.DMA((ring,)),
                pltpu.SemaphoreType.DMA((2,)),
            ],
        ),
        out_shape=jax.ShapeDtypeStruct((n_rows * S8, V7X_LANES), F32),
        compiler_params=pltpu.CompilerParams(
            dimension_semantics=("arbitrary",),
            vmem_limit_bytes=_vmem_limit(8 * TOP_K * T * D * 4)),
        name="dispatch",
    )(tail_start, n_used, plan, slot, h2)


def _ffn_kernel(be_ref, bv_ref, bf_ref, nu_ref, nxt_ref, x_ref, w1_hbm, b1g_ref, b1l_ref, w2_hbm,
                b2_ref, y_ref, w1f_s, w2f_s, w1_s, w2_s, sem):
    i = pl.program_id(0)
    bm, D = x_ref.shape[0] // V7X_SUBLANES, w2_s.shape[1]
    n_ff = w2_s.shape[0]
    pw = 2 * V7X_LANES

    def fetch(e):
        return (pltpu.make_async_copy(w1_hbm.at[e], w1f_s, sem.at[0]),
                pltpu.make_async_copy(w2_hbm.at[e], w2f_s, sem.at[1]))

    @pl.when(i == 0)
    def _():
        for cp in fetch(be_ref[0]):
            cp.start()

    @pl.when(jnp.logical_and(i < nu_ref[0], bf_ref[i] == 1))
    def _():
        e = be_ref[i]
        for cp in fetch(e):
            cp.wait()
        src = lax.broadcasted_iota(jnp.int32, (pw, pw), 0)
        dst = lax.broadcasted_iota(jnp.int32, (pw, pw), 1)
        want = jnp.where(dst < V7X_LANES, 2 * dst, 2 * (dst - V7X_LANES) + 1)
        perm = (src == want).astype(BF16)
        for c in range(w1f_s.shape[1] // pw):
            wp = jnp.dot(w1f_s[:, c * pw:(c + 1) * pw].astype(BF16), perm,
                         preferred_element_type=F32).astype(BF16)
            w1_s[:, c * V7X_LANES:(c + 1) * V7X_LANES] = wp[:, :V7X_LANES]
            w1_s[:, n_ff + c * V7X_LANES:n_ff + (c + 1) * V7X_LANES] = wp[:, V7X_LANES:]
        w2_s[...] = w2f_s[...].astype(BF16)

        @pl.when(nxt_ref[e] >= 0)
        def _():
            for cp in fetch(nxt_ref[e]):
                cp.start()

    def mlp(n_rows):
        x = _from_slabs(x_ref, rows=n_rows).astype(BF16)
        hid = jnp.dot(x, w1_s[...], preferred_element_type=F32)
        glu = hid[:, :n_ff] + b1g_ref[0]
        lin = hid[:, n_ff:] + b1l_ref[0]
        glu = jnp.minimum(glu, SWIGLU_LIMIT)
        lin = jnp.clip(lin, -SWIGLU_LIMIT, SWIGLU_LIMIT)
        act = glu * _sigmoid(SWIGLU_ALPHA * glu) * (lin + 1.0)
        _to_slabs(y_ref, jnp.dot(act.astype(BF16), w2_s[...], preferred_element_type=F32) + b2_ref[0])
        if n_rows < bm:
            y_ref[n_rows * V7X_SUBLANES:, :] = jnp.zeros(
                ((bm - n_rows) * V7X_SUBLANES, V7X_LANES), F32)

    used = i < nu_ref[0]
    half = bm // 2

    @pl.when(jnp.logical_and(used, bv_ref[i] > half))
    def _():
        mlp(bm)

    @pl.when(jnp.logical_and(used, bv_ref[i] <= half))
    def _():
        mlp(half)

    @pl.when(i >= nu_ref[0])
    def _():
        y_ref[...] = jnp.zeros_like(y_ref)


def _ffn(xs, blk_e, blk_valid, blk_first, n_used, next_e, w1, b1g, b1l, w2, b2, bm):
    E, D, F2 = w1.shape
    F = F2 // 2
    slab = bm * V7X_SUBLANES
    nblk = xs.shape[0] // slab

    def row_blk(i, be, bv, bf, nu, nxt):
        return (jnp.minimum(i, nu[0] - 1), 0)

    def per_e(i, be, bv, bf, nu, nxt):
        return (be[i], 0, 0)

    vmem = (D * F2 + F * D) * (4 + 2) + 8 * bm * F2 * 4
    return pl.pallas_call(
        _ffn_kernel,
        grid_spec=pltpu.PrefetchScalarGridSpec(
            num_scalar_prefetch=5,
            grid=(nblk,),
            in_specs=[
                pl.BlockSpec((slab, V7X_LANES), row_blk),
                pl.BlockSpec(memory_space=pl.ANY),
                pl.BlockSpec((1, 1, F), per_e),
                pl.BlockSpec((1, 1, F), per_e),
                pl.BlockSpec(memory_space=pl.ANY),
                pl.BlockSpec((1, 1, D), per_e),
            ],
            out_specs=pl.BlockSpec((slab, V7X_LANES), lambda i, be, bv, bf, nu, nxt: (i, 0)),
            scratch_shapes=[
                pltpu.VMEM((D, F2), F32),
                pltpu.VMEM((F, D), F32),
                pltpu.VMEM((D, F2), BF16),
                pltpu.VMEM((F, D), BF16),
                pltpu.SemaphoreType.DMA((2,)),
            ],
        ),
        out_shape=jax.ShapeDtypeStruct(xs.shape, F32),
        compiler_params=pltpu.CompilerParams(
            dimension_semantics=("arbitrary",),
            vmem_limit_bytes=_vmem_limit(vmem)),
        name="ffn",
    )(blk_e, blk_valid, blk_first, n_used, next_e, xs, w1, b1g, b1l, w2, b2)


def _combine_kernel(plan_ref, plan1_ref, plan2_ref, ys_ref, x1_ref, slot_ref, gate_ref, gpost_ref,
                    g2_ref, o_ref, buf, sem):
    i = pl.program_id(0)
    n = pl.num_programs(0)
    T = x1_ref.shape[0]
    depth = buf.shape[0]
    assert depth == 3
    cur = i % depth

    def gather(plan, b):
        _run_copies(plan, T, lambda l, d, n: pltpu.make_async_copy(
            _row_slab(ys_ref, d, n), _row_slab(buf, l, n, lead=(b,)), sem.at[b]))

    @pl.when(i == 0)
    def _():
        gather(plan_ref, 0)

        @pl.when(n > 1)
        def _():
            gather(plan1_ref, 1)

    @pl.when(i + 2 < n)
    def _():
        gather(plan2_ref, (i + 2) % depth)

    pltpu.make_async_copy(_row_slab(ys_ref, 0, TOP_K * T), buf.at[cur], sem.at[cur]).wait()

    slot = slot_ref[...]
    gates = gate_ref[...]
    r_id = lax.broadcasted_iota(jnp.int32, (TOP_K * T, T), 0)
    wt = jnp.zeros((TOP_K * T, T), F32)
    for k in range(TOP_K):
        wt = jnp.where(r_id == slot[k:k + 1, :], gates[k:k + 1, :], wt)
    ff = lax.dot_general(wt.astype(BF16), _from_slabs(buf, lead=(cur,)).astype(BF16),
                         (((0,), (0,)), ((), ())), preferred_element_type=F32)
    o_ref[...] = x1_ref[...] + g2_ref[0] * _rms(ff, gpost_ref[...])


def _combine(ys, plan, x1, slot_tk, gates_tk, gpost, g2, S):
    N, D = x1.shape
    T = _tiles()["moe_t"]
    per_b = S // T
    n_tiles = N // T
    plan_spec = lambda f: pl.BlockSpec((_plan_len(T),), f, memory_space=pltpu.SMEM)
    return pl.pallas_call(
        _combine_kernel,
        grid=(n_tiles,),
        in_specs=[
            plan_spec(lambda i: (i,)),
            plan_spec(lambda i: (jnp.minimum(i + 1, n_tiles - 1),)),
            plan_spec(lambda i: (jnp.minimum(i + 2, n_tiles - 1),)),
            pl.BlockSpec(memory_space=pl.ANY),
            pl.BlockSpec((T, D), lambda i: (i, 0)),
            pl.BlockSpec((TOP_K, T), lambda i: (0, i)),
            pl.BlockSpec((TOP_K, T), lambda i: (0, i)),
            pl.BlockSpec((1, D), lambda i: (0, 0)),
            pl.BlockSpec((1, 1, D), lambda i: (i // per_b, 0, 0)),
        ],
        out_specs=pl.BlockSpec((T, D), lambda i: (i, 0)),
        out_shape=jax.ShapeDtypeStruct((N, D), F32),
        scratch_shapes=[
            pltpu.VMEM((3, TOP_K * T * V7X_SUBLANES, V7X_LANES), F32),
            pltpu.SemaphoreType.DMA((3,)),
        ],
        compiler_params=pltpu.CompilerParams(
            dimension_semantics=("arbitrary",),
            vmem_limit_bytes=_vmem_limit(11 * TOP_K * T * D * 4)),
        name="combine",
    )(plan, plan, plan, ys, x1, slot_tk, gates_tk, gpost, g2)


def _col_scaled(w):
    scale = HEAD_DIM ** -0.5
    assert math.log2(scale).is_integer()
    col = np.arange(D_IN)
    q0, g0 = 2 * D_RNN, 2 * D_RNN + Q_DIM + 2 * KV_DIM
    factor = np.where((col >= q0) & (col < q0 + Q_DIM), scale, np.where(col >= g0, 0.5, 1.0))
    return w * jnp.asarray(factor, w.dtype)


def _layer(x2, c, B, S, p):
    D = D_MODEL
    N = B * S
    ada = _ada(c, p["w_ada"], p["b_ada"])
    sh1, sc1, g1, sh2, sc2, g2 = [a.reshape(B, 1, D) for a in jnp.split(ada, 6, axis=-1)]
    row = lambda v: v.reshape(1, -1)

    gw = _tiles()["rnn_group"]
    proj_rest, y_rnn = _mixin(
        x2, row(p["norm_pre_mix"]), sc1, sh1,
        _col_scaled(p["w_in"]).astype(BF16), _col_scaled(row(p["b_in"])),
        p["conv_w"], row(p["conv_b"]),
        (0.5 * _block_diag_tiles(p["rg_w_a"], gw)).astype(BF16), 0.5 * row(p["rg_b_a"]),
        (0.5 * _block_diag_tiles(p["rg_w_x"], gw)).astype(BF16), 0.5 * row(p["rg_b_x"]),
        row(p["rg_lambda"]), S)
    y_att = _attn(proj_rest, p["attn_sinks"], S)

    x1, h2, logits_t = _merge(
        x2, y_rnn, y_att, proj_rest,
        p["w_o_rnn"].astype(BF16), p["w_o_attn"].astype(BF16), p["w_out"].astype(BF16),
        row(p["norm_post_mix"]), g1, row(p["norm_pre_ffn"]), sc2, sh2,
        p["router_w"].T.astype(BF16), p["router_b"].reshape(N_EXPERTS, 1), S)

    gates, slot, counts, tile_cnt, tile_car, tile_lst = _route(logits_t)

    bm = _tiles()["moe_bm"]
    n_tiles = N // _tiles()["moe_t"]
    n_rows = N * TOP_K + N_EXPERTS * bm
    nblk = n_rows // bm
    counts = counts.reshape(N_EXPERTS)
    padded = ((counts + bm - 1) // bm) * bm
    pend = jnp.cumsum(padded)
    pstart = pend - padded
    eids = jnp.arange(N_EXPERTS, dtype=jnp.int32)
    per_run = lambda a: a[:, :n_tiles].T.astype(jnp.int32)
    plan = _copy_plan(per_run(tile_cnt), per_run(tile_lst), per_run(tile_car + pstart[:, None]),
                      _tiles()["moe_t"])
    zr = bm // 2
    tail_start = jnp.where(counts > 0, (pstart + counts - 1) // zr * zr, -1).astype(jnp.int32)
    n_used = (pend[-1] // bm).astype(jnp.int32).reshape(1)
    blk_row0 = jnp.minimum(jnp.arange(nblk, dtype=jnp.int32), n_used[0] - 1) * bm
    blk_e = jnp.sum(blk_row0[:, None] >= pend[None, :], axis=1).astype(jnp.int32)
    mine = blk_e[:, None] == eids[None, :]
    blk_cnt = jnp.sum(jnp.where(mine, counts, 0), axis=1)
    blk_pstart = jnp.sum(jnp.where(mine, pstart, 0), axis=1)
    blk_valid = jnp.clip(blk_cnt - (blk_row0 - blk_pstart), 0, bm).astype(jnp.int32)
    blk_first = (blk_row0 == blk_pstart).astype(jnp.int32)

    xs = _dispatch(h2, slot, plan, tail_start, n_used * (bm // zr), n_rows, zr)
    b1 = p["moe_b1"].reshape(N_EXPERTS, D_FF, 2)
    later = jnp.logical_and(counts[None, :] > 0, eids[None, :] > eids[:, None])
    next_e = jnp.min(jnp.where(later, eids[None, :], N_EXPERTS), axis=1)
    next_e = jnp.where(next_e == N_EXPERTS, -1, next_e).astype(jnp.int32)
    ys = _ffn(xs, blk_e, blk_valid, blk_first, n_used, next_e, p["moe_w1"],
              b1[:, :, 0].reshape(N_EXPERTS, 1, D_FF), b1[:, :, 1].reshape(N_EXPERTS, 1, D_FF),
              p["moe_w2"], p["moe_b2"].reshape(N_EXPERTS, 1, D), bm)
    return _combine(ys, plan, x1, slot, gates, row(p["norm_post_ffn"]), g2, S)


def kernel(x, c, w_ada, b_ada, norm_pre_mix, norm_post_mix, norm_pre_ffn, norm_post_ffn, w_in, b_in, conv_w, conv_b, rg_w_a, rg_b_a, rg_w_x, rg_b_x, rg_lambda, attn_sinks, w_o_rnn, w_o_attn, w_out, router_w, router_b, moe_w1, moe_b1, moe_w2, moe_b2):
    B, S, D = x.shape
    params = dict(
        w_ada=w_ada, b_ada=b_ada, norm_pre_mix=norm_pre_mix, norm_post_mix=norm_post_mix,
        norm_pre_ffn=norm_pre_ffn, norm_post_ffn=norm_post_ffn, w_in=w_in, b_in=b_in,
        conv_w=conv_w, conv_b=conv_b, rg_w_a=rg_w_a, rg_b_a=rg_b_a, rg_w_x=rg_w_x, rg_b_x=rg_b_x,
        rg_lambda=rg_lambda, attn_sinks=attn_sinks, w_o_rnn=w_o_rnn, w_o_attn=w_o_attn,
        w_out=w_out, router_w=router_w, router_b=router_b, moe_w1=moe_w1, moe_b1=moe_b1,
        moe_w2=moe_w2, moe_b2=moe_b2)
    x2 = x.reshape(B * S, D)
    for layer in range(w_ada.shape[0]):
        x2 = _layer(x2, c, B, S, {k: v[layer] for k, v in params.items()})
    return x2.reshape(B, S, D)
```

```python
import functools
import math

import jax
import jax.numpy as jnp
import numpy as np
from jax import lax
from jax.experimental import pallas as pl
from jax.experimental.pallas import tpu as pltpu

D_MODEL = 1024
D_RNN = 1024
RNN_BLOCKS = 16
RNN_BW = D_RNN // RNN_BLOCKS
CONV_W = 4
LRU_C = 8.0
N_HEADS = 16
N_KV = 4
HEAD_DIM = 64
GROUP = N_HEADS // N_KV
WINDOW = 128
Q_DIM = N_HEADS * HEAD_DIM
KV_DIM = N_KV * HEAD_DIM
N_EXPERTS = 32
TOP_K = 4
D_FF = 1024
SWIGLU_LIMIT = 7.0
SWIGLU_ALPHA = 1.702
EPS = 1e-6
D_IN = 2 * D_RNN + Q_DIM + 2 * KV_DIM + 2 * D_MODEL
D_REST = D_IN - 2 * D_RNN

V7X_LANES = 128
V7X_SUBLANES = 8
V7X_MXU_DIM = 256
V7X_VMEM_BYTES = 64 * 1024 * 1024

F32 = jnp.float32
BF16 = jnp.bfloat16


def _tiles():
    return dict(
        ada_tn=4 * V7X_MXU_DIM,
        tok=2 * V7X_MXU_DIM,
        attn_q=WINDOW,
        attn_blocks=2,
        moe_t=V7X_MXU_DIM,
        route_tiles=4,
        moe_bm=2 * V7X_MXU_DIM,
        rnn_group=V7X_MXU_DIM,
    )


V7X_VMEM_MIN_LIMIT = 16 * 1024 * 1024
V7X_VMEM_HEADROOM = 8 * 1024 * 1024


def _vmem_limit(nbytes):
    return int(min(max(nbytes, V7X_VMEM_MIN_LIMIT), V7X_VMEM_BYTES - V7X_VMEM_HEADROOM))


def _rms(x, g):
    return x * lax.rsqrt(jnp.mean(x * x, axis=-1, keepdims=True) + EPS) * g


def _ada_kernel(ct_ref, w_ref, b_ref, o_ref):
    ct = ct_ref[...]
    sc = ct * jax.nn.sigmoid(ct)
    w = w_ref[...]
    rows = [jnp.sum(w * sc[:, b:b + 1], axis=0, keepdims=True) for b in range(ct.shape[1])]
    o_ref[...] = jnp.concatenate(rows, axis=0) + b_ref[...]


def _ada(c, w_ada, b_ada):
    B, D = c.shape
    n_out = w_ada.shape[1]
    tn = _tiles()["ada_tn"]
    return pl.pallas_call(
        _ada_kernel,
        grid=(n_out // tn,),
        in_specs=[
            pl.BlockSpec((D, B), lambda j: (0, 0)),
            pl.BlockSpec((D, tn), lambda j: (0, j)),
            pl.BlockSpec((1, tn), lambda j: (0, j)),
        ],
        out_specs=pl.BlockSpec((B, tn), lambda j: (0, j)),
        out_shape=jax.ShapeDtypeStruct((B, n_out), F32),
        compiler_params=pltpu.CompilerParams(
            dimension_semantics=("arbitrary",),
            vmem_limit_bytes=_vmem_limit(4 * D * tn * 4)),
        name="ada",
    )(c.T, w_ada, b_ada.reshape(1, n_out))


def _gelu_tanh(x):
    return 0.5 * x * (1.0 + jnp.tanh(math.sqrt(2.0 / math.pi) * (x + 0.044715 * (x * x * x))))


def _softplus(z):
    return jnp.maximum(z, 0.0) + jnp.log1p(jnp.exp(-jnp.abs(z)))


def _sigmoid(x):
    return 0.5 * jnp.tanh(0.5 * x) + 0.5


def _rest_col(c):
    c -= 2 * D_RNN
    q_end, k_end, v_end = Q_DIM, Q_DIM + KV_DIM, Q_DIM + 2 * KV_DIM
    if c < q_end:
        return c
    if c < k_end:
        return Q_DIM + 2 * D_MODEL + (c - q_end)
    if c < v_end:
        return Q_DIM + 2 * D_MODEL + KV_DIM + (c - k_end)
    return Q_DIM + (c - v_end)


def _mixin_kernel(x_ref, g_ref, sc_ref, sh_ref, w_ref, b_ref, cw_ref, cb_ref, wa_ref, ba_ref,
                  wx_ref, bx_ref, lam_ref, rest_ref, y_ref,
                  hb_s, xbuf, gr_s, a_s, b_s, h_s, ga_s, gb_s, gc_s, carry, *, per_b, gw, chunk):
    t = pl.program_id(0) % per_b
    tt = x_ref.shape[0]
    halo = V7X_SUBLANES
    ng = tt // V7X_SUBLANES
    n_lt = a_s.shape[0]
    per_g = gw // V7X_LANES
    lanes = lambda j: slice(j * V7X_LANES, (j + 1) * V7X_LANES)
    slab = lambda j, r: (j, pl.ds(r, ng, stride=V7X_SUBLANES), slice(None))

    @pl.when(t == 0)
    def _():
        xbuf[:, 0:halo, :] = jnp.zeros((n_lt, halo, V7X_LANES), F32)
        carry[...] = jnp.zeros_like(carry)

    hb_s[...] = (_rms(x_ref[...], g_ref[...]) * (1.0 + sc_ref[0]) + sh_ref[0]).astype(BF16)

    def proj(c0):
        return (jnp.dot(hb_s[...], w_ref[:, c0:c0 + chunk], preferred_element_type=F32)
                + b_ref[:, c0:c0 + chunk])

    def proj_rest(c0):
        acc = proj(c0).astype(BF16)
        for p0 in range(0, chunk, KV_DIM):
            d0 = _rest_col(c0 + p0)
            rest_ref[:, d0:d0 + KV_DIM] = acc[:, p0:p0 + KV_DIM]

    for c0 in range(0, D_RNN, chunk):
        acc = proj(c0)
        for j in range(chunk // V7X_LANES):
            xbuf[c0 // V7X_LANES + j, halo:halo + tt, :] = acc[:, lanes(j)]
    for c0 in range(D_RNN, 2 * D_RNN, chunk):
        gr_s[:, c0 - D_RNN:c0 - D_RNN + chunk] = proj(c0)

    def conv(j):
        ls = lanes(j)
        taps = {o: xbuf[slab(j, halo + o)] for o in range(-(CONV_W - 1), V7X_SUBLANES)}
        for r in range(V7X_SUBLANES):
            acc = cb_ref[:, ls] + taps[r - (CONV_W - 1)] * cw_ref[0:1, ls]
            for kk in range(1, CONV_W):
                acc = acc + taps[r - (CONV_W - 1) + kk] * cw_ref[kk:kk + 1, ls]
            h_s[slab(j, r)] = acc
        xbuf[j, 0:halo, :] = xbuf[j, tt:tt + halo, :]

    def gates(g):
        cs = slice(g * gw, (g + 1) * gw)
        rate = (-LRU_C * math.log2(math.e)) * _softplus(-lam_ref[:, cs])
        reset = jnp.logical_and(t == 0, lax.broadcasted_iota(jnp.int32, (tt, gw), 0) == 0)
        xc = jnp.concatenate([h_s[g * per_g + j] for j in range(per_g)], axis=1)
        xg = xc.astype(BF16)
        gate_r = 0.5 * jnp.tanh(jnp.dot(xg, wa_ref[g], preferred_element_type=F32) + ba_ref[:, cs]) + 0.5
        gate_i = 0.5 * jnp.tanh(jnp.dot(xg, wx_ref[g], preferred_element_type=F32) + bx_ref[:, cs]) + 0.5
        a = jnp.exp2(gate_r * rate)
        v = (1.0 - a) * (1.0 + a)
        mult = jnp.where(reset, 1.0, jnp.where(v > 0.0, v * lax.rsqrt(v), 0.0))
        bt = (xc * gate_i) * mult
        for j in range(per_g):
            a_s[g * per_g + j] = a[:, lanes(j)]
            b_s[g * per_g + j] = bt[:, lanes(j)]

    def scan_groups(j):
        acc_a = a_s[slab(j, 0)]
        acc_h = b_s[slab(j, 0)]
        h_s[slab(j, 0)] = acc_h
        for r in range(1, V7X_SUBLANES):
            a_r = a_s[slab(j, r)]
            acc_h = a_r * acc_h + b_s[slab(j, r)]
            acc_a = a_r * acc_a
            h_s[slab(j, r)] = acc_h
            a_s[slab(j, r)] = acc_a
        ga_s[:, lanes(j)] = acc_a
        gb_s[:, lanes(j)] = acc_h

    def across(gi, h_prev):
        gc_s[pl.ds(gi, 1), :] = h_prev
        return ga_s[pl.ds(gi, 1), :] * h_prev + gb_s[pl.ds(gi, 1), :]

    def finish(j):
        h_in = gc_s[:, lanes(j)]
        for r in range(V7X_SUBLANES):
            h_s[slab(j, r)] = a_s[slab(j, r)] * h_in + h_s[slab(j, r)]
        y_ref[:, lanes(j)] = (h_s[j] * _gelu_tanh(gr_s[:, lanes(j)])).astype(BF16)

    rest = [functools.partial(proj_rest, c0) for c0 in range(2 * D_RNN, w_ref.shape[1], chunk)]
    before = ([functools.partial(conv, j) for j in range(n_lt)]
              + [functools.partial(gates, g) for g in range(n_lt // per_g)]
              + [functools.partial(scan_groups, j) for j in range(n_lt)])
    after = [functools.partial(finish, j) for j in range(n_lt)]
    n_before = (len(rest) * len(before)) // (len(before) + len(after))

    def interleave(steps, chunks):
        every = -(-len(steps) // max(len(chunks), 1))
        for n, step in enumerate(steps):
            step()
            if (n + 1) % every == 0 and chunks:
                chunks.pop(0)()
        while chunks:
            chunks.pop(0)()

    interleave(before, rest[:n_before])
    carry[...] = lax.fori_loop(0, ng, across, carry[...])
    interleave(after, rest[n_before:])


def _block_diag_tiles(w, gw):
    nb, bw, _ = w.shape
    per = gw // bw
    w4 = w.reshape(nb // per, per, bw, bw)
    eye = jnp.eye(per, dtype=w.dtype)
    return jnp.einsum("gpij,pq->gpiqj", w4, eye).reshape(nb // per, gw, gw)


def _mixin(x2, g, sc, sh, w_bf, b_in, conv_w, conv_b, wa, ba, wx, bx, lam, S):
    N, D = x2.shape
    C = D_RNN
    tt = _tiles()["tok"]
    gw = _tiles()["rnn_group"]
    per_b = S // tt
    chunk = 2 * V7X_MXU_DIM
    vec = lambda: pl.BlockSpec((1, C), lambda i: (0, 0))
    bvec = lambda: pl.BlockSpec((1, 1, D), lambda i: (i // per_b, 0, 0))
    gate_w = lambda: pl.BlockSpec((C // gw, gw, gw), lambda i: (0, 0, 0))
    slabs = lambda rows: pltpu.VMEM((C // V7X_LANES, rows, V7X_LANES), F32)
    groups = lambda: pltpu.VMEM((tt // V7X_SUBLANES, C), F32)
    vmem = D * D_IN * 2 + 2 * tt * (D * 4 + D_REST * 2 + C * 2) + tt * C * (2 + 5 * 4) + 8 * tt * chunk * 4
    return pl.pallas_call(
        functools.partial(_mixin_kernel, per_b=per_b, gw=gw, chunk=chunk),
        grid=(N // tt,),
        in_specs=[
            pl.BlockSpec((tt, D), lambda i: (i, 0)),
            pl.BlockSpec((1, D), lambda i: (0, 0)),
            bvec(), bvec(),
            pl.BlockSpec((D, D_IN), lambda i: (0, 0), pipeline_mode=pl.Buffered(1)),
            pl.BlockSpec((1, D_IN), lambda i: (0, 0)),
            pl.BlockSpec((CONV_W, C), lambda i: (0, 0)),
            vec(), gate_w(), vec(), gate_w(), vec(), vec(),
        ],
        out_specs=[
            pl.BlockSpec((tt, D_REST), lambda i: (i, 0)),
            pl.BlockSpec((tt, C), lambda i: (i, 0)),
        ],
        out_shape=[
            jax.ShapeDtypeStruct((N, D_REST), BF16),
            jax.ShapeDtypeStruct((N, C), BF16),
        ],
        scratch_shapes=[
            pltpu.VMEM((tt, D), BF16),
            slabs(tt + V7X_SUBLANES),
            pltpu.VMEM((tt, C), F32),
            slabs(tt), slabs(tt), slabs(tt),
            groups(), groups(), groups(),
            pltpu.VMEM((1, C), F32),
        ],
        compiler_params=pltpu.CompilerParams(
            dimension_semantics=("arbitrary",),
            vmem_limit_bytes=_vmem_limit(vmem)),
        name="mixin",
    )(x2, g, sc, sh, w_bf, b_in, conv_w, conv_b, wa, ba, wx, bx, lam)


def _alibi_slopes():
    return [2.0 ** (-8.0 * (h + 1) / N_HEADS) for h in range(N_HEADS)]


def _attn_kernel(sink_ref, q_ref, kp_ref, kc_ref, vp_ref, vc_ref, o_ref, bias_s, *, per_b):
    first = pl.program_id(0) % per_b == 0
    bq = kp_ref.shape[0]
    slopes = _alibi_slopes()

    @pl.when(pl.program_id(0) == 0)
    def _():
        qi = lax.broadcasted_iota(jnp.int32, (bq, 2 * bq), 0)
        ci = lax.broadcasted_iota(jnp.int32, (bq, 2 * bq), 1)
        dist = qi + bq - ci
        valid = (dist >= 0) & (dist < WINDOW)
        distf = dist.astype(F32)
        for h in range(N_HEADS):
            b = jnp.where(valid, -slopes[h] * distf, -jnp.inf)
            bias_s[0, h] = jnp.where(ci >= bq, b, -jnp.inf)
            bias_s[1, h] = b

    assert 2 * HEAD_DIM == V7X_LANES and GROUP % 2 == 0
    low = lax.broadcasted_iota(jnp.int32, (bq, V7X_LANES), 1) < HEAD_DIM
    zero = jnp.zeros((bq, V7X_LANES), BF16)
    for sub in range(q_ref.shape[0] // bq):
        rows = slice(sub * bq, (sub + 1) * bq)
        before = slice((sub - 1) * bq, sub * bq)
        table = jnp.where(first, 0, 1) if sub == 0 else 1
        for kvt in range(N_KV // 2):
            lt = slice(kvt * V7X_LANES, (kvt + 1) * V7X_LANES)
            k_prev, v_prev = (kp_ref[:, lt], vp_ref[:, lt]) if sub == 0 else (kc_ref[before, lt], vc_ref[before, lt])
            k_t = jnp.concatenate([k_prev, kc_ref[rows, lt]], axis=0)
            v_t = jnp.concatenate([v_prev, vc_ref[rows, lt]], axis=0)
            k_swapped = pltpu.roll(k_t, HEAD_DIM, 1)
            for kv_half in range(2):
                kv = 2 * kvt + kv_half
                for pair in range(GROUP // 2):
                    h0 = kv * GROUP + 2 * pair
                    tile = slice((h0 // 2) * V7X_LANES, (h0 // 2 + 1) * V7X_LANES)
                    q_t = q_ref[rows, tile]
                    halves = []
                    for q_half in range(2):
                        h = h0 + q_half
                        qm = jnp.where(low if q_half == 0 else ~low, q_t, zero)
                        kk = k_t if q_half == kv_half else k_swapped
                        s = lax.dot_general(qm, kk, (((1,), (1,)), ((), ())),
                                            preferred_element_type=F32)
                        s = s + bias_s[table, h]
                        sink = sink_ref[h]
                        m = jnp.maximum(jnp.max(s, axis=-1, keepdims=True), sink)
                        p = jnp.exp(s - m)
                        denom = jnp.sum(p, axis=-1, keepdims=True) + jnp.exp(sink - m)
                        halves.append(jnp.dot(p.astype(BF16), v_t, preferred_element_type=F32) / denom)
                    if kv_half == 0:
                        o = jnp.where(low, halves[0], pltpu.roll(halves[1], HEAD_DIM, 1))
                    else:
                        o = jnp.where(low, pltpu.roll(halves[0], HEAD_DIM, 1), halves[1])
                    o_ref[rows, tile] = o.astype(BF16)


def _attn(proj_rest, sinks, S):
    N = proj_rest.shape[0]
    bq = _tiles()["attn_q"]
    nsub = _tiles()["attn_blocks"]
    tq = nsub * bq
    per_b = S // tq
    k_col = (Q_DIM + 2 * D_MODEL) // KV_DIM
    v_col = k_col + 1

    def prev(i):
        return jnp.where(i % per_b == 0, i * nsub, i * nsub - 1)

    return pl.pallas_call(
        functools.partial(_attn_kernel, per_b=per_b),
        grid=(N // tq,),
        in_specs=[
            pl.BlockSpec(memory_space=pltpu.SMEM),
            pl.BlockSpec((tq, Q_DIM), lambda i: (i, 0)),
            pl.BlockSpec((bq, KV_DIM), lambda i: (prev(i), k_col)),
            pl.BlockSpec((tq, KV_DIM), lambda i: (i, k_col)),
            pl.BlockSpec((bq, KV_DIM), lambda i: (prev(i), v_col)),
            pl.BlockSpec((tq, KV_DIM), lambda i: (i, v_col)),
        ],
        out_specs=pl.BlockSpec((tq, Q_DIM), lambda i: (i, 0)),
        out_shape=jax.ShapeDtypeStruct((N, Q_DIM), BF16),
        scratch_shapes=[pltpu.VMEM((2, N_HEADS, bq, 2 * bq), F32)],
        compiler_params=pltpu.CompilerParams(
            dimension_semantics=("arbitrary",),
            vmem_limit_bytes=_vmem_limit(3 * 2 * N_HEADS * bq * 2 * bq * 4)),
        name="attn",
    )(sinks, proj_rest, proj_rest, proj_rest, proj_rest, proj_rest)


def _merge_kernel(x_ref, yr_ref, ya_ref, gr_ref, ga_ref, wr_ref, wa_ref, wo_ref,
                  gpost_ref, g1_ref, gpre_ref, sc2_ref, sh2_ref, rwt_ref, rb_ref,
                  x1_ref, h2_ref, lg_ref):
    r = jnp.dot(yr_ref[...], wr_ref[...], preferred_element_type=F32)
    a = jnp.dot(ya_ref[...], wa_ref[...], preferred_element_type=F32)
    merged = ((0.5 * jnp.tanh(gr_ref[...].astype(F32)) + 0.5) * r
              + (0.5 * jnp.tanh(ga_ref[...].astype(F32)) + 0.5) * a)
    mix = jnp.dot(merged.astype(BF16), wo_ref[...], preferred_element_type=F32)
    x1 = x_ref[...] + g1_ref[0] * _rms(mix, gpost_ref[...])
    x1_ref[...] = x1
    h2 = (_rms(x1, gpre_ref[...]) * (1.0 + sc2_ref[0]) + sh2_ref[0]).astype(BF16)
    h2_ref[...] = h2
    lg = lax.dot_general(rwt_ref[...], h2, (((1,), (1,)), ((), ())),
                         preferred_element_type=F32)
    lg_ref[...] = lg + rb_ref[...]


def _merge(x2, y_rnn, y_att, proj_rest, wr, wa, wo, gpost, g1, gpre, sc2, sh2, rwt, rb, S):
    N, D = x2.shape
    tm = _tiles()["tok"]
    per_b = S // tm
    gate_r_col = Q_DIM // D
    mat = lambda: pl.BlockSpec((D, D), lambda i: (0, 0))
    vec = lambda: pl.BlockSpec((1, D), lambda i: (0, 0))
    bvec = lambda: pl.BlockSpec((1, 1, D), lambda i: (i // per_b, 0, 0))
    tile = lambda col=0: pl.BlockSpec((tm, D), lambda i: (i, col))
    return pl.pallas_call(
        _merge_kernel,
        grid=(N // tm,),
        in_specs=[
            tile(), tile(), tile(), tile(gate_r_col), tile(gate_r_col + 1),
            mat(), mat(), mat(),
            vec(), bvec(), vec(), bvec(), bvec(),
            pl.BlockSpec((N_EXPERTS, D), lambda i: (0, 0)),
            pl.BlockSpec((N_EXPERTS, 1), lambda i: (0, 0)),
        ],
        out_specs=[
            tile(), tile(),
            pl.BlockSpec((N_EXPERTS, tm), lambda i: (0, i)),
        ],
        out_shape=[
            jax.ShapeDtypeStruct((N, D), F32),
            jax.ShapeDtypeStruct((N, D), BF16),
            jax.ShapeDtypeStruct((N_EXPERTS, N), F32),
        ],
        compiler_params=pltpu.CompilerParams(
            dimension_semantics=("arbitrary",),
            vmem_limit_bytes=_vmem_limit(6 * D * D * 2 + 24 * tm * D * 4)),
        name="merge",
    )(x2, y_rnn, y_att, proj_rest, proj_rest, wr, wa, wo, gpost, g1, gpre, sc2, sh2, rwt, rb)


def _route_kernel(lg_ref, g_ref, slot_ref, cnt_ref, tcnt_ref, tcar_ref, tlst_ref, carry, *, T):
    i = pl.program_id(0)
    E = lg_ref.shape[0]
    n_sub = lg_ref.shape[1] // T

    @pl.when(i == 0)
    def _():
        carry[...] = jnp.zeros_like(carry)
        tcnt_ref[...] = jnp.zeros_like(tcnt_ref)
        tcar_ref[...] = jnp.zeros_like(tcar_ref)
        tlst_ref[...] = jnp.zeros_like(tlst_ref)

    row = lax.broadcasted_iota(jnp.int32, (E, T), 0).astype(F32)
    tri_t = (lax.broadcasted_iota(jnp.int32, (T, T), 0)
             < lax.broadcasted_iota(jnp.int32, (T, T), 1)).astype(BF16)
    tri_e = (lax.broadcasted_iota(jnp.int32, (E, E), 1)
             < lax.broadcasted_iota(jnp.int32, (E, E), 0)).astype(BF16)
    tile_lane = lax.broadcasted_iota(jnp.int32, tcnt_ref.shape, 1)
    running = carry[...]
    for sub in range(n_sub):
        cols = slice(sub * T, (sub + 1) * T)
        l = lg_ref[:, cols]
        vals, idxs = [], []
        for _ in range(TOP_K):
            m = jnp.max(l, axis=0, keepdims=True)
            idx = jnp.min(jnp.where(l == m, row, float(E)), axis=0, keepdims=True)
            vals.append(m)
            idxs.append(idx)
            l = jnp.where(row == idx, -jnp.inf, l)
        ex = [jnp.exp(v - vals[0]) for v in vals]
        tot = ex[0]
        for e in ex[1:]:
            tot = tot + e
        g_ref[:, cols] = jnp.concatenate([e / tot for e in ex], axis=0)

        hot = [row == idx for idx in idxs]
        onehot = jnp.zeros((E, T), F32)
        for hk in hot:
            onehot = onehot + hk.astype(F32)
        before = jnp.dot(onehot.astype(BF16), tri_t, preferred_element_type=F32)
        cnt = jnp.sum(onehot, axis=1, keepdims=True)
        lstart = jnp.dot(tri_e, jnp.broadcast_to(cnt, (E, V7X_LANES)).astype(BF16),
                         preferred_element_type=F32)[:, 0:1]
        local = before + lstart
        slots = [jnp.sum(jnp.where(hk, local, 0.0), axis=0, keepdims=True) for hk in hot]
        slot_ref[:, cols] = jnp.concatenate(slots, axis=0).astype(jnp.int32)

        mine = tile_lane == i * n_sub + sub
        tcnt_ref[...] = jnp.where(mine, cnt.astype(jnp.int32), tcnt_ref[...])
        tcar_ref[...] = jnp.where(mine, running.astype(jnp.int32), tcar_ref[...])
        tlst_ref[...] = jnp.where(mine, lstart.astype(jnp.int32), tlst_ref[...])
        running = running + cnt
    carry[...] = running
    cnt_ref[...] = running.astype(jnp.int32)


def _route(logits_t):
    E, N = logits_t.shape
    T = _tiles()["moe_t"]
    TS = T * _tiles()["route_tiles"]
    assert T <= 2 ** 8 and N // T <= V7X_LANES
    out = lambda: pl.BlockSpec((TOP_K, TS), lambda i: (0, i))
    per_tile = lambda: pl.BlockSpec((E, V7X_LANES), lambda i: (0, 0))
    return pl.pallas_call(
        functools.partial(_route_kernel, T=T),
        grid=(N // TS,),
        in_specs=[pl.BlockSpec((E, TS), lambda i: (0, i))],
        out_specs=[out(), out(), pl.BlockSpec((E, 1), lambda i: (0, 0)),
                   per_tile(), per_tile(), per_tile()],
        out_shape=[
            jax.ShapeDtypeStruct((TOP_K, N), F32),
            jax.ShapeDtypeStruct((TOP_K, N), jnp.int32),
            jax.ShapeDtypeStruct((E, 1), jnp.int32),
            jax.ShapeDtypeStruct((E, V7X_LANES), jnp.int32),
            jax.ShapeDtypeStruct((E, V7X_LANES), jnp.int32),
            jax.ShapeDtypeStruct((E, V7X_LANES), jnp.int32),
        ],
        scratch_shapes=[pltpu.VMEM((E, 1), F32)],
        compiler_params=pltpu.CompilerParams(dimension_semantics=("arbitrary",)),
        name="route",
    )(logits_t)


def _run_sizes(T):
    return [2 ** b for b in range(int(math.log2(T)), -1, -1)]


def _copy_plan(tile_cnt, tile_lst, tile_dst, T):
    sizes = jnp.asarray(_run_sizes(T), jnp.int32)[None, :, None]
    cnt, lst, dst = (a[:, None, :] for a in (tile_cnt, tile_lst, tile_dst))
    has = (cnt & sizes) != 0
    off = (cnt // (2 * sizes)) * (2 * sizes)
    eids = jnp.arange(N_EXPERTS)
    earlier = eids[:, None] < eids[None, :]
    place = jnp.sum(jnp.where(earlier, has[..., :, None], False), axis=-2)
    hit = has[..., :, None] & (place[..., :, None] == eids)
    pick = lambda v: jnp.sum(jnp.where(hit, v[..., :, None], 0), axis=-2)
    n_tiles, nb = tile_cnt.shape[0], sizes.shape[1]
    parts = [pick(jnp.broadcast_to(lst + off, has.shape)).reshape(n_tiles, -1),
             pick(jnp.broadcast_to(dst + off, has.shape)).reshape(n_tiles, -1),
             jnp.sum(has, axis=-1).reshape(n_tiles, nb)]
    plan = jnp.concatenate(parts, axis=1).astype(jnp.int32)
    pad = _plan_len(T) - plan.shape[1]
    return jnp.pad(plan, ((0, 0), (0, pad))).reshape(-1)


def _plan_len(T):
    nb = len(_run_sizes(T))
    return max(V7X_LANES, pl.next_power_of_2(2 * nb * N_EXPERTS + nb))


def _run_copies(plan_ref, T, make):
    sizes = _run_sizes(T)
    nb = len(sizes)
    for bi, b in enumerate(sizes):
        def body(j, c, bi=bi, b=b):
            make(plan_ref[bi * N_EXPERTS + j], plan_ref[(nb + bi) * N_EXPERTS + j], b).start()
            return c

        lax.fori_loop(0, plan_ref[2 * nb * N_EXPERTS + bi], body, 0)


def _row_slab(ref, row, nrows, lead=()):
    rows = pl.ds(pl.multiple_of(row * V7X_SUBLANES, V7X_SUBLANES), nrows * V7X_SUBLANES)
    return ref.at[(*lead, rows)]


def _to_slabs(ref, val, lead=()):
    rows = val.shape[0]
    for s in range(V7X_SUBLANES):
        ref[(*lead, pl.ds(s, rows, stride=V7X_SUBLANES), slice(None))] = (
            val[:, s * V7X_LANES:(s + 1) * V7X_LANES])


def _from_slabs(ref, lead=(), rows=None):
    rows = ref.shape[-2] // V7X_SUBLANES if rows is None else rows
    return jnp.concatenate(
        [ref[(*lead, pl.ds(s, rows, stride=V7X_SUBLANES), slice(None))]
         for s in range(V7X_SUBLANES)], axis=1)


def _dispatch_kernel(tail_ref, nu_ref, plan_ref, slot_ref, h_ref, xs_ref, buf, zbuf, sem, zsem):
    i = pl.program_id(0)
    T = h_ref.shape[0]
    bm = zbuf.shape[0] // V7X_SUBLANES
    nblk = xs_ref.shape[0] // zbuf.shape[0]
    depth = buf.shape[0]
    cur = i % depth

    def zero_copy(row0, s):
        return pltpu.make_async_copy(zbuf, _row_slab(xs_ref, row0, bm), zsem.at[s])

    def read_chunk(e, fn):
        @pl.when(tail_ref[e] >= 0)
        def _():
            fn(zero_copy(tail_ref[e], 0))

    def unread_chunks(fn):
        def per_expert(e, c):
            @pl.when(jnp.logical_and(tail_ref[e] >= 0, tail_ref[e] % (2 * bm) == 0))
            def _():
                fn(zero_copy(tail_ref[e] + bm, 1))
            return c

        def per_unused(j, c):
            fn(zero_copy(j * bm, 1))
            return c

        lax.fori_loop(0, N_EXPERTS, per_expert, 0)
        lax.fori_loop(nu_ref[0], nblk, per_unused, 0)

    @pl.when(i == 0)
    def _():
        zbuf[...] = jnp.zeros_like(zbuf)

        def start(e, c):
            read_chunk(e, lambda cp: cp.start())
            return c

        def wait(e, c):
            read_chunk(e, lambda cp: cp.wait())
            return c

        lax.fori_loop(0, N_EXPERTS, start, 0)
        unread_chunks(lambda cp: cp.start())
        lax.fori_loop(0, N_EXPERTS, wait, 0)

    slot = slot_ref[...]
    r_id = lax.broadcasted_iota(jnp.int32, (TOP_K * T, T), 0)
    perm = r_id == slot[0:1, :]
    for k in range(1, TOP_K):
        perm = jnp.logical_or(perm, r_id == slot[k:k + 1, :])
    grouped = jnp.dot(perm.astype(BF16), h_ref[...], preferred_element_type=F32)
    _to_slabs(buf, grouped, lead=(cur,))

    def wait_tile(b):
        pltpu.make_async_copy(buf.at[b], _row_slab(xs_ref, 0, TOP_K * T), sem.at[b]).wait()

    @pl.when(i >= depth - 1)
    def _():
        wait_tile((i + 1) % depth)

    _run_copies(plan_ref, T, lambda l, d, n: pltpu.make_async_copy(
        _row_slab(buf, l, n, lead=(cur,)), _row_slab(xs_ref, d, n), sem.at[cur]))

    @pl.when(i == pl.num_programs(0) - 1)
    def _():
        for back in range(depth - 1):
            @pl.when(i >= back)
            def _(back=back):
                wait_tile((i - back) % depth)
        unread_chunks(lambda cp: cp.wait())


def _dispatch(h2, slot, plan, tail_start, n_used, n_rows, bm):
    N, D = h2.shape
    T = _tiles()["moe_t"]
    ring = 4
    S8 = V7X_SUBLANES
    imap = lambda f: (lambda i, *_: f(i))
    return pl.pallas_call(
        _dispatch_kernel,
        grid_spec=pltpu.PrefetchScalarGridSpec(
            num_scalar_prefetch=2,
            grid=(N // T,),
            in_specs=[
                pl.BlockSpec((_plan_len(T),), imap(lambda i: (i,)), memory_space=pltpu.SMEM),
                pl.BlockSpec((TOP_K, T), imap(lambda i: (0, i))),
                pl.BlockSpec((T, D), imap(lambda i: (i, 0))),
            ],
            out_specs=pl.BlockSpec(memory_space=pl.ANY),
            scratch_shapes=[
                pltpu.VMEM((ring, TOP_K * T * S8, V7X_LANES), F32),
                pltpu.VMEM((bm * S8, V7X_LANES), F32),
                pltpu.SemaphoreType.DMA((ring,)),
                pltpu.SemaphoreType.DMA((2,)),
            ],
        ),
        out_shape=jax.ShapeDtypeStruct((n_rows * S8, V7X_LANES), F32),
        compiler_params=pltpu.CompilerParams(
            dimension_semantics=("arbitrary",),
            vmem_limit_bytes=_vmem_limit(8 * TOP_K * T * D * 4)),
        name="dispatch",
    )(tail_start, n_used, plan, slot, h2)


def _ffn_kernel(be_ref, bv_ref, bf_ref, nu_ref, nxt_ref, x_ref, w1_hbm, b1g_ref, b1l_ref, w2_hbm,
                b2_ref, y_ref, w1f_s, w2f_s, w1_s, w2_s, sem):
    i = pl.program_id(0)
    bm, D = x_ref.shape[0] // V7X_SUBLANES, w2_s.shape[1]
    n_ff = w2_s.shape[0]
    pw = 2 * V7X_LANES

    def fetch(e):
        return (pltpu.make_async_copy(w1_hbm.at[e], w1f_s, sem.at[0]),
                pltpu.make_async_copy(w2_hbm.at[e], w2f_s, sem.at[1]))

    @pl.when(i == 0)
    def _():
        for cp in fetch(be_ref[0]):
            cp.start()

    @pl.when(jnp.logical_and(i < nu_ref[0], bf_ref[i] == 1))
    def _():
        e = be_ref[i]
        for cp in fetch(e):
            cp.wait()
        src = lax.broadcasted_iota(jnp.int32, (pw, pw), 0)
        dst = lax.broadcasted_iota(jnp.int32, (pw, pw), 1)
        want = jnp.where(dst < V7X_LANES, 2 * dst, 2 * (dst - V7X_LANES) + 1)
        perm = (src == want).astype(BF16)
        for c in range(w1f_s.shape[1] // pw):
            wp = jnp.dot(w1f_s[:, c * pw:(c + 1) * pw].astype(BF16), perm,
                         preferred_element_type=F32).astype(BF16)
            w1_s[:, c * V7X_LANES:(c + 1) * V7X_LANES] = wp[:, :V7X_LANES]
            w1_s[:, n_ff + c * V7X_LANES:n_ff + (c + 1) * V7X_LANES] = wp[:, V7X_LANES:]
        w2_s[...] = w2f_s[...].astype(BF16)

        @pl.when(nxt_ref[e] >= 0)
        def _():
            for cp in fetch(nxt_ref[e]):
                cp.start()

    def mlp(n_rows):
        x = _from_slabs(x_ref, rows=n_rows).astype(BF16)
        hid = jnp.dot(x, w1_s[...], preferred_element_type=F32)
        glu = hid[:, :n_ff] + b1g_ref[0]
        lin = hid[:, n_ff:] + b1l_ref[0]
        glu = jnp.minimum(glu, SWIGLU_LIMIT)
        lin = jnp.clip(lin, -SWIGLU_LIMIT, SWIGLU_LIMIT)
        act = glu * _sigmoid(SWIGLU_ALPHA * glu) * (lin + 1.0)
        _to_slabs(y_ref, jnp.dot(act.astype(BF16), w2_s[...], preferred_element_type=F32) + b2_ref[0])
        if n_rows < bm:
            y_ref[n_rows * V7X_SUBLANES:, :] = jnp.zeros(
                ((bm - n_rows) * V7X_SUBLANES, V7X_LANES), F32)

    used = i < nu_ref[0]
    half = bm // 2

    @pl.when(jnp.logical_and(used, bv_ref[i] > half))
    def _():
        mlp(bm)

    @pl.when(jnp.logical_and(used, bv_ref[i] <= half))
    def _():
        mlp(half)

    @pl.when(i >= nu_ref[0])
    def _():
        y_ref[...] = jnp.zeros_like(y_ref)


def _ffn(xs, blk_e, blk_valid, blk_first, n_used, next_e, w1, b1g, b1l, w2, b2, bm):
    E, D, F2 = w1.shape
    F = F2 // 2
    slab = bm * V7X_SUBLANES
    nblk = xs.shape[0] // slab

    def row_blk(i, be, bv, bf, nu, nxt):
        return (jnp.minimum(i, nu[0] - 1), 0)

    def per_e(i, be, bv, bf, nu, nxt):
        return (be[i], 0, 0)

    vmem = (D * F2 + F * D) * (4 + 2) + 8 * bm * F2 * 4
    return pl.pallas_call(
        _ffn_kernel,
        grid_spec=pltpu.PrefetchScalarGridSpec(
            num_scalar_prefetch=5,
            grid=(nblk,),
            in_specs=[
                pl.BlockSpec((slab, V7X_LANES), row_blk),
                pl.BlockSpec(memory_space=pl.ANY),
                pl.BlockSpec((1, 1, F), per_e),
                pl.BlockSpec((1, 1, F), per_e),
                pl.BlockSpec(memory_space=pl.ANY),
                pl.BlockSpec((1, 1, D), per_e),
            ],
            out_specs=pl.BlockSpec((slab, V7X_LANES), lambda i, be, bv, bf, nu, nxt: (i, 0)),
            scratch_shapes=[
                pltpu.VMEM((D, F2), F32),
                pltpu.VMEM((F, D), F32),
                pltpu.VMEM((D, F2), BF16),
                pltpu.VMEM((F, D), BF16),
                pltpu.SemaphoreType.DMA((2,)),
            ],
        ),
        out_shape=jax.ShapeDtypeStruct(xs.shape, F32),
        compiler_params=pltpu.CompilerParams(
            dimension_semantics=("arbitrary",),
            vmem_limit_bytes=_vmem_limit(vmem)),
        name="ffn",
    )(blk_e, blk_valid, blk_first, n_used, next_e, xs, w1, b1g, b1l, w2, b2)


def _combine_kernel(plan_ref, plan1_ref, plan2_ref, ys_ref, x1_ref, slot_ref, gate_ref, gpost_ref,
                    g2_ref, o_ref, buf, sem):
    i = pl.program_id(0)
    n = pl.num_programs(0)
    T = x1_ref.shape[0]
    depth = buf.shape[0]
    assert depth == 3
    cur = i % depth

    def gather(plan, b):
        _run_copies(plan, T, lambda l, d, n: pltpu.make_async_copy(
            _row_slab(ys_ref, d, n), _row_slab(buf, l, n, lead=(b,)), sem.at[b]))

    @pl.when(i == 0)
    def _():
        gather(plan_ref, 0)

        @pl.when(n > 1)
        def _():
            gather(plan1_ref, 1)

    @pl.when(i + 2 < n)
    def _():
        gather(plan2_ref, (i + 2) % depth)

    pltpu.make_async_copy(_row_slab(ys_ref, 0, TOP_K * T), buf.at[cur], sem.at[cur]).wait()

    slot = slot_ref[...]
    gates = gate_ref[...]
    r_id = lax.broadcasted_iota(jnp.int32, (TOP_K * T, T), 0)
    wt = jnp.zeros((TOP_K * T, T), F32)
    for k in range(TOP_K):
        wt = jnp.where(r_id == slot[k:k + 1, :], gates[k:k + 1, :], wt)
    ff = lax.dot_general(wt.astype(BF16), _from_slabs(buf, lead=(cur,)).astype(BF16),
                         (((0,), (0,)), ((), ())), preferred_element_type=F32)
    o_ref[...] = x1_ref[...] + g2_ref[0] * _rms(ff, gpost_ref[...])


def _combine(ys, plan, x1, slot_tk, gates_tk, gpost, g2, S):
    N, D = x1.shape
    T = _tiles()["moe_t"]
    per_b = S // T
    n_tiles = N // T
    plan_spec = lambda f: pl.BlockSpec((_plan_len(T),), f, memory_space=pltpu.SMEM)
    return pl.pallas_call(
        _combine_kernel,
        grid=(n_tiles,),
        in_specs=[
            plan_spec(lambda i: (i,)),
            plan_spec(lambda i: (jnp.minimum(i + 1, n_tiles - 1),)),
            plan_spec(lambda i: (jnp.minimum(i + 2, n_tiles - 1),)),
            pl.BlockSpec(memory_space=pl.ANY),
            pl.BlockSpec((T, D), lambda i: (i, 0)),
            pl.BlockSpec((TOP_K, T), lambda i: (0, i)),
            pl.BlockSpec((TOP_K, T), lambda i: (0, i)),
            pl.BlockSpec((1, D), lambda i: (0, 0)),
            pl.BlockSpec((1, 1, D), lambda i: (i // per_b, 0, 0)),
        ],
        out_specs=pl.BlockSpec((T, D), lambda i: (i, 0)),
        out_shape=jax.ShapeDtypeStruct((N, D), F32),
        scratch_shapes=[
            pltpu.VMEM((3, TOP_K * T * V7X_SUBLANES, V7X_LANES), F32),
            pltpu.SemaphoreType.DMA((3,)),
        ],
        compiler_params=pltpu.CompilerParams(
            dimension_semantics=("arbitrary",),
            vmem_limit_bytes=_vmem_limit(11 * TOP_K * T * D * 4)),
        name="combine",
    )(plan, plan, plan, ys, x1, slot_tk, gates_tk, gpost, g2)


def _col_scaled(w):
    scale = HEAD_DIM ** -0.5
    assert math.log2(scale).is_integer()
    col = np.arange(D_IN)
    q0, g0 = 2 * D_RNN, 2 * D_RNN + Q_DIM + 2 * KV_DIM
    factor = np.where((col >= q0) & (col < q0 + Q_DIM), scale, np.where(col >= g0, 0.5, 1.0))
    return w * jnp.asarray(factor, w.dtype)


def _layer(x2, c, B, S, p):
    D = D_MODEL
    N = B * S
    ada = _ada(c, p["w_ada"], p["b_ada"])
    sh1, sc1, g1, sh2, sc2, g2 = [a.reshape(B, 1, D) for a in jnp.split(ada, 6, axis=-1)]
    row = lambda v: v.reshape(1, -1)

    gw = _tiles()["rnn_group"]
    proj_rest, y_rnn = _mixin(
        x2, row(p["norm_pre_mix"]), sc1, sh1,
        _col_scaled(p["w_in"]).astype(BF16), _col_scaled(row(p["b_in"])),
        p["conv_w"], row(p["conv_b"]),
        (0.5 * _block_diag_tiles(p["rg_w_a"], gw)).astype(BF16), 0.5 * row(p["rg_b_a"]),
        (0.5 * _block_diag_tiles(p["rg_w_x"], gw)).astype(BF16), 0.5 * row(p["rg_b_x"]),
        row(p["rg_lambda"]), S)
    y_att = _attn(proj_rest, p["attn_sinks"], S)

    x1, h2, logits_t = _merge(
        x2, y_rnn, y_att, proj_rest,
        p["w_o_rnn"].astype(BF16), p["w_o_attn"].astype(BF16), p["w_out"].astype(BF16),
        row(p["norm_post_mix"]), g1, row(p["norm_pre_ffn"]), sc2, sh2,
        p["router_w"].T.astype(BF16), p["router_b"].reshape(N_EXPERTS, 1), S)

    gates, slot, counts, tile_cnt, tile_car, tile_lst = _route(logits_t)

    bm = _tiles()["moe_bm"]
    n_tiles = N // _tiles()["moe_t"]
    n_rows = N * TOP_K + N_EXPERTS * bm
    nblk = n_rows // bm
    counts = counts.reshape(N_EXPERTS)
    padded = ((counts + bm - 1) // bm) * bm
    pend = jnp.cumsum(padded)
    pstart = pend - padded
    eids = jnp.arange(N_EXPERTS, dtype=jnp.int32)
    per_run = lambda a: a[:, :n_tiles].T.astype(jnp.int32)
    plan = _copy_plan(per_run(tile_cnt), per_run(tile_lst), per_run(tile_car + pstart[:, None]),
                      _tiles()["moe_t"])
    zr = bm // 2
    tail_start = jnp.where(counts > 0, (pstart + counts - 1) // zr * zr, -1).astype(jnp.int32)
    n_used = (pend[-1] // bm).astype(jnp.int32).reshape(1)
    blk_row0 = jnp.minimum(jnp.arange(nblk, dtype=jnp.int32), n_used[0] - 1) * bm
    blk_e = jnp.sum(blk_row0[:, None] >= pend[None, :], axis=1).astype(jnp.int32)
    mine = blk_e[:, None] == eids[None, :]
    blk_cnt = jnp.sum(jnp.where(mine, counts, 0), axis=1)
    blk_pstart = jnp.sum(jnp.where(mine, pstart, 0), axis=1)
    blk_valid = jnp.clip(blk_cnt - (blk_row0 - blk_pstart), 0, bm).astype(jnp.int32)
    blk_first = (blk_row0 == blk_pstart).astype(jnp.int32)

    xs = _dispatch(h2, slot, plan, tail_start, n_used * (bm // zr), n_rows, zr)
    b1 = p["moe_b1"].reshape(N_EXPERTS, D_FF, 2)
    later = jnp.logical_and(counts[None, :] > 0, eids[None, :] > eids[:, None])
    next_e = jnp.min(jnp.where(later, eids[None, :], N_EXPERTS), axis=1)
    next_e = jnp.where(next_e == N_EXPERTS, -1, next_e).astype(jnp.int32)
    ys = _ffn(xs, blk_e, blk_valid, blk_first, n_used, next_e, p["moe_w1"],
              b1[:, :, 0].reshape(N_EXPERTS, 1, D_FF), b1[:, :, 1].reshape(N_EXPERTS, 1, D_FF),
              p["moe_w2"], p["moe_b2"].reshape(N_EXPERTS, 1, D), bm)
    return _combine(ys, plan, x1, slot, gates, row(p["norm_post_ffn"]), g2, S)


def kernel(x, c, w_ada, b_ada, norm_pre_mix, norm_post_mix, norm_pre_ffn, norm_post_ffn, w_in, b_in, conv_w, conv_b, rg_w_a, rg_b_a, rg_w_x, rg_b_x, rg_lambda, attn_sinks, w_o_rnn, w_o_attn, w_out, router_w, router_b, moe_w1, moe_b1, moe_w2, moe_b2):
    B, S, D = x.shape
    params = dict(
        w_ada=w_ada, b_ada=b_ada, norm_pre_mix=norm_pre_mix, norm_post_mix=norm_post_mix,
        norm_pre_ffn=norm_pre_ffn, norm_post_ffn=norm_post_ffn, w_in=w_in, b_in=b_in,
        conv_w=conv_w, conv_b=conv_b, rg_w_a=rg_w_a, rg_b_a=rg_b_a, rg_w_x=rg_w_x, rg_b_x=rg_b_x,
        rg_lambda=rg_lambda, attn_sinks=attn_sinks, w_o_rnn=w_o_rnn, w_o_attn=w_o_attn,
        w_out=w_out, router_w=router_w, router_b=router_b, moe_w1=moe_w1, moe_b1=moe_b1,
        moe_w2=moe_w2, moe_b2=moe_b2)
    x2 = x.reshape(B * S, D)
    for layer in range(w_ada.shape[0]):
        x2 = _layer(x2, c, B, S, {k: v[layer] for k, v in params.items()})
    return x2.reshape(B, S, D)
```

```python
import functools
import math

import jax
import jax.numpy as jnp
import numpy as np
from jax import lax
from jax.experimental import pallas as pl
from jax.experimental.pallas import tpu as pltpu

D_MODEL = 1024
D_RNN = 1024
RNN_BLOCKS = 16
RNN_BW = D_RNN // RNN_BLOCKS
CONV_W = 4
LRU_C = 8.0
N_HEADS = 16
N_KV = 4
HEAD_DIM = 64
GROUP = N_HEADS // N_KV
WINDOW = 128
Q_DIM = N_HEADS * HEAD_DIM
KV_DIM = N_KV * HEAD_DIM
N_EXPERTS = 32
TOP_K = 4
D_FF = 1024
SWIGLU_LIMIT = 7.0
SWIGLU_ALPHA = 1.702
EPS = 1e-6
D_IN = 2 * D_RNN + Q_DIM + 2 * KV_DIM + 2 * D_MODEL
D_REST = D_IN - 2 * D_RNN

V7X_LANES = 128
V7X_SUBLANES = 8
V7X_MXU_DIM = 256
V7X_VMEM_BYTES = 64 * 1024 * 1024

F32 = jnp.float32
BF16 = jnp.bfloat16


def _tiles():
    return dict(
        ada_tn=4 * V7X_MXU_DIM,
        tok=2 * V7X_MXU_DIM,
        attn_q=WINDOW,
        attn_blocks=2,
        moe_t=V7X_MXU_DIM,
        route_tiles=4,
        moe_bm=2 * V7X_MXU_DIM,
        rnn_group=V7X_MXU_DIM,
    )


V7X_VMEM_MIN_LIMIT = 16 * 1024 * 1024
V7X_VMEM_HEADROOM = 8 * 1024 * 1024


def _vmem_limit(nbytes):
    return int(min(max(nbytes, V7X_VMEM_MIN_LIMIT), V7X_VMEM_BYTES - V7X_VMEM_HEADROOM))


def _rms(x, g):
    return x * lax.rsqrt(jnp.mean(x * x, axis=-1, keepdims=True) + EPS) * g


def _ada_kernel(ct_ref, w_ref, b_ref, o_ref):
    ct = ct_ref[...]
    sc = ct * jax.nn.sigmoid(ct)
    w = w_ref[...]
    rows = [jnp.sum(w * sc[:, b:b + 1], axis=0, keepdims=True) for b in range(ct.shape[1])]
    o_ref[...] = jnp.concatenate(rows, axis=0) + b_ref[...]


def _ada(c, w_ada, b_ada):
    B, D = c.shape
    n_out = w_ada.shape[1]
    tn = _tiles()["ada_tn"]
    return pl.pallas_call(
        _ada_kernel,
        grid=(n_out // tn,),
        in_specs=[
            pl.BlockSpec((D, B), lambda j: (0, 0)),
            pl.BlockSpec((D, tn), lambda j: (0, j)),
            pl.BlockSpec((1, tn), lambda j: (0, j)),
        ],
        out_specs=pl.BlockSpec((B, tn), lambda j: (0, j)),
        out_shape=jax.ShapeDtypeStruct((B, n_out), F32),
        compiler_params=pltpu.CompilerParams(
            dimension_semantics=("arbitrary",),
            vmem_limit_bytes=_vmem_limit(4 * D * tn * 4)),
        name="ada",
    )(c.T, w_ada, b_ada.reshape(1, n_out))


def _gelu_tanh(x):
    return 0.5 * x * (1.0 + jnp.tanh(math.sqrt(2.0 / math.pi) * (x + 0.044715 * (x * x * x))))


def _softplus(z):
    return jnp.maximum(z, 0.0) + jnp.log1p(jnp.exp(-jnp.abs(z)))


def _sigmoid(x):
    return 0.5 * jnp.tanh(0.5 * x) + 0.5


def _rest_col(c):
    c -= 2 * D_RNN
    q_end, k_end, v_end = Q_DIM, Q_DIM + KV_DIM, Q_DIM + 2 * KV_DIM
    if c < q_end:
        return c
    if c < k_end:
        return Q_DIM + 2 * D_MODEL + (c - q_end)
    if c < v_end:
        return Q_DIM + 2 * D_MODEL + KV_DIM + (c - k_end)
    return Q_DIM + (c - v_end)


def _mixin_kernel(x_ref, g_ref, sc_ref, sh_ref, w_ref, b_ref, cw_ref, cb_ref, wa_ref, ba_ref,
                  wx_ref, bx_ref, lam_ref, rest_ref, y_ref,
                  hb_s, xbuf, gr_s, a_s, b_s, h_s, ga_s, gb_s, gc_s, carry, *, per_b, gw, chunk):
    t = pl.program_id(0) % per_b
    tt = x_ref.shape[0]
    halo = V7X_SUBLANES
    ng = tt // V7X_SUBLANES
    n_lt = a_s.shape[0]
    per_g = gw // V7X_LANES
    lanes = lambda j: slice(j * V7X_LANES, (j + 1) * V7X_LANES)
    slab = lambda j, r: (j, pl.ds(r, ng, stride=V7X_SUBLANES), slice(None))

    @pl.when(t == 0)
    def _():
        xbuf[:, 0:halo, :] = jnp.zeros((n_lt, halo, V7X_LANES), F32)
        carry[...] = jnp.zeros_like(carry)

    hb_s[...] = (_rms(x_ref[...], g_ref[...]) * (1.0 + sc_ref[0]) + sh_ref[0]).astype(BF16)

    def proj(c0):
        return (jnp.dot(hb_s[...], w_ref[:, c0:c0 + chunk], preferred_element_type=F32)
                + b_ref[:, c0:c0 + chunk])

    def proj_rest(c0):
        acc = proj(c0).astype(BF16)
        for p0 in range(0, chunk, KV_DIM):
            d0 = _rest_col(c0 + p0)
            rest_ref[:, d0:d0 + KV_DIM] = acc[:, p0:p0 + KV_DIM]

    for c0 in range(0, D_RNN, chunk):
        acc = proj(c0)
        for j in range(chunk // V7X_LANES):
            xbuf[c0 // V7X_LANES + j, halo:halo + tt, :] = acc[:, lanes(j)]
    for c0 in range(D_RNN, 2 * D_RNN, chunk):
        gr_s[:, c0 - D_RNN:c0 - D_RNN + chunk] = proj(c0)

    def conv(j):
        ls = lanes(j)
        taps = {o: xbuf[slab(j, halo + o)] for o in range(-(CONV_W - 1), V7X_SUBLANES)}
        for r in range(V7X_SUBLANES):
            acc = cb_ref[:, ls] + taps[r - (CONV_W - 1)] * cw_ref[0:1, ls]
            for kk in range(1, CONV_W):
                acc = acc + taps[r - (CONV_W - 1) + kk] * cw_ref[kk:kk + 1, ls]
            h_s[slab(j, r)] = acc
        xbuf[j, 0:halo, :] = xbuf[j, tt:tt + halo, :]

    def gates(g):
        cs = slice(g * gw, (g + 1) * gw)
        rate = (-LRU_C * math.log2(math.e)) * _softplus(-lam_ref[:, cs])
        reset = jnp.logical_and(t == 0, lax.broadcasted_iota(jnp.int32, (tt, gw), 0) == 0)
        xc = jnp.concatenate([h_s[g * per_g + j] for j in range(per_g)], axis=1)
        xg = xc.astype(BF16)
        gate_r = 0.5 * jnp.tanh(jnp.dot(xg, wa_ref[g], preferred_element_type=F32) + ba_ref[:, cs]) + 0.5
        gate_i = 0.5 * jnp.tanh(jnp.dot(xg, wx_ref[g], preferred_element_type=F32) + bx_ref[:, cs]) + 0.5
        a = jnp.exp2(gate_r * rate)
        v = (1.0 - a) * (1.0 + a)
        mult = jnp.where(reset, 1.0, jnp.where(v > 0.0, v * lax.rsqrt(v), 0.0))
        bt = (xc * gate_i) * mult
        for j in range(per_g):
            a_s[g * per_g + j] = a[:, lanes(j)]
            b_s[g * per_g + j] = bt[:, lanes(j)]

    def scan_groups(j):
        acc_a = a_s[slab(j, 0)]
        acc_h = b_s[slab(j, 0)]
        h_s[slab(j, 0)] = acc_h
        for r in range(1, V7X_SUBLANES):
            a_r = a_s[slab(j, r)]
            acc_h = a_r * acc_h + b_s[slab(j, r)]
            acc_a = a_r * acc_a
            h_s[slab(j, r)] = acc_h
            a_s[slab(j, r)] = acc_a
        ga_s[:, lanes(j)] = acc_a
        gb_s[:, lanes(j)] = acc_h

    def across(gi, h_prev):
        gc_s[pl.ds(gi, 1), :] = h_prev
        return ga_s[pl.ds(gi, 1), :] * h_prev + gb_s[pl.ds(gi, 1), :]

    def finish(j):
        h_in = gc_s[:, lanes(j)]
        for r in range(V7X_SUBLANES):
            h_s[slab(j, r)] = a_s[slab(j, r)] * h_in + h_s[slab(j, r)]
        y_ref[:, lanes(j)] = (h_s[j] * _gelu_tanh(gr_s[:, lanes(j)])).astype(BF16)

    rest = [functools.partial(proj_rest, c0) for c0 in range(2 * D_RNN, w_ref.shape[1], chunk)]
    before = ([functools.partial(conv, j) for j in range(n_lt)]
              + [functools.partial(gates, g) for g in range(n_lt // per_g)]
              + [functools.partial(scan_groups, j) for j in range(n_lt)])
    after = [functools.partial(finish, j) for j in range(n_lt)]
    n_before = (len(rest) * len(before)) // (len(before) + len(after))

    def interleave(steps, chunks):
        every = -(-len(steps) // max(len(chunks), 1))
        for n, step in enumerate(steps):
            step()
            if (n + 1) % every == 0 and chunks:
                chunks.pop(0)()
        while chunks:
            chunks.pop(0)()

    interleave(before, rest[:n_before])
    carry[...] = lax.fori_loop(0, ng, across, carry[...])
    interleave(after, rest[n_before:])


def _block_diag_tiles(w, gw):
    nb, bw, _ = w.shape
    per = gw // bw
    w4 = w.reshape(nb // per, per, bw, bw)
    eye = jnp.eye(per, dtype=w.dtype)
    return jnp.einsum("gpij,pq->gpiqj", w4, eye).reshape(nb // per, gw, gw)


def _mixin(x2, g, sc, sh, w_bf, b_in, conv_w, conv_b, wa, ba, wx, bx, lam, S):
    N, D = x2.shape
    C = D_RNN
    tt = _tiles()["tok"]
    gw = _tiles()["rnn_group"]
    per_b = S // tt
    chunk = 2 * V7X_MXU_DIM
    vec = lambda: pl.BlockSpec((1, C), lambda i: (0, 0))
    bvec = lambda: pl.BlockSpec((1, 1, D), lambda i: (i // per_b, 0, 0))
    gate_w = lambda: pl.BlockSpec((C // gw, gw, gw), lambda i: (0, 0, 0))
    slabs = lambda rows: pltpu.VMEM((C // V7X_LANES, rows, V7X_LANES), F32)
    groups = lambda: pltpu.VMEM((tt // V7X_SUBLANES, C), F32)
    vmem = D * D_IN * 2 + 2 * tt * (D * 4 + D_REST * 2 + C * 2) + tt * C * (2 + 5 * 4) + 8 * tt * chunk * 4
    return pl.pallas_call(
        functools.partial(_mixin_kernel, per_b=per_b, gw=gw, chunk=chunk),
        grid=(N // tt,),
        in_specs=[
            pl.BlockSpec((tt, D), lambda i: (i, 0)),
            pl.BlockSpec((1, D), lambda i: (0, 0)),
            bvec(), bvec(),
            pl.BlockSpec((D, D_IN), lambda i: (0, 0), pipeline_mode=pl.Buffered(1)),
            pl.BlockSpec((1, D_IN), lambda i: (0, 0)),
            pl.BlockSpec((CONV_W, C), lambda i: (0, 0)),
            vec(), gate_w(), vec(), gate_w(), vec(), vec(),
        ],
        out_specs=[
            pl.BlockSpec((tt, D_REST), lambda i: (i, 0)),
            pl.BlockSpec((tt, C), lambda i: (i, 0)),
        ],
        out_shape=[
            jax.ShapeDtypeStruct((N, D_REST), BF16),
            jax.ShapeDtypeStruct((N, C), BF16),
        ],
        scratch_shapes=[
            pltpu.VMEM((tt, D), BF16),
            slabs(tt + V7X_SUBLANES),
            pltpu.VMEM((tt, C), F32),
            slabs(tt), slabs(tt), slabs(tt),
            groups(), groups(), groups(),
            pltpu.VMEM((1, C), F32),
        ],
        compiler_params=pltpu.CompilerParams(
            dimension_semantics=("arbitrary",),
            vmem_limit_bytes=_vmem_limit(vmem)),
        name="mixin",
    )(x2, g, sc, sh, w_bf, b_in, conv_w, conv_b, wa, ba, wx, bx, lam)


def _alibi_slopes():
    return [2.0 ** (-8.0 * (h + 1) / N_HEADS) for h in range(N_HEADS)]


def _attn_kernel(sink_ref, q_ref, kp_ref, kc_ref, vp_ref, vc_ref, o_ref, bias_s, *, per_b):
    first = pl.program_id(0) % per_b == 0
    bq = kp_ref.shape[0]
    slopes = _alibi_slopes()

    @pl.when(pl.program_id(0) == 0)
    def _():
        qi = lax.broadcasted_iota(jnp.int32, (bq, 2 * bq), 0)
        ci = lax.broadcasted_iota(jnp.int32, (bq, 2 * bq), 1)
        dist = qi + bq - ci
        valid = (dist >= 0) & (dist < WINDOW)
        distf = dist.astype(F32)
        for h in range(N_HEADS):
            b = jnp.where(valid, -slopes[h] * distf, -jnp.inf)
            bias_s[0, h] = jnp.where(ci >= bq, b, -jnp.inf)
            bias_s[1, h] = b

    assert 2 * HEAD_DIM == V7X_LANES and GROUP % 2 == 0
    low = lax.broadcasted_iota(jnp.int32, (bq, V7X_LANES), 1) < HEAD_DIM
    zero = jnp.zeros((bq, V7X_LANES), BF16)
    for sub in range(q_ref.shape[0] // bq):
        rows = slice(sub * bq, (sub + 1) * bq)
        before = slice((sub - 1) * bq, sub * bq)
        table = jnp.where(first, 0, 1) if sub == 0 else 1
        for kvt in range(N_KV // 2):
            lt = slice(kvt * V7X_LANES, (kvt + 1) * V7X_LANES)
            k_prev, v_prev = (kp_ref[:, lt], vp_ref[:, lt]) if sub == 0 else (kc_ref[before, lt], vc_ref[before, lt])
            k_t = jnp.concatenate([k_prev, kc_ref[rows, lt]], axis=0)
            v_t = jnp.concatenate([v_prev, vc_ref[rows, lt]], axis=0)
            k_swapped = pltpu.roll(k_t, HEAD_DIM, 1)
            for kv_half in range(2):
                kv = 2 * kvt + kv_half
                for pair in range(GROUP // 2):
                    h0 = kv * GROUP + 2 * pair
                    tile = slice((h0 // 2) * V7X_LANES, (h0 // 2 + 1) * V7X_LANES)
                    q_t = q_ref[rows, tile]
                    halves = []
                    for q_half in range(2):
                        h = h0 + q_half
                        qm = jnp.where(low if q_half == 0 else ~low, q_t, zero)
                        kk = k_t if q_half == kv_half else k_swapped
                        s = lax.dot_general(qm, kk, (((1,), (1,)), ((), ())),
                                            preferred_element_type=F32)
                        s = s + bias_s[table, h]
                        sink = sink_ref[h]
                        m = jnp.maximum(jnp.max(s, axis=-1, keepdims=True), sink)
                        p = jnp.exp(s - m)
                        denom = jnp.sum(p, axis=-1, keepdims=True) + jnp.exp(sink - m)
                        halves.append(jnp.dot(p.astype(BF16), v_t, preferred_element_type=F32) / denom)
                    if kv_half == 0:
                        o = jnp.where(low, halves[0], pltpu.roll(halves[1], HEAD_DIM, 1))
                    else:
                        o = jnp.where(low, pltpu.roll(halves[0], HEAD_DIM, 1), halves[1])
                    o_ref[rows, tile] = o.astype(BF16)


def _attn(proj_rest, sinks, S):
    N = proj_rest.shape[0]
    bq = _tiles()["attn_q"]
    nsub = _tiles()["attn_blocks"]
    tq = nsub * bq
    per_b = S // tq
    k_col = (Q_DIM + 2 * D_MODEL) // KV_DIM
    v_col = k_col + 1

    def prev(i):
        return jnp.where(i % per_b == 0, i * nsub, i * nsub - 1)

    return pl.pallas_call(
        functools.partial(_attn_kernel, per_b=per_b),
        grid=(N // tq,),
        in_specs=[
            pl.BlockSpec(memory_space=pltpu.SMEM),
            pl.BlockSpec((tq, Q_DIM), lambda i: (i, 0)),
            pl.BlockSpec((bq, KV_DIM), lambda i: (prev(i), k_col)),
            pl.BlockSpec((tq, KV_DIM), lambda i: (i, k_col)),
            pl.BlockSpec((bq, KV_DIM), lambda i: (prev(i), v_col)),
            pl.BlockSpec((tq, KV_DIM), lambda i: (i, v_col)),
        ],
        out_specs=pl.BlockSpec((tq, Q_DIM), lambda i: (i, 0)),
        out_shape=jax.ShapeDtypeStruct((N, Q_DIM), BF16),
        scratch_shapes=[pltpu.VMEM((2, N_HEADS, bq, 2 * bq), F32)],
        compiler_params=pltpu.CompilerParams(
            dimension_semantics=("arbitrary",),
            vmem_limit_bytes=_vmem_limit(3 * 2 * N_HEADS * bq * 2 * bq * 4)),
        name="attn",
    )(sinks, proj_rest, proj_rest, proj_rest, proj_rest, proj_rest)


def _merge_kernel(x_ref, yr_ref, ya_ref, gr_ref, ga_ref, wr_ref, wa_ref, wo_ref,
                  gpost_ref, g1_ref, gpre_ref, sc2_ref, sh2_ref, rwt_ref, rb_ref,
                  x1_ref, h2_ref, lg_ref):
    r = jnp.dot(yr_ref[...], wr_ref[...], preferred_element_type=F32)
    a = jnp.dot(ya_ref[...], wa_ref[...], preferred_element_type=F32)
    merged = ((0.5 * jnp.tanh(gr_ref[...].astype(F32)) + 0.5) * r
              + (0.5 * jnp.tanh(ga_ref[...].astype(F32)) + 0.5) * a)
    mix = jnp.dot(merged.astype(BF16), wo_ref[...], preferred_element_type=F32)
    x1 = x_ref[...] + g1_ref[0] * _rms(mix, gpost_ref[...])
    x1_ref[...] = x1
    h2 = (_rms(x1, gpre_ref[...]) * (1.0 + sc2_ref[0]) + sh2_ref[0]).astype(BF16)
    h2_ref[...] = h2
    lg = lax.dot_general(rwt_ref[...], h2, (((1,), (1,)), ((), ())),
                         preferred_element_type=F32)
    lg_ref[...] = lg + rb_ref[...]


def _merge(x2, y_rnn, y_att, proj_rest, wr, wa, wo, gpost, g1, gpre, sc2, sh2, rwt, rb, S):
    N, D = x2.shape
    tm = _tiles()["tok"]
    per_b = S // tm
    gate_r_col = Q_DIM // D
    mat = lambda: pl.BlockSpec((D, D), lambda i: (0, 0))
    vec = lambda: pl.BlockSpec((1, D), lambda i: (0, 0))
    bvec = lambda: pl.BlockSpec((1, 1, D), lambda i: (i // per_b, 0, 0))
    tile = lambda col=0: pl.BlockSpec((tm, D), lambda i: (i, col))
    return pl.pallas_call(
        _merge_kernel,
        grid=(N // tm,),
        in_specs=[
            tile(), tile(), tile(), tile(gate_r_col), tile(gate_r_col + 1),
            mat(), mat(), mat(),
            vec(), bvec(), vec(), bvec(), bvec(),
            pl.BlockSpec((N_EXPERTS, D), lambda i: (0, 0)),
            pl.BlockSpec((N_EXPERTS, 1), lambda i: (0, 0)),
        ],
        out_specs=[
            tile(), tile(),
            pl.BlockSpec((N_EXPERTS, tm), lambda i: (0, i)),
        ],
        out_shape=[
            jax.ShapeDtypeStruct((N, D), F32),
            jax.ShapeDtypeStruct((N, D), BF16),
            jax.ShapeDtypeStruct((N_EXPERTS, N), F32),
        ],
        compiler_params=pltpu.CompilerParams(
            dimension_semantics=("arbitrary",),
            vmem_limit_bytes=_vmem_limit(6 * D * D * 2 + 24 * tm * D * 4)),
        name="merge",
    )(x2, y_rnn, y_att, proj_rest, proj_rest, wr, wa, wo, gpost, g1, gpre, sc2, sh2, rwt, rb)


def _route_kernel(lg_ref, g_ref, slot_ref, cnt_ref, tcnt_ref, tcar_ref, tlst_ref, carry, *, T):
    i = pl.program_id(0)
    E = lg_ref.shape[0]
    n_sub = lg_ref.shape[1] // T

    @pl.when(i == 0)
    def _():
        carry[...] = jnp.zeros_like(carry)
        tcnt_ref[...] = jnp.zeros_like(tcnt_ref)
        tcar_ref[...] = jnp.zeros_like(tcar_ref)
        tlst_ref[...] = jnp.zeros_like(tlst_ref)

    row = lax.broadcasted_iota(jnp.int32, (E, T), 0).astype(F32)
    tri_t = (lax.broadcasted_iota(jnp.int32, (T, T), 0)
             < lax.broadcasted_iota(jnp.int32, (T, T), 1)).astype(BF16)
    tri_e = (lax.broadcasted_iota(jnp.int32, (E, E), 1)
             < lax.broadcasted_iota(jnp.int32, (E, E), 0)).astype(BF16)
    tile_lane = lax.broadcasted_iota(jnp.int32, tcnt_ref.shape, 1)
    running = carry[...]
    for sub in range(n_sub):
        cols = slice(sub * T, (sub + 1) * T)
        l = lg_ref[:, cols]
        vals, idxs = [], []
        for _ in range(TOP_K):
            m = jnp.max(l, axis=0, keepdims=True)
            idx = jnp.min(jnp.where(l == m, row, float(E)), axis=0, keepdims=True)
            vals.append(m)
            idxs.append(idx)
            l = jnp.where(row == idx, -jnp.inf, l)
        ex = [jnp.exp(v - vals[0]) for v in vals]
        tot = ex[0]
        for e in ex[1:]:
            tot = tot + e
        g_ref[:, cols] = jnp.concatenate([e / tot for e in ex], axis=0)

        hot = [row == idx for idx in idxs]
        onehot = jnp.zeros((E, T), F32)
        for hk in hot:
            onehot = onehot + hk.astype(F32)
        before = jnp.dot(onehot.astype(BF16), tri_t, preferred_element_type=F32)
        cnt = jnp.sum(onehot, axis=1, keepdims=True)
        lstart = jnp.dot(tri_e, jnp.broadcast_to(cnt, (E, V7X_LANES)).astype(BF16),
                         preferred_element_type=F32)[:, 0:1]
        local = before + lstart
        slots = [jnp.sum(jnp.where(hk, local, 0.0), axis=0, keepdims=True) for hk in hot]
        slot_ref[:, cols] = jnp.concatenate(slots, axis=0).astype(jnp.int32)

        mine = tile_lane == i * n_sub + sub
        tcnt_ref[...] = jnp.where(mine, cnt.astype(jnp.int32), tcnt_ref[...])
        tcar_ref[...] = jnp.where(mine, running.astype(jnp.int32), tcar_ref[...])
        tlst_ref[...] = jnp.where(mine, lstart.astype(jnp.int32), tlst_ref[...])
        running = running + cnt
    carry[...] = running
    cnt_ref[...] = running.astype(jnp.int32)


def _route(logits_t):
    E, N = logits_t.shape
    T = _tiles()["moe_t"]
    TS = T * _tiles()["route_tiles"]
    assert T <= 2 ** 8 and N // T <= V7X_LANES
    out = lambda: pl.BlockSpec((TOP_K, TS), lambda i: (0, i))
    per_tile = lambda: pl.BlockSpec((E, V7X_LANES), lambda i: (0, 0))
    return pl.pallas_call(
        functools.partial(_route_kernel, T=T),
        grid=(N // TS,),
        in_specs=[pl.BlockSpec((E, TS), lambda i: (0, i))],
        out_specs=[out(), out(), pl.BlockSpec((E, 1), lambda i: (0, 0)),
                   per_tile(), per_tile(), per_tile()],
        out_shape=[
            jax.ShapeDtypeStruct((TOP_K, N), F32),
            jax.ShapeDtypeStruct((TOP_K, N), jnp.int32),
            jax.ShapeDtypeStruct((E, 1), jnp.int32),
            jax.ShapeDtypeStruct((E, V7X_LANES), jnp.int32),
            jax.ShapeDtypeStruct((E, V7X_LANES), jnp.int32),
            jax.ShapeDtypeStruct((E, V7X_LANES), jnp.int32),
        ],
        scratch_shapes=[pltpu.VMEM((E, 1), F32)],
        compiler_params=pltpu.CompilerParams(dimension_semantics=("arbitrary",)),
        name="route",
    )(logits_t)


def _run_sizes(T):
    return [2 ** b for b in range(int(math.log2(T)), -1, -1)]


def _copy_plan(tile_cnt, tile_lst, tile_dst, T):
    sizes = jnp.asarray(_run_sizes(T), jnp.int32)[None, :, None]
    cnt, lst, dst = (a[:, None, :] for a in (tile_cnt, tile_lst, tile_dst))
    has = (cnt & sizes) != 0
    off = (cnt // (2 * sizes)) * (2 * sizes)
    eids = jnp.arange(N_EXPERTS)
    earlier = eids[:, None] < eids[None, :]
    place = jnp.sum(jnp.where(earlier, has[..., :, None], False), axis=-2)
    hit = has[..., :, None] & (place[..., :, None] == eids)
    pick = lambda v: jnp.sum(jnp.where(hit, v[..., :, None], 0), axis=-2)
    n_tiles, nb = tile_cnt.shape[0], sizes.shape[1]
    parts = [pick(jnp.broadcast_to(lst + off, has.shape)).reshape(n_tiles, -1),
             pick(jnp.broadcast_to(dst + off, has.shape)).reshape(n_tiles, -1),
             jnp.sum(has, axis=-1).reshape(n_tiles, nb)]
    plan = jnp.concatenate(parts, axis=1).astype(jnp.int32)
    pad = _plan_len(T) - plan.shape[1]
    return jnp.pad(plan, ((0, 0), (0, pad))).reshape(-1)


def _plan_len(T):
    nb = len(_run_sizes(T))
    return max(V7X_LANES, pl.next_power_of_2(2 * nb * N_EXPERTS + nb))


def _run_copies(plan_ref, T, make):
    sizes = _run_sizes(T)
    nb = len(sizes)
    for bi, b in enumerate(sizes):
        def body(j, c, bi=bi, b=b):
            make(plan_ref[bi * N_EXPERTS + j], plan_ref[(nb + bi) * N_EXPERTS + j], b).start(
                priority=bi % 2)
            return c

        lax.fori_loop(0, plan_ref[2 * nb * N_EXPERTS + bi], body, 0)


def _row_slab(ref, row, nrows, lead=()):
    rows = pl.ds(pl.multiple_of(row * V7X_SUBLANES, V7X_SUBLANES), nrows * V7X_SUBLANES)
    return ref.at[(*lead, rows)]


def _to_slabs(ref, val, lead=()):
    rows = val.shape[0]
    for s in range(V7X_SUBLANES):
        ref[(*lead, pl.ds(s, rows, stride=V7X_SUBLANES), slice(None))] = (
            val[:, s * V7X_LANES:(s + 1) * V7X_LANES])


def _from_slabs(ref, lead=(), rows=None):
    rows = ref.shape[-2] // V7X_SUBLANES if rows is None else rows
    return jnp.concatenate(
        [ref[(*lead, pl.ds(s, rows, stride=V7X_SUBLANES), slice(None))]
         for s in range(V7X_SUBLANES)], axis=1)


def _dispatch_kernel(tail_ref, nu_ref, plan_ref, slot_ref, h_ref, xs_ref, buf, zbuf, sem, zsem):
    i = pl.program_id(0)
    T = h_ref.shape[0]
    bm = zbuf.shape[0] // V7X_SUBLANES
    nblk = xs_ref.shape[0] // zbuf.shape[0]
    depth = buf.shape[0]
    cur = i % depth

    def zero_copy(row0, s):
        return pltpu.make_async_copy(zbuf, _row_slab(xs_ref, row0, bm), zsem.at[s])

    def read_chunk(e, fn):
        @pl.when(tail_ref[e] >= 0)
        def _():
            fn(zero_copy(tail_ref[e], 0))

    def unread_chunks(fn):
        def per_expert(e, c):
            @pl.when(jnp.logical_and(tail_ref[e] >= 0, tail_ref[e] % (2 * bm) == 0))
            def _():
                fn(zero_copy(tail_ref[e] + bm, 1))
            return c

        def per_unused(j, c):
            fn(zero_copy(j * bm, 1))
            return c

        lax.fori_loop(0, N_EXPERTS, per_expert, 0)
        lax.fori_loop(nu_ref[0], nblk, per_unused, 0)

    @pl.when(i == 0)
    def _():
        zbuf[...] = jnp.zeros_like(zbuf)

        def start(e, c):
            read_chunk(e, lambda cp: cp.start())
            return c

        def wait(e, c):
            read_chunk(e, lambda cp: cp.wait())
            return c

        lax.fori_loop(0, N_EXPERTS, start, 0)
        unread_chunks(lambda cp: cp.start())
        lax.fori_loop(0, N_EXPERTS, wait, 0)

    slot = slot_ref[...]
    r_id = lax.broadcasted_iota(jnp.int32, (TOP_K * T, T), 0)
    perm = r_id == slot[0:1, :]
    for k in range(1, TOP_K):
        perm = jnp.logical_or(perm, r_id == slot[k:k + 1, :])
    grouped = jnp.dot(perm.astype(BF16), h_ref[...], preferred_element_type=F32)
    _to_slabs(buf, grouped, lead=(cur,))

    def wait_tile(b):
        pltpu.make_async_copy(buf.at[b], _row_slab(xs_ref, 0, TOP_K * T), sem.at[b]).wait()

    @pl.when(i >= depth - 1)
    def _():
        wait_tile((i + 1) % depth)

    _run_copies(plan_ref, T, lambda l, d, n: pltpu.make_async_copy(
        _row_slab(buf, l, n, lead=(cur,)), _row_slab(xs_ref, d, n), sem.at[cur]))

    @pl.when(i == pl.num_programs(0) - 1)
    def _():
        for back in range(depth - 1):
            @pl.when(i >= back)
            def _(back=back):
                wait_tile((i - back) % depth)
        unread_chunks(lambda cp: cp.wait())


def _dispatch(h2, slot, plan, tail_start, n_used, n_rows, bm):
    N, D = h2.shape
    T = _tiles()["moe_t"]
    ring = 4
    S8 = V7X_SUBLANES
    imap = lambda f: (lambda i, *_: f(i))
    return pl.pallas_call(
        _dispatch_kernel,
        grid_spec=pltpu.PrefetchScalarGridSpec(
            num_scalar_prefetch=2,
            grid=(N // T,),
            in_specs=[
                pl.BlockSpec((_plan_len(T),), imap(lambda i: (i,)), memory_space=pltpu.SMEM),
                pl.BlockSpec((TOP_K, T), imap(lambda i: (0, i))),
                pl.BlockSpec((T, D), imap(lambda i: (i, 0))),
            ],
            out_specs=pl.BlockSpec(memory_space=pl.ANY),
            scratch_shapes=[
                pltpu.VMEM((ring, TOP_K * T * S8, V7X_LANES), F32),
                pltpu.VMEM((bm * S8, V7X_LANES), F32),
                pltpu.SemaphoreType.DMA((ring,)),
                pltpu.SemaphoreType.DMA((2,)),
            ],
        ),
        out_shape=jax.ShapeDtypeStruct((n_rows * S8, V7X_LANES), F32),
        compiler_params=pltpu.CompilerParams(
            dimension_semantics=("arbitrary",),
            vmem_limit_bytes=_vmem_limit(8 * TOP_K * T * D * 4)),
        name="dispatch",
    )(tail_start, n_used, plan, slot, h2)


def _ffn_kernel(be_ref, bv_ref, bf_ref, nu_ref, nxt_ref, x_ref, w1_hbm, b1g_ref, b1l_ref, w2_hbm,
                b2_ref, y_ref, w1f_s, w2f_s, w1_s, w2_s, sem):
    i = pl.program_id(0)
    bm, D = x_ref.shape[0] // V7X_SUBLANES, w2_s.shape[1]
    n_ff = w2_s.shape[0]
    pw = 2 * V7X_LANES

    def fetch(e):
        return (pltpu.make_async_copy(w1_hbm.at[e], w1f_s, sem.at[0]),
                pltpu.make_async_copy(w2_hbm.at[e], w2f_s, sem.at[1]))

    @pl.when(i == 0)
    def _():
        for cp in fetch(be_ref[0]):
            cp.start()

    @pl.when(jnp.logical_and(i < nu_ref[0], bf_ref[i] == 1))
    def _():
        e = be_ref[i]
        for cp in fetch(e):
            cp.wait()
        src = lax.broadcasted_iota(jnp.int32, (pw, pw), 0)
        dst = lax.broadcasted_iota(jnp.int32, (pw, pw), 1)
        want = jnp.where(dst < V7X_LANES, 2 * dst, 2 * (dst - V7X_LANES) + 1)
        perm = (src == want).astype(BF16)
        for c in range(w1f_s.shape[1] // pw):
            wp = jnp.dot(w1f_s[:, c * pw:(c + 1) * pw].astype(BF16), perm,
                         preferred_element_type=F32).astype(BF16)
            w1_s[:, c * V7X_LANES:(c + 1) * V7X_LANES] = wp[:, :V7X_LANES]
            w1_s[:, n_ff + c * V7X_LANES:n_ff + (c + 1) * V7X_LANES] = wp[:, V7X_LANES:]
        w2_s[...] = w2f_s[...].astype(BF16)

        @pl.when(nxt_ref[e] >= 0)
        def _():
            for cp in fetch(nxt_ref[e]):
                cp.start()

    def mlp(n_rows):
        x = _from_slabs(x_ref, rows=n_rows).astype(BF16)
        hid = jnp.dot(x, w1_s[...], preferred_element_type=F32)
        glu = hid[:, :n_ff] + b1g_ref[0]
        lin = hid[:, n_ff:] + b1l_ref[0]
        glu = jnp.minimum(glu, SWIGLU_LIMIT)
        lin = jnp.clip(lin, -SWIGLU_LIMIT, SWIGLU_LIMIT)
        act = glu * _sigmoid(SWIGLU_ALPHA * glu) * (lin + 1.0)
        _to_slabs(y_ref, jnp.dot(act.astype(BF16), w2_s[...], preferred_element_type=F32) + b2_ref[0])
        if n_rows < bm:
            y_ref[n_rows * V7X_SUBLANES:, :] = jnp.zeros(
                ((bm - n_rows) * V7X_SUBLANES, V7X_LANES), F32)

    used = i < nu_ref[0]
    half = bm // 2

    @pl.when(jnp.logical_and(used, bv_ref[i] > half))
    def _():
        mlp(bm)

    @pl.when(jnp.logical_and(used, bv_ref[i] <= half))
    def _():
        mlp(half)

    @pl.when(i >= nu_ref[0])
    def _():
        y_ref[...] = jnp.zeros_like(y_ref)


def _ffn(xs, blk_e, blk_valid, blk_first, n_used, next_e, w1, b1g, b1l, w2, b2, bm):
    E, D, F2 = w1.shape
    F = F2 // 2
    slab = bm * V7X_SUBLANES
    nblk = xs.shape[0] // slab

    def row_blk(i, be, bv, bf, nu, nxt):
        return (jnp.minimum(i, nu[0] - 1), 0)

    def per_e(i, be, bv, bf, nu, nxt):
        return (be[i], 0, 0)

    vmem = (D * F2 + F * D) * (4 + 2) + 8 * bm * F2 * 4
    return pl.pallas_call(
        _ffn_kernel,
        grid_spec=pltpu.PrefetchScalarGridSpec(
            num_scalar_prefetch=5,
            grid=(nblk,),
            in_specs=[
                pl.BlockSpec((slab, V7X_LANES), row_blk),
                pl.BlockSpec(memory_space=pl.ANY),
                pl.BlockSpec((1, 1, F), per_e),
                pl.BlockSpec((1, 1, F), per_e),
                pl.BlockSpec(memory_space=pl.ANY),
                pl.BlockSpec((1, 1, D), per_e),
            ],
            out_specs=pl.BlockSpec((slab, V7X_LANES), lambda i, be, bv, bf, nu, nxt: (i, 0)),
            scratch_shapes=[
                pltpu.VMEM((D, F2), F32),
                pltpu.VMEM((F, D), F32),
                pltpu.VMEM((D, F2), BF16),
                pltpu.VMEM((F, D), BF16),
                pltpu.SemaphoreType.DMA((2,)),
            ],
        ),
        out_shape=jax.ShapeDtypeStruct(xs.shape, F32),
        compiler_params=pltpu.CompilerParams(
            dimension_semantics=("arbitrary",),
            vmem_limit_bytes=_vmem_limit(vmem)),
        name="ffn",
    )(blk_e, blk_valid, blk_first, n_used, next_e, xs, w1, b1g, b1l, w2, b2)


def _combine_kernel(plan_ref, plan1_ref, plan2_ref, ys_ref, x1_ref, slot_ref, gate_ref, gpost_ref,
                    g2_ref, o_ref, buf, sem):
    i = pl.program_id(0)
    n = pl.num_programs(0)
    T = x1_ref.shape[0]
    depth = buf.shape[0]
    assert depth == 3
    cur = i % depth

    def gather(plan, b):
        _run_copies(plan, T, lambda l, d, n: pltpu.make_async_copy(
            _row_slab(ys_ref, d, n), _row_slab(buf, l, n, lead=(b,)), sem.at[b]))

    @pl.when(i == 0)
    def _():
        gather(plan_ref, 0)

        @pl.when(n > 1)
        def _():
            gather(plan1_ref, 1)

    @pl.when(i + 2 < n)
    def _():
        gather(plan2_ref, (i + 2) % depth)

    pltpu.make_async_copy(_row_slab(ys_ref, 0, TOP_K * T), buf.at[cur], sem.at[cur]).wait()

    slot = slot_ref[...]
    gates = gate_ref[...]
    r_id = lax.broadcasted_iota(jnp.int32, (TOP_K * T, T), 0)
    wt = jnp.zeros((TOP_K * T, T), F32)
    for k in range(TOP_K):
        wt = jnp.where(r_id == slot[k:k + 1, :], gates[k:k + 1, :], wt)
    ff = lax.dot_general(wt.astype(BF16), _from_slabs(buf, lead=(cur,)).astype(BF16),
                         (((0,), (0,)), ((), ())), preferred_element_type=F32)
    o_ref[...] = x1_ref[...] + g2_ref[0] * _rms(ff, gpost_ref[...])


def _combine(ys, plan, x1, slot_tk, gates_tk, gpost, g2, S):
    N, D = x1.shape
    T = _tiles()["moe_t"]
    per_b = S // T
    n_tiles = N // T
    plan_spec = lambda f: pl.BlockSpec((_plan_len(T),), f, memory_space=pltpu.SMEM)
    return pl.pallas_call(
        _combine_kernel,
        grid=(n_tiles,),
        in_specs=[
            plan_spec(lambda i: (i,)),
            plan_spec(lambda i: (jnp.minimum(i + 1, n_tiles - 1),)),
            plan_spec(lambda i: (jnp.minimum(i + 2, n_tiles - 1),)),
            pl.BlockSpec(memory_space=pl.ANY),
            pl.BlockSpec((T, D), lambda i: (i, 0)),
            pl.BlockSpec((TOP_K, T), lambda i: (0, i)),
            pl.BlockSpec((TOP_K, T), lambda i: (0, i)),
            pl.BlockSpec((1, D), lambda i: (0, 0)),
            pl.BlockSpec((1, 1, D), lambda i: (i // per_b, 0, 0)),
        ],
        out_specs=pl.BlockSpec((T, D), lambda i: (i, 0)),
        out_shape=jax.ShapeDtypeStruct((N, D), F32),
        scratch_shapes=[
            pltpu.VMEM((3, TOP_K * T * V7X_SUBLANES, V7X_LANES), F32),
            pltpu.SemaphoreType.DMA((3,)),
        ],
        compiler_params=pltpu.CompilerParams(
            dimension_semantics=("arbitrary",),
            vmem_limit_bytes=_vmem_limit(11 * TOP_K * T * D * 4)),
        name="combine",
    )(plan, plan, plan, ys, x1, slot_tk, gates_tk, gpost, g2)


def _col_scaled(w):
    scale = HEAD_DIM ** -0.5
    assert math.log2(scale).is_integer()
    col = np.arange(D_IN)
    q0, g0 = 2 * D_RNN, 2 * D_RNN + Q_DIM + 2 * KV_DIM
    factor = np.where((col >= q0) & (col < q0 + Q_DIM), scale, np.where(col >= g0, 0.5, 1.0))
    return w * jnp.asarray(factor, w.dtype)


def _layer(x2, c, B, S, p):
    D = D_MODEL
    N = B * S
    ada = _ada(c, p["w_ada"], p["b_ada"])
    sh1, sc1, g1, sh2, sc2, g2 = [a.reshape(B, 1, D) for a in jnp.split(ada, 6, axis=-1)]
    row = lambda v: v.reshape(1, -1)

    gw = _tiles()["rnn_group"]
    proj_rest, y_rnn = _mixin(
        x2, row(p["norm_pre_mix"]), sc1, sh1,
        _col_scaled(p["w_in"]).astype(BF16), _col_scaled(row(p["b_in"])),
        p["conv_w"], row(p["conv_b"]),
        (0.5 * _block_diag_tiles(p["rg_w_a"], gw)).astype(BF16), 0.5 * row(p["rg_b_a"]),
        (0.5 * _block_diag_tiles(p["rg_w_x"], gw)).astype(BF16), 0.5 * row(p["rg_b_x"]),
        row(p["rg_lambda"]), S)
    y_att = _attn(proj_rest, p["attn_sinks"], S)

    x1, h2, logits_t = _merge(
        x2, y_rnn, y_att, proj_rest,
        p["w_o_rnn"].astype(BF16), p["w_o_attn"].astype(BF16), p["w_out"].astype(BF16),
        row(p["norm_post_mix"]), g1, row(p["norm_pre_ffn"]), sc2, sh2,
        p["router_w"].T.astype(BF16), p["router_b"].reshape(N_EXPERTS, 1), S)

    gates, slot, counts, tile_cnt, tile_car, tile_lst = _route(logits_t)

    bm = _tiles()["moe_bm"]
    n_tiles = N // _tiles()["moe_t"]
    n_rows = N * TOP_K + N_EXPERTS * bm
    nblk = n_rows // bm
    counts = counts.reshape(N_EXPERTS)
    padded = ((counts + bm - 1) // bm) * bm
    pend = jnp.cumsum(padded)
    pstart = pend - padded
    eids = jnp.arange(N_EXPERTS, dtype=jnp.int32)
    per_run = lambda a: a[:, :n_tiles].T.astype(jnp.int32)
    plan = _copy_plan(per_run(tile_cnt), per_run(tile_lst), per_run(tile_car + pstart[:, None]),
                      _tiles()["moe_t"])
    zr = bm // 2
    tail_start = jnp.where(counts > 0, (pstart + counts - 1) // zr * zr, -1).astype(jnp.int32)
    n_used = (pend[-1] // bm).astype(jnp.int32).reshape(1)
    blk_row0 = jnp.minimum(jnp.arange(nblk, dtype=jnp.int32), n_used[0] - 1) * bm
    blk_e = jnp.sum(blk_row0[:, None] >= pend[None, :], axis=1).astype(jnp.int32)
    mine = blk_e[:, None] == eids[None, :]
    blk_cnt = jnp.sum(jnp.where(mine, counts, 0), axis=1)
    blk_pstart = jnp.sum(jnp.where(mine, pstart, 0), axis=1)
    blk_valid = jnp.clip(blk_cnt - (blk_row0 - blk_pstart), 0, bm).astype(jnp.int32)
    blk_first = (blk_row0 == blk_pstart).astype(jnp.int32)

    xs = _dispatch(h2, slot, plan, tail_start, n_used * (bm // zr), n_rows, zr)
    b1 = p["moe_b1"].reshape(N_EXPERTS, D_FF, 2)
    later = jnp.logical_and(counts[None, :] > 0, eids[None, :] > eids[:, None])
    next_e = jnp.min(jnp.where(later, eids[None, :], N_EXPERTS), axis=1)
    next_e = jnp.where(next_e == N_EXPERTS, -1, next_e).astype(jnp.int32)
    ys = _ffn(xs, blk_e, blk_valid, blk_first, n_used, next_e, p["moe_w1"],
              b1[:, :, 0].reshape(N_EXPERTS, 1, D_FF), b1[:, :, 1].reshape(N_EXPERTS, 1, D_FF),
              p["moe_w2"], p["moe_b2"].reshape(N_EXPERTS, 1, D), bm)
    return _combine(ys, plan, x1, slot, gates, row(p["norm_post_ffn"]), g2, S)


def kernel(x, c, w_ada, b_ada, norm_pre_mix, norm_post_mix, norm_pre_ffn, norm_post_ffn, w_in, b_in, conv_w, conv_b, rg_w_a, rg_b_a, rg_w_x, rg_b_x, rg_lambda, attn_sinks, w_o_rnn, w_o_attn, w_out, router_w, router_b, moe_w1, moe_b1, moe_w2, moe_b2):
    B, S, D = x.shape
    params = dict(
        w_ada=w_ada, b_ada=b_ada, norm_pre_mix=norm_pre_mix, norm_post_mix=norm_post_mix,
        norm_pre_ffn=norm_pre_ffn, norm_post_ffn=norm_post_ffn, w_in=w_in, b_in=b_in,
        conv_w=conv_w, conv_b=conv_b, rg_w_a=rg_w_a, rg_b_a=rg_b_a, rg_w_x=rg_w_x, rg_b_x=rg_b_x,
        rg_lambda=rg_lambda, attn_sinks=attn_sinks, w_o_rnn=w_o_rnn, w_o_attn=w_o_attn,
        w_out=w_out, router_w=router_w, router_b=router_b, moe_w1=moe_w1, moe_b1=moe_b1,
        moe_w2=moe_w2, moe_b2=moe_b2)
    x2 = x.reshape(B * S, D)
    for layer in range(w_ada.shape[0]):
        x2 = _layer(x2, c, B, S, {k: v[layer] for k, v in params.items()})
    return x2.reshape(B, S, D)
```
